```python
import math
import jax
import jax.numpy as jnp
from jax import lax
import numpy as np

D_MODEL = 1024
BATCH = 8
SEQ = 4096
DEPTH = 2

GRID_W = 64
CTX_LEN = 256
EPS = 1e-6

SHORT_CONV = 4
CONV_LEFT = 2

LRU_HEADS = 4
LRU_HEAD_DIM = 64
LRU_WIDTH = LRU_HEADS * LRU_HEAD_DIM
LRU_C = 8.0

SSD_HEADS = 4
SSD_HEAD_DIM = 64
SSD_WIDTH = SSD_HEADS * SSD_HEAD_DIM
SSD_GROUPS = 2
SSD_STATE = 128
SSD_CHUNK = 128
SSD_CONV_DIM = SSD_WIDTH + 2 * SSD_GROUPS * SSD_STATE

DA_HEADS = 4
DA_QK_DIM = 64
DA_V_DIM = 2 * DA_QK_DIM
DA_QK_WIDTH = DA_HEADS * 2 * DA_QK_DIM
DA_WIDTH = DA_HEADS * DA_V_DIM
DA_SCALE = DA_QK_DIM ** -0.5
Q_BLOCK = 128
ROPE_BASE = 10000.0

D_MIX = LRU_WIDTH + SSD_WIDTH + DA_WIDTH

OFF_LRU_X = LRU_WIDTH
OFF_SSD_Z = 2 * LRU_WIDTH
OFF_SSD_XBC = OFF_SSD_Z + SSD_WIDTH
OFF_SSD_DT = OFF_SSD_XBC + SSD_CONV_DIM
OFF_DA_Q = OFF_SSD_DT + 2 * SSD_HEADS
OFF_DA_K = OFF_DA_Q + DA_QK_WIDTH
OFF_DA_V = OFF_DA_K + DA_QK_WIDTH
D_IN = OFF_DA_V + DA_WIDTH
IN_SPLITS = (OFF_LRU_X, OFF_SSD_Z, OFF_SSD_XBC, OFF_SSD_DT, OFF_DA_Q, OFF_DA_K, OFF_DA_V)

N_EXPERTS = 32
TOP_K = 4
D_EXPERT = 1024
SWIGLU_ALPHA = 1.702
SWIGLU_LIMIT = 7.0
MOE_BLOCK = 256

kernel_name = 'hymba_lru_ssd_diffattn_moe_dit'


def rmsnorm(x, g):
    xf = x.astype(jnp.float32)
    y = xf * lax.rsqrt(jnp.mean(xf * xf, axis=-1, keepdims=True) + EPS)
    return (y * g.astype(jnp.float32)).astype(x.dtype)


def modulate(h, shift, scale):
    return h * (1.0 + scale) + shift


def dwconv(x, w, b):
    k = w.shape[0]
    y = lax.conv_general_dilated(x, w[:, None, :].astype(x.dtype), window_strides=(1,),
                                 padding=[(CONV_LEFT, k - 1 - CONV_LEFT)],
                                 dimension_numbers=('NWC', 'WIO', 'NWC'),
                                 feature_group_count=x.shape[-1])
    return y + b.astype(x.dtype)


def rglru_coeffs(xc, wa, ba, wx, bx, lam):
    b, l, w = xc.shape
    xh = xc.reshape(b, l, LRU_HEADS, LRU_HEAD_DIM)
    r = jax.nn.sigmoid(jnp.einsum('blhi,hij->blhj', xh, wa.astype(jnp.float32)).reshape(b, l, w) + ba.astype(jnp.float32))
    i = jax.nn.sigmoid(jnp.einsum('blhi,hij->blhj', xh, wx.astype(jnp.float32)).reshape(b, l, w) + bx.astype(jnp.float32))
    log_a = -LRU_C * r * jax.nn.softplus(-lam.astype(jnp.float32))
    a = jnp.exp(log_a)
    u = jnp.sqrt(-jnp.expm1(2.0 * log_a)) * (i * xc)
    return a, u


def _lin_combine(lhs, rhs):
    a1, b1 = lhs
    a2, b2 = rhs
    return a1 * a2, a2 * b1 + b2


def linear_scan(a, u, h0, reverse):
    edge = -1 if reverse else 0
    u = u.at[:, edge].add(a[:, edge] * h0)
    _, h = lax.associative_scan(_lin_combine, (a, u), reverse=reverse, axis=1)
    return h


def lru_mix(gate_in, x_in, conv_w, conv_b, wa, ba, wx, bx, lam, h0f, h0b, need_y):
    xc = dwconv(x_in, conv_w, conv_b).astype(jnp.float32)
    af, uf = rglru_coeffs(xc, wa[0], ba[0], wx[0], bx[0], lam[0])
    ab, ub = rglru_coeffs(xc, wa[1], ba[1], wx[1], bx[1], lam[1])
    hf = linear_scan(af, uf, h0f, False)
    hb = linear_scan(ab, ub, h0b, True)
    if not need_y:
        return None, hf[:, -1], hb[:, 0]
    y = jax.nn.gelu(gate_in.astype(jnp.float32)) * (hf + hb)
    return y.astype(gate_in.dtype), hf[:, -1], hb[:, 0]


def segsum(x):
    t = x.shape[-1]
    xx = jnp.broadcast_to(x[..., None], x.shape + (t,))
    xx = jnp.where(jnp.tril(jnp.ones((t, t), bool), -1), xx, 0.0)
    cs = jnp.cumsum(xx, axis=-2)
    return jnp.where(jnp.tril(jnp.ones((t, t), bool)), cs, -jnp.inf)


def ssd_scan(xdt, adt, bm, cm, h0, need_y):
    b, l, h, p = xdt.shape
    n = bm.shape[-1]
    nc = l // SSD_CHUNK
    X = xdt.reshape(b, nc, SSD_CHUNK, h, p)
    Bc = bm.reshape(b, nc, SSD_CHUNK, h, n)
    Cc = cm.reshape(b, nc, SSD_CHUNK, h, n)
    A = adt.reshape(b, nc, SSD_CHUNK, h).transpose(0, 3, 1, 2)
    A_cs = jnp.cumsum(A, axis=-1)
    decay_states = jnp.exp(A_cs[..., -1:] - A_cs)
    states = jnp.einsum('bclhn,bhcl,bclhp->bchpn', Bc, decay_states, X)
    states = jnp.concatenate([h0[:, None], states], axis=1)
    decay_chunk = jnp.exp(segsum(jnp.pad(A_cs[..., -1], ((0, 0), (0, 0), (1, 0)))))
    new_states = jnp.einsum('bhzc,bchpn->bzhpn', decay_chunk, states)
    final = new_states[:, -1]
    if not need_y:
        return None, final
    states = new_states[:, :-1]
    Lmat = jnp.exp(segsum(A))
    y_diag = jnp.einsum('bclhn,bcshn,bhcls,bcshp->bclhp', Cc, Bc, Lmat, X)
    y_off = jnp.einsum('bclhn,bchpn,bhcl->bclhp', Cc, states, jnp.exp(A_cs))
    return (y_diag + y_off).reshape(b, l, h, p), final


def ssd_mix(z, xbc, dt_raw, conv_w, conv_b, a_log, dt_bias, d_skip, norm_g, h0f, h0b, need_y):
    b, l, _ = xbc.shape
    xbc = jax.nn.silu(dwconv(xbc, conv_w, conv_b)).astype(jnp.float32)
    xs, bm, cm = jnp.split(xbc, [SSD_WIDTH, SSD_WIDTH + SSD_GROUPS * SSD_STATE], axis=-1)
    xs = xs.reshape(b, l, SSD_HEADS, SSD_HEAD_DIM)
    rep = SSD_HEADS // SSD_GROUPS
    bm = jnp.repeat(bm.reshape(b, l, SSD_GROUPS, SSD_STATE), rep, axis=2)
    cm = jnp.repeat(cm.reshape(b, l, SSD_GROUPS, SSD_STATE), rep, axis=2)
    dt_all = jax.nn.softplus(dt_raw.astype(jnp.float32).reshape(b, l, 2, SSD_HEADS) + dt_bias.astype(jnp.float32))
    a = -jnp.exp(a_log.astype(jnp.float32))
    dtf, dtb = dt_all[:, :, 0], dt_all[:, :, 1]
    yf, ff = ssd_scan(xs * dtf[..., None], dtf * a[0], bm, cm, h0f, need_y)
    yb, fb = ssd_scan((xs * dtb[..., None])[:, ::-1], (dtb * a[1])[:, ::-1], bm[:, ::-1], cm[:, ::-1], h0b, need_y)
    if not need_y:
        return None, ff, fb
    y = yf + yb[:, ::-1] + d_skip.astype(jnp.float32)[:, None] * xs
    y = y.reshape(b, l, SSD_WIDTH) * jax.nn.silu(z.astype(jnp.float32))
    return rmsnorm(y, norm_g).astype(z.dtype), ff, fb


def axial_rope(rows):
    r, col = jnp.meshgrid(jnp.arange(rows, dtype=jnp.float32), jnp.arange(GRID_W, dtype=jnp.float32), indexing='ij')
    per_axis = DA_QK_DIM // 4
    inv = ROPE_BASE ** (-jnp.arange(per_axis, dtype=jnp.float32) / per_axis)
    ang = jnp.concatenate([r.reshape(-1, 1) * inv, col.reshape(-1, 1) * inv], axis=-1)
    return jnp.cos(ang), jnp.sin(ang)


def apply_rope(t, cos, sin):
    tf = t.astype(jnp.float32)
    half = DA_QK_DIM // 2
    t1, t2 = tf[..., :half], tf[..., half:]
    c = cos[:, None, None, :]
    s = sin[:, None, None, :]
    return jnp.concatenate([t1 * c - t2 * s, t2 * c + t1 * s], axis=-1).astype(t.dtype)


def da_qk(t, g, cos, sin):
    b, l, _ = t.shape
    t = rmsnorm(t.reshape(b, l, DA_HEADS, 2, DA_QK_DIM), g)
    if cos is not None:
        t = apply_rope(t, cos, sin)
    return t


def diff_lambda(lam_q, lam_k, lam_init):
    lq = lam_q.astype(jnp.float32)
    lk = lam_k.astype(jnp.float32)
    return jnp.exp(jnp.sum(lq[0] * lk[0])) - jnp.exp(jnp.sum(lq[1] * lk[1])) + lam_init


def diff_block(q, k, v, lam):
    s = jnp.einsum('bqhtd,bkhtd->bhtqk', q, k, preferred_element_type=jnp.float32) * DA_SCALE
    p = jax.nn.softmax(s, axis=-1)
    w = p[:, :, 0] - lam * p[:, :, 1]
    return jnp.einsum('bhqk,bkhe->bqhe', w.astype(v.dtype), v)


def diff_attn_latent(q, k, v, k_ctx, v_ctx, lam):
    b, l = q.shape[:2]
    k_all = jnp.concatenate([k_ctx, k], axis=1)
    v_all = jnp.concatenate([v_ctx, v], axis=1)
    nb = l // Q_BLOCK
    qb = q.reshape(b, nb, Q_BLOCK, DA_HEADS, 2, DA_QK_DIM).swapaxes(0, 1)
    o = lax.map(lambda qi: diff_block(qi, k_all, v_all, lam), qb)
    return o.swapaxes(0, 1).reshape(b, l, DA_HEADS, DA_V_DIM)


def diff_heads_out(o, g, lam_init):
    b, l = o.shape[:2]
    return (rmsnorm(o, g) * (1.0 - lam_init)).reshape(b, l, DA_WIDTH)


def expert_swiglu(xb, wgu, bgu, wd, bd):
    gu = jnp.matmul(xb, wgu) + bgu
    gate, up = gu[..., ::2], gu[..., 1::2]
    gate = jnp.minimum(gate, SWIGLU_LIMIT)
    up = jnp.clip(up, -SWIGLU_LIMIT, SWIGLU_LIMIT)
    glu = gate * jax.nn.sigmoid(SWIGLU_ALPHA * gate)
    return jnp.matmul((up + 1.0) * glu, wd) + bd


def moe_ffn(h, router_w, router_b, w_gu, b_gu, w_down, b_down):
    t, d = h.shape
    n = t * TOP_K
    logits = jnp.matmul(h, router_w, preferred_element_type=jnp.float32) + router_b.astype(jnp.float32)
    top_v, top_e = lax.top_k(logits, TOP_K)
    gates = jax.nn.softmax(top_v, axis=-1)
    flat_e = top_e.reshape(n)
    order = jnp.argsort(flat_e)
    e_sorted = flat_e[order]
    tok_sorted = order // TOP_K
    gate_sorted = gates.reshape(n)[order]
    counts = jnp.bincount(flat_e, length=N_EXPERTS)
    padded = (counts + MOE_BLOCK - 1) // MOE_BLOCK * MOE_BLOCK
    start = jnp.cumsum(counts) - counts
    pend = jnp.cumsum(padded)
    pstart = pend - padded
    dest = pstart[e_sorted] + jnp.arange(n) - start[e_sorted]
    nb = (n + N_EXPERTS * (MOE_BLOCK - 1)) // MOE_BLOCK
    slot_tok = jnp.full((nb * MOE_BLOCK,), t, jnp.int32).at[dest].set(tok_sorted.astype(jnp.int32))
    block_e = jnp.minimum(jnp.searchsorted(pend, jnp.arange(nb) * MOE_BLOCK, side='right'), N_EXPERTS - 1)
    h_pad = jnp.concatenate([h, jnp.zeros((1, d), h.dtype)], axis=0)
    xb = h_pad[slot_tok].reshape(nb, MOE_BLOCK, d)
    yb = lax.map(lambda args: expert_swiglu(args[0], w_gu[args[1]], b_gu[args[1]], w_down[args[1]], b_down[args[1]]), (xb, block_e))
    y = yb.reshape(nb * MOE_BLOCK, d)[dest] * gate_sorted[:, None].astype(h.dtype)
    return jnp.zeros((t, d), h.dtype).at[tok_sorted].add(y)


def setup_inputs(seed: int = 0) -> dict:
    key = jax.random.key(seed)
    ks = iter(jax.random.split(key, 48))
    f32 = jnp.float32

    def nrm(shape, scale):
        return jax.random.normal(next(ks), shape, f32) * scale

    def gain(shape):
        return 1.0 + nrm(shape, 0.02)

    L = DEPTH
    u_lru = jax.random.uniform(next(ks), (L, 2, LRU_WIDTH), f32, 0.9, 0.999)
    a_lru = u_lru ** (1.0 / LRU_C)
    dt0 = jnp.exp(jax.random.uniform(next(ks), (L, 2, SSD_HEADS), f32, math.log(1e-3), math.log(1e-1)))
    return {
        'x': nrm((BATCH, SEQ, D_MODEL), 1.0),
        'c': nrm((BATCH, D_MODEL), 1.0),
        'ctx': nrm((BATCH, CTX_LEN, D_MODEL), 1.0),
        'c_ctx': nrm((D_MODEL,), 1.0),
        'w_mod': nrm((L, D_MODEL, 6 * D_MODEL), D_MODEL ** -0.5),
        'b_mod': nrm((L, 6 * D_MODEL), 0.02),
        'norm1_g': gain((L, D_MODEL)),
        'norm2_g': gain((L, D_MODEL)),
        'w_in': nrm((L, D_MODEL, D_IN), D_MODEL ** -0.5),
        'w_out': nrm((L, D_MIX, D_MODEL), D_MIX ** -0.5),
        'lru_conv_w': nrm((L, SHORT_CONV, LRU_WIDTH), SHORT_CONV ** -0.5),
        'lru_conv_b': nrm((L, LRU_WIDTH), 0.02),
        'lru_wa': nrm((L, 2, LRU_HEADS, LRU_HEAD_DIM, LRU_HEAD_DIM), LRU_HEAD_DIM ** -0.5),
        'lru_ba': nrm((L, 2, LRU_WIDTH), 0.02),
        'lru_wx': nrm((L, 2, LRU_HEADS, LRU_HEAD_DIM, LRU_HEAD_DIM), LRU_HEAD_DIM ** -0.5),
        'lru_bx': nrm((L, 2, LRU_WIDTH), 0.02),
        'lru_lam': jnp.log(a_lru) - jnp.log1p(-a_lru),
        'ssd_conv_w': nrm((L, SHORT_CONV, SSD_CONV_DIM), SHORT_CONV ** -0.5),
        'ssd_conv_b': nrm((L, SSD_CONV_DIM), 0.02),
        'ssd_a_log': jnp.log(jax.random.uniform(next(ks), (L, 2, SSD_HEADS), f32, 1.0, 16.0)),
        'ssd_dt_bias': dt0 + jnp.log(-jnp.expm1(-dt0)),
        'ssd_d': 1.0 + nrm((L, SSD_HEADS), 0.1),
        'ssd_norm_g': gain((L, SSD_WIDTH)),
        'da_q_norm': gain((L, DA_QK_DIM)),
        'da_k_norm': gain((L, DA_QK_DIM)),
        'da_lam_q': nrm((L, 2, DA_QK_DIM), 0.1),
        'da_lam_k': nrm((L, 2, DA_QK_DIM), 0.1),
        'da_subln_g': gain((L, DA_V_DIM)),
        'router_w': nrm((L, D_MODEL, N_EXPERTS), D_MODEL ** -0.5),
        'router_b': nrm((L, N_EXPERTS), 0.01),
        'exp_w_gu': nrm((L, N_EXPERTS, D_MODEL, 2 * D_EXPERT), D_MODEL ** -0.5),
        'exp_b_gu': nrm((L, N_EXPERTS, 2 * D_EXPERT), 0.02),
        'exp_w_down': nrm((L, N_EXPERTS, D_EXPERT, D_MODEL), D_EXPERT ** -0.5),
        'exp_b_down': nrm((L, N_EXPERTS, D_MODEL), 0.02),
    }


def reference(x, c, ctx, c_ctx, w_mod, b_mod, norm1_g, norm2_g, w_in, w_out,
              lru_conv_w, lru_conv_b, lru_wa, lru_ba, lru_wx, lru_bx, lru_lam,
              ssd_conv_w, ssd_conv_b, ssd_a_log, ssd_dt_bias, ssd_d, ssd_norm_g,
              da_q_norm, da_k_norm, da_lam_q, da_lam_k, da_subln_g,
              router_w, router_b, exp_w_gu, exp_b_gu, exp_w_down, exp_b_down):
    b, seq_len, d = x.shape
    ctx_len = ctx.shape[1]
    rows = seq_len // GRID_W
    cos, sin = axial_rope(rows)
    x_lat, x_ctx = x, ctx
    for l in range(DEPTH):
        need_ctx = l < DEPTH - 1
        lam_init = 0.8 - 0.6 * math.exp(-0.3 * l)
        mod = jnp.matmul(jax.nn.silu(c), w_mod[l]) + b_mod[l]
        mod_c = jnp.matmul(jax.nn.silu(c_ctx), w_mod[l]) + b_mod[l]
        sh1, sc1, g1, sh2, sc2, g2 = jnp.split(mod[:, None, :], 6, axis=-1)
        sh1c, sc1c, g1c, sh2c, sc2c, g2c = jnp.split(mod_c, 6, axis=-1)

        h_lat = modulate(rmsnorm(x_lat, norm1_g[l]), sh1, sc1)
        h_ctx = modulate(rmsnorm(x_ctx, norm1_g[l]), sh1c, sc1c)
        lg_l, lx_l, sz_l, sxbc_l, sdt_l, q_l, k_l, v_l = jnp.split(jnp.matmul(h_lat, w_in[l]), IN_SPLITS, axis=-1)
        lg_c, lx_c, sz_c, sxbc_c, sdt_c, q_c, k_c, v_c = jnp.split(jnp.matmul(h_ctx, w_in[l]), IN_SPLITS, axis=-1)

        zl = jnp.zeros((b, LRU_WIDTH), jnp.float32)
        lru_c, lru_sf, lru_sb = lru_mix(lg_c, lx_c, lru_conv_w[l], lru_conv_b[l], lru_wa[l], lru_ba[l],
                                        lru_wx[l], lru_bx[l], lru_lam[l], zl, zl, need_ctx)
        lru_y, _, _ = lru_mix(lg_l, lx_l, lru_conv_w[l], lru_conv_b[l], lru_wa[l], lru_ba[l],
                              lru_wx[l], lru_bx[l], lru_lam[l], lru_sf, lru_sb, True)

        zs = jnp.zeros((b, SSD_HEADS, SSD_HEAD_DIM, SSD_STATE), jnp.float32)
        ssd_c, ssd_sf, ssd_sb = ssd_mix(sz_c, sxbc_c, sdt_c, ssd_conv_w[l], ssd_conv_b[l], ssd_a_log[l],
                                        ssd_dt_bias[l], ssd_d[l], ssd_norm_g[l], zs, zs, need_ctx)
        ssd_y, _, _ = ssd_mix(sz_l, sxbc_l, sdt_l, ssd_conv_w[l], ssd_conv_b[l], ssd_a_log[l],
                              ssd_dt_bias[l], ssd_d[l], ssd_norm_g[l], ssd_sf, ssd_sb, True)

        lam = diff_lambda(da_lam_q[l], da_lam_k[l], lam_init)
        kc = da_qk(k_c, da_k_norm[l], None, None)
        vc = v_c.reshape(b, ctx_len, DA_HEADS, DA_V_DIM)
        ql = da_qk(q_l, da_q_norm[l], cos, sin)
        kl = da_qk(k_l, da_k_norm[l], cos, sin)
        vl = v_l.reshape(b, seq_len, DA_HEADS, DA_V_DIM)
        da_y = diff_heads_out(diff_attn_latent(ql, kl, vl, kc, vc, lam), da_subln_g[l], lam_init)

        o_lat = jnp.matmul(jnp.concatenate([lru_y, ssd_y, da_y], axis=-1), w_out[l])
        if need_ctx:
            qc = da_qk(q_c, da_q_norm[l], None, None)
            da_c = diff_heads_out(diff_block(qc, kc, vc, lam), da_subln_g[l], lam_init)
            o_ctx = jnp.matmul(jnp.concatenate([lru_c, ssd_c, da_c], axis=-1), w_out[l])
            x_ctx = x_ctx + g1c * o_ctx
        x_lat = x_lat + g1 * o_lat

        h2_lat = modulate(rmsnorm(x_lat, norm2_g[l]), sh2, sc2)
        if need_ctx:
            h2_ctx = modulate(rmsnorm(x_ctx, norm2_g[l]), sh2c, sc2c)
            tok = jnp.concatenate([h2_ctx, h2_lat], axis=1).reshape(-1, d)
            f = moe_ffn(tok, router_w[l], router_b[l], exp_w_gu[l], exp_b_gu[l],
                        exp_w_down[l], exp_b_down[l]).reshape(b, ctx_len + seq_len, d)
            x_ctx = x_ctx + g2c * f[:, :ctx_len]
            x_lat = x_lat + g2 * f[:, ctx_len:]
        else:
            f = moe_ffn(h2_lat.reshape(-1, d), router_w[l], router_b[l], exp_w_gu[l], exp_b_gu[l],
                        exp_w_down[l], exp_b_down[l]).reshape(b, seq_len, d)
            x_lat = x_lat + g2 * f
    return x_lat
```

```python
import functools
import math

import jax
import jax.numpy as jnp
from jax import lax
from jax.experimental import pallas as pl
from jax.experimental.pallas import tpu as pltpu

F32 = jnp.float32
BF16 = jnp.bfloat16
I32 = jnp.int32

GRID_W = 64
EPS = 1e-6
CONV_K = 4
LRU_HEADS = 4
LRU_HEAD_DIM = 64
LRU_WIDTH = 256
LRU_C = 8.0
SSD_HEADS = 4
SSD_HEAD_DIM = 64
SSD_WIDTH = 256
SSD_GROUPS = 2
SSD_STATE = 128
SSD_CHUNK = 128
SSD_CONV_DIM = 768
DA_HEADS = 4
DA_QK_DIM = 64
DA_V_DIM = 128
DA_QK_WIDTH = 512
DA_WIDTH = 512
DA_SCALE = DA_QK_DIM ** -0.5
ROPE_BASE = 10000.0
N_EXPERTS = 32
TOP_K = 4
SWIGLU_ALPHA = 1.702
SWIGLU_LIMIT = 7.0

LANES = 128
SUBLANES = 8
VMEM_LIMIT = 56 * 1024 * 1024

PROJ_TM = 512
ATT_TQ = 128
LRU_TL = 512
ROUTE_TM = 256
MOE_TM = 256
NEG_BIG = -1e30


def _cparams(sem):
    return pltpu.CompilerParams(dimension_semantics=sem, vmem_limit_bytes=VMEM_LIMIT)


def _sigmoid(x):
    return 1.0 / (1.0 + jnp.exp(-x))


def _silu(x):
    return x * _sigmoid(x)


def _softplus(x):
    return jnp.maximum(x, 0.0) + jnp.log(1.0 + jnp.exp(-jnp.abs(x)))


def _mod_kernel(c_ref, w_ref, b_ref, o_ref):
    a = _silu(c_ref[...])
    o_ref[...] = jnp.dot(a, w_ref[...], preferred_element_type=F32,
                         precision=lax.Precision.HIGHEST) + b_ref[...]


def _modulation(rows, w, b):
    m, d = rows.shape
    n = w.shape[1]
    tn = 512
    return pl.pallas_call(
        _mod_kernel,
        out_shape=jax.ShapeDtypeStruct((m, n), F32),
        grid=(n // tn,),
        in_specs=[pl.BlockSpec((m, d), lambda j: (0, 0)),
                  pl.BlockSpec((d, tn), lambda j: (0, j)),
                  pl.BlockSpec((1, tn), lambda j: (0, j))],
        out_specs=pl.BlockSpec((m, tn), lambda j: (0, j)),
        compiler_params=_cparams(("arbitrary",)),
        name="modulation",
    )(rows, w, b.reshape(1, n))


def _norm_mod(x, g, sh, sc):
    ms = jnp.mean(x * x, axis=-1, keepdims=True)
    y = x * lax.rsqrt(ms + EPS) * g
    return y * (1.0 + sc) + sh


def _proj_in_kernel(x_ref, g_ref, sh_ref, sc_ref, w_ref, *o_refs, splits):
    h = _norm_mod(x_ref[0], g_ref[...], sh_ref[0], sc_ref[0]).astype(BF16)
    for o_ref, (lo, hi) in zip(o_refs, splits):
        o_ref[0] = jnp.dot(h, w_ref[:, lo:hi], preferred_element_type=F32).astype(o_ref.dtype)


_IN_GROUPS = (("lg", 256, F32), ("lx", 256, F32), ("z", 256, F32), ("xbc", 768, F32),
              ("dt", 128, F32), ("q", 512, F32), ("k", 512, F32), ("v", 512, BF16))


def _prep_w_in(w_in):
    d = w_in.shape[0]
    lo = 2 * LRU_WIDTH + SSD_WIDTH + SSD_CONV_DIM
    hi = lo + 2 * SSD_HEADS
    dt = jnp.concatenate([w_in[:, lo:hi], jnp.zeros((d, LANES - (hi - lo)), w_in.dtype)], axis=1)
    w = jnp.concatenate([w_in[:, :lo], dt, w_in[:, hi:]], axis=1)
    return w.astype(BF16)


def _proj_in(x, g, sh, sc, w):
    b, s, d = x.shape
    tm = min(PROJ_TM, s)
    splits, off = [], 0
    for _, wd, _ in _IN_GROUPS:
        splits.append((off, off + wd))
        off += wd
    out_shape = [jax.ShapeDtypeStruct((b, s, wd), dt) for _, wd, dt in _IN_GROUPS]
    out_specs = [pl.BlockSpec((1, tm, wd), lambda bi, i: (bi, i, 0)) for _, wd, _ in _IN_GROUPS]
    outs = pl.pallas_call(
        functools.partial(_proj_in_kernel, splits=tuple(splits)),
        out_shape=out_shape,
        grid=(b, s // tm),
        in_specs=[pl.BlockSpec((1, tm, d), lambda bi, i: (bi, i, 0)),
                  pl.BlockSpec((1, d), lambda bi, i: (0, 0)),
                  pl.BlockSpec((1, 1, d), lambda bi, i: (bi, 0, 0)),
                  pl.BlockSpec((1, 1, d), lambda bi, i: (bi, 0, 0)),
                  pl.BlockSpec((d, off), lambda bi, i: (0, 0))],
        out_specs=out_specs,
        compiler_params=_cparams(("arbitrary", "arbitrary")),
        name="proj_in",
    )(x, g.reshape(1, d), sh, sc, w)
    return {name: o for (name, _, _), o in zip(_IN_GROUPS, outs)}


def _dwconv_kernel(x_ref, w_ref, b_ref, o_ref, *, act):
    x = x_ref[0]
    s = x.shape[0]
    row = lax.broadcasted_iota(I32, x.shape, 0)
    acc = x * w_ref[2:3, :] + b_ref[...]
    acc = acc + jnp.where(row >= 2, pltpu.roll(x, 2, 0), 0.0) * w_ref[0:1, :]
    acc = acc + jnp.where(row >= 1, pltpu.roll(x, 1, 0), 0.0) * w_ref[1:2, :]
    acc = acc + jnp.where(row < s - 1, pltpu.roll(x, s - 1, 0), 0.0) * w_ref[3:4, :]
    if act:
        acc = _silu(acc)
    o_ref[0] = acc


def _dwconv(x, w, bias, act):
    b, s, c = x.shape
    tc = 256
    return pl.pallas_call(
        functools.partial(_dwconv_kernel, act=act),
        out_shape=jax.ShapeDtypeStruct((b, s, c), F32),
        grid=(b, c // tc),
        in_specs=[pl.BlockSpec((1, s, tc), lambda bi, j: (bi, 0, j)),
                  pl.BlockSpec((CONV_K, tc), lambda bi, j: (0, j)),
                  pl.BlockSpec((1, tc), lambda bi, j: (0, j))],
        out_specs=pl.BlockSpec((1, s, tc), lambda bi, j: (bi, 0, j)),
        compiler_params=_cparams(("arbitrary", "arbitrary")),
        name="dwconv",
    )(x, w, bias.reshape(1, c))


def _lru_coeffs(x, w_ref, b_ref, sp_ref, a_s, u_s):
    g = jnp.dot(x.astype(BF16), w_ref[...], preferred_element_type=F32) + b_ref[...]
    r = _sigmoid(g[:, :LRU_WIDTH])
    i = _sigmoid(g[:, LRU_WIDTH:])
    log_a = -LRU_C * r * sp_ref[...]
    a_s[...] = jnp.exp(log_a)
    u_s[...] = jnp.sqrt(1.0 - jnp.exp(2.0 * log_a)) * (i * x)


def _lru_kernel(xf_ref, xb_ref, wf_ref, wb_ref, bf_ref, bb_ref, spf_ref, spb_ref,
                h0f_ref, h0b_ref, hf_ref, hb_ref, a_s, u_s, cf_s, cb_s):
    c = pl.program_id(1)
    tl = xf_ref.shape[1]
    ngroups = tl // SUBLANES
    shape = (SUBLANES, LRU_WIDTH)
    row = lax.broadcasted_iota(I32, shape, 0)

    @pl.when(c == 0)
    def _():
        cf_s[...] = jnp.broadcast_to(h0f_ref[0], shape)
        cb_s[...] = jnp.broadcast_to(h0b_ref[0], shape)

    _lru_coeffs(xf_ref[0], wf_ref, bf_ref, spf_ref, a_s, u_s)

    def fwd(gi, carry):
        sl = pl.ds(pl.multiple_of(gi * SUBLANES, SUBLANES), SUBLANES)
        a = a_s[sl, :]
        u = u_s[sl, :]
        for sh in (1, 2, 4):
            keep = row >= sh
            a_sh = jnp.where(keep, pltpu.roll(a, sh, 0), 1.0)
            u_sh = jnp.where(keep, pltpu.roll(u, sh, 0), 0.0)
            u = a * u_sh + u
            a = a * a_sh
        h = a * carry + u
        hf_ref[0, sl, :] = h
        return jnp.broadcast_to(h[SUBLANES - 1:SUBLANES, :], shape)

    cf_s[...] = lax.fori_loop(0, ngroups, fwd, cf_s[...])

    _lru_coeffs(xb_ref[0], wb_ref, bb_ref, spb_ref, a_s, u_s)

    def bwd(gj, carry):
        gi = ngroups - 1 - gj
        sl = pl.ds(pl.multiple_of(gi * SUBLANES, SUBLANES), SUBLANES)
        a = a_s[sl, :]
        u = u_s[sl, :]
        for sh in (1, 2, 4):
            keep = row < SUBLANES - sh
            a_sh = jnp.where(keep, pltpu.roll(a, SUBLANES - sh, 0), 1.0)
            u_sh = jnp.where(keep, pltpu.roll(u, SUBLANES - sh, 0), 0.0)
            u = a * u_sh + u
            a = a * a_sh
        h = a * carry + u
        hb_ref[0, sl, :] = h
        return jnp.broadcast_to(h[0:1, :], shape)

    cb_s[...] = lax.fori_loop(0, ngroups, bwd, cb_s[...])


def _lru_gate_weights(wa, wx):
    def dense(wh):
        m = jnp.zeros((LRU_WIDTH, LRU_WIDTH), wh.dtype)
        for h in range(LRU_HEADS):
            lo = h * LRU_HEAD_DIM
            m = m.at[lo:lo + LRU_HEAD_DIM, lo:lo + LRU_HEAD_DIM].set(wh[h])
        return m
    return jnp.concatenate([dense(wa), dense(wx)], axis=1).astype(BF16)


def _lru_scan(xc, wa, ba, wx, bx, lam, h0f, h0b):
    b, s, w = xc.shape
    tl = min(LRU_TL, s)
    nc = s // tl
    wf = _lru_gate_weights(wa[0], wx[0])
    wb = _lru_gate_weights(wa[1], wx[1])
    bf = jnp.concatenate([ba[0], bx[0]]).reshape(1, 2 * w)
    bb = jnp.concatenate([ba[1], bx[1]]).reshape(1, 2 * w)
    sp = jax.nn.softplus(-lam.astype(F32))
    const = lambda bi, c: (0, 0)
    hf, hb = pl.pallas_call(
        _lru_kernel,
        out_shape=[jax.ShapeDtypeStruct((b, s, w), F32)] * 2,
        grid=(b, nc),
        in_specs=[pl.BlockSpec((1, tl, w), lambda bi, c: (bi, c, 0)),
                  pl.BlockSpec((1, tl, w), lambda bi, c: (bi, nc - 1 - c, 0)),
                  pl.BlockSpec((w, 2 * w), const), pl.BlockSpec((w, 2 * w), const),
                  pl.BlockSpec((1, 2 * w), const), pl.BlockSpec((1, 2 * w), const),
                  pl.BlockSpec((1, w), const), pl.BlockSpec((1, w), const),
                  pl.BlockSpec((1, 1, w), lambda bi, c: (bi, 0, 0)),
                  pl.BlockSpec((1, 1, w), lambda bi, c: (bi, 0, 0))],
        out_specs=[pl.BlockSpec((1, tl, w), lambda bi, c: (bi, c, 0)),
                   pl.BlockSpec((1, tl, w), lambda bi, c: (bi, nc - 1 - c, 0))],
        scratch_shapes=[pltpu.VMEM((tl, w), F32), pltpu.VMEM((tl, w), F32),
                        pltpu.VMEM((SUBLANES, w), F32), pltpu.VMEM((SUBLANES, w), F32)],
        compiler_params=_cparams(("arbitrary", "arbitrary")),
        name="lru_scan",
    )(xc, xc, wf, wb, bf, bb, sp[0:1], sp[1:2], h0f.reshape(b, 1, w), h0b.reshape(b, 1, w))
    return hf, hb


def _ssd_direction(xbc, dtraw, avec_ref, dtb_ref, state_s, y_ref, lane0, rev):
    t = SSD_CHUNK
    ri = lax.broadcasted_iota(I32, (t, t), 0)
    ci = lax.broadcasted_iota(I32, (t, t), 1)
    causal = (ci >= ri) if rev else (ci <= ri)
    cum_m = jnp.where(causal, 1.0, 0.0).astype(F32)

    dt = _softplus(dtraw + dtb_ref[...])
    a_all = dt * avec_ref[...]
    acs = jnp.dot(cum_m, a_all, preferred_element_type=F32, precision=lax.Precision.HIGHEST)
    acs_t = acs.T
    edge = 0 if rev else t - 1

    xs = xbc[:, :SSD_WIDTH]
    hrow = lax.broadcasted_iota(I32, (t, t), 0)
    for g in range(SSD_GROUPS):
        bm = xbc[:, SSD_WIDTH + g * SSD_STATE:SSD_WIDTH + (g + 1) * SSD_STATE]
        cm = xbc[:, SSD_WIDTH + (SSD_GROUPS + g) * SSD_STATE:SSD_WIDTH + (SSD_GROUPS + g + 1) * SSD_STATE]
        bm_b = bm.astype(BF16)
        cm_b = cm.astype(BF16)
        gmat = lax.dot_general(cm_b, bm_b, (((1,), (1,)), ((), ())), preferred_element_type=F32)
        s_prev = state_s[g]
        off_all = lax.dot_general(cm_b, s_prev.astype(BF16), (((1,), (1,)), ((), ())),
                                  preferred_element_type=F32)
        xd_parts, tot = [], []
        for hl in range(2):
            h = g * 2 + hl
            ln = lane0 + h
            col = acs[:, ln:ln + 1]
            rowv = acs_t[ln:ln + 1, :]
            lmat = jnp.where(causal, jnp.exp(col - rowv), 0.0)
            xh = xs[:, h * SSD_HEAD_DIM:(h + 1) * SSD_HEAD_DIM] * dt[:, ln:ln + 1]
            ydiag = jnp.dot((gmat * lmat).astype(BF16), xh.astype(BF16), preferred_element_type=F32)
            yoff = off_all[:, hl * SSD_HEAD_DIM:(hl + 1) * SSD_HEAD_DIM] * jnp.exp(col)
            y_ref[0, :, h * SSD_HEAD_DIM:(h + 1) * SSD_HEAD_DIM] = ydiag + yoff
            total = acs[edge:edge + 1, ln:ln + 1]
            xd_parts.append(xh * jnp.exp(total - col))
            tot.append(jnp.exp(total))
        xd = jnp.concatenate(xd_parts, axis=1)
        xd_t = xd.T
        st = jnp.dot(xd_t.astype(BF16), bm_b, preferred_element_type=F32)
        fac = jnp.where(hrow < SSD_HEAD_DIM, tot[0], tot[1])
        state_s[g] = fac * s_prev + st


def _ssd_kernel(xf_ref, dtf_ref, xb_ref, dtb_ref, avec_ref, bias_ref, h0f_ref, h0b_ref,
                yf_ref, yb_ref, sf_ref, sb_ref, stf_s, stb_s):
    c = pl.program_id(1)

    @pl.when(c == 0)
    def _():
        stf_s[...] = h0f_ref[0]
        stb_s[...] = h0b_ref[0]

    _ssd_direction(xf_ref[0], dtf_ref[0], avec_ref, bias_ref, stf_s, yf_ref, 0, False)
    _ssd_direction(xb_ref[0], dtb_ref[0], avec_ref, bias_ref, stb_s, yb_ref, SSD_HEADS, True)
    sf_ref[0] = stf_s[...]
    sb_ref[0] = stb_s[...]


def _ssd_scan(xbc, dt, a_log, dt_bias, h0f, h0b):
    b, s, _ = xbc.shape
    t = SSD_CHUNK
    nc = s // t
    avec = jnp.zeros((1, LANES), F32).at[0, :2 * SSD_HEADS].set(-jnp.exp(a_log.astype(F32)).reshape(-1))
    bias = jnp.zeros((1, LANES), F32).at[0, :2 * SSD_HEADS].set(dt_bias.astype(F32).reshape(-1))
    st_shape = (b, SSD_GROUPS, 2 * SSD_HEAD_DIM, SSD_STATE)
    const = lambda bi, c: (0, 0)
    st_spec = pl.BlockSpec((1,) + st_shape[1:], lambda bi, c: (bi, 0, 0, 0))
    yf, yb, sf, sb = pl.pallas_call(
        _ssd_kernel,
        out_shape=[jax.ShapeDtypeStruct((b, s, SSD_WIDTH), F32)] * 2
        + [jax.ShapeDtypeStruct(st_shape, F32)] * 2,
        grid=(b, nc),
        in_specs=[pl.BlockSpec((1, t, SSD_CONV_DIM), lambda bi, c: (bi, c, 0)),
                  pl.BlockSpec((1, t, LANES), lambda bi, c: (bi, c, 0)),
                  pl.BlockSpec((1, t, SSD_CONV_DIM), lambda bi, c: (bi, nc - 1 - c, 0)),
                  pl.BlockSpec((1, t, LANES), lambda bi, c: (bi, nc - 1 - c, 0)),
                  pl.BlockSpec((1, LANES), const), pl.BlockSpec((1, LANES), const),
                  st_spec, st_spec],
        out_specs=[pl.BlockSpec((1, t, SSD_WIDTH), lambda bi, c: (bi, c, 0)),
                   pl.BlockSpec((1, t, SSD_WIDTH), lambda bi, c: (bi, nc - 1 - c, 0)),
                   st_spec, st_spec],
        scratch_shapes=[pltpu.VMEM(st_shape[1:], F32), pltpu.VMEM(st_shape[1:], F32)],
        compiler_params=_cparams(("arbitrary", "arbitrary")),
        name="ssd_scan",
    )(xbc, dt, xbc, dt, avec, bias, h0f, h0b)
    return yf, yb, sf, sb


def _qk_prep_kernel(x_ref, g_ref, cos_ref, sin_ref, seg_ref, o_ref, *, rope, scale):
    x = x_ref[0]
    sq = x * x
    hi = sq.astype(BF16)
    lo = (sq - hi.astype(F32)).astype(BF16)
    ssum = (jnp.dot(hi, seg_ref[...], preferred_element_type=F32)
            + jnp.dot(lo, seg_ref[...], preferred_element_type=F32))
    y = x * lax.rsqrt(ssum * (1.0 / DA_QK_DIM) + EPS) * g_ref[...]
    if rope:
        n = x.shape[1]
        lane = lax.broadcasted_iota(I32, x.shape, 1)
        first = (lane & (DA_QK_DIM - 1)) < DA_QK_DIM // 2
        partner = jnp.where(first, pltpu.roll(y, n - DA_QK_DIM // 2, 1), pltpu.roll(y, DA_QK_DIM // 2, 1))
        cosv = jnp.concatenate([cos_ref[...]] * (n // LANES), axis=1)
        sinv = jnp.concatenate([sin_ref[...]] * (n // LANES), axis=1)
        y = y * cosv + partner * sinv
    y = (y * scale).astype(BF16)
    for j in range(2 * DA_HEADS):
        o_ref[0, j] = y[:, j * DA_QK_DIM:(j + 1) * DA_QK_DIM]


def _rope_tables(s):
    pos = jnp.arange(s, dtype=I32)
    r = (pos // GRID_W).astype(F32)
    col = (pos % GRID_W).astype(F32)
    per_axis = DA_QK_DIM // 4
    inv = ROPE_BASE ** (-jnp.arange(per_axis, dtype=F32) / per_axis)
    ang = jnp.concatenate([r[:, None] * inv, col[:, None] * inv], axis=-1)
    cos, sin = jnp.cos(ang), jnp.sin(ang)
    cos64 = jnp.concatenate([cos, cos], axis=-1)
    sin64 = jnp.concatenate([-sin, sin], axis=-1)
    return jnp.concatenate([cos64, cos64], axis=-1), jnp.concatenate([sin64, sin64], axis=-1)


def _qk_prep(x, g, cos, sin, rope, scale):
    b, s, n = x.shape
    ts = min(512, s)
    gfull = jnp.tile(g.astype(F32), n // DA_QK_DIM).reshape(1, n)
    idx = jnp.arange(n) // DA_QK_DIM
    seg = (idx[:, None] == idx[None, :]).astype(BF16)
    return pl.pallas_call(
        functools.partial(_qk_prep_kernel, rope=rope, scale=scale),
        out_shape=jax.ShapeDtypeStruct((b, 2 * DA_HEADS, s, DA_QK_DIM), BF16),
        grid=(b, s // ts),
        in_specs=[pl.BlockSpec((1, ts, n), lambda bi, i: (bi, i, 0)),
                  pl.BlockSpec((1, n), lambda bi, i: (0, 0)),
                  pl.BlockSpec((ts, LANES), lambda bi, i: (i, 0)),
                  pl.BlockSpec((ts, LANES), lambda bi, i: (i, 0)),
                  pl.BlockSpec((n, n), lambda bi, i: (0, 0))],
        out_specs=pl.BlockSpec((1, 2 * DA_HEADS, ts, DA_QK_DIM), lambda bi, i: (bi, 0, i, 0)),
        compiler_params=_cparams(("arbitrary", "arbitrary")),
        name="qk_prep",
    )(x, gfull, cos, sin, seg)


def _attn_kernel(lam_ref, q_ref, k_ref, v_ref, g_ref, o_ref, *, out_scale):
    lam = lam_ref[0, 0]
    probs = []
    for sub in range(2):
        s = lax.dot_general(q_ref[0, sub], k_ref[0, sub], (((1,), (1,)), ((), ())),
                            preferred_element_type=F32)
        m = jnp.max(s, axis=-1, keepdims=True)
        p = jnp.exp(s - m)
        l = jnp.sum(p, axis=-1, keepdims=True)
        probs.append(p * (1.0 / l))
    w = (probs[0] - lam * probs[1]).astype(BF16)
    o = jnp.dot(w, v_ref[0], preferred_element_type=F32)
    ms = jnp.mean(o * o, axis=-1, keepdims=True)
    o_ref[0] = o * lax.rsqrt(ms + EPS) * g_ref[...] * out_scale


def _diff_attn(qp, kp, v, lam, g, lam_init):
    b, _, s, d = qp.shape
    sk = kp.shape[2]
    tq = min(ATT_TQ, s)
    return pl.pallas_call(
        functools.partial(_attn_kernel, out_scale=1.0 - lam_init),
        out_shape=jax.ShapeDtypeStruct((b, s, DA_WIDTH), F32),
        grid=(b, DA_HEADS, s // tq),
        in_specs=[pl.BlockSpec(memory_space=pltpu.SMEM),
                  pl.BlockSpec((1, 2, tq, d), lambda bi, h, i: (bi, h, i, 0)),
                  pl.BlockSpec((1, 2, sk, d), lambda bi, h, i: (bi, h, 0, 0)),
                  pl.BlockSpec((1, sk, DA_V_DIM), lambda bi, h, i: (bi, 0, h)),
                  pl.BlockSpec((1, DA_V_DIM), lambda bi, h, i: (0, 0))],
        out_specs=pl.BlockSpec((1, tq, DA_V_DIM), lambda bi, h, i: (bi, i, h)),
        compiler_params=_cparams(("arbitrary", "arbitrary", "arbitrary")),
        name="diff_attn",
    )(lam.reshape(1, 1).astype(F32), qp, kp, v, g.reshape(1, DA_V_DIM).astype(F32))


def _gelu_tanh(x):
    return 0.5 * x * (1.0 + jnp.tanh(math.sqrt(2.0 / math.pi) * (x + 0.044715 * x * x * x)))


def _proj_out_kernel(x_ref, lg_ref, hf_ref, hb_ref, yf_ref, yb_ref, xs_ref, z_ref, da_ref,
                     dvec_ref, ng_ref, gate_ref, w_ref, o_ref):
    lru = _gelu_tanh(lg_ref[0]) * (hf_ref[0] + hb_ref[0])
    y = yf_ref[0] + yb_ref[0] + dvec_ref[...] * xs_ref[0]
    y = y * _silu(z_ref[0])
    ssd = y * lax.rsqrt(jnp.mean(y * y, axis=-1, keepdims=True) + EPS) * ng_ref[...]
    o = jnp.dot(lru.astype(BF16), w_ref[0:LRU_WIDTH, :], preferred_element_type=F32)
    o = o + jnp.dot(ssd.astype(BF16), w_ref[LRU_WIDTH:LRU_WIDTH + SSD_WIDTH, :], preferred_element_type=F32)
    o = o + jnp.dot(da_ref[0].astype(BF16), w_ref[LRU_WIDTH + SSD_WIDTH:, :], preferred_element_type=F32)
    o_ref[0] = x_ref[0] + gate_ref[0] * o


def _proj_out(x, lg, hf, hb, yf, yb, xbc, z, da, dvec, ng, gate, w):
    b, s, d = x.shape
    tm = min(PROJ_TM, s)
    tok = lambda wd: pl.BlockSpec((1, tm, wd), lambda bi, i: (bi, i, 0))
    const2 = lambda bi, i: (0, 0)
    return pl.pallas_call(
        _proj_out_kernel,
        out_shape=jax.ShapeDtypeStruct((b, s, d), F32),
        grid=(b, s // tm),
        in_specs=[tok(d), tok(LRU_WIDTH), tok(LRU_WIDTH), tok(LRU_WIDTH), tok(SSD_WIDTH),
                  tok(SSD_WIDTH), tok(SSD_WIDTH), tok(SSD_WIDTH), tok(DA_WIDTH),
                  pl.BlockSpec((1, SSD_WIDTH), const2), pl.BlockSpec((1, SSD_WIDTH), const2),
                  pl.BlockSpec((1, 1, d), lambda bi, i: (bi, 0, 0)),
                  pl.BlockSpec(w.shape, const2)],
        out_specs=tok(d),
        compiler_params=_cparams(("arbitrary", "arbitrary")),
        name="proj_out",
    )(x, lg, hf, hb, yf, yb, xbc, z, da, dvec, ng, gate, w)


def _route_kernel(x_ref, g_ref, sh_ref, sc_ref, rw_ref, rb_ref, h_ref, ri_ref, gt_ref, cnt_ref, carry_s):
    i = pl.program_id(0)
    tm = x_ref.shape[0]

    @pl.when(i == 0)
    def _():
        carry_s[...] = jnp.zeros_like(carry_s)

    h = _norm_mod(x_ref[...], g_ref[...], sh_ref[0], sc_ref[0])
    h_ref[...] = h
    logits = jnp.dot(h, rw_ref[...], preferred_element_type=F32,
                     precision=lax.Precision.HIGHEST) + rb_ref[...]
    lane = lax.broadcasted_iota(I32, logits.shape, 1)
    vals, idxs = [], []
    cur = logits
    for _ in range(TOP_K):
        m = jnp.max(cur, axis=-1, keepdims=True)
        idx = jnp.min(jnp.where(cur >= m, lane, LANES), axis=-1, keepdims=True)
        vals.append(m)
        idxs.append(idx)
        cur = jnp.where(lane == idx, 2.0 * NEG_BIG, cur)
    es = [jnp.exp(v - vals[0]) for v in vals]
    inv = 1.0 / (es[0] + es[1] + es[2] + es[3])
    onehot = jnp.zeros(logits.shape, F32)
    for idx in idxs:
        onehot = onehot + jnp.where(lane == idx, 1.0, 0.0)
    ri = lax.broadcasted_iota(I32, (tm, tm), 0)
    ci = lax.broadcasted_iota(I32, (tm, tm), 1)
    strict = jnp.where(ci < ri, 1.0, 0.0).astype(BF16)
    before = jnp.dot(strict, onehot.astype(BF16), preferred_element_type=F32) + carry_s[...]
    ri_out = jnp.zeros(logits.shape, I32)
    gt_out = jnp.zeros(logits.shape, F32)
    for k in range(TOP_K):
        rank = jnp.sum(jnp.where(lane == idxs[k], before, 0.0), axis=-1, keepdims=True)
        ri_out = ri_out + jnp.where(lane == k, idxs[k], 0) + jnp.where(lane == TOP_K + k, rank.astype(I32), 0)
        gt_out = gt_out + jnp.where(lane == k, es[k] * inv, 0.0)
    ri_ref[...] = ri_out
    gt_ref[...] = gt_out
    carry_s[...] = carry_s[...] + jnp.sum(onehot, axis=0, keepdims=True)
    cnt_ref[...] = carry_s[...].astype(I32)


def _route(x, g, sh, sc, mod_index, router_w, router_b):
    t, d = x.shape
    tm = ROUTE_TM
    rw = jnp.zeros((d, LANES), F32).at[:, :N_EXPERTS].set(router_w.astype(F32))
    rb = jnp.full((1, LANES), NEG_BIG, F32).at[0, :N_EXPERTS].set(router_b.astype(F32))
    tile = lambda wd: pl.BlockSpec((tm, wd), lambda i: (i, 0))
    const = lambda i: (0, 0)
    return pl.pallas_call(
        _route_kernel,
        out_shape=[jax.ShapeDtypeStruct((t, d), F32), jax.ShapeDtypeStruct((t, LANES), I32),
                   jax.ShapeDtypeStruct((t, LANES), F32), jax.ShapeDtypeStruct((1, LANES), I32)],
        grid=(t // tm,),
        in_specs=[tile(d), pl.BlockSpec((1, d), const),
                  pl.BlockSpec((1, 1, d), lambda i: (mod_index(i), 0, 0)),
                  pl.BlockSpec((1, 1, d), lambda i: (mod_index(i), 0, 0)),
                  pl.BlockSpec((d, LANES), const), pl.BlockSpec((1, LANES), const)],
        out_specs=[tile(d), tile(LANES), tile(LANES), pl.BlockSpec((1, LANES), const)],
        scratch_shapes=[pltpu.VMEM((1, LANES), F32)],
        compiler_params=_cparams(("arbitrary",)),
        name="moe_route",
    )(x, g.reshape(1, d), sh, sc, rw, rb)


def _dispatch_kernel(dest_ref, h_ref, xs_in, xs_out, sem):
    del xs_in
    tm = h_ref.shape[0]

    def issue(r, carry):
        for k in range(TOP_K):
            d = dest_ref[0, 0, r * TOP_K + k]
            pltpu.make_async_copy(h_ref.at[pl.ds(r, 1), :], xs_out.at[pl.ds(d, 1), :], sem).start()
        return carry

    lax.fori_loop(0, tm, issue, 0)
    for _ in range(TOP_K):
        pltpu.make_async_copy(h_ref, xs_out.at[pl.ds(0, tm), :], sem).wait()


def _dispatch(h, dest3, n_slots):
    t, d = h.shape
    tm = ROUTE_TM
    xs0 = jnp.zeros((n_slots, d), F32)
    return pl.pallas_call(
        _dispatch_kernel,
        out_shape=jax.ShapeDtypeStruct((n_slots, d), F32),
        grid=(t // tm,),
        in_specs=[pl.BlockSpec((1, 1, tm * TOP_K), lambda i: (i, 0, 0), memory_space=pltpu.SMEM),
                  pl.BlockSpec((tm, d), lambda i: (i, 0)),
                  pl.BlockSpec(memory_space=pl.ANY)],
        out_specs=pl.BlockSpec(memory_space=pl.ANY),
        scratch_shapes=[pltpu.SemaphoreType.DMA(())],
        input_output_aliases={2: 0},
        compiler_params=_cparams(("arbitrary",)),
        name="moe_dispatch",
    )(dest3, h, xs0)


def _gmm_kernel(be_ref, xs_ref, wg_ref, wu_ref, bg_ref, bu_ref, wd_ref, bd_ref, y_ref):
    del be_ref
    x = xs_ref[...].astype(BF16)
    gate = jnp.dot(x, wg_ref[...], preferred_element_type=F32) + bg_ref[...]
    up = jnp.dot(x, wu_ref[...], preferred_element_type=F32) + bu_ref[...]
    gate = jnp.minimum(gate, SWIGLU_LIMIT)
    up = jnp.clip(up, -SWIGLU_LIMIT, SWIGLU_LIMIT)
    glu = gate * _sigmoid(SWIGLU_ALPHA * gate)
    act = ((up + 1.0) * glu).astype(BF16)
    y_ref[...] = jnp.dot(act, wd_ref[...], preferred_element_type=F32) + bd_ref[...]


def _gmm(xs, block_e, wg, wu, bg, bu, wd, bd):
    n_slots, d = xs.shape
    de = wg.shape[2]
    tm = MOE_TM
    nb = n_slots // tm
    ex = lambda shape: pl.BlockSpec((None,) + shape, lambda i, be: (be[i], 0, 0))
    grid_spec = pltpu.PrefetchScalarGridSpec(
        num_scalar_prefetch=1,
        grid=(nb,),
        in_specs=[pl.BlockSpec((tm, d), lambda i, be: (i, 0)),
                  ex((d, de)), ex((d, de)), ex((1, de)), ex((1, de)), ex((de, d)), ex((1, d))],
        out_specs=pl.BlockSpec((tm, d), lambda i, be: (i, 0)),
    )
    return pl.pallas_call(
        _gmm_kernel,
        out_shape=jax.ShapeDtypeStruct((n_slots, d), F32),
        grid_spec=grid_spec,
        compiler_params=_cparams(("arbitrary",)),
        name="moe_experts",
    )(block_e, xs, wg, wu, bg, bu, wd, bd)


def _combine_kernel(dest_ref, gt_ref, x_ref, g2_ref, y_hbm, o_ref, ybuf, sem):
    tm = x_ref.shape[0]

    def issue(r, carry):
        for k in range(TOP_K):
            d = dest_ref[0, 0, r * TOP_K + k]
            pltpu.make_async_copy(y_hbm.at[pl.ds(d, 1), :], ybuf.at[k, pl.ds(r, 1), :], sem).start()
        return carry

    lax.fori_loop(0, tm, issue, 0)
    for k in range(TOP_K):
        pltpu.make_async_copy(y_hbm.at[pl.ds(0, tm), :], ybuf.at[k], sem).wait()
    gt = gt_ref[...]
    f = gt[:, 0:1] * ybuf[0]
    for k in range(1, TOP_K):
        f = f + gt[:, k:k + 1] * ybuf[k]
    o_ref[...] = x_ref[...] + g2_ref[0] * f


def _combine(y, dest3, gates, x, g2, mod_index):
    t, d = x.shape
    tm = ROUTE_TM
    return pl.pallas_call(
        _combine_kernel,
        out_shape=jax.ShapeDtypeStruct((t, d), F32),
        grid=(t // tm,),
        in_specs=[pl.BlockSpec((1, 1, tm * TOP_K), lambda i: (i, 0, 0), memory_space=pltpu.SMEM),
                  pl.BlockSpec((tm, LANES), lambda i: (i, 0)),
                  pl.BlockSpec((tm, d), lambda i: (i, 0)),
                  pl.BlockSpec((1, 1, d), lambda i: (mod_index(i), 0, 0)),
                  pl.BlockSpec(memory_space=pl.ANY)],
        out_specs=pl.BlockSpec((tm, d), lambda i: (i, 0)),
        scratch_shapes=[pltpu.VMEM((TOP_K, tm, d), F32), pltpu.SemaphoreType.DMA(())],
        compiler_params=_cparams(("arbitrary",)),
        name="moe_combine",
    )(dest3, gates, x, g2, y)


def _moe(x_tok, g, sh, sc, g2, mod_index, router_w, router_b, w_gu, b_gu, w_down, b_down):
    t, d = x_tok.shape
    h, route_i, gates, counts = _route(x_tok, g, sh, sc, mod_index, router_w, router_b)
    top_e = route_i[:, :TOP_K]
    rank = route_i[:, TOP_K:2 * TOP_K]
    cnt = counts[0, :N_EXPERTS]
    padded = (cnt + MOE_TM - 1) // MOE_TM * MOE_TM
    pend = jnp.cumsum(padded)
    pstart = pend - padded
    dest = pstart[top_e] + rank
    nb = (t * TOP_K + N_EXPERTS * (MOE_TM - 1)) // MOE_TM
    block_e = jnp.minimum(jnp.searchsorted(pend, jnp.arange(nb, dtype=I32) * MOE_TM, side="right"),
                          N_EXPERTS - 1).astype(I32)
    dest3 = dest.astype(I32).reshape(t // ROUTE_TM, 1, ROUTE_TM * TOP_K)
    xs = _dispatch(h, dest3, nb * MOE_TM)
    wg = w_gu[:, :, 0::2].astype(BF16)
    wu = w_gu[:, :, 1::2].astype(BF16)
    bg = b_gu[:, None, 0::2].astype(F32)
    bu = b_gu[:, None, 1::2].astype(F32)
    y = _gmm(xs, block_e, wg, wu, bg, bu, w_down.astype(BF16), b_down[:, None, :].astype(F32))
    return _combine(y, dest3, gates, x_tok, g2, mod_index)


def kernel(x, c, ctx, c_ctx, w_mod, b_mod, norm1_g, norm2_g, w_in, w_out, lru_conv_w, lru_conv_b, lru_wa, lru_ba, lru_wx, lru_bx, lru_lam, ssd_conv_w, ssd_conv_b, ssd_a_log, ssd_dt_bias, ssd_d, ssd_norm_g, da_q_norm, da_k_norm, da_lam_q, da_lam_k, da_subln_g, router_w, router_b, exp_w_gu, exp_b_gu, exp_w_down, exp_b_down):
    b, s, d = x.shape
    n_ctx = ctx.shape[1]
    depth = w_mod.shape[0]
    cos, sin = _rope_tables(s)
    x_lat, x_ctx = x, ctx
    mod_rows = jnp.zeros((2 * SUBLANES, d), F32).at[:b].set(c).at[b].set(c_ctx)

    for l in range(depth):
        need_ctx = l < depth - 1
        lam_init = 0.8 - 0.6 * math.exp(-0.3 * l)
        mod = _modulation(mod_rows, w_mod[l], b_mod[l])
        m_lat = mod[:b].reshape(b, 1, 6, d)
        m_ctx = jnp.broadcast_to(mod[b].reshape(1, 1, 6, d), (b, 1, 6, d))
        part = lambda m, j: m[:, :, j, :]

        w_in_l = _prep_w_in(w_in[l])
        p_lat = _proj_in(x_lat, norm1_g[l], part(m_lat, 0), part(m_lat, 1), w_in_l)
        p_ctx = _proj_in(x_ctx, norm1_g[l], part(m_ctx, 0), part(m_ctx, 1), w_in_l)

        zl = jnp.zeros((b, LRU_WIDTH), F32)
        lru_args = (lru_wa[l], lru_ba[l], lru_wx[l], lru_bx[l], lru_lam[l])
        xc_ctx = _dwconv(p_ctx["lx"], lru_conv_w[l], lru_conv_b[l], act=False)
        hf_c, hb_c = _lru_scan(xc_ctx, *lru_args, zl, zl)
        xc_lat = _dwconv(p_lat["lx"], lru_conv_w[l], lru_conv_b[l], act=False)
        hf_l, hb_l = _lru_scan(xc_lat, *lru_args, hf_c[:, -1], hb_c[:, 0])

        zs = jnp.zeros((b, SSD_GROUPS, 2 * SSD_HEAD_DIM, SSD_STATE), F32)
        xbc_ctx = _dwconv(p_ctx["xbc"], ssd_conv_w[l], ssd_conv_b[l], act=True)
        yf_c, yb_c, sf_c, sb_c = _ssd_scan(xbc_ctx, p_ctx["dt"], ssd_a_log[l], ssd_dt_bias[l], zs, zs)
        xbc_lat = _dwconv(p_lat["xbc"], ssd_conv_w[l], ssd_conv_b[l], act=True)
        yf_l, yb_l, _, _ = _ssd_scan(xbc_lat, p_lat["dt"], ssd_a_log[l], ssd_dt_bias[l], sf_c, sb_c)

        lq = da_lam_q[l].astype(F32)
        lk = da_lam_k[l].astype(F32)
        lam = jnp.exp(jnp.sum(lq[0] * lk[0])) - jnp.exp(jnp.sum(lq[1] * lk[1])) + lam_init
        kc = _qk_prep(p_ctx["k"], da_k_norm[l], cos, sin, rope=False, scale=1.0)
        kl = _qk_prep(p_lat["k"], da_k_norm[l], cos, sin, rope=True, scale=1.0)
        ql = _qk_prep(p_lat["q"], da_q_norm[l], cos, sin, rope=True, scale=DA_SCALE)
        k_all = jnp.concatenate([kc, kl], axis=2)
        v_all = jnp.concatenate([p_ctx["v"], p_lat["v"]], axis=1)
        da_l = _diff_attn(ql, k_all, v_all, lam, da_subln_g[l], lam_init)

        dvec = jnp.repeat(ssd_d[l].astype(F32), SSD_HEAD_DIM).reshape(1, SSD_WIDTH)
        ng = ssd_norm_g[l].astype(F32).reshape(1, SSD_WIDTH)
        w_out_l = w_out[l].astype(BF16)
        if need_ctx:
            qc = _qk_prep(p_ctx["q"], da_q_norm[l], cos, sin, rope=False, scale=DA_SCALE)
            da_c = _diff_attn(qc, kc, p_ctx["v"], lam, da_subln_g[l], lam_init)
            x_ctx = _proj_out(x_ctx, p_ctx["lg"], hf_c, hb_c, yf_c, yb_c, xbc_ctx, p_ctx["z"], da_c,
                              dvec, ng, part(m_ctx, 2), w_out_l)
        x_lat = _proj_out(x_lat, p_lat["lg"], hf_l, hb_l, yf_l, yb_l, xbc_lat, p_lat["z"], da_l,
                          dvec, ng, part(m_lat, 2), w_out_l)

        moe_w = (router_w[l], router_b[l], exp_w_gu[l], exp_b_gu[l], exp_w_down[l], exp_b_down[l])
        if need_ctx:
            tiles_c = n_ctx // ROUTE_TM
            tiles_b = (n_ctx + s) // ROUTE_TM
            tok = jnp.concatenate([x_ctx, x_lat], axis=1).reshape(-1, d)
            stack = lambda j: jnp.concatenate([part(m_ctx, j), part(m_lat, j)], axis=1).reshape(2 * b, 1, d)
            mod_index = lambda i: 2 * (i // tiles_b) + jnp.minimum((i % tiles_b) // tiles_c, 1)
            out = _moe(tok, norm2_g[l], stack(3), stack(4), stack(5), mod_index, *moe_w)
            out = out.reshape(b, n_ctx + s, d)
            x_ctx, x_lat = out[:, :n_ctx], out[:, n_ctx:]
        else:
            tiles_b = s // ROUTE_TM
            mod_index = lambda i: i // tiles_b
            out = _moe(x_lat.reshape(-1, d), norm2_g[l], part(m_lat, 3), part(m_lat, 4), part(m_lat, 5),
                       mod_index, *moe_w)
            x_lat = out.reshape(b, s, d)
    return x_lat
```

```python
import functools
import math

import jax
import jax.numpy as jnp
from jax import lax
from jax.experimental import pallas as pl
from jax.experimental.pallas import tpu as pltpu

F32 = jnp.float32
BF16 = jnp.bfloat16
I32 = jnp.int32

GRID_W = 64
EPS = 1e-6
CONV_K = 4
LRU_HEADS = 4
LRU_HEAD_DIM = 64
LRU_WIDTH = 256
LRU_C = 8.0
SSD_HEADS = 4
SSD_HEAD_DIM = 64
SSD_WIDTH = 256
SSD_GROUPS = 2
SSD_STATE = 128
SSD_CHUNK = 128
SSD_CONV_DIM = 768
DA_HEADS = 4
DA_QK_DIM = 64
DA_V_DIM = 128
DA_QK_WIDTH = 512
DA_WIDTH = 512
DA_SCALE = DA_QK_DIM ** -0.5
ROPE_BASE = 10000.0
N_EXPERTS = 32
TOP_K = 4
SWIGLU_ALPHA = 1.702
SWIGLU_LIMIT = 7.0

LANES = 128
SUBLANES = 8
VMEM_LIMIT = 56 * 1024 * 1024

PROJ_TM = 512
ATT_TQ = 128
LRU_TL = 512
ROUTE_TM = 256
MOE_TM = 256
NEG_BIG = -1e30


def _cparams(sem):
    return pltpu.CompilerParams(dimension_semantics=sem, vmem_limit_bytes=VMEM_LIMIT)


def _sigmoid(x):
    return 1.0 / (1.0 + jnp.exp(-x))


def _silu(x):
    return x * _sigmoid(x)


def _softplus(x):
    return jnp.maximum(x, 0.0) + jnp.log(1.0 + jnp.exp(-jnp.abs(x)))


def _mod_kernel(c_ref, w_ref, b_ref, o_ref):
    a = _silu(c_ref[...])
    o_ref[...] = jnp.dot(a, w_ref[...], preferred_element_type=F32,
                         precision=lax.Precision.HIGHEST) + b_ref[...]


def _modulation(rows, w, b):
    m, d = rows.shape
    n = w.shape[1]
    tn = 512
    return pl.pallas_call(
        _mod_kernel,
        out_shape=jax.ShapeDtypeStruct((m, n), F32),
        grid=(n // tn,),
        in_specs=[pl.BlockSpec((m, d), lambda j: (0, 0)),
                  pl.BlockSpec((d, tn), lambda j: (0, j)),
                  pl.BlockSpec((1, tn), lambda j: (0, j))],
        out_specs=pl.BlockSpec((m, tn), lambda j: (0, j)),
        compiler_params=_cparams(("arbitrary",)),
        name="modulation",
    )(rows, w, b.reshape(1, n))


def _norm_mod(x, g, sh, sc):
    ms = jnp.mean(x * x, axis=-1, keepdims=True)
    y = x * lax.rsqrt(ms + EPS) * g
    return y * (1.0 + sc) + sh


def _proj_in_kernel(x_ref, g_ref, sh_ref, sc_ref, w_ref, *o_refs, splits):
    h = _norm_mod(x_ref[0], g_ref[...], sh_ref[0], sc_ref[0]).astype(BF16)
    for o_ref, (lo, hi) in zip(o_refs, splits):
        o_ref[0] = jnp.dot(h, w_ref[:, lo:hi], preferred_element_type=F32).astype(o_ref.dtype)


_IN_GROUPS = (("lg", 256, F32), ("lx", 256, F32), ("z", 256, F32), ("xbc", 768, F32),
              ("dt", 128, F32), ("q", 512, F32), ("k", 512, F32), ("v", 512, BF16))


def _prep_w_in(w_in):
    d = w_in.shape[0]
    lo = 2 * LRU_WIDTH + SSD_WIDTH + SSD_CONV_DIM
    hi = lo + 2 * SSD_HEADS
    dt = jnp.concatenate([w_in[:, lo:hi], jnp.zeros((d, LANES - (hi - lo)), w_in.dtype)], axis=1)
    w = jnp.concatenate([w_in[:, :lo], dt, w_in[:, hi:]], axis=1)
    return w.astype(BF16)


def _proj_in(x, g, sh, sc, w):
    b, s, d = x.shape
    tm = min(PROJ_TM, s)
    splits, off = [], 0
    for _, wd, _ in _IN_GROUPS:
        splits.append((off, off + wd))
        off += wd
    out_shape = [jax.ShapeDtypeStruct((b, s, wd), dt) for _, wd, dt in _IN_GROUPS]
    out_specs = [pl.BlockSpec((1, tm, wd), lambda bi, i: (bi, i, 0)) for _, wd, _ in _IN_GROUPS]
    outs = pl.pallas_call(
        functools.partial(_proj_in_kernel, splits=tuple(splits)),
        out_shape=out_shape,
        grid=(b, s // tm),
        in_specs=[pl.BlockSpec((1, tm, d), lambda bi, i: (bi, i, 0)),
                  pl.BlockSpec((1, d), lambda bi, i: (0, 0)),
                  pl.BlockSpec((1, 1, d), lambda bi, i: (bi, 0, 0)),
                  pl.BlockSpec((1, 1, d), lambda bi, i: (bi, 0, 0)),
                  pl.BlockSpec((d, off), lambda bi, i: (0, 0))],
        out_specs=out_specs,
        compiler_params=_cparams(("arbitrary", "arbitrary")),
        name="proj_in",
    )(x, g.reshape(1, d), sh, sc, w)
    return {name: o for (name, _, _), o in zip(_IN_GROUPS, outs)}


def _dwconv_kernel(x_ref, w_ref, b_ref, o_ref, *, act):
    x = x_ref[0]
    s = x.shape[0]
    row = lax.broadcasted_iota(I32, x.shape, 0)
    acc = x * w_ref[2:3, :] + b_ref[...]
    acc = acc + jnp.where(row >= 2, pltpu.roll(x, 2, 0), 0.0) * w_ref[0:1, :]
    acc = acc + jnp.where(row >= 1, pltpu.roll(x, 1, 0), 0.0) * w_ref[1:2, :]
    acc = acc + jnp.where(row < s - 1, pltpu.roll(x, s - 1, 0), 0.0) * w_ref[3:4, :]
    if act:
        acc = _silu(acc)
    o_ref[0] = acc


def _dwconv(x, w, bias, act):
    b, s, c = x.shape
    tc = 256
    return pl.pallas_call(
        functools.partial(_dwconv_kernel, act=act),
        out_shape=jax.ShapeDtypeStruct((b, s, c), F32),
        grid=(b, c // tc),
        in_specs=[pl.BlockSpec((1, s, tc), lambda bi, j: (bi, 0, j)),
                  pl.BlockSpec((CONV_K, tc), lambda bi, j: (0, j)),
                  pl.BlockSpec((1, tc), lambda bi, j: (0, j))],
        out_specs=pl.BlockSpec((1, s, tc), lambda bi, j: (bi, 0, j)),
        compiler_params=_cparams(("arbitrary", "arbitrary")),
        name="dwconv",
    )(x, w, bias.reshape(1, c))


def _lru_coeffs(x, w_ref, b_ref, sp_ref, a_s, u_s):
    g = jnp.dot(x.astype(BF16), w_ref[...], preferred_element_type=F32) + b_ref[...]
    r = _sigmoid(g[:, :LRU_WIDTH])
    i = _sigmoid(g[:, LRU_WIDTH:])
    log_a = -LRU_C * r * sp_ref[...]
    a_s[...] = jnp.exp(log_a)
    u_s[...] = jnp.sqrt(1.0 - jnp.exp(2.0 * log_a)) * (i * x)


def _lru_kernel(xf_ref, xb_ref, wf_ref, wb_ref, bf_ref, bb_ref, spf_ref, spb_ref,
                h0f_ref, h0b_ref, hf_ref, hb_ref, a_s, u_s, cf_s, cb_s):
    c = pl.program_id(1)
    tl = xf_ref.shape[1]
    ngroups = tl // SUBLANES
    shape = (SUBLANES, LRU_WIDTH)
    row = lax.broadcasted_iota(I32, shape, 0)

    @pl.when(c == 0)
    def _():
        cf_s[...] = jnp.broadcast_to(h0f_ref[0], shape)
        cb_s[...] = jnp.broadcast_to(h0b_ref[0], shape)

    _lru_coeffs(xf_ref[0], wf_ref, bf_ref, spf_ref, a_s, u_s)

    def fwd(gi, carry):
        sl = pl.ds(pl.multiple_of(gi * SUBLANES, SUBLANES), SUBLANES)
        a = a_s[sl, :]
        u = u_s[sl, :]
        for sh in (1, 2, 4):
            keep = row >= sh
            a_sh = jnp.where(keep, pltpu.roll(a, sh, 0), 1.0)
            u_sh = jnp.where(keep, pltpu.roll(u, sh, 0), 0.0)
            u = a * u_sh + u
            a = a * a_sh
        h = a * carry + u
        hf_ref[0, sl, :] = h
        return jnp.broadcast_to(h[SUBLANES - 1:SUBLANES, :], shape)

    cf_s[...] = lax.fori_loop(0, ngroups, fwd, cf_s[...])

    _lru_coeffs(xb_ref[0], wb_ref, bb_ref, spb_ref, a_s, u_s)

    def bwd(gj, carry):
        gi = ngroups - 1 - gj
        sl = pl.ds(pl.multiple_of(gi * SUBLANES, SUBLANES), SUBLANES)
        a = a_s[sl, :]
        u = u_s[sl, :]
        for sh in (1, 2, 4):
            keep = row < SUBLANES - sh
            a_sh = jnp.where(keep, pltpu.roll(a, SUBLANES - sh, 0), 1.0)
            u_sh = jnp.where(keep, pltpu.roll(u, SUBLANES - sh, 0), 0.0)
            u = a * u_sh + u
            a = a * a_sh
        h = a * carry + u
        hb_ref[0, sl, :] = h
        return jnp.broadcast_to(h[0:1, :], shape)

    cb_s[...] = lax.fori_loop(0, ngroups, bwd, cb_s[...])


def _lru_gate_weights(wa, wx):
    def dense(wh):
        m = jnp.zeros((LRU_WIDTH, LRU_WIDTH), wh.dtype)
        for h in range(LRU_HEADS):
            lo = h * LRU_HEAD_DIM
            m = m.at[lo:lo + LRU_HEAD_DIM, lo:lo + LRU_HEAD_DIM].set(wh[h])
        return m
    return jnp.concatenate([dense(wa), dense(wx)], axis=1).astype(BF16)


def _lru_scan(xc, wa, ba, wx, bx, lam, h0f, h0b):
    b, s, w = xc.shape
    tl = min(LRU_TL, s)
    nc = s // tl
    wf = _lru_gate_weights(wa[0], wx[0])
    wb = _lru_gate_weights(wa[1], wx[1])
    bf = jnp.concatenate([ba[0], bx[0]]).reshape(1, 2 * w)
    bb = jnp.concatenate([ba[1], bx[1]]).reshape(1, 2 * w)
    sp = jax.nn.softplus(-lam.astype(F32))
    const = lambda bi, c: (0, 0)
    hf, hb = pl.pallas_call(
        _lru_kernel,
        out_shape=[jax.ShapeDtypeStruct((b, s, w), F32)] * 2,
        grid=(b, nc),
        in_specs=[pl.BlockSpec((1, tl, w), lambda bi, c: (bi, c, 0)),
                  pl.BlockSpec((1, tl, w), lambda bi, c: (bi, nc - 1 - c, 0)),
                  pl.BlockSpec((w, 2 * w), const), pl.BlockSpec((w, 2 * w), const),
                  pl.BlockSpec((1, 2 * w), const), pl.BlockSpec((1, 2 * w), const),
                  pl.BlockSpec((1, w), const), pl.BlockSpec((1, w), const),
                  pl.BlockSpec((1, 1, w), lambda bi, c: (bi, 0, 0)),
                  pl.BlockSpec((1, 1, w), lambda bi, c: (bi, 0, 0))],
        out_specs=[pl.BlockSpec((1, tl, w), lambda bi, c: (bi, c, 0)),
                   pl.BlockSpec((1, tl, w), lambda bi, c: (bi, nc - 1 - c, 0))],
        scratch_shapes=[pltpu.VMEM((tl, w), F32), pltpu.VMEM((tl, w), F32),
                        pltpu.VMEM((SUBLANES, w), F32), pltpu.VMEM((SUBLANES, w), F32)],
        compiler_params=_cparams(("arbitrary", "arbitrary")),
        name="lru_scan",
    )(xc, xc, wf, wb, bf, bb, sp[0:1], sp[1:2], h0f.reshape(b, 1, w), h0b.reshape(b, 1, w))
    return hf, hb


def _ssd_direction(xbc, dtraw, avec_ref, dtb_ref, state_s, y_ref, lane0, rev):
    t = SSD_CHUNK
    ri = lax.broadcasted_iota(I32, (t, t), 0)
    ci = lax.broadcasted_iota(I32, (t, t), 1)
    causal = (ci >= ri) if rev else (ci <= ri)
    cum_m = jnp.where(causal, 1.0, 0.0).astype(F32)

    dt = _softplus(dtraw + dtb_ref[...])
    a_all = dt * avec_ref[...]
    acs = jnp.dot(cum_m, a_all, preferred_element_type=F32, precision=lax.Precision.HIGHEST)
    acs_t = acs.T
    edge = 0 if rev else t - 1

    xs = xbc[:, :SSD_WIDTH]
    hrow = lax.broadcasted_iota(I32, (t, t), 0)
    for g in range(SSD_GROUPS):
        bm = xbc[:, SSD_WIDTH + g * SSD_STATE:SSD_WIDTH + (g + 1) * SSD_STATE]
        cm = xbc[:, SSD_WIDTH + (SSD_GROUPS + g) * SSD_STATE:SSD_WIDTH + (SSD_GROUPS + g + 1) * SSD_STATE]
        bm_b = bm.astype(BF16)
        cm_b = cm.astype(BF16)
        gmat = lax.dot_general(cm_b, bm_b, (((1,), (1,)), ((), ())), preferred_element_type=F32)
        s_prev = state_s[g]
        off_all = lax.dot_general(cm_b, s_prev.astype(BF16), (((1,), (1,)), ((), ())),
                                  preferred_element_type=F32)
        xd_parts, tot = [], []
        for hl in range(2):
            h = g * 2 + hl
            ln = lane0 + h
            col = acs[:, ln:ln + 1]
            rowv = acs_t[ln:ln + 1, :]
            lmat = jnp.where(causal, jnp.exp(col - rowv), 0.0)
            xh = xs[:, h * SSD_HEAD_DIM:(h + 1) * SSD_HEAD_DIM] * dt[:, ln:ln + 1]
            ydiag = jnp.dot((gmat * lmat).astype(BF16), xh.astype(BF16), preferred_element_type=F32)
            yoff = off_all[:, hl * SSD_HEAD_DIM:(hl + 1) * SSD_HEAD_DIM] * jnp.exp(col)
            y_ref[0, :, h * SSD_HEAD_DIM:(h + 1) * SSD_HEAD_DIM] = ydiag + yoff
            total = acs[edge:edge + 1, ln:ln + 1]
            xd_parts.append(xh * jnp.exp(total - col))
            tot.append(jnp.exp(total))
        xd = jnp.concatenate(xd_parts, axis=1)
        xd_t = xd.T
        st = jnp.dot(xd_t.astype(BF16), bm_b, preferred_element_type=F32)
        fac = jnp.where(hrow < SSD_HEAD_DIM, tot[0], tot[1])
        state_s[g] = fac * s_prev + st


def _ssd_kernel(xf_ref, dtf_ref, xb_ref, dtb_ref, avec_ref, bias_ref, h0f_ref, h0b_ref,
                yf_ref, yb_ref, sf_ref, sb_ref, stf_s, stb_s):
    c = pl.program_id(1)

    @pl.when(c == 0)
    def _():
        stf_s[...] = h0f_ref[0]
        stb_s[...] = h0b_ref[0]

    _ssd_direction(xf_ref[0], dtf_ref[0], avec_ref, bias_ref, stf_s, yf_ref, 0, False)
    _ssd_direction(xb_ref[0], dtb_ref[0], avec_ref, bias_ref, stb_s, yb_ref, SSD_HEADS, True)
    sf_ref[0] = stf_s[...]
    sb_ref[0] = stb_s[...]


def _ssd_scan(xbc, dt, a_log, dt_bias, h0f, h0b):
    b, s, _ = xbc.shape
    t = SSD_CHUNK
    nc = s // t
    avec = jnp.zeros((1, LANES), F32).at[0, :2 * SSD_HEADS].set(-jnp.exp(a_log.astype(F32)).reshape(-1))
    bias = jnp.zeros((1, LANES), F32).at[0, :2 * SSD_HEADS].set(dt_bias.astype(F32).reshape(-1))
    st_shape = (b, SSD_GROUPS, 2 * SSD_HEAD_DIM, SSD_STATE)
    const = lambda bi, c: (0, 0)
    st_spec = pl.BlockSpec((1,) + st_shape[1:], lambda bi, c: (bi, 0, 0, 0))
    yf, yb, sf, sb = pl.pallas_call(
        _ssd_kernel,
        out_shape=[jax.ShapeDtypeStruct((b, s, SSD_WIDTH), F32)] * 2
        + [jax.ShapeDtypeStruct(st_shape, F32)] * 2,
        grid=(b, nc),
        in_specs=[pl.BlockSpec((1, t, SSD_CONV_DIM), lambda bi, c: (bi, c, 0)),
                  pl.BlockSpec((1, t, LANES), lambda bi, c: (bi, c, 0)),
                  pl.BlockSpec((1, t, SSD_CONV_DIM), lambda bi, c: (bi, nc - 1 - c, 0)),
                  pl.BlockSpec((1, t, LANES), lambda bi, c: (bi, nc - 1 - c, 0)),
                  pl.BlockSpec((1, LANES), const), pl.BlockSpec((1, LANES), const),
                  st_spec, st_spec],
        out_specs=[pl.BlockSpec((1, t, SSD_WIDTH), lambda bi, c: (bi, c, 0)),
                   pl.BlockSpec((1, t, SSD_WIDTH), lambda bi, c: (bi, nc - 1 - c, 0)),
                   st_spec, st_spec],
        scratch_shapes=[pltpu.VMEM(st_shape[1:], F32), pltpu.VMEM(st_shape[1:], F32)],
        compiler_params=_cparams(("arbitrary", "arbitrary")),
        name="ssd_scan",
    )(xbc, dt, xbc, dt, avec, bias, h0f, h0b)
    return yf, yb, sf, sb


def _qk_prep_kernel(x_ref, g_ref, cos_ref, sin_ref, seg_ref, o_ref, *, rope, scale):
    x = x_ref[0]
    sq = x * x
    hi = sq.astype(BF16)
    lo = (sq - hi.astype(F32)).astype(BF16)
    ssum = (jnp.dot(hi, seg_ref[...], preferred_element_type=F32)
            + jnp.dot(lo, seg_ref[...], preferred_element_type=F32))
    y = x * lax.rsqrt(ssum * (1.0 / DA_QK_DIM) + EPS) * g_ref[...]
    if rope:
        n = x.shape[1]
        lane = lax.broadcasted_iota(I32, x.shape, 1)
        first = (lane & (DA_QK_DIM - 1)) < DA_QK_DIM // 2
        partner = jnp.where(first, pltpu.roll(y, n - DA_QK_DIM // 2, 1), pltpu.roll(y, DA_QK_DIM // 2, 1))
        cosv = jnp.concatenate([cos_ref[...]] * (n // LANES), axis=1)
        sinv = jnp.concatenate([sin_ref[...]] * (n // LANES), axis=1)
        y = y * cosv + partner * sinv
    y = (y * scale).astype(BF16)
    for j in range(2 * DA_HEADS):
        o_ref[0, j] = y[:, j * DA_QK_DIM:(j + 1) * DA_QK_DIM]


def _rope_tables(s):
    pos = jnp.arange(s, dtype=I32)
    r = (pos // GRID_W).astype(F32)
    col = (pos % GRID_W).astype(F32)
    per_axis = DA_QK_DIM // 4
    inv = ROPE_BASE ** (-jnp.arange(per_axis, dtype=F32) / per_axis)
    ang = jnp.concatenate([r[:, None] * inv, col[:, None] * inv], axis=-1)
    cos, sin = jnp.cos(ang), jnp.sin(ang)
    cos64 = jnp.concatenate([cos, cos], axis=-1)
    sin64 = jnp.concatenate([-sin, sin], axis=-1)
    return jnp.concatenate([cos64, cos64], axis=-1), jnp.concatenate([sin64, sin64], axis=-1)


def _qk_prep(x, g, cos, sin, rope, scale):
    b, s, n = x.shape
    ts = min(512, s)
    gfull = jnp.tile(g.astype(F32), n // DA_QK_DIM).reshape(1, n)
    idx = jnp.arange(n) // DA_QK_DIM
    seg = (idx[:, None] == idx[None, :]).astype(BF16)
    return pl.pallas_call(
        functools.partial(_qk_prep_kernel, rope=rope, scale=scale),
        out_shape=jax.ShapeDtypeStruct((b, 2 * DA_HEADS, s, DA_QK_DIM), BF16),
        grid=(b, s // ts),
        in_specs=[pl.BlockSpec((1, ts, n), lambda bi, i: (bi, i, 0)),
                  pl.BlockSpec((1, n), lambda bi, i: (0, 0)),
                  pl.BlockSpec((ts, LANES), lambda bi, i: (i, 0)),
                  pl.BlockSpec((ts, LANES), lambda bi, i: (i, 0)),
                  pl.BlockSpec((n, n), lambda bi, i: (0, 0))],
        out_specs=pl.BlockSpec((1, 2 * DA_HEADS, ts, DA_QK_DIM), lambda bi, i: (bi, 0, i, 0)),
        compiler_params=_cparams(("arbitrary", "arbitrary")),
        name="qk_prep",
    )(x, gfull, cos, sin, seg)


def _attn_kernel(lam_ref, q_ref, k_ref, v_ref, g_ref, o_ref, *, out_scale):
    lam = lam_ref[0, 0]
    probs = []
    for sub in range(2):
        s = lax.dot_general(q_ref[0, sub], k_ref[0, sub], (((1,), (1,)), ((), ())),
                            preferred_element_type=F32)
        m = jnp.max(s, axis=-1, keepdims=True)
        p = jnp.exp(s - m)
        l = jnp.sum(p, axis=-1, keepdims=True)
        probs.append(p * (1.0 / l))
    w = (probs[0] - lam * probs[1]).astype(BF16)
    o = jnp.dot(w, v_ref[0], preferred_element_type=F32)
    ms = jnp.mean(o * o, axis=-1, keepdims=True)
    o_ref[0] = o * lax.rsqrt(ms + EPS) * g_ref[...] * out_scale


def _diff_attn(qp, kp, v, lam, g, lam_init):
    b, _, s, d = qp.shape
    sk = kp.shape[2]
    tq = min(ATT_TQ, s)
    return pl.pallas_call(
        functools.partial(_attn_kernel, out_scale=1.0 - lam_init),
        out_shape=jax.ShapeDtypeStruct((b, s, DA_WIDTH), F32),
        grid=(b, DA_HEADS, s // tq),
        in_specs=[pl.BlockSpec(memory_space=pltpu.SMEM),
                  pl.BlockSpec((1, 2, tq, d), lambda bi, h, i: (bi, h, i, 0)),
                  pl.BlockSpec((1, 2, sk, d), lambda bi, h, i: (bi, h, 0, 0)),
                  pl.BlockSpec((1, sk, DA_V_DIM), lambda bi, h, i: (bi, 0, h)),
                  pl.BlockSpec((1, DA_V_DIM), lambda bi, h, i: (0, 0))],
        out_specs=pl.BlockSpec((1, tq, DA_V_DIM), lambda bi, h, i: (bi, i, h)),
        compiler_params=_cparams(("arbitrary", "arbitrary", "arbitrary")),
        name="diff_attn",
    )(lam.reshape(1, 1).astype(F32), qp, kp, v, g.reshape(1, DA_V_DIM).astype(F32))


def _gelu_tanh(x):
    return 0.5 * x * (1.0 + jnp.tanh(math.sqrt(2.0 / math.pi) * (x + 0.044715 * x * x * x)))


def _proj_out_kernel(x_ref, lg_ref, hf_ref, hb_ref, yf_ref, yb_ref, xs_ref, z_ref, da_ref,
                     dvec_ref, ng_ref, gate_ref, w_ref, o_ref):
    lru = _gelu_tanh(lg_ref[0]) * (hf_ref[0] + hb_ref[0])
    y = yf_ref[0] + yb_ref[0] + dvec_ref[...] * xs_ref[0]
    y = y * _silu(z_ref[0])
    ssd = y * lax.rsqrt(jnp.mean(y * y, axis=-1, keepdims=True) + EPS) * ng_ref[...]
    o = jnp.dot(lru.astype(BF16), w_ref[0:LRU_WIDTH, :], preferred_element_type=F32)
    o = o + jnp.dot(ssd.astype(BF16), w_ref[LRU_WIDTH:LRU_WIDTH + SSD_WIDTH, :], preferred_element_type=F32)
    o = o + jnp.dot(da_ref[0].astype(BF16), w_ref[LRU_WIDTH + SSD_WIDTH:, :], preferred_element_type=F32)
    o_ref[0] = x_ref[0] + gate_ref[0] * o


def _proj_out(x, lg, hf, hb, yf, yb, xbc, z, da, dvec, ng, gate, w):
    b, s, d = x.shape
    tm = min(PROJ_TM, s)
    tok = lambda wd: pl.BlockSpec((1, tm, wd), lambda bi, i: (bi, i, 0))
    const2 = lambda bi, i: (0, 0)
    return pl.pallas_call(
        _proj_out_kernel,
        out_shape=jax.ShapeDtypeStruct((b, s, d), F32),
        grid=(b, s // tm),
        in_specs=[tok(d), tok(LRU_WIDTH), tok(LRU_WIDTH), tok(LRU_WIDTH), tok(SSD_WIDTH),
                  tok(SSD_WIDTH), tok(SSD_WIDTH), tok(SSD_WIDTH), tok(DA_WIDTH),
                  pl.BlockSpec((1, SSD_WIDTH), const2), pl.BlockSpec((1, SSD_WIDTH), const2),
                  pl.BlockSpec((1, 1, d), lambda bi, i: (bi, 0, 0)),
                  pl.BlockSpec(w.shape, const2)],
        out_specs=tok(d),
        compiler_params=_cparams(("arbitrary", "arbitrary")),
        name="proj_out",
    )(x, lg, hf, hb, yf, yb, xbc, z, da, dvec, ng, gate, w)


def _route_kernel(x_ref, g_ref, sh_ref, sc_ref, rw_ref, rb_ref, h_ref, ri_ref, gt_ref, cnt_ref, carry_s):
    i = pl.program_id(0)
    tm = x_ref.shape[0]

    @pl.when(i == 0)
    def _():
        carry_s[...] = jnp.zeros_like(carry_s)

    h = _norm_mod(x_ref[...], g_ref[...], sh_ref[0], sc_ref[0])
    h_ref[...] = h
    logits = jnp.dot(h, rw_ref[...], preferred_element_type=F32,
                     precision=lax.Precision.HIGHEST) + rb_ref[...]
    lane = lax.broadcasted_iota(I32, logits.shape, 1)
    vals, idxs = [], []
    cur = logits
    for _ in range(TOP_K):
        m = jnp.max(cur, axis=-1, keepdims=True)
        idx = jnp.min(jnp.where(cur >= m, lane, LANES), axis=-1, keepdims=True)
        vals.append(m)
        idxs.append(idx)
        cur = jnp.where(lane == idx, 2.0 * NEG_BIG, cur)
    es = [jnp.exp(v - vals[0]) for v in vals]
    inv = 1.0 / (es[0] + es[1] + es[2] + es[3])
    onehot = jnp.zeros(logits.shape, F32)
    for idx in idxs:
        onehot = onehot + jnp.where(lane == idx, 1.0, 0.0)
    ri = lax.broadcasted_iota(I32, (tm, tm), 0)
    ci = lax.broadcasted_iota(I32, (tm, tm), 1)
    strict = jnp.where(ci < ri, 1.0, 0.0).astype(BF16)
    before = jnp.dot(strict, onehot.astype(BF16), preferred_element_type=F32) + carry_s[...]
    ri_out = jnp.zeros(logits.shape, I32)
    gt_out = jnp.zeros(logits.shape, F32)
    for k in range(TOP_K):
        rank = jnp.sum(jnp.where(lane == idxs[k], before, 0.0), axis=-1, keepdims=True)
        ri_out = ri_out + jnp.where(lane == k, idxs[k], 0) + jnp.where(lane == TOP_K + k, rank.astype(I32), 0)
        gt_out = gt_out + jnp.where(lane == k, es[k] * inv, 0.0)
    ri_ref[...] = ri_out
    gt_ref[...] = gt_out
    carry_s[...] = carry_s[...] + jnp.sum(onehot, axis=0, keepdims=True)
    cnt_ref[...] = carry_s[...].astype(I32)


def _route(x, g, sh, sc, mod_index, router_w, router_b):
    t, d = x.shape
    tm = ROUTE_TM
    rw = jnp.zeros((d, LANES), F32).at[:, :N_EXPERTS].set(router_w.astype(F32))
    rb = jnp.full((1, LANES), NEG_BIG, F32).at[0, :N_EXPERTS].set(router_b.astype(F32))
    tile = lambda wd: pl.BlockSpec((tm, wd), lambda i: (i, 0))
    const = lambda i: (0, 0)
    return pl.pallas_call(
        _route_kernel,
        out_shape=[jax.ShapeDtypeStruct((t, d), F32), jax.ShapeDtypeStruct((t, LANES), I32),
                   jax.ShapeDtypeStruct((t, LANES), F32), jax.ShapeDtypeStruct((1, LANES), I32)],
        grid=(t // tm,),
        in_specs=[tile(d), pl.BlockSpec((1, d), const),
                  pl.BlockSpec((1, 1, d), lambda i: (mod_index(i), 0, 0)),
                  pl.BlockSpec((1, 1, d), lambda i: (mod_index(i), 0, 0)),
                  pl.BlockSpec((d, LANES), const), pl.BlockSpec((1, LANES), const)],
        out_specs=[tile(d), tile(LANES), tile(LANES), pl.BlockSpec((1, LANES), const)],
        scratch_shapes=[pltpu.VMEM((1, LANES), F32)],
        compiler_params=_cparams(("arbitrary",)),
        name="moe_route",
    )(x, g.reshape(1, d), sh, sc, rw, rb)


def _dispatch_kernel(dest_ref, h_ref, xs_in, xs_out, sem):
    del xs_in
    tm = h_ref.shape[0]

    def issue(r, carry):
        for k in range(TOP_K):
            d = dest_ref[0, 0, r * TOP_K + k]
            pltpu.make_async_copy(h_ref.at[pl.ds(r, 1), :], xs_out.at[pl.ds(d, 1), :], sem).start()
        return carry

    lax.fori_loop(0, tm, issue, 0)
    for _ in range(TOP_K):
        pltpu.make_async_copy(h_ref, xs_out.at[pl.ds(0, tm), :], sem).wait()


def _dispatch(h, dest3, n_slots):
    t, d = h.shape
    tm = ROUTE_TM
    xs0 = jnp.zeros((n_slots, d), F32)
    return pl.pallas_call(
        _dispatch_kernel,
        out_shape=jax.ShapeDtypeStruct((n_slots, d), F32),
        grid=(t // tm,),
        in_specs=[pl.BlockSpec((1, 1, tm * TOP_K), lambda i: (i, 0, 0), memory_space=pltpu.SMEM),
                  pl.BlockSpec((tm, d), lambda i: (i, 0)),
                  pl.BlockSpec(memory_space=pl.ANY)],
        out_specs=pl.BlockSpec(memory_space=pl.ANY),
        scratch_shapes=[pltpu.SemaphoreType.DMA(())],
        input_output_aliases={2: 0},
        compiler_params=_cparams(("arbitrary",)),
        name="moe_dispatch",
    )(dest3, h, xs0)


def _gmm_kernel(be_ref, xs_ref, wg_ref, wu_ref, bg_ref, bu_ref, wd_ref, bd_ref, y_ref):
    del be_ref
    x = xs_ref[...].astype(BF16)
    gate = jnp.dot(x, wg_ref[...], preferred_element_type=F32) + bg_ref[...]
    up = jnp.dot(x, wu_ref[...], preferred_element_type=F32) + bu_ref[...]
    gate = jnp.minimum(gate, SWIGLU_LIMIT)
    up = jnp.clip(up, -SWIGLU_LIMIT, SWIGLU_LIMIT)
    glu = gate * _sigmoid(SWIGLU_ALPHA * gate)
    act = ((up + 1.0) * glu).astype(BF16)
    y_ref[...] = jnp.dot(act, wd_ref[...], preferred_element_type=F32) + bd_ref[...]


def _gmm(xs, block_e, wg, wu, bg, bu, wd, bd):
    n_slots, d = xs.shape
    de = wg.shape[2]
    tm = MOE_TM
    nb = n_slots // tm
    ex = lambda shape: pl.BlockSpec((None,) + shape, lambda i, be: (be[i], 0, 0))
    grid_spec = pltpu.PrefetchScalarGridSpec(
        num_scalar_prefetch=1,
        grid=(nb,),
        in_specs=[pl.BlockSpec((tm, d), lambda i, be: (i, 0)),
                  ex((d, de)), ex((d, de)), ex((1, de)), ex((1, de)), ex((de, d)), ex((1, d))],
        out_specs=pl.BlockSpec((tm, d), lambda i, be: (i, 0)),
    )
    return pl.pallas_call(
        _gmm_kernel,
        out_shape=jax.ShapeDtypeStruct((n_slots, d), F32),
        grid_spec=grid_spec,
        compiler_params=_cparams(("arbitrary",)),
        name="moe_experts",
    )(block_e, xs, wg, wu, bg, bu, wd, bd)


_SPLIT_W = 2 * LANES


def _split_gu_kernel(w_ref, p_ref, g_ref, u_ref):
    w = w_ref[...].astype(BF16)
    for j in range(w.shape[1] // _SPLIT_W):
        r = jnp.dot(w[:, j * _SPLIT_W:(j + 1) * _SPLIT_W], p_ref[...], preferred_element_type=F32)
        g_ref[:, j * LANES:(j + 1) * LANES] = r[:, :LANES].astype(BF16)
        u_ref[:, j * LANES:(j + 1) * LANES] = r[:, LANES:].astype(BF16)


def _split_gu(w_gu):
    e, d, n2 = w_gu.shape
    tn = 2 * _SPLIT_W
    src = jnp.arange(_SPLIT_W)
    dst = jnp.where(src % 2 == 0, src // 2, LANES + src // 2)
    perm = (dst[:, None] == jnp.arange(_SPLIT_W)[None, :]).astype(BF16)
    return pl.pallas_call(
        _split_gu_kernel,
        out_shape=[jax.ShapeDtypeStruct((e, d, n2 // 2), BF16)] * 2,
        grid=(e, n2 // tn),
        in_specs=[pl.BlockSpec((None, d, tn), lambda ei, j: (ei, 0, j)),
                  pl.BlockSpec((_SPLIT_W, _SPLIT_W), lambda ei, j: (0, 0))],
        out_specs=[pl.BlockSpec((None, d, tn // 2), lambda ei, j: (ei, 0, j))] * 2,
        compiler_params=_cparams(("arbitrary", "arbitrary")),
        name="split_gate_up",
    )(w_gu, perm)


def _combine_kernel(dest_ref, gt_ref, x_ref, g2_ref, y_hbm, o_ref, ybuf, sem):
    tm = x_ref.shape[0]

    def issue(r, carry):
        for k in range(TOP_K):
            d = dest_ref[0, 0, r * TOP_K + k]
            pltpu.make_async_copy(y_hbm.at[pl.ds(d, 1), :], ybuf.at[k, pl.ds(r, 1), :], sem).start()
        return carry

    lax.fori_loop(0, tm, issue, 0)
    for k in range(TOP_K):
        pltpu.make_async_copy(y_hbm.at[pl.ds(0, tm), :], ybuf.at[k], sem).wait()
    gt = gt_ref[...]
    f = gt[:, 0:1] * ybuf[0]
    for k in range(1, TOP_K):
        f = f + gt[:, k:k + 1] * ybuf[k]
    o_ref[...] = x_ref[...] + g2_ref[0] * f


def _combine(y, dest3, gates, x, g2, mod_index):
    t, d = x.shape
    tm = ROUTE_TM
    return pl.pallas_call(
        _combine_kernel,
        out_shape=jax.ShapeDtypeStruct((t, d), F32),
        grid=(t // tm,),
        in_specs=[pl.BlockSpec((1, 1, tm * TOP_K), lambda i: (i, 0, 0), memory_space=pltpu.SMEM),
                  pl.BlockSpec((tm, LANES), lambda i: (i, 0)),
                  pl.BlockSpec((tm, d), lambda i: (i, 0)),
                  pl.BlockSpec((1, 1, d), lambda i: (mod_index(i), 0, 0)),
                  pl.BlockSpec(memory_space=pl.ANY)],
        out_specs=pl.BlockSpec((tm, d), lambda i: (i, 0)),
        scratch_shapes=[pltpu.VMEM((TOP_K, tm, d), F32), pltpu.SemaphoreType.DMA(())],
        compiler_params=_cparams(("arbitrary",)),
        name="moe_combine",
    )(dest3, gates, x, g2, y)


def _moe(x_tok, g, sh, sc, g2, mod_index, router_w, router_b, w_gu, b_gu, w_down, b_down):
    t, d = x_tok.shape
    h, route_i, gates, counts = _route(x_tok, g, sh, sc, mod_index, router_w, router_b)
    top_e = route_i[:, :TOP_K]
    rank = route_i[:, TOP_K:2 * TOP_K]
    cnt = counts[0, :N_EXPERTS]
    padded = (cnt + MOE_TM - 1) // MOE_TM * MOE_TM
    pend = jnp.cumsum(padded)
    pstart = pend - padded
    dest = pstart[top_e] + rank
    nb = (t * TOP_K + N_EXPERTS * (MOE_TM - 1)) // MOE_TM
    starts = jnp.arange(nb, dtype=I32) * MOE_TM
    block_e = jnp.minimum(jnp.sum((pend[None, :] <= starts[:, None]).astype(I32), axis=1), N_EXPERTS - 1)
    dest3 = dest.astype(I32).reshape(t // ROUTE_TM, 1, ROUTE_TM * TOP_K)
    xs = _dispatch(h, dest3, nb * MOE_TM)
    wg, wu = _split_gu(w_gu)
    bg = b_gu[:, None, 0::2].astype(F32)
    bu = b_gu[:, None, 1::2].astype(F32)
    y = _gmm(xs, block_e, wg, wu, bg, bu, w_down.astype(BF16), b_down[:, None, :].astype(F32))
    return _combine(y, dest3, gates, x_tok, g2, mod_index)


def kernel(x, c, ctx, c_ctx, w_mod, b_mod, norm1_g, norm2_g, w_in, w_out, lru_conv_w, lru_conv_b, lru_wa, lru_ba, lru_wx, lru_bx, lru_lam, ssd_conv_w, ssd_conv_b, ssd_a_log, ssd_dt_bias, ssd_d, ssd_norm_g, da_q_norm, da_k_norm, da_lam_q, da_lam_k, da_subln_g, router_w, router_b, exp_w_gu, exp_b_gu, exp_w_down, exp_b_down):
    b, s, d = x.shape
    n_ctx = ctx.shape[1]
    depth = w_mod.shape[0]
    cos, sin = _rope_tables(s)
    x_lat, x_ctx = x, ctx
    mod_rows = jnp.zeros((2 * SUBLANES, d), F32).at[:b].set(c).at[b].set(c_ctx)

    for l in range(depth):
        need_ctx = l < depth - 1
        lam_init = 0.8 - 0.6 * math.exp(-0.3 * l)
        mod = _modulation(mod_rows, w_mod[l], b_mod[l])
        m_lat = mod[:b].reshape(b, 1, 6, d)
        m_ctx = jnp.broadcast_to(mod[b].reshape(1, 1, 6, d), (b, 1, 6, d))
        part = lambda m, j: m[:, :, j, :]

        w_in_l = _prep_w_in(w_in[l])
        p_lat = _proj_in(x_lat, norm1_g[l], part(m_lat, 0), part(m_lat, 1), w_in_l)
        p_ctx = _proj_in(x_ctx, norm1_g[l], part(m_ctx, 0), part(m_ctx, 1), w_in_l)

        zl = jnp.zeros((b, LRU_WIDTH), F32)
        lru_args = (lru_wa[l], lru_ba[l], lru_wx[l], lru_bx[l], lru_lam[l])
        xc_ctx = _dwconv(p_ctx["lx"], lru_conv_w[l], lru_conv_b[l], act=False)
        hf_c, hb_c = _lru_scan(xc_ctx, *lru_args, zl, zl)
        xc_lat = _dwconv(p_lat["lx"], lru_conv_w[l], lru_conv_b[l], act=False)
        hf_l, hb_l = _lru_scan(xc_lat, *lru_args, hf_c[:, -1], hb_c[:, 0])

        zs = jnp.zeros((b, SSD_GROUPS, 2 * SSD_HEAD_DIM, SSD_STATE), F32)
        xbc_ctx = _dwconv(p_ctx["xbc"], ssd_conv_w[l], ssd_conv_b[l], act=True)
        yf_c, yb_c, sf_c, sb_c = _ssd_scan(xbc_ctx, p_ctx["dt"], ssd_a_log[l], ssd_dt_bias[l], zs, zs)
        xbc_lat = _dwconv(p_lat["xbc"], ssd_conv_w[l], ssd_conv_b[l], act=True)
        yf_l, yb_l, _, _ = _ssd_scan(xbc_lat, p_lat["dt"], ssd_a_log[l], ssd_dt_bias[l], sf_c, sb_c)

        lq = da_lam_q[l].astype(F32)
        lk = da_lam_k[l].astype(F32)
        lam = jnp.exp(jnp.sum(lq[0] * lk[0])) - jnp.exp(jnp.sum(lq[1] * lk[1])) + lam_init
        kc = _qk_prep(p_ctx["k"], da_k_norm[l], cos, sin, rope=False, scale=1.0)
        kl = _qk_prep(p_lat["k"], da_k_norm[l], cos, sin, rope=True, scale=1.0)
        ql = _qk_prep(p_lat["q"], da_q_norm[l], cos, sin, rope=True, scale=DA_SCALE)
        k_all = jnp.concatenate([kc, kl], axis=2)
        v_all = jnp.concatenate([p_ctx["v"], p_lat["v"]], axis=1)
        da_l = _diff_attn(ql, k_all, v_all, lam, da_subln_g[l], lam_init)

        dvec = jnp.repeat(ssd_d[l].astype(F32), SSD_HEAD_DIM).reshape(1, SSD_WIDTH)
        ng = ssd_norm_g[l].astype(F32).reshape(1, SSD_WIDTH)
        w_out_l = w_out[l].astype(BF16)
        if need_ctx:
            qc = _qk_prep(p_ctx["q"], da_q_norm[l], cos, sin, rope=False, scale=DA_SCALE)
            da_c = _diff_attn(qc, kc, p_ctx["v"], lam, da_subln_g[l], lam_init)
            x_ctx = _proj_out(x_ctx, p_ctx["lg"], hf_c, hb_c, yf_c, yb_c, xbc_ctx, p_ctx["z"], da_c,
                              dvec, ng, part(m_ctx, 2), w_out_l)
        x_lat = _proj_out(x_lat, p_lat["lg"], hf_l, hb_l, yf_l, yb_l, xbc_lat, p_lat["z"], da_l,
                          dvec, ng, part(m_lat, 2), w_out_l)

        moe_w = (router_w[l], router_b[l], exp_w_gu[l], exp_b_gu[l], exp_w_down[l], exp_b_down[l])
        if need_ctx:
            tiles_c = n_ctx // ROUTE_TM
            tiles_b = (n_ctx + s) // ROUTE_TM
            tok = jnp.concatenate([x_ctx, x_lat], axis=1).reshape(-1, d)
            stack = lambda j: jnp.concatenate([part(m_ctx, j), part(m_lat, j)], axis=1).reshape(2 * b, 1, d)
            mod_index = lambda i: 2 * (i // tiles_b) + jnp.minimum((i % tiles_b) // tiles_c, 1)
            out = _moe(tok, norm2_g[l], stack(3), stack(4), stack(5), mod_index, *moe_w)
            out = out.reshape(b, n_ctx + s, d)
            x_ctx, x_lat = out[:, :n_ctx], out[:, n_ctx:]
        else:
            tiles_b = s // ROUTE_TM
            mod_index = lambda i: i // tiles_b
            out = _moe(x_lat.reshape(-1, d), norm2_g[l], part(m_lat, 3), part(m_lat, 4), part(m_lat, 5),
                       mod_index, *moe_w)
            x_lat = out.reshape(b, s, d)
    return x_lat
```

```python
import functools
import math

import jax
import jax.numpy as jnp
from jax import lax
from jax.experimental import pallas as pl
from jax.experimental.pallas import tpu as pltpu

F32 = jnp.float32
BF16 = jnp.bfloat16
I32 = jnp.int32

GRID_W = 64
EPS = 1e-6
CONV_K = 4
LRU_HEADS = 4
LRU_HEAD_DIM = 64
LRU_WIDTH = 256
LRU_C = 8.0
SSD_HEADS = 4
SSD_HEAD_DIM = 64
SSD_WIDTH = 256
SSD_GROUPS = 2
SSD_STATE = 128
SSD_CHUNK = 128
SSD_CONV_DIM = 768
DA_HEADS = 4
DA_QK_DIM = 64
DA_V_DIM = 128
DA_QK_WIDTH = 512
DA_WIDTH = 512
DA_SCALE = DA_QK_DIM ** -0.5
ROPE_BASE = 10000.0
N_EXPERTS = 32
TOP_K = 4
SWIGLU_ALPHA = 1.702
SWIGLU_LIMIT = 7.0

LANES = 128
SUBLANES = 8
VMEM_LIMIT = 56 * 1024 * 1024

PROJ_TM = 512
ATT_TQ = 512
ATT_KC = 256
LOG2E = math.log2(math.e)
LRU_TL = 512
SSD_STEP_CHUNKS = 4
ROUTE_TM = 256
MOE_TM = 256
NEG_BIG = -1e30


def _cparams(sem):
    return pltpu.CompilerParams(dimension_semantics=sem, vmem_limit_bytes=VMEM_LIMIT)


def _sigmoid(x):
    return 1.0 / (1.0 + jnp.exp(-x))


def _silu(x):
    return x * _sigmoid(x)


def _softplus(x):
    return jnp.maximum(x, 0.0) + jnp.log(1.0 + jnp.exp(-jnp.abs(x)))


def _mod_kernel(c_ref, w_ref, b_ref, o_ref):
    a = _silu(c_ref[...])
    o_ref[...] = jnp.dot(a, w_ref[...], preferred_element_type=F32,
                         precision=lax.Precision.HIGHEST) + b_ref[...]


def _modulation(rows, w, b):
    m, d = rows.shape
    n = w.shape[1]
    tn = 512
    return pl.pallas_call(
        _mod_kernel,
        out_shape=jax.ShapeDtypeStruct((m, n), F32),
        grid=(n // tn,),
        in_specs=[pl.BlockSpec((m, d), lambda j: (0, 0)),
                  pl.BlockSpec((d, tn), lambda j: (0, j)),
                  pl.BlockSpec((1, tn), lambda j: (0, j))],
        out_specs=pl.BlockSpec((m, tn), lambda j: (0, j)),
        compiler_params=_cparams(("arbitrary",)),
        name="modulation",
    )(rows, w, b.reshape(1, n))


def _norm_mod(x, g, sh, sc):
    ms = jnp.mean(x * x, axis=-1, keepdims=True)
    y = x * lax.rsqrt(ms + EPS) * g
    return y * (1.0 + sc) + sh


def _proj_in_kernel(x_ref, g_ref, sh_ref, sc_ref, w_ref, *o_refs, splits):
    h = _norm_mod(x_ref[0], g_ref[...], sh_ref[0], sc_ref[0]).astype(BF16)
    for o_ref, (lo, hi) in zip(o_refs, splits):
        o_ref[0] = jnp.dot(h, w_ref[:, lo:hi], preferred_element_type=F32).astype(o_ref.dtype)


_IN_GROUPS = (("lg", 256, F32), ("lx", 256, F32), ("z", 256, F32), ("xbc", 768, F32),
              ("dt", 128, F32), ("q", 512, F32), ("k", 512, F32), ("v", 512, BF16))


def _prep_w_in(w_in):
    d = w_in.shape[0]
    lo = 2 * LRU_WIDTH + SSD_WIDTH + SSD_CONV_DIM
    hi = lo + 2 * SSD_HEADS
    dt = jnp.concatenate([w_in[:, lo:hi], jnp.zeros((d, LANES - (hi - lo)), w_in.dtype)], axis=1)
    w = jnp.concatenate([w_in[:, :lo], dt, w_in[:, hi:]], axis=1)
    return w.astype(BF16)


def _proj_in(x, g, sh, sc, w):
    b, s, d = x.shape
    tm = min(PROJ_TM, s)
    splits, off = [], 0
    for _, wd, _ in _IN_GROUPS:
        splits.append((off, off + wd))
        off += wd
    out_shape = [jax.ShapeDtypeStruct((b, s, wd), dt) for _, wd, dt in _IN_GROUPS]
    out_specs = [pl.BlockSpec((1, tm, wd), lambda bi, i: (bi, i, 0)) for _, wd, _ in _IN_GROUPS]
    outs = pl.pallas_call(
        functools.partial(_proj_in_kernel, splits=tuple(splits)),
        out_shape=out_shape,
        grid=(b, s // tm),
        in_specs=[pl.BlockSpec((1, tm, d), lambda bi, i: (bi, i, 0)),
                  pl.BlockSpec((1, d), lambda bi, i: (0, 0)),
                  pl.BlockSpec((1, 1, d), lambda bi, i: (bi, 0, 0)),
                  pl.BlockSpec((1, 1, d), lambda bi, i: (bi, 0, 0)),
                  pl.BlockSpec((d, off), lambda bi, i: (0, 0))],
        out_specs=out_specs,
        compiler_params=_cparams(("arbitrary", "arbitrary")),
        name="proj_in",
    )(x, g.reshape(1, d), sh, sc, w)
    return {name: o for (name, _, _), o in zip(_IN_GROUPS, outs)}


def _dwconv_kernel(x_ref, w_ref, b_ref, o_ref, *, act):
    x = x_ref[0]
    s = x.shape[0]
    row = lax.broadcasted_iota(I32, x.shape, 0)
    acc = x * w_ref[2:3, :] + b_ref[...]
    acc = acc + jnp.where(row >= 2, pltpu.roll(x, 2, 0), 0.0) * w_ref[0:1, :]
    acc = acc + jnp.where(row >= 1, pltpu.roll(x, 1, 0), 0.0) * w_ref[1:2, :]
    acc = acc + jnp.where(row < s - 1, pltpu.roll(x, s - 1, 0), 0.0) * w_ref[3:4, :]
    if act:
        acc = _silu(acc)
    o_ref[0] = acc


def _dwconv(x, w, bias, act):
    b, s, c = x.shape
    tc = 256
    return pl.pallas_call(
        functools.partial(_dwconv_kernel, act=act),
        out_shape=jax.ShapeDtypeStruct((b, s, c), F32),
        grid=(b, c // tc),
        in_specs=[pl.BlockSpec((1, s, tc), lambda bi, j: (bi, 0, j)),
                  pl.BlockSpec((CONV_K, tc), lambda bi, j: (0, j)),
                  pl.BlockSpec((1, tc), lambda bi, j: (0, j))],
        out_specs=pl.BlockSpec((1, s, tc), lambda bi, j: (bi, 0, j)),
        compiler_params=_cparams(("arbitrary", "arbitrary")),
        name="dwconv",
    )(x, w, bias.reshape(1, c))


def _lru_coeffs(x, w_ref, b_ref, sp_ref, a_s, u_s):
    g = jnp.dot(x.astype(BF16), w_ref[...], preferred_element_type=F32) + b_ref[...]
    r = _sigmoid(g[:, :LRU_WIDTH])
    i = _sigmoid(g[:, LRU_WIDTH:])
    log_a = -LRU_C * r * sp_ref[...]
    a_s[...] = jnp.exp(log_a)
    u_s[...] = jnp.sqrt(1.0 - jnp.exp(2.0 * log_a)) * (i * x)


def _lru_kernel(xf_ref, xb_ref, wf_ref, wb_ref, bf_ref, bb_ref, spf_ref, spb_ref,
                h0f_ref, h0b_ref, hf_ref, hb_ref, a_s, u_s, cf_s, cb_s):
    c = pl.program_id(1)
    tl = xf_ref.shape[1]
    ngroups = tl // SUBLANES
    shape = (SUBLANES, LRU_WIDTH)
    row = lax.broadcasted_iota(I32, shape, 0)

    @pl.when(c == 0)
    def _():
        cf_s[...] = jnp.broadcast_to(h0f_ref[0], shape)
        cb_s[...] = jnp.broadcast_to(h0b_ref[0], shape)

    _lru_coeffs(xf_ref[0], wf_ref, bf_ref, spf_ref, a_s, u_s)

    def fwd(gi, carry):
        sl = pl.ds(pl.multiple_of(gi * SUBLANES, SUBLANES), SUBLANES)
        a = a_s[sl, :]
        u = u_s[sl, :]
        for sh in (1, 2, 4):
            keep = row >= sh
            a_sh = jnp.where(keep, pltpu.roll(a, sh, 0), 1.0)
            u_sh = jnp.where(keep, pltpu.roll(u, sh, 0), 0.0)
            u = a * u_sh + u
            a = a * a_sh
        h = a * carry + u
        hf_ref[0, sl, :] = h
        return jnp.broadcast_to(h[SUBLANES - 1:SUBLANES, :], shape)

    cf_s[...] = lax.fori_loop(0, ngroups, fwd, cf_s[...])

    _lru_coeffs(xb_ref[0], wb_ref, bb_ref, spb_ref, a_s, u_s)

    def bwd(gj, carry):
        gi = ngroups - 1 - gj
        sl = pl.ds(pl.multiple_of(gi * SUBLANES, SUBLANES), SUBLANES)
        a = a_s[sl, :]
        u = u_s[sl, :]
        for sh in (1, 2, 4):
            keep = row < SUBLANES - sh
            a_sh = jnp.where(keep, pltpu.roll(a, SUBLANES - sh, 0), 1.0)
            u_sh = jnp.where(keep, pltpu.roll(u, SUBLANES - sh, 0), 0.0)
            u = a * u_sh + u
            a = a * a_sh
        h = a * carry + u
        hb_ref[0, sl, :] = h
        return jnp.broadcast_to(h[0:1, :], shape)

    cb_s[...] = lax.fori_loop(0, ngroups, bwd, cb_s[...])


def _lru_gate_weights(wa, wx):
    def dense(wh):
        m = jnp.zeros((LRU_WIDTH, LRU_WIDTH), wh.dtype)
        for h in range(LRU_HEADS):
            lo = h * LRU_HEAD_DIM
            m = m.at[lo:lo + LRU_HEAD_DIM, lo:lo + LRU_HEAD_DIM].set(wh[h])
        return m
    return jnp.concatenate([dense(wa), dense(wx)], axis=1).astype(BF16)


def _lru_scan(xc, wa, ba, wx, bx, lam, h0f, h0b):
    b, s, w = xc.shape
    tl = min(LRU_TL, s)
    nc = s // tl
    wf = _lru_gate_weights(wa[0], wx[0])
    wb = _lru_gate_weights(wa[1], wx[1])
    bf = jnp.concatenate([ba[0], bx[0]]).reshape(1, 2 * w)
    bb = jnp.concatenate([ba[1], bx[1]]).reshape(1, 2 * w)
    sp = jax.nn.softplus(-lam.astype(F32))
    const = lambda bi, c: (0, 0)
    hf, hb = pl.pallas_call(
        _lru_kernel,
        out_shape=[jax.ShapeDtypeStruct((b, s, w), F32)] * 2,
        grid=(b, nc),
        in_specs=[pl.BlockSpec((1, tl, w), lambda bi, c: (bi, c, 0)),
                  pl.BlockSpec((1, tl, w), lambda bi, c: (bi, nc - 1 - c, 0)),
                  pl.BlockSpec((w, 2 * w), const), pl.BlockSpec((w, 2 * w), const),
                  pl.BlockSpec((1, 2 * w), const), pl.BlockSpec((1, 2 * w), const),
                  pl.BlockSpec((1, w), const), pl.BlockSpec((1, w), const),
                  pl.BlockSpec((1, 1, w), lambda bi, c: (bi, 0, 0)),
                  pl.BlockSpec((1, 1, w), lambda bi, c: (bi, 0, 0))],
        out_specs=[pl.BlockSpec((1, tl, w), lambda bi, c: (bi, c, 0)),
                   pl.BlockSpec((1, tl, w), lambda bi, c: (bi, nc - 1 - c, 0))],
        scratch_shapes=[pltpu.VMEM((tl, w), F32), pltpu.VMEM((tl, w), F32),
                        pltpu.VMEM((SUBLANES, w), F32), pltpu.VMEM((SUBLANES, w), F32)],
        compiler_params=_cparams(("arbitrary", "arbitrary")),
        name="lru_scan",
    )(xc, xc, wf, wb, bf, bb, sp[0:1], sp[1:2], h0f.reshape(b, 1, w), h0b.reshape(b, 1, w))
    return hf, hb


def _ssd_direction(x_ref, dt_ref, row0, avec_ref, dtb_ref, state_s, y_ref, lane0, rev):
    t = SSD_CHUNK
    xbc = x_ref[0, row0:row0 + t, :]
    dtraw = dt_ref[0, row0:row0 + t, :]
    ri = lax.broadcasted_iota(I32, (t, t), 0)
    ci = lax.broadcasted_iota(I32, (t, t), 1)
    causal = (ci >= ri) if rev else (ci <= ri)
    cum_m = jnp.where(causal, 1.0, 0.0).astype(F32)

    dt = _softplus(dtraw + dtb_ref[...])
    a_all = dt * avec_ref[...]
    acs = jnp.dot(cum_m, a_all, preferred_element_type=F32, precision=lax.Precision.HIGHEST)
    acs_t = acs.T
    edge = 0 if rev else t - 1

    xs = xbc[:, :SSD_WIDTH]
    hrow = lax.broadcasted_iota(I32, (t, t), 0)
    for g in range(SSD_GROUPS):
        bm = xbc[:, SSD_WIDTH + g * SSD_STATE:SSD_WIDTH + (g + 1) * SSD_STATE]
        cm = xbc[:, SSD_WIDTH + (SSD_GROUPS + g) * SSD_STATE:SSD_WIDTH + (SSD_GROUPS + g + 1) * SSD_STATE]
        bm_b = bm.astype(BF16)
        cm_b = cm.astype(BF16)
        gmat = lax.dot_general(cm_b, bm_b, (((1,), (1,)), ((), ())), preferred_element_type=F32)
        s_prev = state_s[g]
        off_all = lax.dot_general(cm_b, s_prev.astype(BF16), (((1,), (1,)), ((), ())),
                                  preferred_element_type=F32)
        xd_parts, tot = [], []
        for hl in range(2):
            h = g * 2 + hl
            ln = lane0 + h
            col = jnp.broadcast_to(acs[:, ln:ln + 1], (t, t))
            rowv = acs_t[ln:ln + 1, :]
            lmat = jnp.where(causal, jnp.exp(col - rowv), 0.0)
            xh = xs[:, h * SSD_HEAD_DIM:(h + 1) * SSD_HEAD_DIM] * dt[:, ln:ln + 1]
            ydiag = jnp.dot((gmat * lmat).astype(BF16), xh.astype(BF16), preferred_element_type=F32)
            colh = col[:, :SSD_HEAD_DIM]
            yoff = off_all[:, hl * SSD_HEAD_DIM:(hl + 1) * SSD_HEAD_DIM] * jnp.exp(colh)
            y_ref[0, row0:row0 + t, h * SSD_HEAD_DIM:(h + 1) * SSD_HEAD_DIM] = ydiag + yoff
            total = acs[edge:edge + 1, ln:ln + 1]
            xd_parts.append(xh * jnp.exp(total - colh))
            tot.append(jnp.exp(total))
        xd = jnp.concatenate(xd_parts, axis=1)
        xd_t = xd.T
        st = jnp.dot(xd_t.astype(BF16), bm_b, preferred_element_type=F32)
        fac = jnp.where(hrow < SSD_HEAD_DIM, tot[0], tot[1])
        state_s[g] = fac * s_prev + st


def _ssd_kernel(xf_ref, dtf_ref, xb_ref, dtb_ref, avec_ref, bias_ref, h0f_ref, h0b_ref,
                yf_ref, yb_ref, sf_ref, sb_ref, stf_s, stb_s):
    c = pl.program_id(1)

    @pl.when(c == 0)
    def _():
        stf_s[...] = h0f_ref[0]
        stb_s[...] = h0b_ref[0]

    nsub = xf_ref.shape[1] // SSD_CHUNK
    for j in range(nsub):
        _ssd_direction(xf_ref, dtf_ref, j * SSD_CHUNK, avec_ref, bias_ref, stf_s, yf_ref, 0, False)
        _ssd_direction(xb_ref, dtb_ref, (nsub - 1 - j) * SSD_CHUNK, avec_ref, bias_ref, stb_s, yb_ref,
                       SSD_HEADS, True)
    sf_ref[0] = stf_s[...]
    sb_ref[0] = stb_s[...]


def _ssd_scan(xbc, dt, a_log, dt_bias, h0f, h0b):
    b, s, _ = xbc.shape
    t = min(SSD_STEP_CHUNKS * SSD_CHUNK, s)
    nc = s // t
    avec = jnp.zeros((1, LANES), F32).at[0, :2 * SSD_HEADS].set(-jnp.exp(a_log.astype(F32)).reshape(-1))
    bias = jnp.zeros((1, LANES), F32).at[0, :2 * SSD_HEADS].set(dt_bias.astype(F32).reshape(-1))
    st_shape = (b, SSD_GROUPS, 2 * SSD_HEAD_DIM, SSD_STATE)
    const = lambda bi, c: (0, 0)
    st_spec = pl.BlockSpec((1,) + st_shape[1:], lambda bi, c: (bi, 0, 0, 0))
    yf, yb, sf, sb = pl.pallas_call(
        _ssd_kernel,
        out_shape=[jax.ShapeDtypeStruct((b, s, SSD_WIDTH), F32)] * 2
        + [jax.ShapeDtypeStruct(st_shape, F32)] * 2,
        grid=(b, nc),
        in_specs=[pl.BlockSpec((1, t, SSD_CONV_DIM), lambda bi, c: (bi, c, 0)),
                  pl.BlockSpec((1, t, LANES), lambda bi, c: (bi, c, 0)),
                  pl.BlockSpec((1, t, SSD_CONV_DIM), lambda bi, c: (bi, nc - 1 - c, 0)),
                  pl.BlockSpec((1, t, LANES), lambda bi, c: (bi, nc - 1 - c, 0)),
                  pl.BlockSpec((1, LANES), const), pl.BlockSpec((1, LANES), const),
                  st_spec, st_spec],
        out_specs=[pl.BlockSpec((1, t, SSD_WIDTH), lambda bi, c: (bi, c, 0)),
                   pl.BlockSpec((1, t, SSD_WIDTH), lambda bi, c: (bi, nc - 1 - c, 0)),
                   st_spec, st_spec],
        scratch_shapes=[pltpu.VMEM(st_shape[1:], F32), pltpu.VMEM(st_shape[1:], F32)],
        compiler_params=_cparams(("arbitrary", "arbitrary")),
        name="ssd_scan",
    )(xbc, dt, xbc, dt, avec, bias, h0f, h0b)
    return yf, yb, sf, sb


def _qk_prep_kernel(x_ref, g_ref, cos_ref, sin_ref, seg_ref, o_ref, *, rope, scale, kc):
    x = x_ref[0]
    sq = x * x
    hi = sq.astype(BF16)
    lo = (sq - hi.astype(F32)).astype(BF16)
    ssum = (jnp.dot(hi, seg_ref[...], preferred_element_type=F32)
            + jnp.dot(lo, seg_ref[...], preferred_element_type=F32))
    y = x * lax.rsqrt(ssum * (1.0 / DA_QK_DIM) + EPS) * g_ref[...]
    if rope:
        n = x.shape[1]
        lane = lax.broadcasted_iota(I32, x.shape, 1)
        first = (lane & (DA_QK_DIM - 1)) < DA_QK_DIM // 2
        partner = jnp.where(first, pltpu.roll(y, n - DA_QK_DIM // 2, 1), pltpu.roll(y, DA_QK_DIM // 2, 1))
        cosv = jnp.concatenate([cos_ref[...]] * (n // LANES), axis=1)
        sinv = jnp.concatenate([sin_ref[...]] * (n // LANES), axis=1)
        y = y * cosv + partner * sinv
    y = y * scale
    if kc:
        for h in range(DA_HEADS):
            head = y[:, h * 2 * DA_QK_DIM:(h + 1) * 2 * DA_QK_DIM]
            for cc in range(y.shape[0] // kc):
                t = head[cc * kc:(cc + 1) * kc, :].T.astype(BF16)
                o_ref[0, 2 * h, cc] = t[:DA_QK_DIM]
                o_ref[0, 2 * h + 1, cc] = t[DA_QK_DIM:]
    else:
        yb = y.astype(BF16)
        for j in range(2 * DA_HEADS):
            o_ref[0, j] = yb[:, j * DA_QK_DIM:(j + 1) * DA_QK_DIM]


def _rope_tables(s):
    pos = jnp.arange(s, dtype=I32)
    r = (pos // GRID_W).astype(F32)
    col = (pos % GRID_W).astype(F32)
    per_axis = DA_QK_DIM // 4
    inv = ROPE_BASE ** (-jnp.arange(per_axis, dtype=F32) / per_axis)
    ang = jnp.concatenate([r[:, None] * inv, col[:, None] * inv], axis=-1)
    cos, sin = jnp.cos(ang), jnp.sin(ang)
    cos64 = jnp.concatenate([cos, cos], axis=-1)
    sin64 = jnp.concatenate([-sin, sin], axis=-1)
    return jnp.concatenate([cos64, cos64], axis=-1), jnp.concatenate([sin64, sin64], axis=-1)


def _qk_prep(x, g, cos, sin, rope, scale, kc):
    b, s, n = x.shape
    ts = min(512, s)
    gfull = jnp.tile(g.astype(F32), n // DA_QK_DIM).reshape(1, n)
    idx = jnp.arange(n) // DA_QK_DIM
    seg = (idx[:, None] == idx[None, :]).astype(BF16)
    if kc:
        out_shape = jax.ShapeDtypeStruct((b, 2 * DA_HEADS, s // kc, DA_QK_DIM, kc), BF16)
        out_spec = pl.BlockSpec((1, 2 * DA_HEADS, ts // kc, DA_QK_DIM, kc), lambda bi, i: (bi, 0, i, 0, 0))
    else:
        out_shape = jax.ShapeDtypeStruct((b, 2 * DA_HEADS, s, DA_QK_DIM), BF16)
        out_spec = pl.BlockSpec((1, 2 * DA_HEADS, ts, DA_QK_DIM), lambda bi, i: (bi, 0, i, 0))
    return pl.pallas_call(
        functools.partial(_qk_prep_kernel, rope=rope, scale=scale, kc=kc),
        out_shape=out_shape,
        grid=(b, s // ts),
        in_specs=[pl.BlockSpec((1, ts, n), lambda bi, i: (bi, i, 0)),
                  pl.BlockSpec((1, n), lambda bi, i: (0, 0)),
                  pl.BlockSpec((ts, LANES), lambda bi, i: (i, 0)),
                  pl.BlockSpec((ts, LANES), lambda bi, i: (i, 0)),
                  pl.BlockSpec((n, n), lambda bi, i: (0, 0))],
        out_specs=out_spec,
        compiler_params=_cparams(("arbitrary", "arbitrary")),
        name="qk_prep",
    )(x, gfull, cos, sin, seg)


def _attn_kernel(lam_ref, q_ref, *refs, nseg, out_scale):
    lam = lam_ref[0, 0]
    segs = [(refs[2 * j], refs[2 * j + 1]) for j in range(nseg)]
    g_ref, o_ref = refs[2 * nseg], refs[2 * nseg + 1]
    tq = q_ref.shape[2]
    m = [jnp.full((tq, LANES), NEG_BIG, F32)] * 2
    l = [jnp.zeros((tq, LANES), F32)] * 2
    acc = [jnp.zeros((tq, DA_V_DIM), F32)] * 2

    for k_ref, v_ref in segs:
        nchunk, kc = k_ref.shape[2], k_ref.shape[4]
        for c in range(nchunk):
            v = v_ref[0, c * kc:(c + 1) * kc, :]
            for sub in range(2):
                s = jnp.dot(q_ref[0, sub], k_ref[0, sub, c], preferred_element_type=F32)
                mx = s[:, :LANES]
                for j in range(1, kc // LANES):
                    mx = jnp.maximum(mx, s[:, j * LANES:(j + 1) * LANES])
                m_new = jnp.maximum(m[sub], jnp.max(mx, axis=-1, keepdims=True))
                alpha = jnp.exp2(m[sub] - m_new)
                p = jnp.exp2(s - jnp.concatenate([m_new] * (kc // LANES), axis=1))
                lsum = alpha * l[sub]
                for j in range(kc // LANES):
                    lsum = lsum + p[:, j * LANES:(j + 1) * LANES]
                l[sub] = lsum
                acc[sub] = alpha * acc[sub] + jnp.dot(p.astype(BF16), v, preferred_element_type=F32)
                m[sub] = m_new

    l1 = jnp.sum(l[0], axis=-1, keepdims=True)
    l2 = jnp.sum(l[1], axis=-1, keepdims=True)
    o = acc[0] * (1.0 / l1) - lam * (acc[1] * (1.0 / l2))
    ms = jnp.mean(o * o, axis=-1, keepdims=True)
    o_ref[0] = o * lax.rsqrt(ms + EPS) * g_ref[...] * out_scale


def _diff_attn(qp, segments, lam, g, lam_init):
    b, _, s, d = qp.shape
    tq = min(ATT_TQ, s)
    seg_specs, seg_args = [], []
    for kt, v in segments:
        nchunk, kc = kt.shape[2], kt.shape[4]
        seg_specs += [pl.BlockSpec((1, 2, nchunk, d, kc), lambda bi, h, i: (bi, h, 0, 0, 0)),
                      pl.BlockSpec((1, nchunk * kc, DA_V_DIM), lambda bi, h, i: (bi, 0, h))]
        seg_args += [kt, v]
    return pl.pallas_call(
        functools.partial(_attn_kernel, nseg=len(segments), out_scale=1.0 - lam_init),
        out_shape=jax.ShapeDtypeStruct((b, s, DA_WIDTH), F32),
        grid=(b, DA_HEADS, s // tq),
        in_specs=[pl.BlockSpec(memory_space=pltpu.SMEM),
                  pl.BlockSpec((1, 2, tq, d), lambda bi, h, i: (bi, h, i, 0))] + seg_specs
        + [pl.BlockSpec((1, DA_V_DIM), lambda bi, h, i: (0, 0))],
        out_specs=pl.BlockSpec((1, tq, DA_V_DIM), lambda bi, h, i: (bi, i, h)),
        compiler_params=_cparams(("arbitrary", "arbitrary", "arbitrary")),
        name="diff_attn",
    )(lam.reshape(1, 1).astype(F32), qp, *seg_args, g.reshape(1, DA_V_DIM).astype(F32))


def _gelu_tanh(x):
    return 0.5 * x * (1.0 + jnp.tanh(math.sqrt(2.0 / math.pi) * (x + 0.044715 * x * x * x)))


def _proj_out_kernel(x_ref, lg_ref, hf_ref, hb_ref, yf_ref, yb_ref, xs_ref, z_ref, da_ref,
                     dvec_ref, ng_ref, gate_ref, w_ref, o_ref):
    lru = _gelu_tanh(lg_ref[0]) * (hf_ref[0] + hb_ref[0])
    y = yf_ref[0] + yb_ref[0] + dvec_ref[...] * xs_ref[0]
    y = y * _silu(z_ref[0])
    ssd = y * lax.rsqrt(jnp.mean(y * y, axis=-1, keepdims=True) + EPS) * ng_ref[...]
    o = jnp.dot(lru.astype(BF16), w_ref[0:LRU_WIDTH, :], preferred_element_type=F32)
    o = o + jnp.dot(ssd.astype(BF16), w_ref[LRU_WIDTH:LRU_WIDTH + SSD_WIDTH, :], preferred_element_type=F32)
    o = o + jnp.dot(da_ref[0].astype(BF16), w_ref[LRU_WIDTH + SSD_WIDTH:, :], preferred_element_type=F32)
    o_ref[0] = x_ref[0] + gate_ref[0] * o


def _proj_out(x, lg, hf, hb, yf, yb, xbc, z, da, dvec, ng, gate, w):
    b, s, d = x.shape
    tm = min(PROJ_TM, s)
    tok = lambda wd: pl.BlockSpec((1, tm, wd), lambda bi, i: (bi, i, 0))
    const2 = lambda bi, i: (0, 0)
    return pl.pallas_call(
        _proj_out_kernel,
        out_shape=jax.ShapeDtypeStruct((b, s, d), F32),
        grid=(b, s // tm),
        in_specs=[tok(d), tok(LRU_WIDTH), tok(LRU_WIDTH), tok(LRU_WIDTH), tok(SSD_WIDTH),
                  tok(SSD_WIDTH), tok(SSD_WIDTH), tok(SSD_WIDTH), tok(DA_WIDTH),
                  pl.BlockSpec((1, SSD_WIDTH), const2), pl.BlockSpec((1, SSD_WIDTH), const2),
                  pl.BlockSpec((1, 1, d), lambda bi, i: (bi, 0, 0)),
                  pl.BlockSpec(w.shape, const2)],
        out_specs=tok(d),
        compiler_params=_cparams(("arbitrary", "arbitrary")),
        name="proj_out",
    )(x, lg, hf, hb, yf, yb, xbc, z, da, dvec, ng, gate, w)


def _route_kernel(x_ref, g_ref, sh_ref, sc_ref, rw_ref, rb_ref, c0_ref, h_ref, ri_ref, gt_ref, cnt_ref, carry_s):
    i = pl.program_id(0)
    tm = x_ref.shape[0]

    @pl.when(i == 0)
    def _():
        carry_s[...] = c0_ref[...].astype(F32)

    h = _norm_mod(x_ref[...], g_ref[...], sh_ref[0], sc_ref[0])
    h_ref[...] = h
    logits = jnp.dot(h, rw_ref[...], preferred_element_type=F32,
                     precision=lax.Precision.HIGHEST) + rb_ref[...]
    lane = lax.broadcasted_iota(I32, logits.shape, 1)
    vals, idxs = [], []
    cur = logits
    for _ in range(TOP_K):
        m = jnp.max(cur, axis=-1, keepdims=True)
        idx = jnp.min(jnp.where(cur >= m, lane, LANES), axis=-1, keepdims=True)
        vals.append(m)
        idxs.append(idx)
        cur = jnp.where(lane == idx, 2.0 * NEG_BIG, cur)
    es = [jnp.exp(v - vals[0]) for v in vals]
    inv = 1.0 / (es[0] + es[1] + es[2] + es[3])
    onehot = jnp.zeros(logits.shape, F32)
    for idx in idxs:
        onehot = onehot + jnp.where(lane == idx, 1.0, 0.0)
    ri = lax.broadcasted_iota(I32, (tm, tm), 0)
    ci = lax.broadcasted_iota(I32, (tm, tm), 1)
    strict = jnp.where(ci < ri, 1.0, 0.0).astype(BF16)
    before = jnp.dot(strict, onehot.astype(BF16), preferred_element_type=F32) + carry_s[...]
    ri_out = jnp.zeros(logits.shape, I32)
    gt_out = jnp.zeros(logits.shape, F32)
    for k in range(TOP_K):
        rank = jnp.sum(jnp.where(lane == idxs[k], before, 0.0), axis=-1, keepdims=True)
        ri_out = ri_out + jnp.where(lane == k, idxs[k], 0) + jnp.where(lane == TOP_K + k, rank.astype(I32), 0)
        gt_out = gt_out + jnp.where(lane == k, es[k] * inv, 0.0)
    ri_ref[...] = ri_out
    gt_ref[...] = gt_out
    carry_s[...] = carry_s[...] + jnp.sum(onehot, axis=0, keepdims=True)
    cnt_ref[...] = carry_s[...].astype(I32)


def _route(x, g, sh, sc, tiles_per_mod, rw, rb, counts0):
    t, d = x.shape
    tm = ROUTE_TM
    tile = lambda wd: pl.BlockSpec((tm, wd), lambda i: (i, 0))
    const = lambda i: (0, 0)
    mod = pl.BlockSpec((1, 1, d), lambda i: (i // tiles_per_mod, 0, 0))
    return pl.pallas_call(
        _route_kernel,
        out_shape=[jax.ShapeDtypeStruct((t, d), F32), jax.ShapeDtypeStruct((t, LANES), I32),
                   jax.ShapeDtypeStruct((t, LANES), F32), jax.ShapeDtypeStruct((1, LANES), I32)],
        grid=(t // tm,),
        in_specs=[tile(d), pl.BlockSpec((1, d), const), mod, mod,
                  pl.BlockSpec((d, LANES), const), pl.BlockSpec((1, LANES), const),
                  pl.BlockSpec((1, LANES), const)],
        out_specs=[tile(d), tile(LANES), tile(LANES), pl.BlockSpec((1, LANES), const)],
        scratch_shapes=[pltpu.VMEM((1, LANES), F32)],
        compiler_params=_cparams(("arbitrary",)),
        name="moe_route",
    )(x, g.reshape(1, d), sh, sc, rw, rb, counts0)


def _dispatch_kernel(dest_ref, *refs, tiles):
    h_refs, xs_out, sem = refs[:-2], refs[-2], refs[-1]
    i = pl.program_id(0)
    tm = h_refs[0].shape[0]
    off = 0
    for h_ref, nt in zip(h_refs, tiles):
        @pl.when((i >= off) & (i < off + nt))
        def _(h_ref=h_ref):
            def issue(r, carry):
                for k in range(TOP_K):
                    d = dest_ref[0, 0, r * TOP_K + k]
                    pltpu.make_async_copy(h_ref.at[pl.ds(r, 1), :], xs_out.at[pl.ds(d, 1), :],
                                          sem).start(priority=k % 2)
                return carry
            lax.fori_loop(0, tm, issue, 0)
        off += nt
    for _ in range(TOP_K):
        pltpu.make_async_copy(h_refs[0], xs_out.at[pl.ds(0, tm), :], sem).wait()


def _dispatch(hs, dest3):
    d = hs[0].shape[1]
    tm = ROUTE_TM
    tiles = tuple(h.shape[0] // tm for h in hs)
    n_rows = sum(h.shape[0] for h in hs) * TOP_K
    specs, off = [], 0
    for nt in tiles:
        specs.append(pl.BlockSpec((tm, d), lambda i, off=off, nt=nt: (jnp.clip(i - off, 0, nt - 1), 0)))
        off += nt
    return pl.pallas_call(
        functools.partial(_dispatch_kernel, tiles=tiles),
        out_shape=jax.ShapeDtypeStruct((n_rows, d), F32),
        grid=(sum(tiles),),
        in_specs=[pl.BlockSpec((1, 1, tm * TOP_K), lambda i: (i, 0, 0), memory_space=pltpu.SMEM)] + specs,
        out_specs=pl.BlockSpec(memory_space=pl.ANY),
        scratch_shapes=[pltpu.SemaphoreType.DMA(())],
        compiler_params=_cparams(("arbitrary",)),
        name="moe_dispatch",
    )(dest3, *hs)


def _gmm_kernel(vt_ref, ve_ref, lo_ref, hi_ref, first_ref, xs_ref, wg_ref, wu_ref, bg_ref, bu_ref,
                wd_ref, bd_ref, y_ref):
    del vt_ref, ve_ref
    v = pl.program_id(0)
    lo, hi = lo_ref[v], hi_ref[v]

    @pl.when(hi > lo)
    def _():
        x = xs_ref[...].astype(BF16)
        gate = jnp.dot(x, wg_ref[...], preferred_element_type=F32) + bg_ref[...]
        up = jnp.dot(x, wu_ref[...], preferred_element_type=F32) + bu_ref[...]
        gate = jnp.minimum(gate, SWIGLU_LIMIT)
        up = jnp.clip(up, -SWIGLU_LIMIT, SWIGLU_LIMIT)
        glu = gate * _sigmoid(SWIGLU_ALPHA * gate)
        act = ((up + 1.0) * glu).astype(BF16)
        y = jnp.dot(act, wd_ref[...], preferred_element_type=F32) + bd_ref[...]
        row = lax.broadcasted_iota(I32, y.shape, 0)
        mine = (row >= lo) & (row < hi)

        @pl.when(first_ref[v] > 0)
        def _():
            y_ref[...] = jnp.where(mine, y, 0.0)

        @pl.when(first_ref[v] == 0)
        def _():
            y_ref[...] = jnp.where(mine, y, y_ref[...])


def _gmm(xs, visits, wg, wu, bg, bu, wd, bd):
    n_rows, d = xs.shape
    de = wg.shape[2]
    tm = MOE_TM
    nv = visits[0].shape[0]
    ex = lambda shape: pl.BlockSpec((None,) + shape, lambda v, vt, ve, lo, hi, fi: (ve[v], 0, 0))
    rows = pl.BlockSpec((tm, d), lambda v, vt, ve, lo, hi, fi: (vt[v], 0))
    grid_spec = pltpu.PrefetchScalarGridSpec(
        num_scalar_prefetch=5,
        grid=(nv,),
        in_specs=[rows, ex((d, de)), ex((d, de)), ex((1, de)), ex((1, de)), ex((de, d)), ex((1, d))],
        out_specs=rows,
    )
    return pl.pallas_call(
        _gmm_kernel,
        out_shape=jax.ShapeDtypeStruct((n_rows, d), F32),
        grid_spec=grid_spec,
        compiler_params=_cparams(("arbitrary",)),
        name="moe_experts",
    )(*visits, xs, wg, wu, bg, bu, wd, bd)


def _visit_tables(cnt, n_rows):
    tm = MOE_TM
    nt = n_rows // tm
    nv = nt + N_EXPERTS - 1
    end = jnp.cumsum(cnt)
    start = end - cnt
    first_tile = start // tm
    last_tile = jnp.maximum(end - 1, 0) // tm
    nvis = jnp.where(cnt > 0, last_tile - first_tile + 1, 0)
    vend = jnp.cumsum(nvis)
    vstart = vend - nvis
    v = jnp.arange(nv, dtype=I32)
    e = jnp.minimum(jnp.sum((vend[None, :] <= v[:, None]).astype(I32), axis=1), N_EXPERTS - 1)
    used = v < vend[N_EXPERTS - 1]
    tile = jnp.where(used, first_tile[e] + v - vstart[e], nt - 1)
    lo = jnp.where(used, jnp.clip(start[e] - tile * tm, 0, tm), 0)
    hi = jnp.where(used, jnp.clip(end[e] - tile * tm, 0, tm), 0)
    prev = jnp.concatenate([jnp.full((1,), -1, I32), tile[:-1].astype(I32)])
    first = (tile != prev).astype(I32)
    return tuple(a.astype(I32) for a in (tile, e, lo, hi, first))


_SPLIT_W = 2 * LANES


def _split_gu_kernel(w_ref, p_ref, g_ref, u_ref):
    w = w_ref[...].astype(BF16)
    for j in range(w.shape[1] // _SPLIT_W):
        r = jnp.dot(w[:, j * _SPLIT_W:(j + 1) * _SPLIT_W], p_ref[...], preferred_element_type=F32)
        g_ref[:, j * LANES:(j + 1) * LANES] = r[:, :LANES].astype(BF16)
        u_ref[:, j * LANES:(j + 1) * LANES] = r[:, LANES:].astype(BF16)


def _split_gu(w_gu):
    e, d, n2 = w_gu.shape
    tn = 2 * _SPLIT_W
    src = jnp.arange(_SPLIT_W)
    dst = jnp.where(src % 2 == 0, src // 2, LANES + src // 2)
    perm = (dst[:, None] == jnp.arange(_SPLIT_W)[None, :]).astype(BF16)
    return pl.pallas_call(
        _split_gu_kernel,
        out_shape=[jax.ShapeDtypeStruct((e, d, n2 // 2), BF16)] * 2,
        grid=(e, n2 // tn),
        in_specs=[pl.BlockSpec((None, d, tn), lambda ei, j: (ei, 0, j)),
                  pl.BlockSpec((_SPLIT_W, _SPLIT_W), lambda ei, j: (0, 0))],
        out_specs=[pl.BlockSpec((None, d, tn // 2), lambda ei, j: (ei, 0, j))] * 2,
        compiler_params=_cparams(("arbitrary", "arbitrary")),
        name="split_gate_up",
    )(w_gu, perm)


def _combine_kernel(dcur_ref, dnext_ref, gt_ref, x_ref, g2_ref, y_hbm, o_ref, ybuf, sems, *, n):
    i = pl.program_id(0)
    tm = x_ref.shape[0]
    slot = i % 2

    def gather(dest_ref, sl):
        def issue(r, carry):
            for k in range(TOP_K):
                d = dest_ref[0, 0, r * TOP_K + k]
                pltpu.make_async_copy(y_hbm.at[pl.ds(d, 1), :], ybuf.at[sl, k, pl.ds(r, 1), :],
                                      sems.at[sl]).start(priority=k % 2)
            return carry
        lax.fori_loop(0, tm, issue, 0)

    @pl.when(i == 0)
    def _():
        gather(dcur_ref, 0)

    @pl.when(i + 1 < n)
    def _():
        gather(dnext_ref, 1 - slot)

    for k in range(TOP_K):
        pltpu.make_async_copy(y_hbm.at[pl.ds(0, tm), :], ybuf.at[slot, k], sems.at[slot]).wait()
    gt = gt_ref[...]
    f = gt[:, 0:1] * ybuf[slot, 0]
    for k in range(1, TOP_K):
        f = f + gt[:, k:k + 1] * ybuf[slot, k]
    o_ref[...] = x_ref[...] + g2_ref[0] * f


def _combine(y, dest3, gates, x, g2, tiles_per_mod):
    t, d = x.shape
    tm = ROUTE_TM
    nt = t // tm
    return pl.pallas_call(
        functools.partial(_combine_kernel, n=nt),
        out_shape=jax.ShapeDtypeStruct((t, d), F32),
        grid=(nt,),
        in_specs=[pl.BlockSpec((1, 1, tm * TOP_K), lambda i: (i, 0, 0), memory_space=pltpu.SMEM),
                  pl.BlockSpec((1, 1, tm * TOP_K), lambda i: (jnp.minimum(i + 1, nt - 1), 0, 0),
                               memory_space=pltpu.SMEM),
                  pl.BlockSpec((tm, LANES), lambda i: (i, 0)),
                  pl.BlockSpec((tm, d), lambda i: (i, 0)),
                  pl.BlockSpec((1, 1, d), lambda i: (i // tiles_per_mod, 0, 0)),
                  pl.BlockSpec(memory_space=pl.ANY)],
        out_specs=pl.BlockSpec((tm, d), lambda i: (i, 0)),
        scratch_shapes=[pltpu.VMEM((2, TOP_K, tm, d), F32), pltpu.SemaphoreType.DMA((2,))],
        compiler_params=_cparams(("arbitrary",)),
        name="moe_combine",
    )(dest3, dest3, gates, x, g2, y)


def _moe(streams, g, router_w, router_b, w_gu, b_gu, w_down, b_down):
    d = streams[0][0].shape[1]
    rw = jnp.zeros((d, LANES), F32).at[:, :N_EXPERTS].set(router_w.astype(F32))
    rb = jnp.full((1, LANES), NEG_BIG, F32).at[0, :N_EXPERTS].set(router_b.astype(F32))
    counts = jnp.zeros((1, LANES), I32)
    routed = []
    for x_tok, sh, sc, _, tpm in streams:
        h, route_i, gates, counts = _route(x_tok, g, sh, sc, tpm, rw, rb, counts)
        routed.append((h, route_i, gates))
    n_rows = sum(x_tok.shape[0] for x_tok, *_ in streams) * TOP_K
    cnt = counts[0, :N_EXPERTS]
    start = jnp.cumsum(cnt) - cnt
    dests = []
    for h, route_i, _ in routed:
        dest = start[route_i[:, :TOP_K]] + route_i[:, TOP_K:2 * TOP_K]
        dests.append(dest.astype(I32).reshape(h.shape[0] // ROUTE_TM, 1, ROUTE_TM * TOP_K))
    xs = _dispatch([h for h, _, _ in routed], jnp.concatenate(dests, axis=0))
    wg, wu = _split_gu(w_gu)
    bg = b_gu[:, None, 0::2].astype(F32)
    bu = b_gu[:, None, 1::2].astype(F32)
    y = _gmm(xs, _visit_tables(cnt, n_rows), wg, wu, bg, bu, w_down.astype(BF16),
             b_down[:, None, :].astype(F32))
    return [_combine(y, dest3, gates, x_tok, g2, tpm)
            for (x_tok, _, _, g2, tpm), (_, _, gates), dest3 in zip(streams, routed, dests)]


def kernel(x, c, ctx, c_ctx, w_mod, b_mod, norm1_g, norm2_g, w_in, w_out, lru_conv_w, lru_conv_b, lru_wa, lru_ba, lru_wx, lru_bx, lru_lam, ssd_conv_w, ssd_conv_b, ssd_a_log, ssd_dt_bias, ssd_d, ssd_norm_g, da_q_norm, da_k_norm, da_lam_q, da_lam_k, da_subln_g, router_w, router_b, exp_w_gu, exp_b_gu, exp_w_down, exp_b_down):
    b, s, d = x.shape
    n_ctx = ctx.shape[1]
    depth = w_mod.shape[0]
    cos, sin = _rope_tables(s)
    x_lat, x_ctx = x, ctx
    mod_rows = jnp.zeros((2 * SUBLANES, d), F32).at[:b].set(c).at[b].set(c_ctx)

    for l in range(depth):
        need_ctx = l < depth - 1
        lam_init = 0.8 - 0.6 * math.exp(-0.3 * l)
        mod = _modulation(mod_rows, w_mod[l], b_mod[l])
        m_lat = mod[:b].reshape(b, 1, 6, d)
        m_ctx = jnp.broadcast_to(mod[b].reshape(1, 1, 6, d), (b, 1, 6, d))
        part = lambda m, j: m[:, :, j, :]

        w_in_l = _prep_w_in(w_in[l])
        p_lat = _proj_in(x_lat, norm1_g[l], part(m_lat, 0), part(m_lat, 1), w_in_l)
        p_ctx = _proj_in(x_ctx, norm1_g[l], part(m_ctx, 0), part(m_ctx, 1), w_in_l)

        zl = jnp.zeros((b, LRU_WIDTH), F32)
        lru_args = (lru_wa[l], lru_ba[l], lru_wx[l], lru_bx[l], lru_lam[l])
        xc_ctx = _dwconv(p_ctx["lx"], lru_conv_w[l], lru_conv_b[l], act=False)
        hf_c, hb_c = _lru_scan(xc_ctx, *lru_args, zl, zl)
        xc_lat = _dwconv(p_lat["lx"], lru_conv_w[l], lru_conv_b[l], act=False)
        hf_l, hb_l = _lru_scan(xc_lat, *lru_args, hf_c[:, -1], hb_c[:, 0])

        zs = jnp.zeros((b, SSD_GROUPS, 2 * SSD_HEAD_DIM, SSD_STATE), F32)
        xbc_ctx = _dwconv(p_ctx["xbc"], ssd_conv_w[l], ssd_conv_b[l], act=True)
        yf_c, yb_c, sf_c, sb_c = _ssd_scan(xbc_ctx, p_ctx["dt"], ssd_a_log[l], ssd_dt_bias[l], zs, zs)
        xbc_lat = _dwconv(p_lat["xbc"], ssd_conv_w[l], ssd_conv_b[l], act=True)
        yf_l, yb_l, _, _ = _ssd_scan(xbc_lat, p_lat["dt"], ssd_a_log[l], ssd_dt_bias[l], sf_c, sb_c)

        lq = da_lam_q[l].astype(F32)
        lk = da_lam_k[l].astype(F32)
        lam = jnp.exp(jnp.sum(lq[0] * lk[0])) - jnp.exp(jnp.sum(lq[1] * lk[1])) + lam_init
        kc = _qk_prep(p_ctx["k"], da_k_norm[l], cos, sin, rope=False, scale=1.0, kc=min(ATT_KC, n_ctx))
        kl = _qk_prep(p_lat["k"], da_k_norm[l], cos, sin, rope=True, scale=1.0, kc=min(ATT_KC, s))
        ql = _qk_prep(p_lat["q"], da_q_norm[l], cos, sin, rope=True, scale=DA_SCALE * LOG2E, kc=0)
        da_l = _diff_attn(ql, [(kc, p_ctx["v"]), (kl, p_lat["v"])], lam, da_subln_g[l], lam_init)

        dvec = jnp.repeat(ssd_d[l].astype(F32), SSD_HEAD_DIM).reshape(1, SSD_WIDTH)
        ng = ssd_norm_g[l].astype(F32).reshape(1, SSD_WIDTH)
        w_out_l = w_out[l].astype(BF16)
        if need_ctx:
            qc = _qk_prep(p_ctx["q"], da_q_norm[l], cos, sin, rope=False, scale=DA_SCALE * LOG2E, kc=0)
            da_c = _diff_attn(qc, [(kc, p_ctx["v"])], lam, da_subln_g[l], lam_init)
            x_ctx = _proj_out(x_ctx, p_ctx["lg"], hf_c, hb_c, yf_c, yb_c, xbc_ctx, p_ctx["z"], da_c,
                              dvec, ng, part(m_ctx, 2), w_out_l)
        x_lat = _proj_out(x_lat, p_lat["lg"], hf_l, hb_l, yf_l, yb_l, xbc_lat, p_lat["z"], da_l,
                          dvec, ng, part(m_lat, 2), w_out_l)

        moe_w = (router_w[l], router_b[l], exp_w_gu[l], exp_b_gu[l], exp_w_down[l], exp_b_down[l])
        streams = [(x_lat.reshape(-1, d), part(m_lat, 3), part(m_lat, 4), part(m_lat, 5), s // ROUTE_TM)]
        if need_ctx:
            streams.append((x_ctx.reshape(-1, d), part(m_ctx, 3), part(m_ctx, 4), part(m_ctx, 5), n_ctx // ROUTE_TM))
        outs = _moe(streams, norm2_g[l], *moe_w)
        x_lat = outs[0].reshape(b, s, d)
        if need_ctx:
            x_ctx = outs[1].reshape(b, n_ctx, d)
    return x_lat
```

```python
import functools
import math

import jax
import jax.numpy as jnp
from jax import lax
from jax.experimental import pallas as pl
from jax.experimental.pallas import tpu as pltpu

F32 = jnp.float32
BF16 = jnp.bfloat16
I32 = jnp.int32

GRID_W = 64
EPS = 1e-6
CONV_K = 4
LRU_HEADS = 4
LRU_HEAD_DIM = 64
LRU_WIDTH = 256
LRU_C = 8.0
SSD_HEADS = 4
SSD_HEAD_DIM = 64
SSD_WIDTH = 256
SSD_GROUPS = 2
SSD_STATE = 128
SSD_CHUNK = 128
SSD_CONV_DIM = 768
DA_HEADS = 4
DA_QK_DIM = 64
DA_V_DIM = 128
DA_QK_WIDTH = 512
DA_WIDTH = 512
DA_SCALE = DA_QK_DIM ** -0.5
ROPE_BASE = 10000.0
N_EXPERTS = 32
TOP_K = 4
SWIGLU_ALPHA = 1.702
SWIGLU_LIMIT = 7.0

LANES = 128
SUBLANES = 8
VMEM_LIMIT = 56 * 1024 * 1024

PROJ_TM = 512
ATT_TQ = 512
ATT_KC = 256
LOG2E = math.log2(math.e)
LRU_TL = 512
SSD_STEP_CHUNKS = 4
ROUTE_TM = 256
MOE_TM = 512
NEG_BIG = -1e30


def _cparams(sem):
    return pltpu.CompilerParams(dimension_semantics=sem, vmem_limit_bytes=VMEM_LIMIT)


def _sigmoid(x):
    return 0.5 * jnp.tanh(0.5 * x) + 0.5


def _silu(x):
    return x * _sigmoid(x)


def _softplus(x):
    return jnp.maximum(x, 0.0) + jnp.log(1.0 + jnp.exp(-jnp.abs(x)))


def _mod_kernel(c_ref, w_ref, b_ref, o_ref):
    a = _silu(c_ref[...])
    o_ref[...] = jnp.dot(a, w_ref[...], preferred_element_type=F32,
                         precision=lax.Precision.HIGHEST) + b_ref[...]


def _modulation(rows, w, b):
    m, d = rows.shape
    n = w.shape[1]
    tn = 512
    return pl.pallas_call(
        _mod_kernel,
        out_shape=jax.ShapeDtypeStruct((m, n), F32),
        grid=(n // tn,),
        in_specs=[pl.BlockSpec((m, d), lambda j: (0, 0)),
                  pl.BlockSpec((d, tn), lambda j: (0, j)),
                  pl.BlockSpec((1, tn), lambda j: (0, j))],
        out_specs=pl.BlockSpec((m, tn), lambda j: (0, j)),
        compiler_params=_cparams(("arbitrary",)),
        name="modulation",
    )(rows, w, b.reshape(1, n))


def _norm_mod(x, g, sh, sc):
    ms = jnp.mean(x * x, axis=-1, keepdims=True)
    y = x * lax.rsqrt(ms + EPS) * g
    return y * (1.0 + sc) + sh


def _proj_in_kernel(x_ref, g_ref, sh_ref, sc_ref, w_ref, *o_refs, splits):
    h = _norm_mod(x_ref[0], g_ref[...], sh_ref[0], sc_ref[0]).astype(BF16)
    for o_ref, (lo, hi) in zip(o_refs, splits):
        o_ref[0] = jnp.dot(h, w_ref[:, lo:hi], preferred_element_type=F32).astype(o_ref.dtype)


_IN_GROUPS = (("lg", 256, F32), ("lx", 256, F32), ("z", 256, F32), ("xbc", 768, F32),
              ("dt", 128, F32), ("q", 512, F32), ("k", 512, F32), ("v", 512, BF16))


def _prep_w_in(w_in):
    d = w_in.shape[0]
    lo = 2 * LRU_WIDTH + SSD_WIDTH + SSD_CONV_DIM
    hi = lo + 2 * SSD_HEADS
    dt = jnp.concatenate([w_in[:, lo:hi], jnp.zeros((d, LANES - (hi - lo)), w_in.dtype)], axis=1)
    w = jnp.concatenate([w_in[:, :lo], dt, w_in[:, hi:]], axis=1)
    return w.astype(BF16)


def _proj_in(x, g, sh, sc, w):
    b, s, d = x.shape
    tm = min(PROJ_TM, s)
    splits, off = [], 0
    for _, wd, _ in _IN_GROUPS:
        splits.append((off, off + wd))
        off += wd
    out_shape = [jax.ShapeDtypeStruct((b, s, wd), dt) for _, wd, dt in _IN_GROUPS]
    out_specs = [pl.BlockSpec((1, tm, wd), lambda bi, i: (bi, i, 0)) for _, wd, _ in _IN_GROUPS]
    outs = pl.pallas_call(
        functools.partial(_proj_in_kernel, splits=tuple(splits)),
        out_shape=out_shape,
        grid=(b, s // tm),
        in_specs=[pl.BlockSpec((1, tm, d), lambda bi, i: (bi, i, 0)),
                  pl.BlockSpec((1, d), lambda bi, i: (0, 0)),
                  pl.BlockSpec((1, 1, d), lambda bi, i: (bi, 0, 0)),
                  pl.BlockSpec((1, 1, d), lambda bi, i: (bi, 0, 0)),
                  pl.BlockSpec((d, off), lambda bi, i: (0, 0))],
        out_specs=out_specs,
        compiler_params=_cparams(("arbitrary", "arbitrary")),
        name="proj_in",
    )(x, g.reshape(1, d), sh, sc, w)
    return {name: o for (name, _, _), o in zip(_IN_GROUPS, outs)}


def _dwconv_kernel(x_ref, w_ref, b_ref, o_ref, *, act):
    x = x_ref[0]
    s = x.shape[0]
    row = lax.broadcasted_iota(I32, x.shape, 0)
    acc = x * w_ref[2:3, :] + b_ref[...]
    acc = acc + jnp.where(row >= 2, pltpu.roll(x, 2, 0), 0.0) * w_ref[0:1, :]
    acc = acc + jnp.where(row >= 1, pltpu.roll(x, 1, 0), 0.0) * w_ref[1:2, :]
    acc = acc + jnp.where(row < s - 1, pltpu.roll(x, s - 1, 0), 0.0) * w_ref[3:4, :]
    if act:
        acc = _silu(acc)
    o_ref[0] = acc


def _dwconv(x, w, bias, act):
    b, s, c = x.shape
    tc = 256
    return pl.pallas_call(
        functools.partial(_dwconv_kernel, act=act),
        out_shape=jax.ShapeDtypeStruct((b, s, c), F32),
        grid=(b, c // tc),
        in_specs=[pl.BlockSpec((1, s, tc), lambda bi, j: (bi, 0, j)),
                  pl.BlockSpec((CONV_K, tc), lambda bi, j: (0, j)),
                  pl.BlockSpec((1, tc), lambda bi, j: (0, j))],
        out_specs=pl.BlockSpec((1, s, tc), lambda bi, j: (bi, 0, j)),
        compiler_params=_cparams(("arbitrary", "arbitrary")),
        name="dwconv",
    )(x, w, bias.reshape(1, c))


def _lru_coeffs(x, w_ref, b_ref, sp_ref, a_s, u_s):
    g = jnp.dot(x.astype(BF16), w_ref[...], preferred_element_type=F32) + b_ref[...]
    r = _sigmoid(g[:, :LRU_WIDTH])
    i = _sigmoid(g[:, LRU_WIDTH:])
    a = jnp.exp(-LRU_C * r * sp_ref[...])
    a_s[...] = a
    u_s[...] = jnp.sqrt(1.0 - a * a) * (i * x)


def _lru_kernel(xf_ref, xb_ref, wf_ref, wb_ref, bf_ref, bb_ref, spf_ref, spb_ref,
                h0f_ref, h0b_ref, hf_ref, hb_ref, af_s, uf_s, ab_s, ub_s, cf_s, cb_s):
    c = pl.program_id(1)
    tl = xf_ref.shape[1]
    ngroups = tl // SUBLANES
    shape = (SUBLANES, LRU_WIDTH)
    row = lax.broadcasted_iota(I32, shape, 0)

    @pl.when(c == 0)
    def _():
        cf_s[...] = jnp.broadcast_to(h0f_ref[0], shape)
        cb_s[...] = jnp.broadcast_to(h0b_ref[0], shape)

    _lru_coeffs(xf_ref[0], wf_ref, bf_ref, spf_ref, af_s, uf_s)
    _lru_coeffs(xb_ref[0], wb_ref, bb_ref, spb_ref, ab_s, ub_s)

    def group_scan(a, u, carry, reverse):
        for sh in (1, 2, 4):
            keep = (row < SUBLANES - sh) if reverse else (row >= sh)
            amt = SUBLANES - sh if reverse else sh
            a_sh = jnp.where(keep, pltpu.roll(a, amt, 0), 1.0)
            u_sh = jnp.where(keep, pltpu.roll(u, amt, 0), 0.0)
            u = a * u_sh + u
            a = a * a_sh
        return a * carry + u

    def step(gj, carry):
        cf, cb = carry
        slf = pl.ds(pl.multiple_of(gj * SUBLANES, SUBLANES), SUBLANES)
        slb = pl.ds(pl.multiple_of((ngroups - 1 - gj) * SUBLANES, SUBLANES), SUBLANES)
        hf = group_scan(af_s[slf, :], uf_s[slf, :], cf, False)
        hb = group_scan(ab_s[slb, :], ub_s[slb, :], cb, True)
        hf_ref[0, slf, :] = hf
        hb_ref[0, slb, :] = hb
        return (jnp.broadcast_to(hf[SUBLANES - 1:SUBLANES, :], shape),
                jnp.broadcast_to(hb[0:1, :], shape))

    cf, cb = lax.fori_loop(0, ngroups, step, (cf_s[...], cb_s[...]), unroll=2)
    cf_s[...] = cf
    cb_s[...] = cb


def _lru_gate_weights(wa, wx):
    def dense(wh):
        m = jnp.zeros((LRU_WIDTH, LRU_WIDTH), wh.dtype)
        for h in range(LRU_HEADS):
            lo = h * LRU_HEAD_DIM
            m = m.at[lo:lo + LRU_HEAD_DIM, lo:lo + LRU_HEAD_DIM].set(wh[h])
        return m
    return jnp.concatenate([dense(wa), dense(wx)], axis=1).astype(BF16)


def _lru_scan(xc, wa, ba, wx, bx, lam, h0f, h0b):
    b, s, w = xc.shape
    tl = min(LRU_TL, s)
    nc = s // tl
    wf = _lru_gate_weights(wa[0], wx[0])
    wb = _lru_gate_weights(wa[1], wx[1])
    bf = jnp.concatenate([ba[0], bx[0]]).reshape(1, 2 * w)
    bb = jnp.concatenate([ba[1], bx[1]]).reshape(1, 2 * w)
    sp = jax.nn.softplus(-lam.astype(F32))
    const = lambda bi, c: (0, 0)
    hf, hb = pl.pallas_call(
        _lru_kernel,
        out_shape=[jax.ShapeDtypeStruct((b, s, w), F32)] * 2,
        grid=(b, nc),
        in_specs=[pl.BlockSpec((1, tl, w), lambda bi, c: (bi, c, 0)),
                  pl.BlockSpec((1, tl, w), lambda bi, c: (bi, nc - 1 - c, 0)),
                  pl.BlockSpec((w, 2 * w), const), pl.BlockSpec((w, 2 * w), const),
                  pl.BlockSpec((1, 2 * w), const), pl.BlockSpec((1, 2 * w), const),
                  pl.BlockSpec((1, w), const), pl.BlockSpec((1, w), const),
                  pl.BlockSpec((1, 1, w), lambda bi, c: (bi, 0, 0)),
                  pl.BlockSpec((1, 1, w), lambda bi, c: (bi, 0, 0))],
        out_specs=[pl.BlockSpec((1, tl, w), lambda bi, c: (bi, c, 0)),
                   pl.BlockSpec((1, tl, w), lambda bi, c: (bi, nc - 1 - c, 0))],
        scratch_shapes=[pltpu.VMEM((tl, w), F32)] * 4
        + [pltpu.VMEM((SUBLANES, w), F32), pltpu.VMEM((SUBLANES, w), F32)],
        compiler_params=_cparams(("arbitrary", "arbitrary")),
        name="lru_scan",
    )(xc, xc, wf, wb, bf, bb, sp[0:1], sp[1:2], h0f.reshape(b, 1, w), h0b.reshape(b, 1, w))
    return hf, hb


def _ssd_direction(x_ref, dt_ref, row0, avec_ref, dtb_ref, state_s, y_ref, lane0, rev):
    t = SSD_CHUNK
    xbc = x_ref[0, row0:row0 + t, :]
    dtraw = dt_ref[0, row0:row0 + t, :]
    ri = lax.broadcasted_iota(I32, (t, t), 0)
    ci = lax.broadcasted_iota(I32, (t, t), 1)
    causal = (ci >= ri) if rev else (ci <= ri)
    cum_m = jnp.where(causal, 1.0, 0.0).astype(F32)

    dt = _softplus(dtraw + dtb_ref[...])
    a_all = dt * avec_ref[...]
    acs = jnp.dot(cum_m, a_all, preferred_element_type=F32, precision=lax.Precision.HIGHEST)
    acs_t = acs.T
    edge = 0 if rev else t - 1

    xs = xbc[:, :SSD_WIDTH]
    hrow = lax.broadcasted_iota(I32, (t, t), 0)
    for g in range(SSD_GROUPS):
        bm = xbc[:, SSD_WIDTH + g * SSD_STATE:SSD_WIDTH + (g + 1) * SSD_STATE]
        cm = xbc[:, SSD_WIDTH + (SSD_GROUPS + g) * SSD_STATE:SSD_WIDTH + (SSD_GROUPS + g + 1) * SSD_STATE]
        bm_b = bm.astype(BF16)
        cm_b = cm.astype(BF16)
        gmat = lax.dot_general(cm_b, bm_b, (((1,), (1,)), ((), ())), preferred_element_type=F32)
        s_prev = state_s[g]
        off_all = lax.dot_general(cm_b, s_prev.astype(BF16), (((1,), (1,)), ((), ())),
                                  preferred_element_type=F32)
        xd_parts, tot = [], []
        for hl in range(2):
            h = g * 2 + hl
            ln = lane0 + h
            col = jnp.broadcast_to(acs[:, ln:ln + 1], (t, t))
            rowv = acs_t[ln:ln + 1, :]
            lmat = jnp.where(causal, jnp.exp(col - rowv), 0.0)
            xh = xs[:, h * SSD_HEAD_DIM:(h + 1) * SSD_HEAD_DIM] * dt[:, ln:ln + 1]
            ydiag = jnp.dot((gmat * lmat).astype(BF16), xh.astype(BF16), preferred_element_type=F32)
            colh = col[:, :SSD_HEAD_DIM]
            yoff = off_all[:, hl * SSD_HEAD_DIM:(hl + 1) * SSD_HEAD_DIM] * jnp.exp(colh)
            y_ref[0, row0:row0 + t, h * SSD_HEAD_DIM:(h + 1) * SSD_HEAD_DIM] = ydiag + yoff
            total = acs[edge:edge + 1, ln:ln + 1]
            xd_parts.append(xh * jnp.exp(total - colh))
            tot.append(jnp.exp(total))
        xd = jnp.concatenate(xd_parts, axis=1)
        xd_t = xd.T
        st = jnp.dot(xd_t.astype(BF16), bm_b, preferred_element_type=F32)
        fac = jnp.where(hrow < SSD_HEAD_DIM, tot[0], tot[1])
        state_s[g] = fac * s_prev + st


def _ssd_kernel(xf_ref, dtf_ref, xb_ref, dtb_ref, avec_ref, bias_ref, h0f_ref, h0b_ref,
                yf_ref, yb_ref, sf_ref, sb_ref, stf_s, stb_s):
    c = pl.program_id(1)

    @pl.when(c == 0)
    def _():
        stf_s[...] = h0f_ref[0]
        stb_s[...] = h0b_ref[0]

    nsub = xf_ref.shape[1] // SSD_CHUNK
    for j in range(nsub):
        _ssd_direction(xf_ref, dtf_ref, j * SSD_CHUNK, avec_ref, bias_ref, stf_s, yf_ref, 0, False)
        _ssd_direction(xb_ref, dtb_ref, (nsub - 1 - j) * SSD_CHUNK, avec_ref, bias_ref, stb_s, yb_ref,
                       SSD_HEADS, True)
    sf_ref[0] = stf_s[...]
    sb_ref[0] = stb_s[...]


def _ssd_scan(xbc, dt, a_log, dt_bias, h0f, h0b):
    b, s, _ = xbc.shape
    t = min(SSD_STEP_CHUNKS * SSD_CHUNK, s)
    nc = s // t
    avec = jnp.zeros((1, LANES), F32).at[0, :2 * SSD_HEADS].set(-jnp.exp(a_log.astype(F32)).reshape(-1))
    bias = jnp.zeros((1, LANES), F32).at[0, :2 * SSD_HEADS].set(dt_bias.astype(F32).reshape(-1))
    st_shape = (b, SSD_GROUPS, 2 * SSD_HEAD_DIM, SSD_STATE)
    const = lambda bi, c: (0, 0)
    st_spec = pl.BlockSpec((1,) + st_shape[1:], lambda bi, c: (bi, 0, 0, 0))
    yf, yb, sf, sb = pl.pallas_call(
        _ssd_kernel,
        out_shape=[jax.ShapeDtypeStruct((b, s, SSD_WIDTH), F32)] * 2
        + [jax.ShapeDtypeStruct(st_shape, F32)] * 2,
        grid=(b, nc),
        in_specs=[pl.BlockSpec((1, t, SSD_CONV_DIM), lambda bi, c: (bi, c, 0)),
                  pl.BlockSpec((1, t, LANES), lambda bi, c: (bi, c, 0)),
                  pl.BlockSpec((1, t, SSD_CONV_DIM), lambda bi, c: (bi, nc - 1 - c, 0)),
                  pl.BlockSpec((1, t, LANES), lambda bi, c: (bi, nc - 1 - c, 0)),
                  pl.BlockSpec((1, LANES), const), pl.BlockSpec((1, LANES), const),
                  st_spec, st_spec],
        out_specs=[pl.BlockSpec((1, t, SSD_WIDTH), lambda bi, c: (bi, c, 0)),
                   pl.BlockSpec((1, t, SSD_WIDTH), lambda bi, c: (bi, nc - 1 - c, 0)),
                   st_spec, st_spec],
        scratch_shapes=[pltpu.VMEM(st_shape[1:], F32), pltpu.VMEM(st_shape[1:], F32)],
        compiler_params=_cparams(("arbitrary", "arbitrary")),
        name="ssd_scan",
    )(xbc, dt, xbc, dt, avec, bias, h0f, h0b)
    return yf, yb, sf, sb


def _qk_prep_kernel(x_ref, g_ref, cos_ref, sin_ref, seg_ref, o_ref, *, rope, scale, kc):
    x = x_ref[0]
    sq = x * x
    hi = sq.astype(BF16)
    lo = (sq - hi.astype(F32)).astype(BF16)
    ssum = (jnp.dot(hi, seg_ref[...], preferred_element_type=F32)
            + jnp.dot(lo, seg_ref[...], preferred_element_type=F32))
    y = x * lax.rsqrt(ssum * (1.0 / DA_QK_DIM) + EPS) * g_ref[...]
    if rope:
        n = x.shape[1]
        lane = lax.broadcasted_iota(I32, x.shape, 1)
        first = (lane & (DA_QK_DIM - 1)) < DA_QK_DIM // 2
        partner = jnp.where(first, pltpu.roll(y, n - DA_QK_DIM // 2, 1), pltpu.roll(y, DA_QK_DIM // 2, 1))
        cosv = jnp.concatenate([cos_ref[...]] * (n // LANES), axis=1)
        sinv = jnp.concatenate([sin_ref[...]] * (n // LANES), axis=1)
        y = y * cosv + partner * sinv
    y = y * scale
    if kc:
        for h in range(DA_HEADS):
            head = y[:, h * 2 * DA_QK_DIM:(h + 1) * 2 * DA_QK_DIM]
            for cc in range(y.shape[0] // kc):
                t = head[cc * kc:(cc + 1) * kc, :].T.astype(BF16)
                o_ref[0, 2 * h, cc] = t[:DA_QK_DIM]
                o_ref[0, 2 * h + 1, cc] = t[DA_QK_DIM:]
    else:
        yb = y.astype(BF16)
        for j in range(2 * DA_HEADS):
            o_ref[0, j] = yb[:, j * DA_QK_DIM:(j + 1) * DA_QK_DIM]


def _rope_tables(s):
    pos = jnp.arange(s, dtype=I32)
    r = (pos // GRID_W).astype(F32)
    col = (pos % GRID_W).astype(F32)
    per_axis = DA_QK_DIM // 4
    inv = ROPE_BASE ** (-jnp.arange(per_axis, dtype=F32) / per_axis)
    ang = jnp.concatenate([r[:, None] * inv, col[:, None] * inv], axis=-1)
    cos, sin = jnp.cos(ang), jnp.sin(ang)
    cos64 = jnp.concatenate([cos, cos], axis=-1)
    sin64 = jnp.concatenate([-sin, sin], axis=-1)
    return jnp.concatenate([cos64, cos64], axis=-1), jnp.concatenate([sin64, sin64], axis=-1)


def _qk_prep(x, g, cos, sin, rope, scale, kc):
    b, s, n = x.shape
    ts = min(512, s)
    gfull = jnp.tile(g.astype(F32), n // DA_QK_DIM).reshape(1, n)
    idx = jnp.arange(n) // DA_QK_DIM
    seg = (idx[:, None] == idx[None, :]).astype(BF16)
    if kc:
        out_shape = jax.ShapeDtypeStruct((b, 2 * DA_HEADS, s // kc, DA_QK_DIM, kc), BF16)
        out_spec = pl.BlockSpec((1, 2 * DA_HEADS, ts // kc, DA_QK_DIM, kc), lambda bi, i: (bi, 0, i, 0, 0))
    else:
        out_shape = jax.ShapeDtypeStruct((b, 2 * DA_HEADS, s, DA_QK_DIM), BF16)
        out_spec = pl.BlockSpec((1, 2 * DA_HEADS, ts, DA_QK_DIM), lambda bi, i: (bi, 0, i, 0))
    return pl.pallas_call(
        functools.partial(_qk_prep_kernel, rope=rope, scale=scale, kc=kc),
        out_shape=out_shape,
        grid=(b, s // ts),
        in_specs=[pl.BlockSpec((1, ts, n), lambda bi, i: (bi, i, 0)),
                  pl.BlockSpec((1, n), lambda bi, i: (0, 0)),
                  pl.BlockSpec((ts, LANES), lambda bi, i: (i, 0)),
                  pl.BlockSpec((ts, LANES), lambda bi, i: (i, 0)),
                  pl.BlockSpec((n, n), lambda bi, i: (0, 0))],
        out_specs=out_spec,
        compiler_params=_cparams(("arbitrary", "arbitrary")),
        name="qk_prep",
    )(x, gfull, cos, sin, seg)


def _attn_kernel(lam_ref, q_ref, *refs, nseg, out_scale):
    lam = lam_ref[0, 0]
    segs = [(refs[2 * j], refs[2 * j + 1]) for j in range(nseg)]
    g_ref, o_ref = refs[2 * nseg], refs[2 * nseg + 1]
    tq = q_ref.shape[2]
    m = [jnp.full((tq, LANES), NEG_BIG, F32)] * 2
    l = [jnp.zeros((tq, LANES), F32)] * 2
    acc = [jnp.zeros((tq, DA_V_DIM), F32)] * 2

    for k_ref, v_ref in segs:
        nchunk, kc = k_ref.shape[2], k_ref.shape[4]
        for c in range(nchunk):
            v = v_ref[0, c * kc:(c + 1) * kc, :]
            for sub in range(2):
                s = jnp.dot(q_ref[0, sub], k_ref[0, sub, c], preferred_element_type=F32)
                mx = s[:, :LANES]
                for j in range(1, kc // LANES):
                    mx = jnp.maximum(mx, s[:, j * LANES:(j + 1) * LANES])
                m_new = jnp.maximum(m[sub], jnp.max(mx, axis=-1, keepdims=True))
                alpha = jnp.exp2(m[sub] - m_new)
                p = jnp.exp2(s - jnp.concatenate([m_new] * (kc // LANES), axis=1))
                lsum = alpha * l[sub]
                for j in range(kc // LANES):
                    lsum = lsum + p[:, j * LANES:(j + 1) * LANES]
                l[sub] = lsum
                acc[sub] = alpha * acc[sub] + jnp.dot(p.astype(BF16), v, preferred_element_type=F32)
                m[sub] = m_new

    l1 = jnp.sum(l[0], axis=-1, keepdims=True)
    l2 = jnp.sum(l[1], axis=-1, keepdims=True)
    o = acc[0] * (1.0 / l1) - lam * (acc[1] * (1.0 / l2))
    ms = jnp.mean(o * o, axis=-1, keepdims=True)
    o_ref[0] = o * lax.rsqrt(ms + EPS) * g_ref[...] * out_scale


def _diff_attn(qp, segments, lam, g, lam_init):
    b, _, s, d = qp.shape
    tq = min(ATT_TQ, s)
    seg_specs, seg_args = [], []
    for kt, v in segments:
        nchunk, kc = kt.shape[2], kt.shape[4]
        seg_specs += [pl.BlockSpec((1, 2, nchunk, d, kc), lambda bi, h, i: (bi, h, 0, 0, 0)),
                      pl.BlockSpec((1, nchunk * kc, DA_V_DIM), lambda bi, h, i: (bi, 0, h))]
        seg_args += [kt, v]
    return pl.pallas_call(
        functools.partial(_attn_kernel, nseg=len(segments), out_scale=1.0 - lam_init),
        out_shape=jax.ShapeDtypeStruct((b, s, DA_WIDTH), F32),
        grid=(b, DA_HEADS, s // tq),
        in_specs=[pl.BlockSpec(memory_space=pltpu.SMEM),
                  pl.BlockSpec((1, 2, tq, d), lambda bi, h, i: (bi, h, i, 0))] + seg_specs
        + [pl.BlockSpec((1, DA_V_DIM), lambda bi, h, i: (0, 0))],
        out_specs=pl.BlockSpec((1, tq, DA_V_DIM), lambda bi, h, i: (bi, i, h)),
        compiler_params=_cparams(("arbitrary", "arbitrary", "arbitrary")),
        name="diff_attn",
    )(lam.reshape(1, 1).astype(F32), qp, *seg_args, g.reshape(1, DA_V_DIM).astype(F32))


def _gelu_tanh(x):
    return 0.5 * x * (1.0 + jnp.tanh(math.sqrt(2.0 / math.pi) * (x + 0.044715 * x * x * x)))


def _proj_out_kernel(x_ref, lg_ref, hf_ref, hb_ref, yf_ref, yb_ref, xs_ref, z_ref, da_ref,
                     dvec_ref, ng_ref, gate_ref, w_ref, o_ref):
    lru = _gelu_tanh(lg_ref[0]) * (hf_ref[0] + hb_ref[0])
    y = yf_ref[0] + yb_ref[0] + dvec_ref[...] * xs_ref[0]
    y = y * _silu(z_ref[0])
    ssd = y * lax.rsqrt(jnp.mean(y * y, axis=-1, keepdims=True) + EPS) * ng_ref[...]
    o = jnp.dot(lru.astype(BF16), w_ref[0:LRU_WIDTH, :], preferred_element_type=F32)
    o = o + jnp.dot(ssd.astype(BF16), w_ref[LRU_WIDTH:LRU_WIDTH + SSD_WIDTH, :], preferred_element_type=F32)
    o = o + jnp.dot(da_ref[0].astype(BF16), w_ref[LRU_WIDTH + SSD_WIDTH:, :], preferred_element_type=F32)
    o_ref[0] = x_ref[0] + gate_ref[0] * o


def _proj_out(x, lg, hf, hb, yf, yb, xbc, z, da, dvec, ng, gate, w):
    b, s, d = x.shape
    tm = min(PROJ_TM, s)
    tok = lambda wd: pl.BlockSpec((1, tm, wd), lambda bi, i: (bi, i, 0))
    const2 = lambda bi, i: (0, 0)
    return pl.pallas_call(
        _proj_out_kernel,
        out_shape=jax.ShapeDtypeStruct((b, s, d), F32),
        grid=(b, s // tm),
        in_specs=[tok(d), tok(LRU_WIDTH), tok(LRU_WIDTH), tok(LRU_WIDTH), tok(SSD_WIDTH),
                  tok(SSD_WIDTH), tok(SSD_WIDTH), tok(SSD_WIDTH), tok(DA_WIDTH),
                  pl.BlockSpec((1, SSD_WIDTH), const2), pl.BlockSpec((1, SSD_WIDTH), const2),
                  pl.BlockSpec((1, 1, d), lambda bi, i: (bi, 0, 0)),
                  pl.BlockSpec(w.shape, const2)],
        out_specs=tok(d),
        compiler_params=_cparams(("arbitrary", "arbitrary")),
        name="proj_out",
    )(x, lg, hf, hb, yf, yb, xbc, z, da, dvec, ng, gate, w)


def _route_kernel(x_ref, g_ref, sh_ref, sc_ref, rwh_ref, rwl_ref, rb_ref, c0_ref, h_ref, ri_ref, gt_ref,
                  cnt_ref, carry_s):
    i = pl.program_id(0)
    tm = x_ref.shape[0]

    @pl.when(i == 0)
    def _():
        carry_s[...] = c0_ref[...].astype(F32)

    h = _norm_mod(x_ref[...], g_ref[...], sh_ref[0], sc_ref[0])
    h_ref[...] = h
    h_hi = h.astype(BF16)
    h_lo = (h - h_hi.astype(F32)).astype(BF16)
    logits = (jnp.dot(h_hi, rwh_ref[...], preferred_element_type=F32)
              + jnp.dot(h_hi, rwl_ref[...], preferred_element_type=F32)
              + jnp.dot(h_lo, rwh_ref[...], preferred_element_type=F32)) + rb_ref[...]
    lane = lax.broadcasted_iota(I32, logits.shape, 1)
    lane_f = lane.astype(F32)
    vals, idxs = [], []
    cur = logits
    for _ in range(TOP_K):
        m = jnp.max(cur, axis=-1, keepdims=True)
        idx = jnp.min(jnp.where(cur >= m, lane_f, float(LANES)), axis=-1, keepdims=True).astype(I32)
        vals.append(m)
        idxs.append(idx)
        cur = jnp.where(lane == idx, 2.0 * NEG_BIG, cur)
    es = [jnp.exp(v - vals[0]) for v in vals]
    inv = 1.0 / (es[0] + es[1] + es[2] + es[3])
    onehot = jnp.zeros(logits.shape, F32)
    for idx in idxs:
        onehot = onehot + jnp.where(lane == idx, 1.0, 0.0)
    ri = lax.broadcasted_iota(I32, (tm, tm), 0)
    ci = lax.broadcasted_iota(I32, (tm, tm), 1)
    strict = jnp.where(ci < ri, 1.0, 0.0).astype(BF16)
    before = jnp.dot(strict, onehot.astype(BF16), preferred_element_type=F32) + carry_s[...]
    ri_out = jnp.zeros(logits.shape, I32)
    gt_out = jnp.zeros(logits.shape, F32)
    for k in range(TOP_K):
        rank = jnp.sum(jnp.where(lane == idxs[k], before, 0.0), axis=-1, keepdims=True)
        ri_out = ri_out + jnp.where(lane == k, idxs[k], 0) + jnp.where(lane == TOP_K + k, rank.astype(I32), 0)
        gt_out = gt_out + jnp.where(lane == k, es[k] * inv, 0.0)
    ri_ref[...] = ri_out
    gt_ref[...] = gt_out
    carry_s[...] = carry_s[...] + jnp.sum(onehot, axis=0, keepdims=True)
    cnt_ref[...] = carry_s[...].astype(I32)


def _route(x, g, sh, sc, tiles_per_mod, rw, rb, counts0):
    t, d = x.shape
    tm = ROUTE_TM
    tile = lambda wd: pl.BlockSpec((tm, wd), lambda i: (i, 0))
    const = lambda i: (0, 0)
    mod = pl.BlockSpec((1, 1, d), lambda i: (i // tiles_per_mod, 0, 0))
    return pl.pallas_call(
        _route_kernel,
        out_shape=[jax.ShapeDtypeStruct((t, d), F32), jax.ShapeDtypeStruct((t, LANES), I32),
                   jax.ShapeDtypeStruct((t, LANES), F32), jax.ShapeDtypeStruct((1, LANES), I32)],
        grid=(t // tm,),
        in_specs=[tile(d), pl.BlockSpec((1, d), const), mod, mod,
                  pl.BlockSpec((d, LANES), const), pl.BlockSpec((d, LANES), const),
                  pl.BlockSpec((1, LANES), const), pl.BlockSpec((1, LANES), const)],
        out_specs=[tile(d), tile(LANES), tile(LANES), pl.BlockSpec((1, LANES), const)],
        scratch_shapes=[pltpu.VMEM((1, LANES), F32)],
        compiler_params=_cparams(("arbitrary",)),
        name="moe_route",
    )(x, g.reshape(1, d), sh, sc, rw[0], rw[1], rb, counts0)


def _dispatch_kernel(dest_ref, *refs, tiles):
    h_refs, xs_out, sem = refs[:-2], refs[-2], refs[-1]
    i = pl.program_id(0)
    tm = h_refs[0].shape[0]
    off = 0
    for h_ref, nt in zip(h_refs, tiles):
        @pl.when((i >= off) & (i < off + nt))
        def _(h_ref=h_ref):
            def issue(r, carry):
                for k in range(TOP_K):
                    d = dest_ref[0, 0, r * TOP_K + k]
                    pltpu.make_async_copy(h_ref.at[pl.ds(r, 1), :], xs_out.at[pl.ds(d, 1), :],
                                          sem).start(priority=k % 2)
                return carry
            lax.fori_loop(0, tm, issue, 0)
        off += nt
    for _ in range(TOP_K):
        pltpu.make_async_copy(h_refs[0], xs_out.at[pl.ds(0, tm), :], sem).wait()


def _dispatch(hs, dest3):
    d = hs[0].shape[1]
    tm = ROUTE_TM
    tiles = tuple(h.shape[0] // tm for h in hs)
    n_rows = sum(h.shape[0] for h in hs) * TOP_K
    specs, off = [], 0
    for nt in tiles:
        specs.append(pl.BlockSpec((tm, d), lambda i, off=off, nt=nt: (jnp.clip(i - off, 0, nt - 1), 0)))
        off += nt
    return pl.pallas_call(
        functools.partial(_dispatch_kernel, tiles=tiles),
        out_shape=jax.ShapeDtypeStruct((n_rows, d), F32),
        grid=(sum(tiles),),
        in_specs=[pl.BlockSpec((1, 1, tm * TOP_K), lambda i: (i, 0, 0), memory_space=pltpu.SMEM)] + specs,
        out_specs=pl.BlockSpec(memory_space=pl.ANY),
        scratch_shapes=[pltpu.SemaphoreType.DMA(())],
        compiler_params=_cparams(("arbitrary",)),
        name="moe_dispatch",
    )(dest3, *hs)


def _gmm_kernel(vt_ref, ve_ref, lo_ref, hi_ref, first_ref, xs_ref, wg_ref, wu_ref, bg_ref, bu_ref,
                wd_ref, bd_ref, y_ref):
    del vt_ref, ve_ref
    v = pl.program_id(0)
    lo, hi = lo_ref[v], hi_ref[v]

    @pl.when(hi > lo)
    def _():
        x = xs_ref[...].astype(BF16)
        gate = jnp.dot(x, wg_ref[...], preferred_element_type=F32) + bg_ref[...]
        up = jnp.dot(x, wu_ref[...], preferred_element_type=F32) + bu_ref[...]
        gate = jnp.minimum(gate, SWIGLU_LIMIT)
        up = jnp.clip(up, -SWIGLU_LIMIT, SWIGLU_LIMIT)
        glu = gate * _sigmoid(SWIGLU_ALPHA * gate)
        act = ((up + 1.0) * glu).astype(BF16)
        down = lambda: jnp.dot(act, wd_ref[...], preferred_element_type=F32) + bd_ref[...]
        whole = (lo == 0) & (hi == xs_ref.shape[0])
        first = first_ref[v] > 0
        row = lax.broadcasted_iota(I32, y_ref.shape, 0)
        mine = (row >= lo) & (row < hi)

        @pl.when(whole)
        def _():
            y_ref[...] = down()

        @pl.when(jnp.logical_not(whole) & first)
        def _():
            y_ref[...] = jnp.where(mine, down(), 0.0)

        @pl.when(jnp.logical_not(whole) & jnp.logical_not(first))
        def _():
            y_ref[...] = jnp.where(mine, down(), y_ref[...])


def _gmm(xs, visits, wg, wu, bg, bu, wd, bd):
    n_rows, d = xs.shape
    de = wg.shape[2]
    tm = MOE_TM
    nv = visits[0].shape[0]
    ex = lambda shape: pl.BlockSpec((None,) + shape, lambda v, vt, ve, lo, hi, fi: (ve[v], 0, 0))
    rows = pl.BlockSpec((tm, d), lambda v, vt, ve, lo, hi, fi: (vt[v], 0))
    grid_spec = pltpu.PrefetchScalarGridSpec(
        num_scalar_prefetch=5,
        grid=(nv,),
        in_specs=[rows, ex((d, de)), ex((d, de)), ex((1, de)), ex((1, de)), ex((de, d)), ex((1, d))],
        out_specs=rows,
    )
    return pl.pallas_call(
        _gmm_kernel,
        out_shape=jax.ShapeDtypeStruct((n_rows, d), F32),
        grid_spec=grid_spec,
        compiler_params=_cparams(("arbitrary",)),
        name="moe_experts",
    )(*visits, xs, wg, wu, bg, bu, wd, bd)


def _visit_tables(cnt, n_rows):
    tm = MOE_TM
    nt = n_rows // tm
    nv = nt + N_EXPERTS - 1
    end = jnp.cumsum(cnt)
    start = end - cnt
    first_tile = start // tm
    last_tile = jnp.maximum(end - 1, 0) // tm
    nvis = jnp.where(cnt > 0, last_tile - first_tile + 1, 0)
    vend = jnp.cumsum(nvis)
    vstart = vend - nvis
    v = jnp.arange(nv, dtype=I32)
    e = jnp.minimum(jnp.sum((vend[None, :] <= v[:, None]).astype(I32), axis=1), N_EXPERTS - 1)
    used = v < vend[N_EXPERTS - 1]
    tile = jnp.where(used, first_tile[e] + v - vstart[e], nt - 1)
    lo = jnp.where(used, jnp.clip(start[e] - tile * tm, 0, tm), 0)
    hi = jnp.where(used, jnp.clip(end[e] - tile * tm, 0, tm), 0)
    prev = jnp.concatenate([jnp.full((1,), -1, I32), tile[:-1].astype(I32)])
    first = (tile != prev).astype(I32)
    return tuple(a.astype(I32) for a in (tile, e, lo, hi, first))


_SPLIT_W = 2 * LANES


def _split_gu_kernel(w_ref, p_ref, g_ref, u_ref):
    w = w_ref[...].astype(BF16)
    for j in range(w.shape[1] // _SPLIT_W):
        r = jnp.dot(w[:, j * _SPLIT_W:(j + 1) * _SPLIT_W], p_ref[...], preferred_element_type=F32)
        g_ref[:, j * LANES:(j + 1) * LANES] = r[:, :LANES].astype(BF16)
        u_ref[:, j * LANES:(j + 1) * LANES] = r[:, LANES:].astype(BF16)


def _split_gu(w_gu, layer):
    _, e, d, n2 = w_gu.shape
    tn = 2 * _SPLIT_W
    src = jnp.arange(_SPLIT_W)
    dst = jnp.where(src % 2 == 0, src // 2, LANES + src // 2)
    perm = (dst[:, None] == jnp.arange(_SPLIT_W)[None, :]).astype(BF16)
    return pl.pallas_call(
        _split_gu_kernel,
        out_shape=[jax.ShapeDtypeStruct((e, d, n2 // 2), BF16)] * 2,
        grid=(e, n2 // tn),
        in_specs=[pl.BlockSpec((None, None, d, tn), lambda ei, j: (layer, ei, 0, j)),
                  pl.BlockSpec((_SPLIT_W, _SPLIT_W), lambda ei, j: (0, 0))],
        out_specs=[pl.BlockSpec((None, d, tn // 2), lambda ei, j: (ei, 0, j))] * 2,
        compiler_params=_cparams(("arbitrary", "arbitrary")),
        name="split_gate_up",
    )(w_gu, perm)


def _combine_kernel(dcur_ref, dnext_ref, gt_ref, x_ref, g2_ref, y_hbm, o_ref, ybuf, sems, *, n):
    i = pl.program_id(0)
    tm = x_ref.shape[0]
    slot = i % 2

    def gather(dest_ref, sl):
        def issue(r, carry):
            for k in range(TOP_K):
                d = dest_ref[0, 0, r * TOP_K + k]
                pltpu.make_async_copy(y_hbm.at[pl.ds(d, 1), :], ybuf.at[sl, k, pl.ds(r, 1), :],
                                      sems.at[sl]).start(priority=k % 2)
            return carry
        lax.fori_loop(0, tm, issue, 0)

    @pl.when(i == 0)
    def _():
        gather(dcur_ref, 0)

    @pl.when(i + 1 < n)
    def _():
        gather(dnext_ref, 1 - slot)

    for k in range(TOP_K):
        pltpu.make_async_copy(y_hbm.at[pl.ds(0, tm), :], ybuf.at[slot, k], sems.at[slot]).wait()
    gt = gt_ref[...]
    f = gt[:, 0:1] * ybuf[slot, 0]
    for k in range(1, TOP_K):
        f = f + gt[:, k:k + 1] * ybuf[slot, k]
    o_ref[...] = x_ref[...] + g2_ref[0] * f


def _combine(y, dest3, gates, x, g2, tiles_per_mod):
    t, d = x.shape
    tm = ROUTE_TM
    nt = t // tm
    return pl.pallas_call(
        functools.partial(_combine_kernel, n=nt),
        out_shape=jax.ShapeDtypeStruct((t, d), F32),
        grid=(nt,),
        in_specs=[pl.BlockSpec((1, 1, tm * TOP_K), lambda i: (i, 0, 0), memory_space=pltpu.SMEM),
                  pl.BlockSpec((1, 1, tm * TOP_K), lambda i: (jnp.minimum(i + 1, nt - 1), 0, 0),
                               memory_space=pltpu.SMEM),
                  pl.BlockSpec((tm, LANES), lambda i: (i, 0)),
                  pl.BlockSpec((tm, d), lambda i: (i, 0)),
                  pl.BlockSpec((1, 1, d), lambda i: (i // tiles_per_mod, 0, 0)),
                  pl.BlockSpec(memory_space=pl.ANY)],
        out_specs=pl.BlockSpec((tm, d), lambda i: (i, 0)),
        scratch_shapes=[pltpu.VMEM((2, TOP_K, tm, d), F32), pltpu.SemaphoreType.DMA((2,))],
        compiler_params=_cparams(("arbitrary",)),
        name="moe_combine",
    )(dest3, dest3, gates, x, g2, y)


def _moe(streams, layer, g, router_w, router_b, w_gu_all, b_gu, w_down, b_down):
    d = streams[0][0].shape[1]
    rw_f = jnp.zeros((d, LANES), F32).at[:, :N_EXPERTS].set(router_w.astype(F32))
    rw_hi = rw_f.astype(BF16)
    rw = (rw_hi, (rw_f - rw_hi.astype(F32)).astype(BF16))
    rb = jnp.full((1, LANES), NEG_BIG, F32).at[0, :N_EXPERTS].set(router_b.astype(F32))
    counts = jnp.zeros((1, LANES), I32)
    routed = []
    for x_tok, sh, sc, _, tpm in streams:
        h, route_i, gates, counts = _route(x_tok, g, sh, sc, tpm, rw, rb, counts)
        routed.append((h, route_i, gates))
    n_rows = sum(x_tok.shape[0] for x_tok, *_ in streams) * TOP_K
    cnt = counts[0, :N_EXPERTS]
    start = jnp.cumsum(cnt) - cnt
    dests = []
    for h, route_i, _ in routed:
        dest = start[route_i[:, :TOP_K]] + route_i[:, TOP_K:2 * TOP_K]
        dests.append(dest.astype(I32).reshape(h.shape[0] // ROUTE_TM, 1, ROUTE_TM * TOP_K))
    xs = _dispatch([h for h, _, _ in routed], jnp.concatenate(dests, axis=0))
    wg, wu = _split_gu(w_gu_all, layer)
    bg = b_gu[:, None, 0::2].astype(F32)
    bu = b_gu[:, None, 1::2].astype(F32)
    y = _gmm(xs, _visit_tables(cnt, n_rows), wg, wu, bg, bu, w_down.astype(BF16),
             b_down[:, None, :].astype(F32))
    return [_combine(y, dest3, gates, x_tok, g2, tpm)
            for (x_tok, _, _, g2, tpm), (_, _, gates), dest3 in zip(streams, routed, dests)]


def kernel(x, c, ctx, c_ctx, w_mod, b_mod, norm1_g, norm2_g, w_in, w_out, lru_conv_w, lru_conv_b, lru_wa, lru_ba, lru_wx, lru_bx, lru_lam, ssd_conv_w, ssd_conv_b, ssd_a_log, ssd_dt_bias, ssd_d, ssd_norm_g, da_q_norm, da_k_norm, da_lam_q, da_lam_k, da_subln_g, router_w, router_b, exp_w_gu, exp_b_gu, exp_w_down, exp_b_down):
    b, s, d = x.shape
    n_ctx = ctx.shape[1]
    depth = w_mod.shape[0]
    cos, sin = _rope_tables(s)
    x_lat, x_ctx = x, ctx
    mod_rows = jnp.zeros((2 * SUBLANES, d), F32).at[:b].set(c).at[b].set(c_ctx)

    for l in range(depth):
        need_ctx = l < depth - 1
        lam_init = 0.8 - 0.6 * math.exp(-0.3 * l)
        mod = _modulation(mod_rows, w_mod[l], b_mod[l])
        m_lat = mod[:b].reshape(b, 1, 6, d)
        m_ctx = jnp.broadcast_to(mod[b].reshape(1, 1, 6, d), (b, 1, 6, d))
        part = lambda m, j: m[:, :, j, :]

        w_in_l = _prep_w_in(w_in[l])
        p_lat = _proj_in(x_lat, norm1_g[l], part(m_lat, 0), part(m_lat, 1), w_in_l)
        p_ctx = _proj_in(x_ctx, norm1_g[l], part(m_ctx, 0), part(m_ctx, 1), w_in_l)

        zl = jnp.zeros((b, LRU_WIDTH), F32)
        lru_args = (lru_wa[l], lru_ba[l], lru_wx[l], lru_bx[l], lru_lam[l])
        xc_ctx = _dwconv(p_ctx["lx"], lru_conv_w[l], lru_conv_b[l], act=False)
        hf_c, hb_c = _lru_scan(xc_ctx, *lru_args, zl, zl)
        xc_lat = _dwconv(p_lat["lx"], lru_conv_w[l], lru_conv_b[l], act=False)
        hf_l, hb_l = _lru_scan(xc_lat, *lru_args, hf_c[:, -1], hb_c[:, 0])

        zs = jnp.zeros((b, SSD_GROUPS, 2 * SSD_HEAD_DIM, SSD_STATE), F32)
        xbc_ctx = _dwconv(p_ctx["xbc"], ssd_conv_w[l], ssd_conv_b[l], act=True)
        yf_c, yb_c, sf_c, sb_c = _ssd_scan(xbc_ctx, p_ctx["dt"], ssd_a_log[l], ssd_dt_bias[l], zs, zs)
        xbc_lat = _dwconv(p_lat["xbc"], ssd_conv_w[l], ssd_conv_b[l], act=True)
        yf_l, yb_l, _, _ = _ssd_scan(xbc_lat, p_lat["dt"], ssd_a_log[l], ssd_dt_bias[l], sf_c, sb_c)

        lq = da_lam_q[l].astype(F32)
        lk = da_lam_k[l].astype(F32)
        lam = jnp.exp(jnp.sum(lq[0] * lk[0])) - jnp.exp(jnp.sum(lq[1] * lk[1])) + lam_init
        kc = _qk_prep(p_ctx["k"], da_k_norm[l], cos, sin, rope=False, scale=1.0, kc=min(ATT_KC, n_ctx))
        kl = _qk_prep(p_lat["k"], da_k_norm[l], cos, sin, rope=True, scale=1.0, kc=min(ATT_KC, s))
        ql = _qk_prep(p_lat["q"], da_q_norm[l], cos, sin, rope=True, scale=DA_SCALE * LOG2E, kc=0)
        da_l = _diff_attn(ql, [(kc, p_ctx["v"]), (kl, p_lat["v"])], lam, da_subln_g[l], lam_init)

        dvec = jnp.repeat(ssd_d[l].astype(F32), SSD_HEAD_DIM).reshape(1, SSD_WIDTH)
        ng = ssd_norm_g[l].astype(F32).reshape(1, SSD_WIDTH)
        w_out_l = w_out[l].astype(BF16)
        if need_ctx:
            qc = _qk_prep(p_ctx["q"], da_q_norm[l], cos, sin, rope=False, scale=DA_SCALE * LOG2E, kc=0)
            da_c = _diff_attn(qc, [(kc, p_ctx["v"])], lam, da_subln_g[l], lam_init)
            x_ctx = _proj_out(x_ctx, p_ctx["lg"], hf_c, hb_c, yf_c, yb_c, xbc_ctx, p_ctx["z"], da_c,
                              dvec, ng, part(m_ctx, 2), w_out_l)
        x_lat = _proj_out(x_lat, p_lat["lg"], hf_l, hb_l, yf_l, yb_l, xbc_lat, p_lat["z"], da_l,
                          dvec, ng, part(m_lat, 2), w_out_l)

        moe_w = (router_w[l], router_b[l], exp_w_gu, exp_b_gu[l], exp_w_down[l], exp_b_down[l])
        streams = [(x_lat.reshape(-1, d), part(m_lat, 3), part(m_lat, 4), part(m_lat, 5), s // ROUTE_TM)]
        if need_ctx:
            streams.append((x_ctx.reshape(-1, d), part(m_ctx, 3), part(m_ctx, 4), part(m_ctx, 5), n_ctx // ROUTE_TM))
        outs = _moe(streams, l, norm2_g[l], *moe_w)
        x_lat = outs[0].reshape(b, s, d)
        if need_ctx:
            x_ctx = outs[1].reshape(b, n_ctx, d)
    return x_lat
```

```python
import functools
import math

import jax
import jax.numpy as jnp
from jax import lax
from jax.experimental import pallas as pl
from jax.experimental.pallas import tpu as pltpu

F32 = jnp.float32
BF16 = jnp.bfloat16
I32 = jnp.int32

GRID_W = 64
EPS = 1e-6
CONV_K = 4
LRU_HEADS = 4
LRU_HEAD_DIM = 64
LRU_WIDTH = 256
LRU_C = 8.0
SSD_HEADS = 4
SSD_HEAD_DIM = 64
SSD_WIDTH = 256
SSD_GROUPS = 2
SSD_STATE = 128
SSD_CHUNK = 128
SSD_CONV_DIM = 768
DA_HEADS = 4
DA_QK_DIM = 64
DA_V_DIM = 128
DA_QK_WIDTH = 512
DA_WIDTH = 512
DA_SCALE = DA_QK_DIM ** -0.5
ROPE_BASE = 10000.0
N_EXPERTS = 32
TOP_K = 4
SWIGLU_ALPHA = 1.702
SWIGLU_LIMIT = 7.0

LANES = 128
SUBLANES = 8
VMEM_LIMIT = 56 * 1024 * 1024

PROJ_TM = 512
ATT_TQ = 512
ATT_KC = 256
LOG2E = math.log2(math.e)
LRU_TL = 512
SSD_STEP_CHUNKS = 4
ROUTE_TM = 256
MOE_TM = 512
NEG_BIG = -1e30


def _cparams(sem):
    return pltpu.CompilerParams(dimension_semantics=sem, vmem_limit_bytes=VMEM_LIMIT)


def _sigmoid(x):
    return 0.5 * jnp.tanh(0.5 * x) + 0.5


def _silu(x):
    return x * _sigmoid(x)


def _softplus(x):
    return jnp.maximum(x, 0.0) + jnp.log(1.0 + jnp.exp(-jnp.abs(x)))


def _round_up8(n):
    return ((n + (SUBLANES - 1)) >> 3) << 3


def _mod_kernel(c_ref, w_ref, b_ref, o_ref):
    a = _silu(c_ref[...])
    o_ref[...] = jnp.dot(a, w_ref[...], preferred_element_type=F32,
                         precision=lax.Precision.HIGHEST) + b_ref[...]


def _modulation(rows, w, b):
    m, d = rows.shape
    n = w.shape[1]
    tn = 512
    return pl.pallas_call(
        _mod_kernel,
        out_shape=jax.ShapeDtypeStruct((m, n), F32),
        grid=(n // tn,),
        in_specs=[pl.BlockSpec((m, d), lambda j: (0, 0)),
                  pl.BlockSpec((d, tn), lambda j: (0, j)),
                  pl.BlockSpec((1, tn), lambda j: (0, j))],
        out_specs=pl.BlockSpec((m, tn), lambda j: (0, j)),
        compiler_params=_cparams(("arbitrary",)),
        name="modulation",
    )(rows, w, b.reshape(1, n))


def _norm_mod(x, g, sh, sc):
    ms = jnp.mean(x * x, axis=-1, keepdims=True)
    y = x * lax.rsqrt(ms + EPS) * g
    return y * (1.0 + sc) + sh


def _proj_in_kernel(x_ref, g_ref, sh_ref, sc_ref, w_ref, *o_refs, splits):
    h = _norm_mod(x_ref[0], g_ref[...], sh_ref[0], sc_ref[0]).astype(BF16)
    for o_ref, (lo, hi) in zip(o_refs, splits):
        o_ref[0] = jnp.dot(h, w_ref[:, lo:hi], preferred_element_type=F32).astype(o_ref.dtype)


_IN_GROUPS = (("lg", 256, F32), ("lx", 256, F32), ("z", 256, F32), ("xbc", 768, F32),
              ("dt", 128, F32), ("q", 512, F32), ("k", 512, F32), ("v", 512, BF16))


def _prep_w_in(w_in):
    d = w_in.shape[0]
    lo = 2 * LRU_WIDTH + SSD_WIDTH + SSD_CONV_DIM
    hi = lo + 2 * SSD_HEADS
    dt = jnp.concatenate([w_in[:, lo:hi], jnp.zeros((d, LANES - (hi - lo)), w_in.dtype)], axis=1)
    w = jnp.concatenate([w_in[:, :lo], dt, w_in[:, hi:]], axis=1)
    return w.astype(BF16)


def _proj_in(x, g, sh, sc, w):
    b, s, d = x.shape
    tm = min(PROJ_TM, s)
    splits, off = [], 0
    for _, wd, _ in _IN_GROUPS:
        splits.append((off, off + wd))
        off += wd
    out_shape = [jax.ShapeDtypeStruct((b, s, wd), dt) for _, wd, dt in _IN_GROUPS]
    out_specs = [pl.BlockSpec((1, tm, wd), lambda bi, i: (bi, i, 0)) for _, wd, _ in _IN_GROUPS]
    outs = pl.pallas_call(
        functools.partial(_proj_in_kernel, splits=tuple(splits)),
        out_shape=out_shape,
        grid=(b, s // tm),
        in_specs=[pl.BlockSpec((1, tm, d), lambda bi, i: (bi, i, 0)),
                  pl.BlockSpec((1, d), lambda bi, i: (0, 0)),
                  pl.BlockSpec((1, 1, d), lambda bi, i: (bi, 0, 0)),
                  pl.BlockSpec((1, 1, d), lambda bi, i: (bi, 0, 0)),
                  pl.BlockSpec((d, off), lambda bi, i: (0, 0))],
        out_specs=out_specs,
        compiler_params=_cparams(("arbitrary", "arbitrary")),
        name="proj_in",
    )(x, g.reshape(1, d), sh, sc, w)
    return {name: o for (name, _, _), o in zip(_IN_GROUPS, outs)}


def _dwconv_kernel(x_ref, w_ref, b_ref, o_ref, *, act):
    x = x_ref[0]
    s = x.shape[0]
    row = lax.broadcasted_iota(I32, x.shape, 0)
    acc = x * w_ref[2:3, :] + b_ref[...]
    acc = acc + jnp.where(row >= 2, pltpu.roll(x, 2, 0), 0.0) * w_ref[0:1, :]
    acc = acc + jnp.where(row >= 1, pltpu.roll(x, 1, 0), 0.0) * w_ref[1:2, :]
    acc = acc + jnp.where(row < s - 1, pltpu.roll(x, s - 1, 0), 0.0) * w_ref[3:4, :]
    if act:
        acc = _silu(acc)
    o_ref[0] = acc


def _dwconv(x, w, bias, act):
    b, s, c = x.shape
    tc = 256
    return pl.pallas_call(
        functools.partial(_dwconv_kernel, act=act),
        out_shape=jax.ShapeDtypeStruct((b, s, c), F32),
        grid=(b, c // tc),
        in_specs=[pl.BlockSpec((1, s, tc), lambda bi, j: (bi, 0, j)),
                  pl.BlockSpec((CONV_K, tc), lambda bi, j: (0, j)),
                  pl.BlockSpec((1, tc), lambda bi, j: (0, j))],
        out_specs=pl.BlockSpec((1, s, tc), lambda bi, j: (bi, 0, j)),
        compiler_params=_cparams(("arbitrary", "arbitrary")),
        name="dwconv",
    )(x, w, bias.reshape(1, c))


def _lru_coeffs(x, w_ref, b_ref, sp_ref, a_s, u_s):
    g = jnp.dot(x.astype(BF16), w_ref[...], preferred_element_type=F32) + b_ref[...]
    r = _sigmoid(g[:, :LRU_WIDTH])
    i = _sigmoid(g[:, LRU_WIDTH:])
    a = jnp.exp(-LRU_C * r * sp_ref[...])
    a_s[...] = a
    u_s[...] = jnp.sqrt(1.0 - a * a) * (i * x)


def _lru_kernel(xf_ref, xb_ref, wf_ref, wb_ref, bf_ref, bb_ref, spf_ref, spb_ref,
                h0f_ref, h0b_ref, hf_ref, hb_ref, af_s, uf_s, ab_s, ub_s, cf_s, cb_s):
    c = pl.program_id(1)
    tl = xf_ref.shape[1]
    ngroups = tl // SUBLANES
    shape = (SUBLANES, LRU_WIDTH)
    row = lax.broadcasted_iota(I32, shape, 0)

    @pl.when(c == 0)
    def _():
        cf_s[...] = jnp.broadcast_to(h0f_ref[0], shape)
        cb_s[...] = jnp.broadcast_to(h0b_ref[0], shape)

    _lru_coeffs(xf_ref[0], wf_ref, bf_ref, spf_ref, af_s, uf_s)
    _lru_coeffs(xb_ref[0], wb_ref, bb_ref, spb_ref, ab_s, ub_s)

    def group_scan(a, u, carry, reverse):
        for sh in (1, 2, 4):
            keep = (row < SUBLANES - sh) if reverse else (row >= sh)
            amt = SUBLANES - sh if reverse else sh
            a_sh = jnp.where(keep, pltpu.roll(a, amt, 0), 1.0)
            u_sh = jnp.where(keep, pltpu.roll(u, amt, 0), 0.0)
            u = a * u_sh + u
            a = a * a_sh
        return a * carry + u

    def step(gj, carry):
        cf, cb = carry
        slf = pl.ds(pl.multiple_of(gj * SUBLANES, SUBLANES), SUBLANES)
        slb = pl.ds(pl.multiple_of((ngroups - 1 - gj) * SUBLANES, SUBLANES), SUBLANES)
        hf = group_scan(af_s[slf, :], uf_s[slf, :], cf, False)
        hb = group_scan(ab_s[slb, :], ub_s[slb, :], cb, True)
        hf_ref[0, slf, :] = hf
        hb_ref[0, slb, :] = hb
        return (jnp.broadcast_to(hf[SUBLANES - 1:SUBLANES, :], shape),
                jnp.broadcast_to(hb[0:1, :], shape))

    cf, cb = lax.fori_loop(0, ngroups, step, (cf_s[...], cb_s[...]), unroll=2)
    cf_s[...] = cf
    cb_s[...] = cb


def _lru_gate_weights(wa, wx):
    def dense(wh):
        m = jnp.zeros((LRU_WIDTH, LRU_WIDTH), wh.dtype)
        for h in range(LRU_HEADS):
            lo = h * LRU_HEAD_DIM
            m = m.at[lo:lo + LRU_HEAD_DIM, lo:lo + LRU_HEAD_DIM].set(wh[h])
        return m
    return jnp.concatenate([dense(wa), dense(wx)], axis=1).astype(BF16)


def _lru_scan(xc, wa, ba, wx, bx, lam, h0f, h0b):
    b, s, w = xc.shape
    tl = min(LRU_TL, s)
    nc = s // tl
    wf = _lru_gate_weights(wa[0], wx[0])
    wb = _lru_gate_weights(wa[1], wx[1])
    bf = jnp.concatenate([ba[0], bx[0]]).reshape(1, 2 * w)
    bb = jnp.concatenate([ba[1], bx[1]]).reshape(1, 2 * w)
    sp = jax.nn.softplus(-lam.astype(F32))
    const = lambda bi, c: (0, 0)
    hf, hb = pl.pallas_call(
        _lru_kernel,
        out_shape=[jax.ShapeDtypeStruct((b, s, w), F32)] * 2,
        grid=(b, nc),
        in_specs=[pl.BlockSpec((1, tl, w), lambda bi, c: (bi, c, 0)),
                  pl.BlockSpec((1, tl, w), lambda bi, c: (bi, nc - 1 - c, 0)),
                  pl.BlockSpec((w, 2 * w), const), pl.BlockSpec((w, 2 * w), const),
                  pl.BlockSpec((1, 2 * w), const), pl.BlockSpec((1, 2 * w), const),
                  pl.BlockSpec((1, w), const), pl.BlockSpec((1, w), const),
                  pl.BlockSpec((1, 1, w), lambda bi, c: (bi, 0, 0)),
                  pl.BlockSpec((1, 1, w), lambda bi, c: (bi, 0, 0))],
        out_specs=[pl.BlockSpec((1, tl, w), lambda bi, c: (bi, c, 0)),
                   pl.BlockSpec((1, tl, w), lambda bi, c: (bi, nc - 1 - c, 0))],
        scratch_shapes=[pltpu.VMEM((tl, w), F32)] * 4
        + [pltpu.VMEM((SUBLANES, w), F32), pltpu.VMEM((SUBLANES, w), F32)],
        compiler_params=_cparams(("arbitrary", "arbitrary")),
        name="lru_scan",
    )(xc, xc, wf, wb, bf, bb, sp[0:1], sp[1:2], h0f.reshape(b, 1, w), h0b.reshape(b, 1, w))
    return hf, hb


def _ssd_direction(x_ref, dt_ref, row0, avec_ref, dtb_ref, state_s, y_ref, lane0, rev):
    t = SSD_CHUNK
    xbc = x_ref[0, row0:row0 + t, :]
    dtraw = dt_ref[0, row0:row0 + t, :]
    ri = lax.broadcasted_iota(I32, (t, t), 0)
    ci = lax.broadcasted_iota(I32, (t, t), 1)
    causal = (ci >= ri) if rev else (ci <= ri)
    cum_m = jnp.where(causal, 1.0, 0.0).astype(F32)

    dt = _softplus(dtraw + dtb_ref[...])
    a_all = dt * avec_ref[...]
    acs = jnp.dot(cum_m, a_all, preferred_element_type=F32, precision=lax.Precision.HIGHEST)
    acs_t = acs.T
    edge = 0 if rev else t - 1

    xs = xbc[:, :SSD_WIDTH]
    hrow = lax.broadcasted_iota(I32, (t, t), 0)
    for g in range(SSD_GROUPS):
        bm = xbc[:, SSD_WIDTH + g * SSD_STATE:SSD_WIDTH + (g + 1) * SSD_STATE]
        cm = xbc[:, SSD_WIDTH + (SSD_GROUPS + g) * SSD_STATE:SSD_WIDTH + (SSD_GROUPS + g + 1) * SSD_STATE]
        bm_b = bm.astype(BF16)
        cm_b = cm.astype(BF16)
        gmat = lax.dot_general(cm_b, bm_b, (((1,), (1,)), ((), ())), preferred_element_type=F32)
        s_prev = state_s[g]
        off_all = lax.dot_general(cm_b, s_prev.astype(BF16), (((1,), (1,)), ((), ())),
                                  preferred_element_type=F32)
        xd_parts, tot = [], []
        for hl in range(2):
            h = g * 2 + hl
            ln = lane0 + h
            col = jnp.broadcast_to(acs[:, ln:ln + 1], (t, t))
            rowv = acs_t[ln:ln + 1, :]
            lmat = jnp.where(causal, jnp.exp(col - rowv), 0.0)
            xh = xs[:, h * SSD_HEAD_DIM:(h + 1) * SSD_HEAD_DIM] * dt[:, ln:ln + 1]
            ydiag = jnp.dot((gmat * lmat).astype(BF16), xh.astype(BF16), preferred_element_type=F32)
            colh = col[:, :SSD_HEAD_DIM]
            yoff = off_all[:, hl * SSD_HEAD_DIM:(hl + 1) * SSD_HEAD_DIM] * jnp.exp(colh)
            y_ref[0, row0:row0 + t, h * SSD_HEAD_DIM:(h + 1) * SSD_HEAD_DIM] = ydiag + yoff
            total = acs[edge:edge + 1, ln:ln + 1]
            xd_parts.append(xh * jnp.exp(total - colh))
            tot.append(jnp.exp(total))
        xd = jnp.concatenate(xd_parts, axis=1)
        xd_t = xd.T
        st = jnp.dot(xd_t.astype(BF16), bm_b, preferred_element_type=F32)
        fac = jnp.where(hrow < SSD_HEAD_DIM, tot[0], tot[1])
        state_s[g] = fac * s_prev + st


def _ssd_kernel(xf_ref, dtf_ref, xb_ref, dtb_ref, avec_ref, bias_ref, h0f_ref, h0b_ref,
                yf_ref, yb_ref, sf_ref, sb_ref, stf_s, stb_s):
    c = pl.program_id(1)

    @pl.when(c == 0)
    def _():
        stf_s[...] = h0f_ref[0]
        stb_s[...] = h0b_ref[0]

    nsub = xf_ref.shape[1] // SSD_CHUNK
    for j in range(nsub):
        _ssd_direction(xf_ref, dtf_ref, j * SSD_CHUNK, avec_ref, bias_ref, stf_s, yf_ref, 0, False)
        _ssd_direction(xb_ref, dtb_ref, (nsub - 1 - j) * SSD_CHUNK, avec_ref, bias_ref, stb_s, yb_ref,
                       SSD_HEADS, True)
    sf_ref[0] = stf_s[...]
    sb_ref[0] = stb_s[...]


def _ssd_scan(xbc, dt, a_log, dt_bias, h0f, h0b):
    b, s, _ = xbc.shape
    t = min(SSD_STEP_CHUNKS * SSD_CHUNK, s)
    nc = s // t
    avec = jnp.zeros((1, LANES), F32).at[0, :2 * SSD_HEADS].set(-jnp.exp(a_log.astype(F32)).reshape(-1))
    bias = jnp.zeros((1, LANES), F32).at[0, :2 * SSD_HEADS].set(dt_bias.astype(F32).reshape(-1))
    st_shape = (b, SSD_GROUPS, 2 * SSD_HEAD_DIM, SSD_STATE)
    const = lambda bi, c: (0, 0)
    st_spec = pl.BlockSpec((1,) + st_shape[1:], lambda bi, c: (bi, 0, 0, 0))
    yf, yb, sf, sb = pl.pallas_call(
        _ssd_kernel,
        out_shape=[jax.ShapeDtypeStruct((b, s, SSD_WIDTH), F32)] * 2
        + [jax.ShapeDtypeStruct(st_shape, F32)] * 2,
        grid=(b, nc),
        in_specs=[pl.BlockSpec((1, t, SSD_CONV_DIM), lambda bi, c: (bi, c, 0)),
                  pl.BlockSpec((1, t, LANES), lambda bi, c: (bi, c, 0)),
                  pl.BlockSpec((1, t, SSD_CONV_DIM), lambda bi, c: (bi, nc - 1 - c, 0)),
                  pl.BlockSpec((1, t, LANES), lambda bi, c: (bi, nc - 1 - c, 0)),
                  pl.BlockSpec((1, LANES), const), pl.BlockSpec((1, LANES), const),
                  st_spec, st_spec],
        out_specs=[pl.BlockSpec((1, t, SSD_WIDTH), lambda bi, c: (bi, c, 0)),
                   pl.BlockSpec((1, t, SSD_WIDTH), lambda bi, c: (bi, nc - 1 - c, 0)),
                   st_spec, st_spec],
        scratch_shapes=[pltpu.VMEM(st_shape[1:], F32), pltpu.VMEM(st_shape[1:], F32)],
        compiler_params=_cparams(("arbitrary", "arbitrary")),
        name="ssd_scan",
    )(xbc, dt, xbc, dt, avec, bias, h0f, h0b)
    return yf, yb, sf, sb


def _qk_prep_kernel(x_ref, g_ref, cos_ref, sin_ref, seg_ref, o_ref, *, rope, scale, kc):
    x = x_ref[0]
    sq = x * x
    hi = sq.astype(BF16)
    lo = (sq - hi.astype(F32)).astype(BF16)
    ssum = (jnp.dot(hi, seg_ref[...], preferred_element_type=F32)
            + jnp.dot(lo, seg_ref[...], preferred_element_type=F32))
    y = x * lax.rsqrt(ssum * (1.0 / DA_QK_DIM) + EPS) * g_ref[...]
    if rope:
        n = x.shape[1]
        lane = lax.broadcasted_iota(I32, x.shape, 1)
        first = (lane & (DA_QK_DIM - 1)) < DA_QK_DIM // 2
        partner = jnp.where(first, pltpu.roll(y, n - DA_QK_DIM // 2, 1), pltpu.roll(y, DA_QK_DIM // 2, 1))
        cosv = jnp.concatenate([cos_ref[...]] * (n // LANES), axis=1)
        sinv = jnp.concatenate([sin_ref[...]] * (n // LANES), axis=1)
        y = y * cosv + partner * sinv
    y = y * scale
    if kc:
        for h in range(DA_HEADS):
            head = y[:, h * 2 * DA_QK_DIM:(h + 1) * 2 * DA_QK_DIM]
            for cc in range(y.shape[0] // kc):
                t = head[cc * kc:(cc + 1) * kc, :].T.astype(BF16)
                o_ref[0, 2 * h, cc] = t[:DA_QK_DIM]
                o_ref[0, 2 * h + 1, cc] = t[DA_QK_DIM:]
    else:
        yb = y.astype(BF16)
        for j in range(2 * DA_HEADS):
            o_ref[0, j] = yb[:, j * DA_QK_DIM:(j + 1) * DA_QK_DIM]


def _rope_tables(s):
    pos = jnp.arange(s, dtype=I32)
    r = (pos // GRID_W).astype(F32)
    col = (pos % GRID_W).astype(F32)
    per_axis = DA_QK_DIM // 4
    inv = ROPE_BASE ** (-jnp.arange(per_axis, dtype=F32) / per_axis)
    ang = jnp.concatenate([r[:, None] * inv, col[:, None] * inv], axis=-1)
    cos, sin = jnp.cos(ang), jnp.sin(ang)
    cos64 = jnp.concatenate([cos, cos], axis=-1)
    sin64 = jnp.concatenate([-sin, sin], axis=-1)
    return jnp.concatenate([cos64, cos64], axis=-1), jnp.concatenate([sin64, sin64], axis=-1)


def _qk_prep(x, g, cos, sin, rope, scale, kc):
    b, s, n = x.shape
    ts = min(512, s)
    gfull = jnp.tile(g.astype(F32), n // DA_QK_DIM).reshape(1, n)
    idx = jnp.arange(n) // DA_QK_DIM
    seg = (idx[:, None] == idx[None, :]).astype(BF16)
    if kc:
        out_shape = jax.ShapeDtypeStruct((b, 2 * DA_HEADS, s // kc, DA_QK_DIM, kc), BF16)
        out_spec = pl.BlockSpec((1, 2 * DA_HEADS, ts // kc, DA_QK_DIM, kc), lambda bi, i: (bi, 0, i, 0, 0))
    else:
        out_shape = jax.ShapeDtypeStruct((b, 2 * DA_HEADS, s, DA_QK_DIM), BF16)
        out_spec = pl.BlockSpec((1, 2 * DA_HEADS, ts, DA_QK_DIM), lambda bi, i: (bi, 0, i, 0))
    return pl.pallas_call(
        functools.partial(_qk_prep_kernel, rope=rope, scale=scale, kc=kc),
        out_shape=out_shape,
        grid=(b, s // ts),
        in_specs=[pl.BlockSpec((1, ts, n), lambda bi, i: (bi, i, 0)),
                  pl.BlockSpec((1, n), lambda bi, i: (0, 0)),
                  pl.BlockSpec((ts, LANES), lambda bi, i: (i, 0)),
                  pl.BlockSpec((ts, LANES), lambda bi, i: (i, 0)),
                  pl.BlockSpec((n, n), lambda bi, i: (0, 0))],
        out_specs=out_spec,
        compiler_params=_cparams(("arbitrary", "arbitrary")),
        name="qk_prep",
    )(x, gfull, cos, sin, seg)


def _attn_kernel(lam_ref, q_ref, *refs, nseg, out_scale):
    lam = lam_ref[0, 0]
    segs = [(refs[2 * j], refs[2 * j + 1]) for j in range(nseg)]
    g_ref, o_ref = refs[2 * nseg], refs[2 * nseg + 1]
    tq = q_ref.shape[2]
    m = [jnp.full((tq, LANES), NEG_BIG, F32)] * 2
    l = [jnp.zeros((tq, LANES), F32)] * 2
    acc = [jnp.zeros((tq, DA_V_DIM), F32)] * 2

    for k_ref, v_ref in segs:
        nchunk, kc = k_ref.shape[2], k_ref.shape[4]
        for c in range(nchunk):
            v = v_ref[0, c * kc:(c + 1) * kc, :]
            for sub in range(2):
                s = jnp.dot(q_ref[0, sub], k_ref[0, sub, c], preferred_element_type=F32)
                mx = s[:, :LANES]
                for j in range(1, kc // LANES):
                    mx = jnp.maximum(mx, s[:, j * LANES:(j + 1) * LANES])
                m_new = jnp.maximum(m[sub], jnp.max(mx, axis=-1, keepdims=True))
                alpha = jnp.exp2(m[sub] - m_new)
                p = jnp.exp2(s - jnp.concatenate([m_new] * (kc // LANES), axis=1))
                lsum = alpha * l[sub]
                for j in range(kc // LANES):
                    lsum = lsum + p[:, j * LANES:(j + 1) * LANES]
                l[sub] = lsum
                acc[sub] = alpha * acc[sub] + jnp.dot(p.astype(BF16), v, preferred_element_type=F32)
                m[sub] = m_new

    l1 = jnp.sum(l[0], axis=-1, keepdims=True)
    l2 = jnp.sum(l[1], axis=-1, keepdims=True)
    o = acc[0] * (1.0 / l1) - lam * (acc[1] * (1.0 / l2))
    ms = jnp.mean(o * o, axis=-1, keepdims=True)
    o_ref[0] = o * lax.rsqrt(ms + EPS) * g_ref[...] * out_scale


def _diff_attn(qp, segments, lam, g, lam_init):
    b, _, s, d = qp.shape
    tq = min(ATT_TQ, s)
    seg_specs, seg_args = [], []
    for kt, v in segments:
        nchunk, kc = kt.shape[2], kt.shape[4]
        seg_specs += [pl.BlockSpec((1, 2, nchunk, d, kc), lambda bi, h, i: (bi, h, 0, 0, 0)),
                      pl.BlockSpec((1, nchunk * kc, DA_V_DIM), lambda bi, h, i: (bi, 0, h))]
        seg_args += [kt, v]
    return pl.pallas_call(
        functools.partial(_attn_kernel, nseg=len(segments), out_scale=1.0 - lam_init),
        out_shape=jax.ShapeDtypeStruct((b, s, DA_WIDTH), F32),
        grid=(b, DA_HEADS, s // tq),
        in_specs=[pl.BlockSpec(memory_space=pltpu.SMEM),
                  pl.BlockSpec((1, 2, tq, d), lambda bi, h, i: (bi, h, i, 0))] + seg_specs
        + [pl.BlockSpec((1, DA_V_DIM), lambda bi, h, i: (0, 0))],
        out_specs=pl.BlockSpec((1, tq, DA_V_DIM), lambda bi, h, i: (bi, i, h)),
        compiler_params=_cparams(("arbitrary", "arbitrary", "arbitrary")),
        name="diff_attn",
    )(lam.reshape(1, 1).astype(F32), qp, *seg_args, g.reshape(1, DA_V_DIM).astype(F32))


def _gelu_tanh(x):
    return 0.5 * x * (1.0 + jnp.tanh(math.sqrt(2.0 / math.pi) * (x + 0.044715 * x * x * x)))


def _proj_out_kernel(x_ref, lg_ref, hf_ref, hb_ref, yf_ref, yb_ref, xs_ref, z_ref, da_ref,
                     dvec_ref, ng_ref, gate_ref, w_ref, o_ref):
    lru = _gelu_tanh(lg_ref[0]) * (hf_ref[0] + hb_ref[0])
    y = yf_ref[0] + yb_ref[0] + dvec_ref[...] * xs_ref[0]
    y = y * _silu(z_ref[0])
    ssd = y * lax.rsqrt(jnp.mean(y * y, axis=-1, keepdims=True) + EPS) * ng_ref[...]
    o = jnp.dot(lru.astype(BF16), w_ref[0:LRU_WIDTH, :], preferred_element_type=F32)
    o = o + jnp.dot(ssd.astype(BF16), w_ref[LRU_WIDTH:LRU_WIDTH + SSD_WIDTH, :], preferred_element_type=F32)
    o = o + jnp.dot(da_ref[0].astype(BF16), w_ref[LRU_WIDTH + SSD_WIDTH:, :], preferred_element_type=F32)
    o_ref[0] = x_ref[0] + gate_ref[0] * o


def _proj_out(x, lg, hf, hb, yf, yb, xbc, z, da, dvec, ng, gate, w):
    b, s, d = x.shape
    tm = min(PROJ_TM, s)
    tok = lambda wd: pl.BlockSpec((1, tm, wd), lambda bi, i: (bi, i, 0))
    const2 = lambda bi, i: (0, 0)
    return pl.pallas_call(
        _proj_out_kernel,
        out_shape=jax.ShapeDtypeStruct((b, s, d), F32),
        grid=(b, s // tm),
        in_specs=[tok(d), tok(LRU_WIDTH), tok(LRU_WIDTH), tok(LRU_WIDTH), tok(SSD_WIDTH),
                  tok(SSD_WIDTH), tok(SSD_WIDTH), tok(SSD_WIDTH), tok(DA_WIDTH),
                  pl.BlockSpec((1, SSD_WIDTH), const2), pl.BlockSpec((1, SSD_WIDTH), const2),
                  pl.BlockSpec((1, 1, d), lambda bi, i: (bi, 0, 0)),
                  pl.BlockSpec(w.shape, const2)],
        out_specs=tok(d),
        compiler_params=_cparams(("arbitrary", "arbitrary")),
        name="proj_out",
    )(x, lg, hf, hb, yf, yb, xbc, z, da, dvec, ng, gate, w)


def _route_kernel(x_ref, g_ref, sh_ref, sc_ref, rwh_ref, rwl_ref, rb_ref, c0_ref, h_ref, ri_ref, gt_ref,
                  n_ref, base_ref, cnt_ref, carry_s):
    i = pl.program_id(0)
    tm = x_ref.shape[0]

    @pl.when(i == 0)
    def _():
        carry_s[...] = c0_ref[...]

    h = _norm_mod(x_ref[...], g_ref[...], sh_ref[0], sc_ref[0])
    h_hi = h.astype(BF16)
    h_ref[...] = h_hi
    h_lo = (h - h_hi.astype(F32)).astype(BF16)
    logits = (jnp.dot(h_hi, rwh_ref[...], preferred_element_type=F32)
              + jnp.dot(h_hi, rwl_ref[...], preferred_element_type=F32)
              + jnp.dot(h_lo, rwh_ref[...], preferred_element_type=F32)) + rb_ref[...]
    lane = lax.broadcasted_iota(I32, logits.shape, 1)
    lane_f = lane.astype(F32)
    vals, idxs = [], []
    cur = logits
    for _ in range(TOP_K):
        m = jnp.max(cur, axis=-1, keepdims=True)
        idx = jnp.min(jnp.where(cur >= m, lane_f, float(LANES)), axis=-1, keepdims=True).astype(I32)
        vals.append(m)
        idxs.append(idx)
        cur = jnp.where(lane == idx, 2.0 * NEG_BIG, cur)
    es = [jnp.exp(v - vals[0]) for v in vals]
    inv = 1.0 / (es[0] + es[1] + es[2] + es[3])
    onehot = jnp.zeros(logits.shape, F32)
    for idx in idxs:
        onehot = onehot + jnp.where(lane == idx, 1.0, 0.0)
    ri = lax.broadcasted_iota(I32, (tm, tm), 0)
    ci = lax.broadcasted_iota(I32, (tm, tm), 1)
    strict = jnp.where(ci < ri, 1.0, 0.0).astype(BF16)
    before = jnp.dot(strict, onehot.astype(BF16), preferred_element_type=F32)
    ri_out = jnp.zeros(logits.shape, I32)
    gt_out = jnp.zeros(logits.shape, F32)
    for k in range(TOP_K):
        rank = jnp.sum(jnp.where(lane == idxs[k], before, 0.0), axis=-1, keepdims=True)
        ri_out = ri_out + jnp.where(lane == k, idxs[k], 0) + jnp.where(lane == TOP_K + k, rank.astype(I32), 0)
        gt_out = gt_out + jnp.where(lane == k, es[k] * inv, 0.0)
    ri_ref[...] = ri_out
    gt_ref[...] = gt_out
    n_tile = jnp.sum(onehot, axis=0, keepdims=True).astype(I32)
    n_ref[0] = n_tile
    base_ref[0] = carry_s[...]
    carry_s[...] = carry_s[...] + _round_up8(n_tile)
    cnt_ref[...] = carry_s[...]


def _route(x, g, sh, sc, tiles_per_mod, rw, rb, counts0):
    t, d = x.shape
    tm = ROUTE_TM
    tile = lambda wd: pl.BlockSpec((tm, wd), lambda i: (i, 0))
    const = lambda i: (0, 0)
    mod = pl.BlockSpec((1, 1, d), lambda i: (i // tiles_per_mod, 0, 0))
    nt = t // tm
    per_tile = pl.BlockSpec((1, 1, LANES), lambda i: (i, 0, 0))
    return pl.pallas_call(
        _route_kernel,
        out_shape=[jax.ShapeDtypeStruct((t, d), BF16), jax.ShapeDtypeStruct((t, LANES), I32),
                   jax.ShapeDtypeStruct((t, LANES), F32), jax.ShapeDtypeStruct((nt, 1, LANES), I32),
                   jax.ShapeDtypeStruct((nt, 1, LANES), I32), jax.ShapeDtypeStruct((1, LANES), I32)],
        grid=(nt,),
        in_specs=[tile(d), pl.BlockSpec((1, d), const), mod, mod,
                  pl.BlockSpec((d, LANES), const), pl.BlockSpec((d, LANES), const),
                  pl.BlockSpec((1, LANES), const), pl.BlockSpec((1, LANES), const)],
        out_specs=[tile(d), tile(LANES), tile(LANES), per_tile, per_tile, pl.BlockSpec((1, LANES), const)],
        scratch_shapes=[pltpu.VMEM((1, LANES), I32)],
        compiler_params=_cparams(("arbitrary",)),
        name="moe_route",
    )(x, g.reshape(1, d), sh, sc, rw[0], rw[1], rb, counts0)


MOE_STAGE = -(-(ROUTE_TM * TOP_K + N_EXPERTS * (SUBLANES - 1)) // 256) * 256
_TAB_N, _TAB_DST, _TAB_OFF, _TAB_TOTAL = 0, N_EXPERTS, 2 * N_EXPERTS, 3 * N_EXPERTS


def _chunk_copies(tab_ref, stage_ref, rows_hbm, sem, to_hbm):
    def expert(e, carry):
        dst = tab_ref[0, 0, _TAB_DST + e]
        off = tab_ref[0, 0, _TAB_OFF + e]

        def chunk(j, c):
            s = stage_ref.at[pl.ds(pl.multiple_of(off + j * SUBLANES, SUBLANES), SUBLANES), :]
            r = rows_hbm.at[pl.ds(pl.multiple_of(dst + j * SUBLANES, SUBLANES), SUBLANES), :]
            if to_hbm:
                pltpu.make_async_copy(s, r, sem).start()
            else:
                pltpu.make_async_copy(r, s, sem).start()
            return c
        return lax.fori_loop(0, tab_ref[0, 0, _TAB_N + e], chunk, carry)
    lax.fori_loop(0, N_EXPERTS, expert, 0)


def _chunk_waits(tab_ref, stage_ref, rows_hbm, sem):
    def one(j, c):
        pltpu.make_async_copy(rows_hbm.at[pl.ds(0, SUBLANES), :], stage_ref.at[pl.ds(0, SUBLANES), :], sem).wait()
        return c
    lax.fori_loop(0, tab_ref[0, 0, _TAB_TOTAL], one, 0)


def _dispatch_kernel(tab_ref, tail_ref, sloc_ref, *refs, tiles):
    h_refs = refs[:len(tiles)]
    xs_out, stage_s, zero_s, sem = refs[len(tiles):]
    i = pl.program_id(0)
    n_steps = sum(tiles)
    row = lax.broadcasted_iota(I32, (MOE_STAGE, ROUTE_TM), 0)
    perm = jnp.zeros((MOE_STAGE, ROUTE_TM), F32)
    for k in range(TOP_K):
        perm = perm + jnp.where(row == sloc_ref[0, k:k + 1, :], 1.0, 0.0)
    perm = perm.astype(BF16)
    off = 0
    for h_ref, nt in zip(h_refs, tiles):
        @pl.when((i >= off) & (i < off + nt))
        def _(h_ref=h_ref):
            stage_s[...] = jnp.dot(perm, h_ref[...], preferred_element_type=F32)
        off += nt
    _chunk_copies(tab_ref, stage_s, xs_out, sem, True)
    _chunk_waits(tab_ref, stage_s, xs_out, sem)

    @pl.when(i == n_steps - 1)
    def _():
        zero_s[...] = jnp.zeros(zero_s.shape, F32)
        zr = zero_s.shape[0]
        n_big, n_small, start = tail_ref[0], tail_ref[1], tail_ref[2]

        def big(j, c):
            pltpu.make_async_copy(zero_s, xs_out.at[pl.ds(pl.multiple_of(start + j * zr, SUBLANES), zr), :], sem).start()
            return c

        def small(j, c):
            r0 = pl.multiple_of(start + n_big * zr + j * SUBLANES, SUBLANES)
            pltpu.make_async_copy(zero_s.at[pl.ds(0, SUBLANES), :], xs_out.at[pl.ds(r0, SUBLANES), :], sem).start()
            return c

        def big_wait(j, c):
            pltpu.make_async_copy(zero_s, xs_out.at[pl.ds(0, zr), :], sem).wait()
            return c

        def small_wait(j, c):
            pltpu.make_async_copy(zero_s.at[pl.ds(0, SUBLANES), :], xs_out.at[pl.ds(0, SUBLANES), :], sem).wait()
            return c

        lax.fori_loop(0, n_big, big, 0)
        lax.fori_loop(0, n_small, small, 0)
        lax.fori_loop(0, n_big, big_wait, 0)
        lax.fori_loop(0, n_small, small_wait, 0)


def _dispatch(hs, tabs, slocs, tail, n_max):
    d = hs[0].shape[1]
    tm = ROUTE_TM
    tiles = tuple(h.shape[0] // tm for h in hs)
    specs, off = [], 0
    for nt in tiles:
        specs.append(pl.BlockSpec((tm, d), lambda i, off=off, nt=nt: (jnp.clip(i - off, 0, nt - 1), 0)))
        off += nt
    return pl.pallas_call(
        functools.partial(_dispatch_kernel, tiles=tiles),
        out_shape=jax.ShapeDtypeStruct((n_max, d), F32),
        grid=(sum(tiles),),
        in_specs=[pl.BlockSpec((1, 1, LANES), lambda i: (i, 0, 0), memory_space=pltpu.SMEM),
                  pl.BlockSpec(memory_space=pltpu.SMEM),
                  pl.BlockSpec((1, SUBLANES, tm), lambda i: (i, 0, 0))] + specs,
        out_specs=pl.BlockSpec(memory_space=pl.ANY),
        scratch_shapes=[pltpu.VMEM((MOE_STAGE, d), F32), pltpu.VMEM((MOE_TM, d), F32),
                        pltpu.SemaphoreType.DMA(())],
        compiler_params=_cparams(("arbitrary",)),
        name="moe_dispatch",
    )(tabs, tail, slocs, *hs)


def _gmm_kernel(vt_ref, ve_ref, lo_ref, hi_ref, first_ref, xs_ref, wg_ref, wu_ref, bg_ref, bu_ref,
                wd_ref, bd_ref, y_ref):
    del vt_ref, ve_ref
    v = pl.program_id(0)
    lo, hi = lo_ref[v], hi_ref[v]

    @pl.when(hi > lo)
    def _():
        x = xs_ref[...].astype(BF16)
        gate = jnp.dot(x, wg_ref[...], preferred_element_type=F32) + bg_ref[...]
        up = jnp.dot(x, wu_ref[...], preferred_element_type=F32) + bu_ref[...]
        gate = jnp.minimum(gate, SWIGLU_LIMIT)
        up = jnp.clip(up, -SWIGLU_LIMIT, SWIGLU_LIMIT)
        glu = gate * _sigmoid(SWIGLU_ALPHA * gate)
        act = ((up + 1.0) * glu).astype(BF16)
        down = lambda: jnp.dot(act, wd_ref[...], preferred_element_type=F32) + bd_ref[...]
        whole = (lo == 0) & (hi == xs_ref.shape[0])
        first = first_ref[v] > 0
        row = lax.broadcasted_iota(I32, y_ref.shape, 0)
        mine = (row >= lo) & (row < hi)

        @pl.when(whole)
        def _():
            y_ref[...] = down()

        @pl.when(jnp.logical_not(whole) & first)
        def _():
            y_ref[...] = jnp.where(mine, down(), 0.0)

        @pl.when(jnp.logical_not(whole) & jnp.logical_not(first))
        def _():
            y_ref[...] = jnp.where(mine, down(), y_ref[...])

    @pl.when((hi <= lo) & (first_ref[v] > 0))
    def _():
        y_ref[...] = jnp.zeros(y_ref.shape, F32)


def _gmm(xs, visits, wg, wu, bg, bu, wd, bd):
    n_rows, d = xs.shape
    de = wg.shape[2]
    tm = MOE_TM
    nv = visits[0].shape[0]
    ex = lambda shape: pl.BlockSpec((None,) + shape, lambda v, vt, ve, lo, hi, fi: (ve[v], 0, 0))
    rows = pl.BlockSpec((tm, d), lambda v, vt, ve, lo, hi, fi: (vt[v], 0))
    grid_spec = pltpu.PrefetchScalarGridSpec(
        num_scalar_prefetch=5,
        grid=(nv,),
        in_specs=[rows, ex((d, de)), ex((d, de)), ex((1, de)), ex((1, de)), ex((de, d)), ex((1, d))],
        out_specs=rows,
    )
    return pl.pallas_call(
        _gmm_kernel,
        out_shape=jax.ShapeDtypeStruct((n_rows, d), F32),
        grid_spec=grid_spec,
        compiler_params=_cparams(("arbitrary",)),
        name="moe_experts",
    )(*visits, xs, wg, wu, bg, bu, wd, bd)


def _visit_tables(cnt, n_rows):
    tm = MOE_TM
    nt = n_rows // tm
    nv = nt + N_EXPERTS - 1
    end = jnp.cumsum(cnt)
    start = end - cnt
    first_tile = start // tm
    last_tile = jnp.maximum(end - 1, 0) // tm
    nvis = jnp.where(cnt > 0, last_tile - first_tile + 1, 0)
    vend = jnp.cumsum(nvis)
    vstart = vend - nvis
    v = jnp.arange(nv, dtype=I32)
    e = jnp.minimum(jnp.sum((vend[None, :] <= v[:, None]).astype(I32), axis=1), N_EXPERTS - 1)
    n_used = vend[N_EXPERTS - 1]
    used = v < n_used
    tiles_used = (end[N_EXPERTS - 1] + tm - 1) // tm
    tile = jnp.where(used, first_tile[e] + v - vstart[e], jnp.minimum(tiles_used + v - n_used, nt - 1))
    lo = jnp.where(used, jnp.clip(start[e] - tile * tm, 0, tm), 0)
    hi = jnp.where(used, jnp.clip(end[e] - tile * tm, 0, tm), 0)
    prev = jnp.concatenate([jnp.full((1,), -1, I32), tile[:-1].astype(I32)])
    first = (tile != prev).astype(I32)
    return tuple(a.astype(I32) for a in (tile, e, lo, hi, first))


_SPLIT_W = 2 * LANES


def _split_gu_kernel(w_ref, p_ref, g_ref, u_ref):
    w = w_ref[...].astype(BF16)
    for j in range(w.shape[1] // _SPLIT_W):
        r = jnp.dot(w[:, j * _SPLIT_W:(j + 1) * _SPLIT_W], p_ref[...], preferred_element_type=F32)
        g_ref[:, j * LANES:(j + 1) * LANES] = r[:, :LANES].astype(BF16)
        u_ref[:, j * LANES:(j + 1) * LANES] = r[:, LANES:].astype(BF16)


def _split_gu(w_gu, layer):
    _, e, d, n2 = w_gu.shape
    tn = 2 * _SPLIT_W
    src = jnp.arange(_SPLIT_W)
    dst = jnp.where(src % 2 == 0, src // 2, LANES + src // 2)
    perm = (dst[:, None] == jnp.arange(_SPLIT_W)[None, :]).astype(BF16)
    return pl.pallas_call(
        _split_gu_kernel,
        out_shape=[jax.ShapeDtypeStruct((e, d, n2 // 2), BF16)] * 2,
        grid=(e, n2 // tn),
        in_specs=[pl.BlockSpec((None, None, d, tn), lambda ei, j: (layer, ei, 0, j)),
                  pl.BlockSpec((_SPLIT_W, _SPLIT_W), lambda ei, j: (0, 0))],
        out_specs=[pl.BlockSpec((None, d, tn // 2), lambda ei, j: (ei, 0, j))] * 2,
        compiler_params=_cparams(("arbitrary", "arbitrary")),
        name="split_gate_up",
    )(w_gu, perm)


def _combine_kernel(tcur_ref, tnext_ref, sloc_ref, gt_ref, x_ref, g2_ref, y_hbm, o_ref, stage_s, sems, *, n):
    i = pl.program_id(0)
    slot = i % 2

    @pl.when(i == 0)
    def _():
        stage_s[...] = jnp.zeros(stage_s.shape, F32)
        _chunk_copies(tcur_ref, stage_s.at[0], y_hbm, sems.at[0], False)

    @pl.when(i + 1 < n)
    def _():
        _chunk_copies(tnext_ref, stage_s.at[1 - slot], y_hbm, sems.at[1 - slot], False)

    _chunk_waits(tcur_ref, stage_s.at[slot], y_hbm, sems.at[slot])
    rows = stage_s[slot].astype(BF16)
    lane = lax.broadcasted_iota(I32, (x_ref.shape[0], MOE_STAGE), 1)
    gt = gt_ref[...]
    sloc = sloc_ref[...]
    gmat = jnp.zeros(lane.shape, F32)
    for k in range(TOP_K):
        gmat = gmat + jnp.where(lane == sloc[:, k:k + 1], gt[:, k:k + 1], 0.0)
    g_hi = gmat.astype(BF16)
    g_lo = (gmat - g_hi.astype(F32)).astype(BF16)
    f = (jnp.dot(g_hi, rows, preferred_element_type=F32)
         + jnp.dot(g_lo, rows, preferred_element_type=F32))
    o_ref[...] = x_ref[...] + g2_ref[0] * f


def _combine(y, tabs, sloc, gates, x, g2, tiles_per_mod):
    t, d = x.shape
    tm = ROUTE_TM
    nt = t // tm
    return pl.pallas_call(
        functools.partial(_combine_kernel, n=nt),
        out_shape=jax.ShapeDtypeStruct((t, d), F32),
        grid=(nt,),
        in_specs=[pl.BlockSpec((1, 1, LANES), lambda i: (i, 0, 0), memory_space=pltpu.SMEM),
                  pl.BlockSpec((1, 1, LANES), lambda i: (jnp.minimum(i + 1, nt - 1), 0, 0),
                               memory_space=pltpu.SMEM),
                  pl.BlockSpec((tm, LANES), lambda i: (i, 0)),
                  pl.BlockSpec((tm, LANES), lambda i: (i, 0)),
                  pl.BlockSpec((tm, d), lambda i: (i, 0)),
                  pl.BlockSpec((1, 1, d), lambda i: (i // tiles_per_mod, 0, 0)),
                  pl.BlockSpec(memory_space=pl.ANY)],
        out_specs=pl.BlockSpec((tm, d), lambda i: (i, 0)),
        scratch_shapes=[pltpu.VMEM((2, MOE_STAGE, d), F32), pltpu.SemaphoreType.DMA((2,))],
        compiler_params=_cparams(("arbitrary",)),
        name="moe_combine",
    )(tabs, tabs, sloc, gates, x, g2, y)


def _moe(streams, layer, g, router_w, router_b, w_gu_all, b_gu, w_down, b_down):
    d = streams[0][0].shape[1]
    tm = ROUTE_TM
    rw_f = jnp.zeros((d, LANES), F32).at[:, :N_EXPERTS].set(router_w.astype(F32))
    rw_hi = rw_f.astype(BF16)
    rw = (rw_hi, (rw_f - rw_hi.astype(F32)).astype(BF16))
    rb = jnp.full((1, LANES), NEG_BIG, F32).at[0, :N_EXPERTS].set(router_b.astype(F32))
    counts = jnp.zeros((1, LANES), I32)
    routed = []
    for x_tok, sh, sc, _, tpm in streams:
        h, route_i, gates, n_tile, seg_base, counts = _route(x_tok, g, sh, sc, tpm, rw, rb, counts)
        routed.append((h, route_i, gates, n_tile, seg_base))
    n_tiles = sum(x_tok.shape[0] for x_tok, *_ in streams) // tm
    n_max = n_tiles * tm * TOP_K + n_tiles * N_EXPERTS * (SUBLANES - 1)
    n_max = (n_max + MOE_TM - 1) // MOE_TM * MOE_TM
    cnt = counts[0, :N_EXPERTS]
    total = jnp.sum(cnt)
    base = jnp.cumsum(cnt) - cnt
    tabs, slocs_t, slocs = [], [], []
    for h, route_i, _, n_tile, seg_base in routed:
        nt = h.shape[0] // tm
        n_te = n_tile[:, 0, :N_EXPERTS]
        pad = _round_up8(n_te)
        off = jnp.cumsum(pad, axis=1) - pad
        dst = base[None, :] + seg_base[:, 0, :N_EXPERTS]
        nch = pad // SUBLANES
        tab = jnp.concatenate([nch, dst, off, jnp.sum(nch, axis=1, keepdims=True),
                               jnp.zeros((nt, LANES - 3 * N_EXPERTS - 1), I32)], axis=1)
        tabs.append(tab.astype(I32).reshape(nt, 1, LANES))
        e = route_i[:, :TOP_K].reshape(nt, tm, TOP_K)
        rank = route_i[:, TOP_K:2 * TOP_K].reshape(nt, tm, TOP_K)
        sloc = jnp.take_along_axis(off[:, None, :], e, axis=2) + rank
        slocs_t.append(jnp.concatenate([sloc.transpose(0, 2, 1),
                                        jnp.full((nt, SUBLANES - TOP_K, tm), -1, I32)], axis=1).astype(I32))
        slocs.append(jnp.concatenate([sloc.reshape(nt * tm, TOP_K),
                                      jnp.full((nt * tm, LANES - TOP_K), -1, I32)], axis=1).astype(I32))
    free = n_max - total
    tail = jnp.stack([free // MOE_TM, (free % MOE_TM) // SUBLANES, total]).astype(I32)
    xs = _dispatch([r[0] for r in routed], jnp.concatenate(tabs, axis=0), jnp.concatenate(slocs_t, axis=0),
                   tail, n_max)
    wg, wu = _split_gu(w_gu_all, layer)
    bg = b_gu[:, None, 0::2].astype(F32)
    bu = b_gu[:, None, 1::2].astype(F32)
    y = _gmm(xs, _visit_tables(cnt, n_max), wg, wu, bg, bu, w_down.astype(BF16),
             b_down[:, None, :].astype(F32))
    return [_combine(y, tab, sloc, r[2], x_tok, g2, tpm)
            for (x_tok, _, _, g2, tpm), r, tab, sloc in zip(streams, routed, tabs, slocs)]


def kernel(x, c, ctx, c_ctx, w_mod, b_mod, norm1_g, norm2_g, w_in, w_out, lru_conv_w, lru_conv_b, lru_wa, lru_ba, lru_wx, lru_bx, lru_lam, ssd_conv_w, ssd_conv_b, ssd_a_log, ssd_dt_bias, ssd_d, ssd_norm_g, da_q_norm, da_k_norm, da_lam_q, da_lam_k, da_subln_g, router_w, router_b, exp_w_gu, exp_b_gu, exp_w_down, exp_b_down):
    b, s, d = x.shape
    n_ctx = ctx.shape[1]
    depth = w_mod.shape[0]
    cos, sin = _rope_tables(s)
    x_lat, x_ctx = x, ctx
    mod_rows = jnp.zeros((2 * SUBLANES, d), F32).at[:b].set(c).at[b].set(c_ctx)

    for l in range(depth):
        need_ctx = l < depth - 1
        lam_init = 0.8 - 0.6 * math.exp(-0.3 * l)
        mod = _modulation(mod_rows, w_mod[l], b_mod[l])
        m_lat = mod[:b].reshape(b, 1, 6, d)
        m_ctx = jnp.broadcast_to(mod[b].reshape(1, 1, 6, d), (b, 1, 6, d))
        part = lambda m, j: m[:, :, j, :]

        w_in_l = _prep_w_in(w_in[l])
        p_lat = _proj_in(x_lat, norm1_g[l], part(m_lat, 0), part(m_lat, 1), w_in_l)
        p_ctx = _proj_in(x_ctx, norm1_g[l], part(m_ctx, 0), part(m_ctx, 1), w_in_l)

        zl = jnp.zeros((b, LRU_WIDTH), F32)
        lru_args = (lru_wa[l], lru_ba[l], lru_wx[l], lru_bx[l], lru_lam[l])
        xc_ctx = _dwconv(p_ctx["lx"], lru_conv_w[l], lru_conv_b[l], act=False)
        hf_c, hb_c = _lru_scan(xc_ctx, *lru_args, zl, zl)
        xc_lat = _dwconv(p_lat["lx"], lru_conv_w[l], lru_conv_b[l], act=False)
        hf_l, hb_l = _lru_scan(xc_lat, *lru_args, hf_c[:, -1], hb_c[:, 0])

        zs = jnp.zeros((b, SSD_GROUPS, 2 * SSD_HEAD_DIM, SSD_STATE), F32)
        xbc_ctx = _dwconv(p_ctx["xbc"], ssd_conv_w[l], ssd_conv_b[l], act=True)
        yf_c, yb_c, sf_c, sb_c = _ssd_scan(xbc_ctx, p_ctx["dt"], ssd_a_log[l], ssd_dt_bias[l], zs, zs)
        xbc_lat = _dwconv(p_lat["xbc"], ssd_conv_w[l], ssd_conv_b[l], act=True)
        yf_l, yb_l, _, _ = _ssd_scan(xbc_lat, p_lat["dt"], ssd_a_log[l], ssd_dt_bias[l], sf_c, sb_c)

        lq = da_lam_q[l].astype(F32)
        lk = da_lam_k[l].astype(F32)
        lam = jnp.exp(jnp.sum(lq[0] * lk[0])) - jnp.exp(jnp.sum(lq[1] * lk[1])) + lam_init
        kc = _qk_prep(p_ctx["k"], da_k_norm[l], cos, sin, rope=False, scale=1.0, kc=min(ATT_KC, n_ctx))
        kl = _qk_prep(p_lat["k"], da_k_norm[l], cos, sin, rope=True, scale=1.0, kc=min(ATT_KC, s))
        ql = _qk_prep(p_lat["q"], da_q_norm[l], cos, sin, rope=True, scale=DA_SCALE * LOG2E, kc=0)
        da_l = _diff_attn(ql, [(kc, p_ctx["v"]), (kl, p_lat["v"])], lam, da_subln_g[l], lam_init)

        dvec = jnp.repeat(ssd_d[l].astype(F32), SSD_HEAD_DIM).reshape(1, SSD_WIDTH)
        ng = ssd_norm_g[l].astype(F32).reshape(1, SSD_WIDTH)
        w_out_l = w_out[l].astype(BF16)
        if need_ctx:
            qc = _qk_prep(p_ctx["q"], da_q_norm[l], cos, sin, rope=False, scale=DA_SCALE * LOG2E, kc=0)
            da_c = _diff_attn(qc, [(kc, p_ctx["v"])], lam, da_subln_g[l], lam_init)
            x_ctx = _proj_out(x_ctx, p_ctx["lg"], hf_c, hb_c, yf_c, yb_c, xbc_ctx, p_ctx["z"], da_c,
                              dvec, ng, part(m_ctx, 2), w_out_l)
        x_lat = _proj_out(x_lat, p_lat["lg"], hf_l, hb_l, yf_l, yb_l, xbc_lat, p_lat["z"], da_l,
                          dvec, ng, part(m_lat, 2), w_out_l)

        moe_w = (router_w[l], router_b[l], exp_w_gu, exp_b_gu[l], exp_w_down[l], exp_b_down[l])
        streams = [(x_lat.reshape(-1, d), part(m_lat, 3), part(m_lat, 4), part(m_lat, 5), s // ROUTE_TM)]
        if need_ctx:
            streams.append((x_ctx.reshape(-1, d), part(m_ctx, 3), part(m_ctx, 4), part(m_ctx, 5), n_ctx // ROUTE_TM))
        outs = _moe(streams, l, norm2_g[l], *moe_w)
        x_lat = outs[0].reshape(b, s, d)
        if need_ctx:
            x_ctx = outs[1].reshape(b, n_ctx, d)
    return x_lat
```

```python
import functools
import math

import jax
import jax.numpy as jnp
from jax import lax
from jax.experimental import pallas as pl
from jax.experimental.pallas import tpu as pltpu

F32 = jnp.float32
BF16 = jnp.bfloat16
I32 = jnp.int32

GRID_W = 64
EPS = 1e-6
CONV_K = 4
LRU_HEADS = 4
LRU_HEAD_DIM = 64
LRU_WIDTH = 256
LRU_C = 8.0
SSD_HEADS = 4
SSD_HEAD_DIM = 64
SSD_WIDTH = 256
SSD_GROUPS = 2
SSD_STATE = 128
SSD_CHUNK = 128
SSD_CONV_DIM = 768
DA_HEADS = 4
DA_QK_DIM = 64
DA_V_DIM = 128
DA_QK_WIDTH = 512
DA_WIDTH = 512
DA_SCALE = DA_QK_DIM ** -0.5
ROPE_BASE = 10000.0
N_EXPERTS = 32
TOP_K = 4
SWIGLU_ALPHA = 1.702
SWIGLU_LIMIT = 7.0

LANES = 128
SUBLANES = 8
VMEM_LIMIT = 56 * 1024 * 1024

PROJ_TM = 512
ATT_TQ = 512
ATT_KC = 256
LOG2E = math.log2(math.e)
LRU_TL = 512
SSD_STEP_CHUNKS = 4
ROUTE_TM = 256
MOE_TM = 512
NEG_BIG = -1e30


def _cparams(sem):
    return pltpu.CompilerParams(dimension_semantics=sem, vmem_limit_bytes=VMEM_LIMIT)


def _sigmoid(x):
    return 0.5 * jnp.tanh(0.5 * x) + 0.5


def _silu(x):
    return x * _sigmoid(x)


def _softplus(x):
    return jnp.maximum(x, 0.0) + jnp.log(1.0 + jnp.exp(-jnp.abs(x)))


def _round_up8(n):
    return ((n + (SUBLANES - 1)) >> 3) << 3


def _mod_kernel(c_ref, w_ref, b_ref, o_ref):
    a = _silu(c_ref[...])
    o_ref[...] = jnp.dot(a, w_ref[...], preferred_element_type=F32,
                         precision=lax.Precision.HIGHEST) + b_ref[...]


def _modulation(rows, w, b):
    m, d = rows.shape
    n = w.shape[1]
    tn = 512
    return pl.pallas_call(
        _mod_kernel,
        out_shape=jax.ShapeDtypeStruct((m, n), F32),
        grid=(n // tn,),
        in_specs=[pl.BlockSpec((m, d), lambda j: (0, 0)),
                  pl.BlockSpec((d, tn), lambda j: (0, j)),
                  pl.BlockSpec((1, tn), lambda j: (0, j))],
        out_specs=pl.BlockSpec((m, tn), lambda j: (0, j)),
        compiler_params=_cparams(("arbitrary",)),
        name="modulation",
    )(rows, w, b.reshape(1, n))


def _norm_mod(x, g, sh, sc):
    ms = jnp.mean(x * x, axis=-1, keepdims=True)
    y = x * lax.rsqrt(ms + EPS) * g
    return y * (1.0 + sc) + sh


def _proj_in_kernel(x_ref, g_ref, sh_ref, sc_ref, w_ref, *o_refs, splits):
    h = _norm_mod(x_ref[0], g_ref[...], sh_ref[0], sc_ref[0]).astype(BF16)
    for o_ref, (lo, hi) in zip(o_refs, splits):
        o_ref[0] = jnp.dot(h, w_ref[:, lo:hi], preferred_element_type=F32).astype(o_ref.dtype)


_IN_GROUPS = (("lg", 256, F32), ("lx", 256, F32), ("z", 256, F32), ("xbc", 768, F32),
              ("dt", 128, F32), ("q", 512, F32), ("k", 512, F32), ("v", 512, BF16))


def _prep_w_in(w_in):
    d = w_in.shape[0]
    lo = 2 * LRU_WIDTH + SSD_WIDTH + SSD_CONV_DIM
    hi = lo + 2 * SSD_HEADS
    dt = jnp.concatenate([w_in[:, lo:hi], jnp.zeros((d, LANES - (hi - lo)), w_in.dtype)], axis=1)
    w = jnp.concatenate([w_in[:, :lo], dt, w_in[:, hi:]], axis=1)
    return w.astype(BF16)


def _proj_in(x, g, sh, sc, w):
    b, s, d = x.shape
    tm = min(PROJ_TM, s)
    splits, off = [], 0
    for _, wd, _ in _IN_GROUPS:
        splits.append((off, off + wd))
        off += wd
    out_shape = [jax.ShapeDtypeStruct((b, s, wd), dt) for _, wd, dt in _IN_GROUPS]
    out_specs = [pl.BlockSpec((1, tm, wd), lambda bi, i: (bi, i, 0)) for _, wd, _ in _IN_GROUPS]
    outs = pl.pallas_call(
        functools.partial(_proj_in_kernel, splits=tuple(splits)),
        out_shape=out_shape,
        grid=(b, s // tm),
        in_specs=[pl.BlockSpec((1, tm, d), lambda bi, i: (bi, i, 0)),
                  pl.BlockSpec((1, d), lambda bi, i: (0, 0)),
                  pl.BlockSpec((1, 1, d), lambda bi, i: (bi, 0, 0)),
                  pl.BlockSpec((1, 1, d), lambda bi, i: (bi, 0, 0)),
                  pl.BlockSpec((d, off), lambda bi, i: (0, 0))],
        out_specs=out_specs,
        compiler_params=_cparams(("arbitrary", "arbitrary")),
        name="proj_in",
    )(x, g.reshape(1, d), sh, sc, w)
    return {name: o for (name, _, _), o in zip(_IN_GROUPS, outs)}


def _dwconv_kernel(x_ref, w_ref, b_ref, o_ref, *, act):
    x = x_ref[0]
    s = x.shape[0]
    row = lax.broadcasted_iota(I32, x.shape, 0)
    acc = x * w_ref[2:3, :] + b_ref[...]
    acc = acc + jnp.where(row >= 2, pltpu.roll(x, 2, 0), 0.0) * w_ref[0:1, :]
    acc = acc + jnp.where(row >= 1, pltpu.roll(x, 1, 0), 0.0) * w_ref[1:2, :]
    acc = acc + jnp.where(row < s - 1, pltpu.roll(x, s - 1, 0), 0.0) * w_ref[3:4, :]
    if act:
        acc = _silu(acc)
    o_ref[0] = acc


def _dwconv(x, w, bias, act):
    b, s, c = x.shape
    tc = 256
    return pl.pallas_call(
        functools.partial(_dwconv_kernel, act=act),
        out_shape=jax.ShapeDtypeStruct((b, s, c), F32),
        grid=(b, c // tc),
        in_specs=[pl.BlockSpec((1, s, tc), lambda bi, j: (bi, 0, j)),
                  pl.BlockSpec((CONV_K, tc), lambda bi, j: (0, j)),
                  pl.BlockSpec((1, tc), lambda bi, j: (0, j))],
        out_specs=pl.BlockSpec((1, s, tc), lambda bi, j: (bi, 0, j)),
        compiler_params=_cparams(("arbitrary", "arbitrary")),
        name="dwconv",
    )(x, w, bias.reshape(1, c))


def _lru_coeffs(x, w_ref, b_ref, sp_ref, a_s, u_s):
    g = jnp.dot(x.astype(BF16), w_ref[...], preferred_element_type=F32) + b_ref[...]
    r = _sigmoid(g[:, :LRU_WIDTH])
    i = _sigmoid(g[:, LRU_WIDTH:])
    a = jnp.exp(-LRU_C * r * sp_ref[...])
    a_s[...] = a
    u_s[...] = jnp.sqrt(1.0 - a * a) * (i * x)


def _lru_kernel(xf_ref, xb_ref, wf_ref, wb_ref, bf_ref, bb_ref, spf_ref, spb_ref,
                h0f_ref, h0b_ref, hf_ref, hb_ref, af_s, uf_s, ab_s, ub_s, cf_s, cb_s):
    c = pl.program_id(1)
    tl = xf_ref.shape[1]
    ngroups = tl // SUBLANES
    shape = (SUBLANES, LRU_WIDTH)
    row = lax.broadcasted_iota(I32, shape, 0)

    @pl.when(c == 0)
    def _():
        cf_s[...] = jnp.broadcast_to(h0f_ref[0], shape)
        cb_s[...] = jnp.broadcast_to(h0b_ref[0], shape)

    _lru_coeffs(xf_ref[0], wf_ref, bf_ref, spf_ref, af_s, uf_s)
    _lru_coeffs(xb_ref[0], wb_ref, bb_ref, spb_ref, ab_s, ub_s)

    def group_scan(a, u, carry, reverse):
        for sh in (1, 2, 4):
            keep = (row < SUBLANES - sh) if reverse else (row >= sh)
            amt = SUBLANES - sh if reverse else sh
            a_sh = jnp.where(keep, pltpu.roll(a, amt, 0), 1.0)
            u_sh = jnp.where(keep, pltpu.roll(u, amt, 0), 0.0)
            u = a * u_sh + u
            a = a * a_sh
        return a * carry + u

    def step(gj, carry):
        cf, cb = carry
        slf = pl.ds(pl.multiple_of(gj * SUBLANES, SUBLANES), SUBLANES)
        slb = pl.ds(pl.multiple_of((ngroups - 1 - gj) * SUBLANES, SUBLANES), SUBLANES)
        hf = group_scan(af_s[slf, :], uf_s[slf, :], cf, False)
        hb = group_scan(ab_s[slb, :], ub_s[slb, :], cb, True)
        hf_ref[0, slf, :] = hf
        hb_ref[0, slb, :] = hb
        return (jnp.broadcast_to(hf[SUBLANES - 1:SUBLANES, :], shape),
                jnp.broadcast_to(hb[0:1, :], shape))

    cf, cb = lax.fori_loop(0, ngroups, step, (cf_s[...], cb_s[...]), unroll=2)
    cf_s[...] = cf
    cb_s[...] = cb


def _lru_gate_weights(wa, wx):
    def dense(wh):
        m = jnp.zeros((LRU_WIDTH, LRU_WIDTH), wh.dtype)
        for h in range(LRU_HEADS):
            lo = h * LRU_HEAD_DIM
            m = m.at[lo:lo + LRU_HEAD_DIM, lo:lo + LRU_HEAD_DIM].set(wh[h])
        return m
    return jnp.concatenate([dense(wa), dense(wx)], axis=1).astype(BF16)


def _lru_scan(xc, wa, ba, wx, bx, lam, h0f, h0b):
    b, s, w = xc.shape
    tl = min(LRU_TL, s)
    nc = s // tl
    wf = _lru_gate_weights(wa[0], wx[0])
    wb = _lru_gate_weights(wa[1], wx[1])
    bf = jnp.concatenate([ba[0], bx[0]]).reshape(1, 2 * w)
    bb = jnp.concatenate([ba[1], bx[1]]).reshape(1, 2 * w)
    sp = jax.nn.softplus(-lam.astype(F32))
    const = lambda bi, c: (0, 0)
    hf, hb = pl.pallas_call(
        _lru_kernel,
        out_shape=[jax.ShapeDtypeStruct((b, s, w), F32)] * 2,
        grid=(b, nc),
        in_specs=[pl.BlockSpec((1, tl, w), lambda bi, c: (bi, c, 0)),
                  pl.BlockSpec((1, tl, w), lambda bi, c: (bi, nc - 1 - c, 0)),
                  pl.BlockSpec((w, 2 * w), const), pl.BlockSpec((w, 2 * w), const),
                  pl.BlockSpec((1, 2 * w), const), pl.BlockSpec((1, 2 * w), const),
                  pl.BlockSpec((1, w), const), pl.BlockSpec((1, w), const),
                  pl.BlockSpec((1, 1, w), lambda bi, c: (bi, 0, 0)),
                  pl.BlockSpec((1, 1, w), lambda bi, c: (bi, 0, 0))],
        out_specs=[pl.BlockSpec((1, tl, w), lambda bi, c: (bi, c, 0)),
                   pl.BlockSpec((1, tl, w), lambda bi, c: (bi, nc - 1 - c, 0))],
        scratch_shapes=[pltpu.VMEM((tl, w), F32)] * 4
        + [pltpu.VMEM((SUBLANES, w), F32), pltpu.VMEM((SUBLANES, w), F32)],
        compiler_params=_cparams(("arbitrary", "arbitrary")),
        name="lru_scan",
    )(xc, xc, wf, wb, bf, bb, sp[0:1], sp[1:2], h0f.reshape(b, 1, w), h0b.reshape(b, 1, w))
    return hf, hb


def _ssd_direction(x_ref, dt_ref, row0, avec_ref, dtb_ref, state_s, y_ref, lane0, rev):
    t = SSD_CHUNK
    xbc = x_ref[0, row0:row0 + t, :]
    dtraw = dt_ref[0, row0:row0 + t, :]
    ri = lax.broadcasted_iota(I32, (t, t), 0)
    ci = lax.broadcasted_iota(I32, (t, t), 1)
    causal = (ci >= ri) if rev else (ci <= ri)
    cum_m = jnp.where(causal, 1.0, 0.0).astype(F32)

    dt = _softplus(dtraw + dtb_ref[...])
    a_all = dt * avec_ref[...]
    acs = jnp.dot(cum_m, a_all, preferred_element_type=F32, precision=lax.Precision.HIGHEST)
    acs_t = acs.T
    edge = 0 if rev else t - 1

    xs = xbc[:, :SSD_WIDTH]
    hrow = lax.broadcasted_iota(I32, (t, t), 0)
    for g in range(SSD_GROUPS):
        bm = xbc[:, SSD_WIDTH + g * SSD_STATE:SSD_WIDTH + (g + 1) * SSD_STATE]
        cm = xbc[:, SSD_WIDTH + (SSD_GROUPS + g) * SSD_STATE:SSD_WIDTH + (SSD_GROUPS + g + 1) * SSD_STATE]
        bm_b = bm.astype(BF16)
        cm_b = cm.astype(BF16)
        gmat = lax.dot_general(cm_b, bm_b, (((1,), (1,)), ((), ())), preferred_element_type=F32)
        s_prev = state_s[g]
        off_all = lax.dot_general(cm_b, s_prev.astype(BF16), (((1,), (1,)), ((), ())),
                                  preferred_element_type=F32)
        xd_parts, tot = [], []
        for hl in range(2):
            h = g * 2 + hl
            ln = lane0 + h
            col = jnp.broadcast_to(acs[:, ln:ln + 1], (t, t))
            rowv = acs_t[ln:ln + 1, :]
            lmat = jnp.where(causal, jnp.exp(col - rowv), 0.0)
            xh = xs[:, h * SSD_HEAD_DIM:(h + 1) * SSD_HEAD_DIM] * dt[:, ln:ln + 1]
            ydiag = jnp.dot((gmat * lmat).astype(BF16), xh.astype(BF16), preferred_element_type=F32)
            colh = col[:, :SSD_HEAD_DIM]
            yoff = off_all[:, hl * SSD_HEAD_DIM:(hl + 1) * SSD_HEAD_DIM] * jnp.exp(colh)
            y_ref[0, row0:row0 + t, h * SSD_HEAD_DIM:(h + 1) * SSD_HEAD_DIM] = ydiag + yoff
            total = acs[edge:edge + 1, ln:ln + 1]
            xd_parts.append(xh * jnp.exp(total - colh))
            tot.append(jnp.exp(total))
        xd = jnp.concatenate(xd_parts, axis=1)
        xd_t = xd.T
        st = jnp.dot(xd_t.astype(BF16), bm_b, preferred_element_type=F32)
        fac = jnp.where(hrow < SSD_HEAD_DIM, tot[0], tot[1])
        state_s[g] = fac * s_prev + st


def _ssd_kernel(xf_ref, dtf_ref, xb_ref, dtb_ref, avec_ref, bias_ref, h0f_ref, h0b_ref,
                yf_ref, yb_ref, sf_ref, sb_ref, stf_s, stb_s):
    c = pl.program_id(1)

    @pl.when(c == 0)
    def _():
        stf_s[...] = h0f_ref[0]
        stb_s[...] = h0b_ref[0]

    nsub = xf_ref.shape[1] // SSD_CHUNK
    for j in range(nsub):
        _ssd_direction(xf_ref, dtf_ref, j * SSD_CHUNK, avec_ref, bias_ref, stf_s, yf_ref, 0, False)
        _ssd_direction(xb_ref, dtb_ref, (nsub - 1 - j) * SSD_CHUNK, avec_ref, bias_ref, stb_s, yb_ref,
                       SSD_HEADS, True)
    sf_ref[0] = stf_s[...]
    sb_ref[0] = stb_s[...]


def _ssd_scan(xbc, dt, a_log, dt_bias, h0f, h0b):
    b, s, _ = xbc.shape
    t = min(SSD_STEP_CHUNKS * SSD_CHUNK, s)
    nc = s // t
    avec = jnp.zeros((1, LANES), F32).at[0, :2 * SSD_HEADS].set(-jnp.exp(a_log.astype(F32)).reshape(-1))
    bias = jnp.zeros((1, LANES), F32).at[0, :2 * SSD_HEADS].set(dt_bias.astype(F32).reshape(-1))
    st_shape = (b, SSD_GROUPS, 2 * SSD_HEAD_DIM, SSD_STATE)
    const = lambda bi, c: (0, 0)
    st_spec = pl.BlockSpec((1,) + st_shape[1:], lambda bi, c: (bi, 0, 0, 0))
    yf, yb, sf, sb = pl.pallas_call(
        _ssd_kernel,
        out_shape=[jax.ShapeDtypeStruct((b, s, SSD_WIDTH), F32)] * 2
        + [jax.ShapeDtypeStruct(st_shape, F32)] * 2,
        grid=(b, nc),
        in_specs=[pl.BlockSpec((1, t, SSD_CONV_DIM), lambda bi, c: (bi, c, 0)),
                  pl.BlockSpec((1, t, LANES), lambda bi, c: (bi, c, 0)),
                  pl.BlockSpec((1, t, SSD_CONV_DIM), lambda bi, c: (bi, nc - 1 - c, 0)),
                  pl.BlockSpec((1, t, LANES), lambda bi, c: (bi, nc - 1 - c, 0)),
                  pl.BlockSpec((1, LANES), const), pl.BlockSpec((1, LANES), const),
                  st_spec, st_spec],
        out_specs=[pl.BlockSpec((1, t, SSD_WIDTH), lambda bi, c: (bi, c, 0)),
                   pl.BlockSpec((1, t, SSD_WIDTH), lambda bi, c: (bi, nc - 1 - c, 0)),
                   st_spec, st_spec],
        scratch_shapes=[pltpu.VMEM(st_shape[1:], F32), pltpu.VMEM(st_shape[1:], F32)],
        compiler_params=_cparams(("arbitrary", "arbitrary")),
        name="ssd_scan",
    )(xbc, dt, xbc, dt, avec, bias, h0f, h0b)
    return yf, yb, sf, sb


def _qk_prep_kernel(x_ref, g_ref, cos_ref, sin_ref, seg_ref, o_ref, *, rope, scale, kc):
    x = x_ref[0]
    sq = x * x
    hi = sq.astype(BF16)
    lo = (sq - hi.astype(F32)).astype(BF16)
    ssum = (jnp.dot(hi, seg_ref[...], preferred_element_type=F32)
            + jnp.dot(lo, seg_ref[...], preferred_element_type=F32))
    y = x * lax.rsqrt(ssum * (1.0 / DA_QK_DIM) + EPS) * g_ref[...]
    if rope:
        n = x.shape[1]
        lane = lax.broadcasted_iota(I32, x.shape, 1)
        first = (lane & (DA_QK_DIM - 1)) < DA_QK_DIM // 2
        partner = jnp.where(first, pltpu.roll(y, n - DA_QK_DIM // 2, 1), pltpu.roll(y, DA_QK_DIM // 2, 1))
        cosv = jnp.concatenate([cos_ref[...]] * (n // LANES), axis=1)
        sinv = jnp.concatenate([sin_ref[...]] * (n // LANES), axis=1)
        y = y * cosv + partner * sinv
    y = y * scale
    if kc:
        for h in range(DA_HEADS):
            head = y[:, h * 2 * DA_QK_DIM:(h + 1) * 2 * DA_QK_DIM]
            for cc in range(y.shape[0] // kc):
                t = head[cc * kc:(cc + 1) * kc, :].T.astype(BF16)
                o_ref[0, 2 * h, cc] = t[:DA_QK_DIM]
                o_ref[0, 2 * h + 1, cc] = t[DA_QK_DIM:]
    else:
        yb = y.astype(BF16)
        for j in range(2 * DA_HEADS):
            o_ref[0, j] = yb[:, j * DA_QK_DIM:(j + 1) * DA_QK_DIM]


def _rope_tables(s):
    pos = jnp.arange(s, dtype=I32)
    r = (pos // GRID_W).astype(F32)
    col = (pos % GRID_W).astype(F32)
    per_axis = DA_QK_DIM // 4
    inv = ROPE_BASE ** (-jnp.arange(per_axis, dtype=F32) / per_axis)
    ang = jnp.concatenate([r[:, None] * inv, col[:, None] * inv], axis=-1)
    cos, sin = jnp.cos(ang), jnp.sin(ang)
    cos64 = jnp.concatenate([cos, cos], axis=-1)
    sin64 = jnp.concatenate([-sin, sin], axis=-1)
    return jnp.concatenate([cos64, cos64], axis=-1), jnp.concatenate([sin64, sin64], axis=-1)


def _qk_prep(x, g, cos, sin, rope, scale, kc):
    b, s, n = x.shape
    ts = min(512, s)
    gfull = jnp.tile(g.astype(F32), n // DA_QK_DIM).reshape(1, n)
    idx = jnp.arange(n) // DA_QK_DIM
    seg = (idx[:, None] == idx[None, :]).astype(BF16)
    if kc:
        out_shape = jax.ShapeDtypeStruct((b, 2 * DA_HEADS, s // kc, DA_QK_DIM, kc), BF16)
        out_spec = pl.BlockSpec((1, 2 * DA_HEADS, ts // kc, DA_QK_DIM, kc), lambda bi, i: (bi, 0, i, 0, 0))
    else:
        out_shape = jax.ShapeDtypeStruct((b, 2 * DA_HEADS, s, DA_QK_DIM), BF16)
        out_spec = pl.BlockSpec((1, 2 * DA_HEADS, ts, DA_QK_DIM), lambda bi, i: (bi, 0, i, 0))
    return pl.pallas_call(
        functools.partial(_qk_prep_kernel, rope=rope, scale=scale, kc=kc),
        out_shape=out_shape,
        grid=(b, s // ts),
        in_specs=[pl.BlockSpec((1, ts, n), lambda bi, i: (bi, i, 0)),
                  pl.BlockSpec((1, n), lambda bi, i: (0, 0)),
                  pl.BlockSpec((ts, LANES), lambda bi, i: (i, 0)),
                  pl.BlockSpec((ts, LANES), lambda bi, i: (i, 0)),
                  pl.BlockSpec((n, n), lambda bi, i: (0, 0))],
        out_specs=out_spec,
        compiler_params=_cparams(("arbitrary", "arbitrary")),
        name="qk_prep",
    )(x, gfull, cos, sin, seg)


def _attn_kernel(lam_ref, q_ref, *refs, nseg, out_scale):
    lam = lam_ref[0, 0]
    segs = [(refs[2 * j], refs[2 * j + 1]) for j in range(nseg)]
    g_ref, o_ref = refs[2 * nseg], refs[2 * nseg + 1]
    tq = q_ref.shape[2]
    m = [jnp.full((tq, LANES), NEG_BIG, F32)] * 2
    l = [jnp.zeros((tq, LANES), F32)] * 2
    acc = [jnp.zeros((tq, DA_V_DIM), F32)] * 2

    for k_ref, v_ref in segs:
        nchunk, kc = k_ref.shape[2], k_ref.shape[4]
        for c in range(nchunk):
            v = v_ref[0, c * kc:(c + 1) * kc, :]
            for sub in range(2):
                s = jnp.dot(q_ref[0, sub], k_ref[0, sub, c], preferred_element_type=F32)
                mx = s[:, :LANES]
                for j in range(1, kc // LANES):
                    mx = jnp.maximum(mx, s[:, j * LANES:(j + 1) * LANES])
                m_new = jnp.maximum(m[sub], jnp.max(mx, axis=-1, keepdims=True))
                alpha = jnp.exp2(m[sub] - m_new)
                p = jnp.exp2(s - jnp.concatenate([m_new] * (kc // LANES), axis=1))
                lsum = alpha * l[sub]
                for j in range(kc // LANES):
                    lsum = lsum + p[:, j * LANES:(j + 1) * LANES]
                l[sub] = lsum
                acc[sub] = alpha * acc[sub] + jnp.dot(p.astype(BF16), v, preferred_element_type=F32)
                m[sub] = m_new

    l1 = jnp.sum(l[0], axis=-1, keepdims=True)
    l2 = jnp.sum(l[1], axis=-1, keepdims=True)
    o = acc[0] * (1.0 / l1) - lam * (acc[1] * (1.0 / l2))
    ms = jnp.mean(o * o, axis=-1, keepdims=True)
    o_ref[0] = o * lax.rsqrt(ms + EPS) * g_ref[...] * out_scale


def _diff_attn(qp, segments, lam, g, lam_init):
    b, _, s, d = qp.shape
    tq = min(ATT_TQ, s)
    seg_specs, seg_args = [], []
    for kt, v in segments:
        nchunk, kc = kt.shape[2], kt.shape[4]
        seg_specs += [pl.BlockSpec((1, 2, nchunk, d, kc), lambda bi, h, i: (bi, h, 0, 0, 0)),
                      pl.BlockSpec((1, nchunk * kc, DA_V_DIM), lambda bi, h, i: (bi, 0, h))]
        seg_args += [kt, v]
    return pl.pallas_call(
        functools.partial(_attn_kernel, nseg=len(segments), out_scale=1.0 - lam_init),
        out_shape=jax.ShapeDtypeStruct((b, s, DA_WIDTH), F32),
        grid=(b, DA_HEADS, s // tq),
        in_specs=[pl.BlockSpec(memory_space=pltpu.SMEM),
                  pl.BlockSpec((1, 2, tq, d), lambda bi, h, i: (bi, h, i, 0))] + seg_specs
        + [pl.BlockSpec((1, DA_V_DIM), lambda bi, h, i: (0, 0))],
        out_specs=pl.BlockSpec((1, tq, DA_V_DIM), lambda bi, h, i: (bi, i, h)),
        compiler_params=_cparams(("arbitrary", "arbitrary", "arbitrary")),
        name="diff_attn",
    )(lam.reshape(1, 1).astype(F32), qp, *seg_args, g.reshape(1, DA_V_DIM).astype(F32))


def _gelu_tanh(x):
    return 0.5 * x * (1.0 + jnp.tanh(math.sqrt(2.0 / math.pi) * (x + 0.044715 * x * x * x)))


def _proj_out_kernel(x_ref, lg_ref, hf_ref, hb_ref, yf_ref, yb_ref, xs_ref, z_ref, da_ref,
                     dvec_ref, ng_ref, gate_ref, w_ref, o_ref):
    lru = _gelu_tanh(lg_ref[0]) * (hf_ref[0] + hb_ref[0])
    y = yf_ref[0] + yb_ref[0] + dvec_ref[...] * xs_ref[0]
    y = y * _silu(z_ref[0])
    ssd = y * lax.rsqrt(jnp.mean(y * y, axis=-1, keepdims=True) + EPS) * ng_ref[...]
    o = jnp.dot(lru.astype(BF16), w_ref[0:LRU_WIDTH, :], preferred_element_type=F32)
    o = o + jnp.dot(ssd.astype(BF16), w_ref[LRU_WIDTH:LRU_WIDTH + SSD_WIDTH, :], preferred_element_type=F32)
    o = o + jnp.dot(da_ref[0].astype(BF16), w_ref[LRU_WIDTH + SSD_WIDTH:, :], preferred_element_type=F32)
    o_ref[0] = x_ref[0] + gate_ref[0] * o


def _proj_out(x, lg, hf, hb, yf, yb, xbc, z, da, dvec, ng, gate, w):
    b, s, d = x.shape
    tm = min(PROJ_TM, s)
    tok = lambda wd: pl.BlockSpec((1, tm, wd), lambda bi, i: (bi, i, 0))
    const2 = lambda bi, i: (0, 0)
    return pl.pallas_call(
        _proj_out_kernel,
        out_shape=jax.ShapeDtypeStruct((b, s, d), F32),
        grid=(b, s // tm),
        in_specs=[tok(d), tok(LRU_WIDTH), tok(LRU_WIDTH), tok(LRU_WIDTH), tok(SSD_WIDTH),
                  tok(SSD_WIDTH), tok(SSD_WIDTH), tok(SSD_WIDTH), tok(DA_WIDTH),
                  pl.BlockSpec((1, SSD_WIDTH), const2), pl.BlockSpec((1, SSD_WIDTH), const2),
                  pl.BlockSpec((1, 1, d), lambda bi, i: (bi, 0, 0)),
                  pl.BlockSpec(w.shape, const2)],
        out_specs=tok(d),
        compiler_params=_cparams(("arbitrary", "arbitrary")),
        name="proj_out",
    )(x, lg, hf, hb, yf, yb, xbc, z, da, dvec, ng, gate, w)


def _route_kernel(x_ref, g_ref, sh_ref, sc_ref, rwh_ref, rwl_ref, rb_ref, c0_ref, h_ref, ri_ref, gt_ref,
                  slt_ref, n_ref, base_ref, cnt_ref, carry_s):
    i = pl.program_id(0)
    tm = x_ref.shape[0]

    @pl.when(i == 0)
    def _():
        carry_s[...] = c0_ref[...]

    h = _norm_mod(x_ref[...], g_ref[...], sh_ref[0], sc_ref[0])
    h_hi = h.astype(BF16)
    h_ref[...] = h_hi
    h_lo = (h - h_hi.astype(F32)).astype(BF16)
    logits = (jnp.dot(h_hi, rwh_ref[...], preferred_element_type=F32)
              + jnp.dot(h_hi, rwl_ref[...], preferred_element_type=F32)
              + jnp.dot(h_lo, rwh_ref[...], preferred_element_type=F32)) + rb_ref[...]
    lane = lax.broadcasted_iota(I32, logits.shape, 1)
    lane_f = lane.astype(F32)
    vals, idxs = [], []
    cur = logits
    for _ in range(TOP_K):
        m = jnp.max(cur, axis=-1, keepdims=True)
        idx = jnp.min(jnp.where(cur >= m, lane_f, float(LANES)), axis=-1, keepdims=True).astype(I32)
        vals.append(m)
        idxs.append(idx)
        cur = jnp.where(lane == idx, 2.0 * NEG_BIG, cur)
    es = [jnp.exp(v - vals[0]) for v in vals]
    inv = 1.0 / (es[0] + es[1] + es[2] + es[3])
    onehot = jnp.zeros(logits.shape, F32)
    for idx in idxs:
        onehot = onehot + jnp.where(lane == idx, 1.0, 0.0)
    ri = lax.broadcasted_iota(I32, (tm, tm), 0)
    ci = lax.broadcasted_iota(I32, (tm, tm), 1)
    strict = jnp.where(ci < ri, 1.0, 0.0).astype(BF16)
    before = jnp.dot(strict, onehot.astype(BF16), preferred_element_type=F32)
    n_tile = jnp.sum(onehot, axis=0, keepdims=True).astype(I32)
    pad_tile = _round_up8(n_tile)
    li = lax.broadcasted_iota(I32, (LANES, LANES), 0)
    lj = lax.broadcasted_iota(I32, (LANES, LANES), 1)
    below = jnp.where(li < lj, 1.0, 0.0).astype(BF16)
    pad_rows = jnp.broadcast_to(pad_tile.astype(F32), (SUBLANES, LANES)).astype(BF16)
    off = jnp.dot(pad_rows, below, preferred_element_type=F32)[0:1, :]
    place = before + off
    ri_out = jnp.zeros(logits.shape, I32)
    gt_out = jnp.zeros(logits.shape, F32)
    sl_f = jnp.full(logits.shape, -1.0, F32)
    for k in range(TOP_K):
        sloc = jnp.sum(jnp.where(lane == idxs[k], place, 0.0), axis=-1, keepdims=True)
        ri_out = ri_out + jnp.where(lane == k, idxs[k], 0) + jnp.where(lane == TOP_K + k, sloc.astype(I32), 0)
        sl_f = jnp.where(lane == k, sloc, sl_f)
        gt_out = gt_out + jnp.where(lane == k, es[k] * inv, 0.0)
    ri_ref[...] = ri_out
    gt_ref[...] = gt_out
    slt_ref[0] = sl_f.T[0:SUBLANES, :].astype(I32)
    n_ref[0] = n_tile
    base_ref[0] = carry_s[...]
    carry_s[...] = carry_s[...] + pad_tile
    cnt_ref[...] = carry_s[...]


def _route(x, g, sh, sc, tiles_per_mod, rw, rb, counts0):
    t, d = x.shape
    tm = ROUTE_TM
    tile = lambda wd: pl.BlockSpec((tm, wd), lambda i: (i, 0))
    const = lambda i: (0, 0)
    mod = pl.BlockSpec((1, 1, d), lambda i: (i // tiles_per_mod, 0, 0))
    nt = t // tm
    per_tile = pl.BlockSpec((1, 1, LANES), lambda i: (i, 0, 0))
    return pl.pallas_call(
        _route_kernel,
        out_shape=[jax.ShapeDtypeStruct((t, d), BF16), jax.ShapeDtypeStruct((t, LANES), I32),
                   jax.ShapeDtypeStruct((t, LANES), F32), jax.ShapeDtypeStruct((nt, SUBLANES, tm), I32),
                   jax.ShapeDtypeStruct((nt, 1, LANES), I32),
                   jax.ShapeDtypeStruct((nt, 1, LANES), I32), jax.ShapeDtypeStruct((1, LANES), I32)],
        grid=(nt,),
        in_specs=[tile(d), pl.BlockSpec((1, d), const), mod, mod,
                  pl.BlockSpec((d, LANES), const), pl.BlockSpec((d, LANES), const),
                  pl.BlockSpec((1, LANES), const), pl.BlockSpec((1, LANES), const)],
        out_specs=[tile(d), tile(LANES), tile(LANES), pl.BlockSpec((1, SUBLANES, tm), lambda i: (i, 0, 0)),
                   per_tile, per_tile, pl.BlockSpec((1, LANES), const)],
        scratch_shapes=[pltpu.VMEM((1, LANES), I32)],
        compiler_params=_cparams(("arbitrary",)),
        name="moe_route",
    )(x, g.reshape(1, d), sh, sc, rw[0], rw[1], rb, counts0)


MOE_STAGE = -(-(ROUTE_TM * TOP_K + N_EXPERTS * (SUBLANES - 1)) // 256) * 256
_TAB_N, _TAB_DST, _TAB_OFF, _TAB_TOTAL = 0, N_EXPERTS, 2 * N_EXPERTS, 3 * N_EXPERTS


def _chunk_copies(tab_ref, stage_ref, rows_hbm, sem, to_hbm):
    def expert(e, carry):
        dst = tab_ref[0, 0, _TAB_DST + e]
        off = tab_ref[0, 0, _TAB_OFF + e]

        def chunk(j, c):
            s = stage_ref.at[pl.ds(pl.multiple_of(off + j * SUBLANES, SUBLANES), SUBLANES), :]
            r = rows_hbm.at[pl.ds(pl.multiple_of(dst + j * SUBLANES, SUBLANES), SUBLANES), :]
            if to_hbm:
                pltpu.make_async_copy(s, r, sem).start()
            else:
                pltpu.make_async_copy(r, s, sem).start()
            return c
        return lax.fori_loop(0, tab_ref[0, 0, _TAB_N + e], chunk, carry)
    lax.fori_loop(0, N_EXPERTS, expert, 0)


def _chunk_waits(tab_ref, stage_ref, rows_hbm, sem):
    def one(j, c):
        pltpu.make_async_copy(rows_hbm.at[pl.ds(0, SUBLANES), :], stage_ref.at[pl.ds(0, SUBLANES), :], sem).wait()
        return c
    lax.fori_loop(0, tab_ref[0, 0, _TAB_TOTAL], one, 0)


def _dispatch_kernel(tab_ref, tprev_ref, tail_ref, sloc_ref, *refs, tiles):
    h_refs = refs[:len(tiles)]
    xs_out, stage_s, zero_s, sems = refs[len(tiles):]
    i = pl.program_id(0)
    n_steps = sum(tiles)
    slot = i % 2
    row = lax.broadcasted_iota(I32, (MOE_STAGE, ROUTE_TM), 0)
    perm = jnp.zeros((MOE_STAGE, ROUTE_TM), F32)
    for k in range(TOP_K):
        perm = perm + jnp.where(row == sloc_ref[0, k:k + 1, :], 1.0, 0.0)
    perm = perm.astype(BF16)
    off = 0
    for h_ref, nt in zip(h_refs, tiles):
        @pl.when((i >= off) & (i < off + nt))
        def _(h_ref=h_ref):
            stage_s[slot] = jnp.dot(perm, h_ref[...], preferred_element_type=F32)
        off += nt
    _chunk_copies(tab_ref, stage_s.at[slot], xs_out, sems.at[slot], True)

    @pl.when(i > 0)
    def _():
        _chunk_waits(tprev_ref, stage_s.at[1 - slot], xs_out, sems.at[1 - slot])

    @pl.when(i == n_steps - 1)
    def _():
        _chunk_waits(tab_ref, stage_s.at[slot], xs_out, sems.at[slot])
        sem = sems.at[slot]
        zero_s[...] = jnp.zeros(zero_s.shape, F32)
        zr = zero_s.shape[0]
        n_big, n_small, start = tail_ref[0], tail_ref[1], tail_ref[2]

        def big(j, c):
            pltpu.make_async_copy(zero_s, xs_out.at[pl.ds(pl.multiple_of(start + j * zr, SUBLANES), zr), :], sem).start()
            return c

        def small(j, c):
            r0 = pl.multiple_of(start + n_big * zr + j * SUBLANES, SUBLANES)
            pltpu.make_async_copy(zero_s.at[pl.ds(0, SUBLANES), :], xs_out.at[pl.ds(r0, SUBLANES), :], sem).start()
            return c

        def big_wait(j, c):
            pltpu.make_async_copy(zero_s, xs_out.at[pl.ds(0, zr), :], sem).wait()
            return c

        def small_wait(j, c):
            pltpu.make_async_copy(zero_s.at[pl.ds(0, SUBLANES), :], xs_out.at[pl.ds(0, SUBLANES), :], sem).wait()
            return c

        lax.fori_loop(0, n_big, big, 0)
        lax.fori_loop(0, n_small, small, 0)
        lax.fori_loop(0, n_big, big_wait, 0)
        lax.fori_loop(0, n_small, small_wait, 0)


def _dispatch(hs, tabs, slocs, tail, n_max):
    d = hs[0].shape[1]
    tm = ROUTE_TM
    tiles = tuple(h.shape[0] // tm for h in hs)
    specs, off = [], 0
    for nt in tiles:
        specs.append(pl.BlockSpec((tm, d), lambda i, off=off, nt=nt: (jnp.clip(i - off, 0, nt - 1), 0)))
        off += nt
    return pl.pallas_call(
        functools.partial(_dispatch_kernel, tiles=tiles),
        out_shape=jax.ShapeDtypeStruct((n_max, d), F32),
        grid=(sum(tiles),),
        in_specs=[pl.BlockSpec((1, 1, LANES), lambda i: (i, 0, 0), memory_space=pltpu.SMEM),
                  pl.BlockSpec((1, 1, LANES), lambda i: (jnp.maximum(i - 1, 0), 0, 0), memory_space=pltpu.SMEM),
                  pl.BlockSpec(memory_space=pltpu.SMEM),
                  pl.BlockSpec((1, SUBLANES, tm), lambda i: (i, 0, 0))] + specs,
        out_specs=pl.BlockSpec(memory_space=pl.ANY),
        scratch_shapes=[pltpu.VMEM((2, MOE_STAGE, d), F32), pltpu.VMEM((MOE_TM, d), F32),
                        pltpu.SemaphoreType.DMA((2,))],
        compiler_params=_cparams(("arbitrary",)),
        name="moe_dispatch",
    )(tabs, tabs, tail, slocs, *hs)


def _gmm_kernel(vt_ref, ve_ref, lo_ref, hi_ref, first_ref, xs_ref, wg_ref, wu_ref, bg_ref, bu_ref,
                wd_ref, bd_ref, y_ref):
    del vt_ref, ve_ref
    v = pl.program_id(0)
    lo, hi = lo_ref[v], hi_ref[v]

    @pl.when(hi > lo)
    def _():
        x = xs_ref[...].astype(BF16)
        gate = jnp.dot(x, wg_ref[...], preferred_element_type=F32) + bg_ref[...]
        up = jnp.dot(x, wu_ref[...], preferred_element_type=F32) + bu_ref[...]
        gate = jnp.minimum(gate, SWIGLU_LIMIT)
        up = jnp.clip(up, -SWIGLU_LIMIT, SWIGLU_LIMIT)
        glu = gate * _sigmoid(SWIGLU_ALPHA * gate)
        act = ((up + 1.0) * glu).astype(BF16)
        down = lambda: jnp.dot(act, wd_ref[...], preferred_element_type=F32) + bd_ref[...]
        whole = (lo == 0) & (hi == xs_ref.shape[0])
        first = first_ref[v] > 0
        row = lax.broadcasted_iota(I32, y_ref.shape, 0)
        mine = (row >= lo) & (row < hi)

        @pl.when(whole)
        def _():
            y_ref[...] = down()

        @pl.when(jnp.logical_not(whole) & first)
        def _():
            y_ref[...] = jnp.where(mine, down(), 0.0)

        @pl.when(jnp.logical_not(whole) & jnp.logical_not(first))
        def _():
            y_ref[...] = jnp.where(mine, down(), y_ref[...])

    @pl.when((hi <= lo) & (first_ref[v] > 0))
    def _():
        y_ref[...] = jnp.zeros(y_ref.shape, F32)


def _gmm(xs, visits, wg, wu, bg, bu, wd, bd):
    n_rows, d = xs.shape
    de = wg.shape[2]
    tm = MOE_TM
    nv = visits[0].shape[0]
    ex = lambda shape: pl.BlockSpec((None,) + shape, lambda v, vt, ve, lo, hi, fi: (ve[v], 0, 0))
    rows = pl.BlockSpec((tm, d), lambda v, vt, ve, lo, hi, fi: (vt[v], 0))
    grid_spec = pltpu.PrefetchScalarGridSpec(
        num_scalar_prefetch=5,
        grid=(nv,),
        in_specs=[rows, ex((d, de)), ex((d, de)), ex((1, de)), ex((1, de)), ex((de, d)), ex((1, d))],
        out_specs=rows,
    )
    return pl.pallas_call(
        _gmm_kernel,
        out_shape=jax.ShapeDtypeStruct((n_rows, d), F32),
        grid_spec=grid_spec,
        compiler_params=_cparams(("arbitrary",)),
        name="moe_experts",
    )(*visits, xs, wg, wu, bg, bu, wd, bd)


def _visit_tables(cnt, n_rows):
    tm = MOE_TM
    nt = n_rows // tm
    nv = nt + N_EXPERTS - 1
    end = jnp.cumsum(cnt)
    start = end - cnt
    first_tile = start // tm
    last_tile = jnp.maximum(end - 1, 0) // tm
    nvis = jnp.where(cnt > 0, last_tile - first_tile + 1, 0)
    vend = jnp.cumsum(nvis)
    vstart = vend - nvis
    v = jnp.arange(nv, dtype=I32)
    e = jnp.minimum(jnp.sum((vend[None, :] <= v[:, None]).astype(I32), axis=1), N_EXPERTS - 1)
    n_used = vend[N_EXPERTS - 1]
    used = v < n_used
    tiles_used = (end[N_EXPERTS - 1] + tm - 1) // tm
    tile = jnp.where(used, first_tile[e] + v - vstart[e], jnp.minimum(tiles_used + v - n_used, nt - 1))
    lo = jnp.where(used, jnp.clip(start[e] - tile * tm, 0, tm), 0)
    hi = jnp.where(used, jnp.clip(end[e] - tile * tm, 0, tm), 0)
    prev = jnp.concatenate([jnp.full((1,), -1, I32), tile[:-1].astype(I32)])
    first = (tile != prev).astype(I32)
    return tuple(a.astype(I32) for a in (tile, e, lo, hi, first))


_SPLIT_W = 2 * LANES


def _split_gu_kernel(w_ref, p_ref, g_ref, u_ref):
    w = w_ref[...].astype(BF16)
    for j in range(w.shape[1] // _SPLIT_W):
        r = jnp.dot(w[:, j * _SPLIT_W:(j + 1) * _SPLIT_W], p_ref[...], preferred_element_type=F32)
        g_ref[:, j * LANES:(j + 1) * LANES] = r[:, :LANES].astype(BF16)
        u_ref[:, j * LANES:(j + 1) * LANES] = r[:, LANES:].astype(BF16)


def _split_gu(w_gu, layer):
    _, e, d, n2 = w_gu.shape
    tn = 2 * _SPLIT_W
    src = jnp.arange(_SPLIT_W)
    dst = jnp.where(src % 2 == 0, src // 2, LANES + src // 2)
    perm = (dst[:, None] == jnp.arange(_SPLIT_W)[None, :]).astype(BF16)
    return pl.pallas_call(
        _split_gu_kernel,
        out_shape=[jax.ShapeDtypeStruct((e, d, n2 // 2), BF16)] * 2,
        grid=(e, n2 // tn),
        in_specs=[pl.BlockSpec((None, None, d, tn), lambda ei, j: (layer, ei, 0, j)),
                  pl.BlockSpec((_SPLIT_W, _SPLIT_W), lambda ei, j: (0, 0))],
        out_specs=[pl.BlockSpec((None, d, tn // 2), lambda ei, j: (ei, 0, j))] * 2,
        compiler_params=_cparams(("arbitrary", "arbitrary")),
        name="split_gate_up",
    )(w_gu, perm)


def _combine_kernel(tcur_ref, tnext_ref, sloc_ref, gt_ref, x_ref, g2_ref, y_hbm, o_ref, stage_s, sems, *, n):
    i = pl.program_id(0)
    slot = i % 2

    @pl.when(i == 0)
    def _():
        stage_s[...] = jnp.zeros(stage_s.shape, F32)
        _chunk_copies(tcur_ref, stage_s.at[0], y_hbm, sems.at[0], False)

    @pl.when(i + 1 < n)
    def _():
        _chunk_copies(tnext_ref, stage_s.at[1 - slot], y_hbm, sems.at[1 - slot], False)

    _chunk_waits(tcur_ref, stage_s.at[slot], y_hbm, sems.at[slot])
    rows = stage_s[slot].astype(BF16)
    lane = lax.broadcasted_iota(I32, (x_ref.shape[0], MOE_STAGE), 1)
    gt = gt_ref[...]
    sloc = sloc_ref[...]
    gmat = jnp.zeros(lane.shape, F32)
    for k in range(TOP_K):
        gmat = gmat + jnp.where(lane == sloc[:, TOP_K + k:TOP_K + k + 1], gt[:, k:k + 1], 0.0)
    g_hi = gmat.astype(BF16)
    g_lo = (gmat - g_hi.astype(F32)).astype(BF16)
    f = (jnp.dot(g_hi, rows, preferred_element_type=F32)
         + jnp.dot(g_lo, rows, preferred_element_type=F32))
    o_ref[...] = x_ref[...] + g2_ref[0] * f


def _combine(y, tabs, sloc, gates, x, g2, tiles_per_mod):
    t, d = x.shape
    tm = ROUTE_TM
    nt = t // tm
    return pl.pallas_call(
        functools.partial(_combine_kernel, n=nt),
        out_shape=jax.ShapeDtypeStruct((t, d), F32),
        grid=(nt,),
        in_specs=[pl.BlockSpec((1, 1, LANES), lambda i: (i, 0, 0), memory_space=pltpu.SMEM),
                  pl.BlockSpec((1, 1, LANES), lambda i: (jnp.minimum(i + 1, nt - 1), 0, 0),
                               memory_space=pltpu.SMEM),
                  pl.BlockSpec((tm, LANES), lambda i: (i, 0)),
                  pl.BlockSpec((tm, LANES), lambda i: (i, 0)),
                  pl.BlockSpec((tm, d), lambda i: (i, 0)),
                  pl.BlockSpec((1, 1, d), lambda i: (i // tiles_per_mod, 0, 0)),
                  pl.BlockSpec(memory_space=pl.ANY)],
        out_specs=pl.BlockSpec((tm, d), lambda i: (i, 0)),
        scratch_shapes=[pltpu.VMEM((2, MOE_STAGE, d), F32), pltpu.SemaphoreType.DMA((2,))],
        compiler_params=_cparams(("arbitrary",)),
        name="moe_combine",
    )(tabs, tabs, sloc, gates, x, g2, y)


def _moe(streams, layer, g, router_w, router_b, w_gu_all, b_gu, w_down, b_down):
    d = streams[0][0].shape[1]
    tm = ROUTE_TM
    rw_f = jnp.zeros((d, LANES), F32).at[:, :N_EXPERTS].set(router_w.astype(F32))
    rw_hi = rw_f.astype(BF16)
    rw = (rw_hi, (rw_f - rw_hi.astype(F32)).astype(BF16))
    rb = jnp.full((1, LANES), NEG_BIG, F32).at[0, :N_EXPERTS].set(router_b.astype(F32))
    counts = jnp.zeros((1, LANES), I32)
    routed = []
    for x_tok, sh, sc, _, tpm in streams:
        h, route_i, gates, sloc_t, n_tile, seg_base, counts = _route(x_tok, g, sh, sc, tpm, rw, rb, counts)
        routed.append((h, route_i, gates, sloc_t, n_tile, seg_base))
    n_tiles = sum(x_tok.shape[0] for x_tok, *_ in streams) // tm
    n_max = n_tiles * tm * TOP_K + n_tiles * N_EXPERTS * (SUBLANES - 1)
    n_max = (n_max + MOE_TM - 1) // MOE_TM * MOE_TM
    cnt = counts[0, :N_EXPERTS]
    total = jnp.sum(cnt)
    base = jnp.cumsum(cnt) - cnt
    tabs = []
    for h, _, _, _, n_tile, seg_base in routed:
        nt = h.shape[0] // tm
        pad = _round_up8(n_tile[:, 0, :N_EXPERTS])
        off = jnp.cumsum(pad, axis=1) - pad
        dst = base[None, :] + seg_base[:, 0, :N_EXPERTS]
        nch = pad // SUBLANES
        tab = jnp.concatenate([nch, dst, off, jnp.sum(nch, axis=1, keepdims=True),
                               jnp.zeros((nt, LANES - 3 * N_EXPERTS - 1), I32)], axis=1)
        tabs.append(tab.astype(I32).reshape(nt, 1, LANES))
    free = n_max - total
    tail = jnp.stack([free // MOE_TM, (free % MOE_TM) // SUBLANES, total]).astype(I32)
    xs = _dispatch([r[0] for r in routed], jnp.concatenate(tabs, axis=0),
                   jnp.concatenate([r[3] for r in routed], axis=0), tail, n_max)
    wg, wu = _split_gu(w_gu_all, layer)
    bg = b_gu[:, None, 0::2].astype(F32)
    bu = b_gu[:, None, 1::2].astype(F32)
    y = _gmm(xs, _visit_tables(cnt, n_max), wg, wu, bg, bu, w_down.astype(BF16),
             b_down[:, None, :].astype(F32))
    return [_combine(y, tab, r[1], r[2], x_tok, g2, tpm)
            for (x_tok, _, _, g2, tpm), r, tab in zip(streams, routed, tabs)]


def kernel(x, c, ctx, c_ctx, w_mod, b_mod, norm1_g, norm2_g, w_in, w_out, lru_conv_w, lru_conv_b, lru_wa, lru_ba, lru_wx, lru_bx, lru_lam, ssd_conv_w, ssd_conv_b, ssd_a_log, ssd_dt_bias, ssd_d, ssd_norm_g, da_q_norm, da_k_norm, da_lam_q, da_lam_k, da_subln_g, router_w, router_b, exp_w_gu, exp_b_gu, exp_w_down, exp_b_down):
    b, s, d = x.shape
    n_ctx = ctx.shape[1]
    depth = w_mod.shape[0]
    cos, sin = _rope_tables(s)
    x_lat, x_ctx = x, ctx
    mod_rows = jnp.zeros((2 * SUBLANES, d), F32).at[:b].set(c).at[b].set(c_ctx)

    for l in range(depth):
        need_ctx = l < depth - 1
        lam_init = 0.8 - 0.6 * math.exp(-0.3 * l)
        mod = _modulation(mod_rows, w_mod[l], b_mod[l])
        m_lat = mod[:b].reshape(b, 1, 6, d)
        m_ctx = jnp.broadcast_to(mod[b].reshape(1, 1, 6, d), (b, 1, 6, d))
        part = lambda m, j: m[:, :, j, :]

        w_in_l = _prep_w_in(w_in[l])
        p_lat = _proj_in(x_lat, norm1_g[l], part(m_lat, 0), part(m_lat, 1), w_in_l)
        p_ctx = _proj_in(x_ctx, norm1_g[l], part(m_ctx, 0), part(m_ctx, 1), w_in_l)

        zl = jnp.zeros((b, LRU_WIDTH), F32)
        lru_args = (lru_wa[l], lru_ba[l], lru_wx[l], lru_bx[l], lru_lam[l])
        xc_ctx = _dwconv(p_ctx["lx"], lru_conv_w[l], lru_conv_b[l], act=False)
        hf_c, hb_c = _lru_scan(xc_ctx, *lru_args, zl, zl)
        xc_lat = _dwconv(p_lat["lx"], lru_conv_w[l], lru_conv_b[l], act=False)
        hf_l, hb_l = _lru_scan(xc_lat, *lru_args, hf_c[:, -1], hb_c[:, 0])

        zs = jnp.zeros((b, SSD_GROUPS, 2 * SSD_HEAD_DIM, SSD_STATE), F32)
        xbc_ctx = _dwconv(p_ctx["xbc"], ssd_conv_w[l], ssd_conv_b[l], act=True)
        yf_c, yb_c, sf_c, sb_c = _ssd_scan(xbc_ctx, p_ctx["dt"], ssd_a_log[l], ssd_dt_bias[l], zs, zs)
        xbc_lat = _dwconv(p_lat["xbc"], ssd_conv_w[l], ssd_conv_b[l], act=True)
        yf_l, yb_l, _, _ = _ssd_scan(xbc_lat, p_lat["dt"], ssd_a_log[l], ssd_dt_bias[l], sf_c, sb_c)

        lq = da_lam_q[l].astype(F32)
        lk = da_lam_k[l].astype(F32)
        lam = jnp.exp(jnp.sum(lq[0] * lk[0])) - jnp.exp(jnp.sum(lq[1] * lk[1])) + lam_init
        kc = _qk_prep(p_ctx["k"], da_k_norm[l], cos, sin, rope=False, scale=1.0, kc=min(ATT_KC, n_ctx))
        kl = _qk_prep(p_lat["k"], da_k_norm[l], cos, sin, rope=True, scale=1.0, kc=min(ATT_KC, s))
        ql = _qk_prep(p_lat["q"], da_q_norm[l], cos, sin, rope=True, scale=DA_SCALE * LOG2E, kc=0)
        da_l = _diff_attn(ql, [(kc, p_ctx["v"]), (kl, p_lat["v"])], lam, da_subln_g[l], lam_init)

        dvec = jnp.repeat(ssd_d[l].astype(F32), SSD_HEAD_DIM).reshape(1, SSD_WIDTH)
        ng = ssd_norm_g[l].astype(F32).reshape(1, SSD_WIDTH)
        w_out_l = w_out[l].astype(BF16)
        if need_ctx:
            qc = _qk_prep(p_ctx["q"], da_q_norm[l], cos, sin, rope=False, scale=DA_SCALE * LOG2E, kc=0)
            da_c = _diff_attn(qc, [(kc, p_ctx["v"])], lam, da_subln_g[l], lam_init)
            x_ctx = _proj_out(x_ctx, p_ctx["lg"], hf_c, hb_c, yf_c, yb_c, xbc_ctx, p_ctx["z"], da_c,
                              dvec, ng, part(m_ctx, 2), w_out_l)
        x_lat = _proj_out(x_lat, p_lat["lg"], hf_l, hb_l, yf_l, yb_l, xbc_lat, p_lat["z"], da_l,
                          dvec, ng, part(m_lat, 2), w_out_l)

        moe_w = (router_w[l], router_b[l], exp_w_gu, exp_b_gu[l], exp_w_down[l], exp_b_down[l])
        streams = [(x_lat.reshape(-1, d), part(m_lat, 3), part(m_lat, 4), part(m_lat, 5), s // ROUTE_TM)]
        if need_ctx:
            streams.append((x_ctx.reshape(-1, d), part(m_ctx, 3), part(m_ctx, 4), part(m_ctx, 5), n_ctx // ROUTE_TM))
        outs = _moe(streams, l, norm2_g[l], *moe_w)
        x_lat = outs[0].reshape(b, s, d)
        if need_ctx:
            x_ctx = outs[1].reshape(b, n_ctx, d)
    return x_lat
```

```python
import functools
import math

import jax
import jax.numpy as jnp
from jax import lax
from jax.experimental import pallas as pl
from jax.experimental.pallas import tpu as pltpu

F32 = jnp.float32
BF16 = jnp.bfloat16
I32 = jnp.int32

GRID_W = 64
EPS = 1e-6
CONV_K = 4
LRU_HEADS = 4
LRU_HEAD_DIM = 64
LRU_WIDTH = 256
LRU_C = 8.0
SSD_HEADS = 4
SSD_HEAD_DIM = 64
SSD_WIDTH = 256
SSD_GROUPS = 2
SSD_STATE = 128
SSD_CHUNK = 128
SSD_CONV_DIM = 768
DA_HEADS = 4
DA_QK_DIM = 64
DA_V_DIM = 128
DA_QK_WIDTH = 512
DA_WIDTH = 512
DA_SCALE = DA_QK_DIM ** -0.5
ROPE_BASE = 10000.0
N_EXPERTS = 32
TOP_K = 4
SWIGLU_ALPHA = 1.702
SWIGLU_LIMIT = 7.0

LANES = 128
SUBLANES = 8
VMEM_LIMIT = 56 * 1024 * 1024

PROJ_TM = 512
ATT_TQ = 512
ATT_KC = 256
LOG2E = math.log2(math.e)
LRU_TL = 512
SSD_STEP_CHUNKS = 4
ROUTE_TM = 256
MOE_TM = 512
NEG_BIG = -1e30


def _cparams(sem):
    return pltpu.CompilerParams(dimension_semantics=sem, vmem_limit_bytes=VMEM_LIMIT)


def _sigmoid(x):
    return 0.5 * jnp.tanh(0.5 * x) + 0.5


def _silu(x):
    return x * _sigmoid(x)


def _softplus(x):
    return jnp.maximum(x, 0.0) + jnp.log(1.0 + jnp.exp(-jnp.abs(x)))


def _round_up8(n):
    return ((n + (SUBLANES - 1)) >> 3) << 3


def _mod_kernel(c_ref, w_ref, b_ref, o_ref):
    a = _silu(c_ref[...])
    o_ref[...] = jnp.dot(a, w_ref[...], preferred_element_type=F32,
                         precision=lax.Precision.HIGHEST) + b_ref[...]


def _modulation(rows, w, b):
    m, d = rows.shape
    n = w.shape[1]
    tn = 512
    return pl.pallas_call(
        _mod_kernel,
        out_shape=jax.ShapeDtypeStruct((m, n), F32),
        grid=(n // tn,),
        in_specs=[pl.BlockSpec((m, d), lambda j: (0, 0)),
                  pl.BlockSpec((d, tn), lambda j: (0, j)),
                  pl.BlockSpec((1, tn), lambda j: (0, j))],
        out_specs=pl.BlockSpec((m, tn), lambda j: (0, j)),
        compiler_params=_cparams(("arbitrary",)),
        name="modulation",
    )(rows, w, b.reshape(1, n))


def _norm_mod(x, g, sh, sc):
    ms = jnp.mean(x * x, axis=-1, keepdims=True)
    y = x * lax.rsqrt(ms + EPS) * g
    return y * (1.0 + sc) + sh


def _proj_in_kernel(x_ref, g_ref, sh_ref, sc_ref, w_ref, *o_refs, splits):
    h = _norm_mod(x_ref[0], g_ref[...], sh_ref[0], sc_ref[0]).astype(BF16)
    for o_ref, (lo, hi) in zip(o_refs, splits):
        o_ref[0] = jnp.dot(h, w_ref[:, lo:hi], preferred_element_type=F32).astype(o_ref.dtype)


_IN_GROUPS = (("lg", 256, F32), ("lx", 256, F32), ("z", 256, F32), ("xbc", 768, F32),
              ("dt", 128, F32), ("q", 512, F32), ("k", 512, F32), ("v", 512, BF16))


def _prep_w_in(w_in):
    d = w_in.shape[0]
    lo = 2 * LRU_WIDTH + SSD_WIDTH + SSD_CONV_DIM
    hi = lo + 2 * SSD_HEADS
    dt = jnp.concatenate([w_in[:, lo:hi], jnp.zeros((d, LANES - (hi - lo)), w_in.dtype)], axis=1)
    w = jnp.concatenate([w_in[:, :lo], dt, w_in[:, hi:]], axis=1)
    return w.astype(BF16)


def _proj_in(x, g, sh, sc, w):
    b, s, d = x.shape
    tm = min(PROJ_TM, s)
    splits, off = [], 0
    for _, wd, _ in _IN_GROUPS:
        splits.append((off, off + wd))
        off += wd
    out_shape = [jax.ShapeDtypeStruct((b, s, wd), dt) for _, wd, dt in _IN_GROUPS]
    out_specs = [pl.BlockSpec((1, tm, wd), lambda bi, i: (bi, i, 0)) for _, wd, _ in _IN_GROUPS]
    outs = pl.pallas_call(
        functools.partial(_proj_in_kernel, splits=tuple(splits)),
        out_shape=out_shape,
        grid=(b, s // tm),
        in_specs=[pl.BlockSpec((1, tm, d), lambda bi, i: (bi, i, 0)),
                  pl.BlockSpec((1, d), lambda bi, i: (0, 0)),
                  pl.BlockSpec((1, 1, d), lambda bi, i: (bi, 0, 0)),
                  pl.BlockSpec((1, 1, d), lambda bi, i: (bi, 0, 0)),
                  pl.BlockSpec((d, off), lambda bi, i: (0, 0))],
        out_specs=out_specs,
        compiler_params=_cparams(("arbitrary", "arbitrary")),
        name="proj_in",
    )(x, g.reshape(1, d), sh, sc, w)
    return {name: o for (name, _, _), o in zip(_IN_GROUPS, outs)}


def _dwconv_kernel(x_ref, w_ref, b_ref, o_ref, *, act):
    x = x_ref[0]
    s = x.shape[0]
    row = lax.broadcasted_iota(I32, x.shape, 0)
    acc = x * w_ref[2:3, :] + b_ref[...]
    acc = acc + jnp.where(row >= 2, pltpu.roll(x, 2, 0), 0.0) * w_ref[0:1, :]
    acc = acc + jnp.where(row >= 1, pltpu.roll(x, 1, 0), 0.0) * w_ref[1:2, :]
    acc = acc + jnp.where(row < s - 1, pltpu.roll(x, s - 1, 0), 0.0) * w_ref[3:4, :]
    if act:
        acc = _silu(acc)
    o_ref[0] = acc


def _dwconv(x, w, bias, act):
    b, s, c = x.shape
    tc = 256
    return pl.pallas_call(
        functools.partial(_dwconv_kernel, act=act),
        out_shape=jax.ShapeDtypeStruct((b, s, c), F32),
        grid=(b, c // tc),
        in_specs=[pl.BlockSpec((1, s, tc), lambda bi, j: (bi, 0, j)),
                  pl.BlockSpec((CONV_K, tc), lambda bi, j: (0, j)),
                  pl.BlockSpec((1, tc), lambda bi, j: (0, j))],
        out_specs=pl.BlockSpec((1, s, tc), lambda bi, j: (bi, 0, j)),
        compiler_params=_cparams(("arbitrary", "arbitrary")),
        name="dwconv",
    )(x, w, bias.reshape(1, c))


def _lru_coeffs(x, w_ref, b_ref, sp_ref, a_s, u_s):
    g = jnp.dot(x.astype(BF16), w_ref[...], preferred_element_type=F32) + b_ref[...]
    r = _sigmoid(g[:, :LRU_WIDTH])
    i = _sigmoid(g[:, LRU_WIDTH:])
    a = jnp.exp(-LRU_C * r * sp_ref[...])
    a_s[...] = a
    u_s[...] = jnp.sqrt(1.0 - a * a) * (i * x)


def _lru_kernel(xf_ref, xb_ref, wf_ref, wb_ref, bf_ref, bb_ref, spf_ref, spb_ref,
                h0f_ref, h0b_ref, hf_ref, hb_ref, af_s, uf_s, ab_s, ub_s, cf_s, cb_s):
    c = pl.program_id(1)
    tl = xf_ref.shape[1]
    ngroups = tl // SUBLANES
    shape = (SUBLANES, LRU_WIDTH)
    row = lax.broadcasted_iota(I32, shape, 0)

    @pl.when(c == 0)
    def _():
        cf_s[...] = jnp.broadcast_to(h0f_ref[0], shape)
        cb_s[...] = jnp.broadcast_to(h0b_ref[0], shape)

    _lru_coeffs(xf_ref[0], wf_ref, bf_ref, spf_ref, af_s, uf_s)
    _lru_coeffs(xb_ref[0], wb_ref, bb_ref, spb_ref, ab_s, ub_s)

    def group_scan(a, u, carry, reverse):
        for sh in (1, 2, 4):
            keep = (row < SUBLANES - sh) if reverse else (row >= sh)
            amt = SUBLANES - sh if reverse else sh
            a_sh = jnp.where(keep, pltpu.roll(a, amt, 0), 1.0)
            u_sh = jnp.where(keep, pltpu.roll(u, amt, 0), 0.0)
            u = a * u_sh + u
            a = a * a_sh
        return a * carry + u

    def step(gj, carry):
        cf, cb = carry
        slf = pl.ds(pl.multiple_of(gj * SUBLANES, SUBLANES), SUBLANES)
        slb = pl.ds(pl.multiple_of((ngroups - 1 - gj) * SUBLANES, SUBLANES), SUBLANES)
        hf = group_scan(af_s[slf, :], uf_s[slf, :], cf, False)
        hb = group_scan(ab_s[slb, :], ub_s[slb, :], cb, True)
        hf_ref[0, slf, :] = hf
        hb_ref[0, slb, :] = hb
        return (jnp.broadcast_to(hf[SUBLANES - 1:SUBLANES, :], shape),
                jnp.broadcast_to(hb[0:1, :], shape))

    cf, cb = lax.fori_loop(0, ngroups, step, (cf_s[...], cb_s[...]), unroll=2)
    cf_s[...] = cf
    cb_s[...] = cb


def _lru_gate_weights(wa, wx):
    def dense(wh):
        m = jnp.zeros((LRU_WIDTH, LRU_WIDTH), wh.dtype)
        for h in range(LRU_HEADS):
            lo = h * LRU_HEAD_DIM
            m = m.at[lo:lo + LRU_HEAD_DIM, lo:lo + LRU_HEAD_DIM].set(wh[h])
        return m
    return jnp.concatenate([dense(wa), dense(wx)], axis=1).astype(BF16)


def _lru_scan(xc, wa, ba, wx, bx, lam, h0f, h0b):
    b, s, w = xc.shape
    tl = min(LRU_TL, s)
    nc = s // tl
    wf = _lru_gate_weights(wa[0], wx[0])
    wb = _lru_gate_weights(wa[1], wx[1])
    bf = jnp.concatenate([ba[0], bx[0]]).reshape(1, 2 * w)
    bb = jnp.concatenate([ba[1], bx[1]]).reshape(1, 2 * w)
    sp = jax.nn.softplus(-lam.astype(F32))
    const = lambda bi, c: (0, 0)
    hf, hb = pl.pallas_call(
        _lru_kernel,
        out_shape=[jax.ShapeDtypeStruct((b, s, w), F32)] * 2,
        grid=(b, nc),
        in_specs=[pl.BlockSpec((1, tl, w), lambda bi, c: (bi, c, 0)),
                  pl.BlockSpec((1, tl, w), lambda bi, c: (bi, nc - 1 - c, 0)),
                  pl.BlockSpec((w, 2 * w), const), pl.BlockSpec((w, 2 * w), const),
                  pl.BlockSpec((1, 2 * w), const), pl.BlockSpec((1, 2 * w), const),
                  pl.BlockSpec((1, w), const), pl.BlockSpec((1, w), const),
                  pl.BlockSpec((1, 1, w), lambda bi, c: (bi, 0, 0)),
                  pl.BlockSpec((1, 1, w), lambda bi, c: (bi, 0, 0))],
        out_specs=[pl.BlockSpec((1, tl, w), lambda bi, c: (bi, c, 0)),
                   pl.BlockSpec((1, tl, w), lambda bi, c: (bi, nc - 1 - c, 0))],
        scratch_shapes=[pltpu.VMEM((tl, w), F32)] * 4
        + [pltpu.VMEM((SUBLANES, w), F32), pltpu.VMEM((SUBLANES, w), F32)],
        compiler_params=_cparams(("arbitrary", "arbitrary")),
        name="lru_scan",
    )(xc, xc, wf, wb, bf, bb, sp[0:1], sp[1:2], h0f.reshape(b, 1, w), h0b.reshape(b, 1, w))
    return hf, hb


def _ssd_direction(x_ref, dt_ref, row0, avec_ref, dtb_ref, state_s, y_ref, lane0, rev):
    t = SSD_CHUNK
    xbc = x_ref[0, row0:row0 + t, :]
    dtraw = dt_ref[0, row0:row0 + t, :]
    ri = lax.broadcasted_iota(I32, (t, t), 0)
    ci = lax.broadcasted_iota(I32, (t, t), 1)
    causal = (ci >= ri) if rev else (ci <= ri)
    cum_m = jnp.where(causal, 1.0, 0.0).astype(F32)

    dt = _softplus(dtraw + dtb_ref[...])
    a_all = dt * avec_ref[...]
    acs = jnp.dot(cum_m, a_all, preferred_element_type=F32, precision=lax.Precision.HIGHEST)
    acs_t = acs.T
    edge = 0 if rev else t - 1

    xs = xbc[:, :SSD_WIDTH]
    hrow = lax.broadcasted_iota(I32, (t, t), 0)
    for g in range(SSD_GROUPS):
        bm = xbc[:, SSD_WIDTH + g * SSD_STATE:SSD_WIDTH + (g + 1) * SSD_STATE]
        cm = xbc[:, SSD_WIDTH + (SSD_GROUPS + g) * SSD_STATE:SSD_WIDTH + (SSD_GROUPS + g + 1) * SSD_STATE]
        bm_b = bm.astype(BF16)
        cm_b = cm.astype(BF16)
        gmat = lax.dot_general(cm_b, bm_b, (((1,), (1,)), ((), ())), preferred_element_type=F32)
        s_prev = state_s[g]
        off_all = lax.dot_general(cm_b, s_prev.astype(BF16), (((1,), (1,)), ((), ())),
                                  preferred_element_type=F32)
        xd_parts, tot = [], []
        for hl in range(2):
            h = g * 2 + hl
            ln = lane0 + h
            col = jnp.broadcast_to(acs[:, ln:ln + 1], (t, t))
            rowv = acs_t[ln:ln + 1, :]
            lmat = jnp.where(causal, jnp.exp(col - rowv), 0.0)
            xh = xs[:, h * SSD_HEAD_DIM:(h + 1) * SSD_HEAD_DIM] * dt[:, ln:ln + 1]
            ydiag = jnp.dot((gmat * lmat).astype(BF16), xh.astype(BF16), preferred_element_type=F32)
            colh = col[:, :SSD_HEAD_DIM]
            yoff = off_all[:, hl * SSD_HEAD_DIM:(hl + 1) * SSD_HEAD_DIM] * jnp.exp(colh)
            y_ref[0, row0:row0 + t, h * SSD_HEAD_DIM:(h + 1) * SSD_HEAD_DIM] = ydiag + yoff
            total = acs[edge:edge + 1, ln:ln + 1]
            xd_parts.append(xh * jnp.exp(total - colh))
            tot.append(jnp.exp(total))
        xd = jnp.concatenate(xd_parts, axis=1)
        xd_t = xd.T
        st = jnp.dot(xd_t.astype(BF16), bm_b, preferred_element_type=F32)
        fac = jnp.where(hrow < SSD_HEAD_DIM, tot[0], tot[1])
        state_s[g] = fac * s_prev + st


def _ssd_kernel(xf_ref, dtf_ref, xb_ref, dtb_ref, avec_ref, bias_ref, h0f_ref, h0b_ref,
                yf_ref, yb_ref, sf_ref, sb_ref, stf_s, stb_s):
    c = pl.program_id(1)

    @pl.when(c == 0)
    def _():
        stf_s[...] = h0f_ref[0]
        stb_s[...] = h0b_ref[0]

    nsub = xf_ref.shape[1] // SSD_CHUNK
    for j in range(nsub):
        _ssd_direction(xf_ref, dtf_ref, j * SSD_CHUNK, avec_ref, bias_ref, stf_s, yf_ref, 0, False)
        _ssd_direction(xb_ref, dtb_ref, (nsub - 1 - j) * SSD_CHUNK, avec_ref, bias_ref, stb_s, yb_ref,
                       SSD_HEADS, True)
    sf_ref[0] = stf_s[...]
    sb_ref[0] = stb_s[...]


def _ssd_scan(xbc, dt, a_log, dt_bias, h0f, h0b):
    b, s, _ = xbc.shape
    t = min(SSD_STEP_CHUNKS * SSD_CHUNK, s)
    nc = s // t
    avec = jnp.zeros((1, LANES), F32).at[0, :2 * SSD_HEADS].set(-jnp.exp(a_log.astype(F32)).reshape(-1))
    bias = jnp.zeros((1, LANES), F32).at[0, :2 * SSD_HEADS].set(dt_bias.astype(F32).reshape(-1))
    st_shape = (b, SSD_GROUPS, 2 * SSD_HEAD_DIM, SSD_STATE)
    const = lambda bi, c: (0, 0)
    st_spec = pl.BlockSpec((1,) + st_shape[1:], lambda bi, c: (bi, 0, 0, 0))
    yf, yb, sf, sb = pl.pallas_call(
        _ssd_kernel,
        out_shape=[jax.ShapeDtypeStruct((b, s, SSD_WIDTH), F32)] * 2
        + [jax.ShapeDtypeStruct(st_shape, F32)] * 2,
        grid=(b, nc),
        in_specs=[pl.BlockSpec((1, t, SSD_CONV_DIM), lambda bi, c: (bi, c, 0)),
                  pl.BlockSpec((1, t, LANES), lambda bi, c: (bi, c, 0)),
                  pl.BlockSpec((1, t, SSD_CONV_DIM), lambda bi, c: (bi, nc - 1 - c, 0)),
                  pl.BlockSpec((1, t, LANES), lambda bi, c: (bi, nc - 1 - c, 0)),
                  pl.BlockSpec((1, LANES), const), pl.BlockSpec((1, LANES), const),
                  st_spec, st_spec],
        out_specs=[pl.BlockSpec((1, t, SSD_WIDTH), lambda bi, c: (bi, c, 0)),
                   pl.BlockSpec((1, t, SSD_WIDTH), lambda bi, c: (bi, nc - 1 - c, 0)),
                   st_spec, st_spec],
        scratch_shapes=[pltpu.VMEM(st_shape[1:], F32), pltpu.VMEM(st_shape[1:], F32)],
        compiler_params=_cparams(("arbitrary", "arbitrary")),
        name="ssd_scan",
    )(xbc, dt, xbc, dt, avec, bias, h0f, h0b)
    return yf, yb, sf, sb


def _qk_prep_kernel(x_ref, g_ref, cos_ref, sin_ref, seg_ref, o_ref, *, rope, scale, kc):
    x = x_ref[0]
    sq = x * x
    hi = sq.astype(BF16)
    lo = (sq - hi.astype(F32)).astype(BF16)
    ssum = (jnp.dot(hi, seg_ref[...], preferred_element_type=F32)
            + jnp.dot(lo, seg_ref[...], preferred_element_type=F32))
    y = x * lax.rsqrt(ssum * (1.0 / DA_QK_DIM) + EPS) * g_ref[...]
    if rope:
        n = x.shape[1]
        lane = lax.broadcasted_iota(I32, x.shape, 1)
        first = (lane & (DA_QK_DIM - 1)) < DA_QK_DIM // 2
        partner = jnp.where(first, pltpu.roll(y, n - DA_QK_DIM // 2, 1), pltpu.roll(y, DA_QK_DIM // 2, 1))
        cosv = jnp.concatenate([cos_ref[...]] * (n // LANES), axis=1)
        sinv = jnp.concatenate([sin_ref[...]] * (n // LANES), axis=1)
        y = y * cosv + partner * sinv
    y = y * scale
    if kc:
        for h in range(DA_HEADS):
            head = y[:, h * 2 * DA_QK_DIM:(h + 1) * 2 * DA_QK_DIM]
            for cc in range(y.shape[0] // kc):
                t = head[cc * kc:(cc + 1) * kc, :].T.astype(BF16)
                o_ref[0, 2 * h, cc] = t[:DA_QK_DIM]
                o_ref[0, 2 * h + 1, cc] = t[DA_QK_DIM:]
    else:
        yb = y.astype(BF16)
        for j in range(2 * DA_HEADS):
            o_ref[0, j] = yb[:, j * DA_QK_DIM:(j + 1) * DA_QK_DIM]


def _rope_tables(s):
    pos = jnp.arange(s, dtype=I32)
    r = (pos // GRID_W).astype(F32)
    col = (pos % GRID_W).astype(F32)
    per_axis = DA_QK_DIM // 4
    inv = ROPE_BASE ** (-jnp.arange(per_axis, dtype=F32) / per_axis)
    ang = jnp.concatenate([r[:, None] * inv, col[:, None] * inv], axis=-1)
    cos, sin = jnp.cos(ang), jnp.sin(ang)
    cos64 = jnp.concatenate([cos, cos], axis=-1)
    sin64 = jnp.concatenate([-sin, sin], axis=-1)
    return jnp.concatenate([cos64, cos64], axis=-1), jnp.concatenate([sin64, sin64], axis=-1)


def _qk_prep(x, g, cos, sin, rope, scale, kc):
    b, s, n = x.shape
    ts = min(512, s)
    gfull = jnp.tile(g.astype(F32), n // DA_QK_DIM).reshape(1, n)
    idx = jnp.arange(n) // DA_QK_DIM
    seg = (idx[:, None] == idx[None, :]).astype(BF16)
    if kc:
        out_shape = jax.ShapeDtypeStruct((b, 2 * DA_HEADS, s // kc, DA_QK_DIM, kc), BF16)
        out_spec = pl.BlockSpec((1, 2 * DA_HEADS, ts // kc, DA_QK_DIM, kc), lambda bi, i: (bi, 0, i, 0, 0))
    else:
        out_shape = jax.ShapeDtypeStruct((b, 2 * DA_HEADS, s, DA_QK_DIM), BF16)
        out_spec = pl.BlockSpec((1, 2 * DA_HEADS, ts, DA_QK_DIM), lambda bi, i: (bi, 0, i, 0))
    return pl.pallas_call(
        functools.partial(_qk_prep_kernel, rope=rope, scale=scale, kc=kc),
        out_shape=out_shape,
        grid=(b, s // ts),
        in_specs=[pl.BlockSpec((1, ts, n), lambda bi, i: (bi, i, 0)),
                  pl.BlockSpec((1, n), lambda bi, i: (0, 0)),
                  pl.BlockSpec((ts, LANES), lambda bi, i: (i, 0)),
                  pl.BlockSpec((ts, LANES), lambda bi, i: (i, 0)),
                  pl.BlockSpec((n, n), lambda bi, i: (0, 0))],
        out_specs=out_spec,
        compiler_params=_cparams(("arbitrary", "arbitrary")),
        name="qk_prep",
    )(x, gfull, cos, sin, seg)


def _attn_kernel(lam_ref, q_ref, *refs, nseg, out_scale):
    lam = lam_ref[0, 0]
    segs = [(refs[2 * j], refs[2 * j + 1]) for j in range(nseg)]
    g_ref, o_ref = refs[2 * nseg], refs[2 * nseg + 1]
    tq = q_ref.shape[2]
    m = [jnp.full((tq, LANES), NEG_BIG, F32)] * 2
    l = [jnp.zeros((tq, LANES), F32)] * 2
    acc = [jnp.zeros((tq, DA_V_DIM), F32)] * 2

    for k_ref, v_ref in segs:
        nchunk, kc = k_ref.shape[2], k_ref.shape[4]
        for c in range(nchunk):
            v = v_ref[0, c * kc:(c + 1) * kc, :]
            for sub in range(2):
                s = jnp.dot(q_ref[0, sub], k_ref[0, sub, c], preferred_element_type=F32)
                mx = s[:, :LANES]
                for j in range(1, kc // LANES):
                    mx = jnp.maximum(mx, s[:, j * LANES:(j + 1) * LANES])
                m_new = jnp.maximum(m[sub], jnp.max(mx, axis=-1, keepdims=True))
                alpha = jnp.exp2(m[sub] - m_new)
                p = jnp.exp2(s - jnp.concatenate([m_new] * (kc // LANES), axis=1))
                lsum = alpha * l[sub]
                for j in range(kc // LANES):
                    lsum = lsum + p[:, j * LANES:(j + 1) * LANES]
                l[sub] = lsum
                acc[sub] = alpha * acc[sub] + jnp.dot(p.astype(BF16), v, preferred_element_type=F32)
                m[sub] = m_new

    l1 = jnp.sum(l[0], axis=-1, keepdims=True)
    l2 = jnp.sum(l[1], axis=-1, keepdims=True)
    o = acc[0] * (1.0 / l1) - lam * (acc[1] * (1.0 / l2))
    ms = jnp.mean(o * o, axis=-1, keepdims=True)
    o_ref[0] = o * lax.rsqrt(ms + EPS) * g_ref[...] * out_scale


def _diff_attn(qp, segments, lam, g, lam_init):
    b, _, s, d = qp.shape
    tq = min(ATT_TQ, s)
    seg_specs, seg_args = [], []
    for kt, v in segments:
        nchunk, kc = kt.shape[2], kt.shape[4]
        seg_specs += [pl.BlockSpec((1, 2, nchunk, d, kc), lambda bi, h, i: (bi, h, 0, 0, 0)),
                      pl.BlockSpec((1, nchunk * kc, DA_V_DIM), lambda bi, h, i: (bi, 0, h))]
        seg_args += [kt, v]
    return pl.pallas_call(
        functools.partial(_attn_kernel, nseg=len(segments), out_scale=1.0 - lam_init),
        out_shape=jax.ShapeDtypeStruct((b, s, DA_WIDTH), F32),
        grid=(b, DA_HEADS, s // tq),
        in_specs=[pl.BlockSpec(memory_space=pltpu.SMEM),
                  pl.BlockSpec((1, 2, tq, d), lambda bi, h, i: (bi, h, i, 0))] + seg_specs
        + [pl.BlockSpec((1, DA_V_DIM), lambda bi, h, i: (0, 0))],
        out_specs=pl.BlockSpec((1, tq, DA_V_DIM), lambda bi, h, i: (bi, i, h)),
        compiler_params=_cparams(("arbitrary", "arbitrary", "arbitrary")),
        name="diff_attn",
    )(lam.reshape(1, 1).astype(F32), qp, *seg_args, g.reshape(1, DA_V_DIM).astype(F32))


def _gelu_tanh(x):
    return 0.5 * x * (1.0 + jnp.tanh(math.sqrt(2.0 / math.pi) * (x + 0.044715 * x * x * x)))


def _proj_out_kernel(x_ref, lg_ref, hf_ref, hb_ref, yf_ref, yb_ref, xs_ref, z_ref, da_ref,
                     dvec_ref, ng_ref, gate_ref, w_ref, o_ref):
    lru = _gelu_tanh(lg_ref[0]) * (hf_ref[0] + hb_ref[0])
    y = yf_ref[0] + yb_ref[0] + dvec_ref[...] * xs_ref[0]
    y = y * _silu(z_ref[0])
    ssd = y * lax.rsqrt(jnp.mean(y * y, axis=-1, keepdims=True) + EPS) * ng_ref[...]
    o = jnp.dot(lru.astype(BF16), w_ref[0:LRU_WIDTH, :], preferred_element_type=F32)
    o = o + jnp.dot(ssd.astype(BF16), w_ref[LRU_WIDTH:LRU_WIDTH + SSD_WIDTH, :], preferred_element_type=F32)
    o = o + jnp.dot(da_ref[0].astype(BF16), w_ref[LRU_WIDTH + SSD_WIDTH:, :], preferred_element_type=F32)
    o_ref[0] = x_ref[0] + gate_ref[0] * o


def _proj_out(x, lg, hf, hb, yf, yb, xbc, z, da, dvec, ng, gate, w):
    b, s, d = x.shape
    tm = min(PROJ_TM, s)
    tok = lambda wd: pl.BlockSpec((1, tm, wd), lambda bi, i: (bi, i, 0))
    const2 = lambda bi, i: (0, 0)
    return pl.pallas_call(
        _proj_out_kernel,
        out_shape=jax.ShapeDtypeStruct((b, s, d), F32),
        grid=(b, s // tm),
        in_specs=[tok(d), tok(LRU_WIDTH), tok(LRU_WIDTH), tok(LRU_WIDTH), tok(SSD_WIDTH),
                  tok(SSD_WIDTH), tok(SSD_WIDTH), tok(SSD_WIDTH), tok(DA_WIDTH),
                  pl.BlockSpec((1, SSD_WIDTH), const2), pl.BlockSpec((1, SSD_WIDTH), const2),
                  pl.BlockSpec((1, 1, d), lambda bi, i: (bi, 0, 0)),
                  pl.BlockSpec(w.shape, const2)],
        out_specs=tok(d),
        compiler_params=_cparams(("arbitrary", "arbitrary")),
        name="proj_out",
    )(x, lg, hf, hb, yf, yb, xbc, z, da, dvec, ng, gate, w)


def _route_kernel(x_ref, g_ref, sh_ref, sc_ref, rwh_ref, rwl_ref, rb_ref, c0_ref, h_ref, ri_ref, gt_ref,
                  slt_ref, n_ref, base_ref, cnt_ref, carry_s):
    i = pl.program_id(0)
    tm = x_ref.shape[0]

    @pl.when(i == 0)
    def _():
        carry_s[...] = c0_ref[...]

    h = _norm_mod(x_ref[...], g_ref[...], sh_ref[0], sc_ref[0])
    h_hi = h.astype(BF16)
    h_ref[...] = h_hi
    h_lo = (h - h_hi.astype(F32)).astype(BF16)
    logits = (jnp.dot(h_hi, rwh_ref[...], preferred_element_type=F32)
              + jnp.dot(h_hi, rwl_ref[...], preferred_element_type=F32)
              + jnp.dot(h_lo, rwh_ref[...], preferred_element_type=F32)) + rb_ref[...]
    lane = lax.broadcasted_iota(I32, logits.shape, 1)
    lane_f = lane.astype(F32)
    vals, idxs = [], []
    cur = logits
    for _ in range(TOP_K):
        m = jnp.max(cur, axis=-1, keepdims=True)
        idx = jnp.min(jnp.where(cur >= m, lane_f, float(LANES)), axis=-1, keepdims=True).astype(I32)
        vals.append(m)
        idxs.append(idx)
        cur = jnp.where(lane == idx, 2.0 * NEG_BIG, cur)
    es = [jnp.exp(v - vals[0]) for v in vals]
    inv = 1.0 / (es[0] + es[1] + es[2] + es[3])
    onehot = jnp.zeros(logits.shape, F32)
    for idx in idxs:
        onehot = onehot + jnp.where(lane == idx, 1.0, 0.0)
    ri = lax.broadcasted_iota(I32, (tm, tm), 0)
    ci = lax.broadcasted_iota(I32, (tm, tm), 1)
    strict = jnp.where(ci < ri, 1.0, 0.0).astype(BF16)
    before = jnp.dot(strict, onehot.astype(BF16), preferred_element_type=F32)
    n_tile = jnp.sum(onehot, axis=0, keepdims=True).astype(I32)
    pad_tile = _round_up8(n_tile)
    li = lax.broadcasted_iota(I32, (LANES, LANES), 0)
    lj = lax.broadcasted_iota(I32, (LANES, LANES), 1)
    below = jnp.where(li < lj, 1.0, 0.0).astype(BF16)
    pad_rows = jnp.broadcast_to(pad_tile.astype(F32), (SUBLANES, LANES)).astype(BF16)
    off = jnp.dot(pad_rows, below, preferred_element_type=F32)[0:1, :]
    place = before + off
    ri_out = jnp.zeros(logits.shape, I32)
    gt_out = jnp.zeros(logits.shape, F32)
    sl_f = jnp.full(logits.shape, -1.0, F32)
    for k in range(TOP_K):
        sloc = jnp.sum(jnp.where(lane == idxs[k], place, 0.0), axis=-1, keepdims=True)
        ri_out = ri_out + jnp.where(lane == k, idxs[k], 0) + jnp.where(lane == TOP_K + k, sloc.astype(I32), 0)
        sl_f = jnp.where(lane == k, sloc, sl_f)
        gt_out = gt_out + jnp.where(lane == k, es[k] * inv, 0.0)
    ri_ref[...] = ri_out
    gt_ref[...] = gt_out
    slt_ref[0] = sl_f.T[0:SUBLANES, :].astype(I32)
    n_ref[0] = n_tile
    base_ref[0] = carry_s[...]
    carry_s[...] = carry_s[...] + pad_tile
    cnt_ref[...] = carry_s[...]


def _route(x, g, sh, sc, tiles_per_mod, rw, rb, counts0):
    t, d = x.shape
    tm = ROUTE_TM
    tile = lambda wd: pl.BlockSpec((tm, wd), lambda i: (i, 0))
    const = lambda i: (0, 0)
    mod = pl.BlockSpec((1, 1, d), lambda i: (i // tiles_per_mod, 0, 0))
    nt = t // tm
    per_tile = pl.BlockSpec((1, 1, LANES), lambda i: (i, 0, 0))
    return pl.pallas_call(
        _route_kernel,
        out_shape=[jax.ShapeDtypeStruct((t, d), BF16), jax.ShapeDtypeStruct((t, LANES), I32),
                   jax.ShapeDtypeStruct((t, LANES), F32), jax.ShapeDtypeStruct((nt, SUBLANES, tm), I32),
                   jax.ShapeDtypeStruct((nt, 1, LANES), I32),
                   jax.ShapeDtypeStruct((nt, 1, LANES), I32), jax.ShapeDtypeStruct((1, LANES), I32)],
        grid=(nt,),
        in_specs=[tile(d), pl.BlockSpec((1, d), const), mod, mod,
                  pl.BlockSpec((d, LANES), const), pl.BlockSpec((d, LANES), const),
                  pl.BlockSpec((1, LANES), const), pl.BlockSpec((1, LANES), const)],
        out_specs=[tile(d), tile(LANES), tile(LANES), pl.BlockSpec((1, SUBLANES, tm), lambda i: (i, 0, 0)),
                   per_tile, per_tile, pl.BlockSpec((1, LANES), const)],
        scratch_shapes=[pltpu.VMEM((1, LANES), I32)],
        compiler_params=_cparams(("arbitrary",)),
        name="moe_route",
    )(x, g.reshape(1, d), sh, sc, rw[0], rw[1], rb, counts0)


MOE_STAGE = -(-(ROUTE_TM * TOP_K + N_EXPERTS * (SUBLANES - 1)) // 256) * 256
_TAB_N, _TAB_DST, _TAB_OFF, _TAB_TOTAL = 0, N_EXPERTS, 2 * N_EXPERTS, 3 * N_EXPERTS


U32 = jnp.uint32
_HI16 = 0xFFFF0000


def _pack_halves(a):
    n = a.shape[1] // 2
    lo = pltpu.bitcast(a[:, :n].astype(BF16).astype(F32), U32)
    hi = pltpu.bitcast(a[:, n:].astype(BF16).astype(F32), U32)
    return (lo >> 16) | (hi & U32(_HI16))


def _unpack_halves(u):
    lo = pltpu.bitcast(u << 16, F32).astype(BF16)
    hi = pltpu.bitcast(u & U32(_HI16), F32).astype(BF16)
    return lo, hi


def _chunk_copies(tab_ref, stage_ref, rows_hbm, sem, to_hbm):
    def expert(e, carry):
        dst = tab_ref[0, 0, _TAB_DST + e]
        off = tab_ref[0, 0, _TAB_OFF + e]

        def chunk(j, c):
            s = stage_ref.at[pl.ds(pl.multiple_of(off + j * SUBLANES, SUBLANES), SUBLANES), :]
            r = rows_hbm.at[pl.ds(pl.multiple_of(dst + j * SUBLANES, SUBLANES), SUBLANES), :]
            if to_hbm:
                pltpu.make_async_copy(s, r, sem).start()
            else:
                pltpu.make_async_copy(r, s, sem).start()
            return c
        return lax.fori_loop(0, tab_ref[0, 0, _TAB_N + e], chunk, carry)
    lax.fori_loop(0, N_EXPERTS, expert, 0)


def _chunk_waits(tab_ref, stage_ref, rows_hbm, sem):
    def one(j, c):
        pltpu.make_async_copy(rows_hbm.at[pl.ds(0, SUBLANES), :], stage_ref.at[pl.ds(0, SUBLANES), :], sem).wait()
        return c
    lax.fori_loop(0, tab_ref[0, 0, _TAB_TOTAL], one, 0)


def _dispatch_kernel(tab_ref, tprev_ref, tail_ref, sloc_ref, *refs, tiles):
    h_refs = refs[:len(tiles)]
    xs_out, stage_s, zero_s, sems = refs[len(tiles):]
    i = pl.program_id(0)
    n_steps = sum(tiles)
    slot = i % 2
    row = lax.broadcasted_iota(I32, (MOE_STAGE, ROUTE_TM), 0)
    perm = jnp.zeros((MOE_STAGE, ROUTE_TM), F32)
    for k in range(TOP_K):
        perm = perm + jnp.where(row == sloc_ref[0, k:k + 1, :], 1.0, 0.0)
    perm = perm.astype(BF16)
    off = 0
    for h_ref, nt in zip(h_refs, tiles):
        @pl.when((i >= off) & (i < off + nt))
        def _(h_ref=h_ref):
            stage_s[slot] = _pack_halves(jnp.dot(perm, h_ref[...], preferred_element_type=F32))
        off += nt
    _chunk_copies(tab_ref, stage_s.at[slot], xs_out, sems.at[slot], True)

    @pl.when(i > 0)
    def _():
        _chunk_waits(tprev_ref, stage_s.at[1 - slot], xs_out, sems.at[1 - slot])

    @pl.when(i == n_steps - 1)
    def _():
        _chunk_waits(tab_ref, stage_s.at[slot], xs_out, sems.at[slot])
        sem = sems.at[slot]
        zero_s[...] = jnp.zeros(zero_s.shape, U32)
        zr = zero_s.shape[0]
        n_big, n_small, start = tail_ref[0], tail_ref[1], tail_ref[2]

        def big(j, c):
            pltpu.make_async_copy(zero_s, xs_out.at[pl.ds(pl.multiple_of(start + j * zr, SUBLANES), zr), :], sem).start()
            return c

        def small(j, c):
            r0 = pl.multiple_of(start + n_big * zr + j * SUBLANES, SUBLANES)
            pltpu.make_async_copy(zero_s.at[pl.ds(0, SUBLANES), :], xs_out.at[pl.ds(r0, SUBLANES), :], sem).start()
            return c

        def big_wait(j, c):
            pltpu.make_async_copy(zero_s, xs_out.at[pl.ds(0, zr), :], sem).wait()
            return c

        def small_wait(j, c):
            pltpu.make_async_copy(zero_s.at[pl.ds(0, SUBLANES), :], xs_out.at[pl.ds(0, SUBLANES), :], sem).wait()
            return c

        lax.fori_loop(0, n_big, big, 0)
        lax.fori_loop(0, n_small, small, 0)
        lax.fori_loop(0, n_big, big_wait, 0)
        lax.fori_loop(0, n_small, small_wait, 0)


def _dispatch(hs, tabs, slocs, tail, n_max):
    d = hs[0].shape[1]
    tm = ROUTE_TM
    tiles = tuple(h.shape[0] // tm for h in hs)
    specs, off = [], 0
    for nt in tiles:
        specs.append(pl.BlockSpec((tm, d), lambda i, off=off, nt=nt: (jnp.clip(i - off, 0, nt - 1), 0)))
        off += nt
    return pl.pallas_call(
        functools.partial(_dispatch_kernel, tiles=tiles),
        out_shape=jax.ShapeDtypeStruct((n_max, d // 2), U32),
        grid=(sum(tiles),),
        in_specs=[pl.BlockSpec((1, 1, LANES), lambda i: (i, 0, 0), memory_space=pltpu.SMEM),
                  pl.BlockSpec((1, 1, LANES), lambda i: (jnp.maximum(i - 1, 0), 0, 0), memory_space=pltpu.SMEM),
                  pl.BlockSpec(memory_space=pltpu.SMEM),
                  pl.BlockSpec((1, SUBLANES, tm), lambda i: (i, 0, 0))] + specs,
        out_specs=pl.BlockSpec(memory_space=pl.ANY),
        scratch_shapes=[pltpu.VMEM((2, MOE_STAGE, d // 2), U32), pltpu.VMEM((MOE_TM, d // 2), U32),
                        pltpu.SemaphoreType.DMA((2,))],
        compiler_params=_cparams(("arbitrary",)),
        name="moe_dispatch",
    )(tabs, tabs, tail, slocs, *hs)


def _gmm_kernel(vt_ref, ve_ref, lo_ref, hi_ref, first_ref, xs_ref, wg_ref, wu_ref, bg_ref, bu_ref,
                wd_ref, bd_ref, y_ref):
    del vt_ref, ve_ref
    v = pl.program_id(0)
    lo, hi = lo_ref[v], hi_ref[v]

    @pl.when(hi > lo)
    def _():
        x_lo, x_hi = _unpack_halves(xs_ref[...])
        half = x_lo.shape[1]
        gate = (jnp.dot(x_lo, wg_ref[:half, :], preferred_element_type=F32)
                + jnp.dot(x_hi, wg_ref[half:, :], preferred_element_type=F32)) + bg_ref[...]
        up = (jnp.dot(x_lo, wu_ref[:half, :], preferred_element_type=F32)
              + jnp.dot(x_hi, wu_ref[half:, :], preferred_element_type=F32)) + bu_ref[...]
        gate = jnp.minimum(gate, SWIGLU_LIMIT)
        up = jnp.clip(up, -SWIGLU_LIMIT, SWIGLU_LIMIT)
        glu = gate * _sigmoid(SWIGLU_ALPHA * gate)
        act = ((up + 1.0) * glu).astype(BF16)
        down = lambda: _pack_halves(jnp.dot(act, wd_ref[...], preferred_element_type=F32) + bd_ref[...])
        whole = (lo == 0) & (hi == xs_ref.shape[0])
        first = first_ref[v] > 0
        row = lax.broadcasted_iota(I32, y_ref.shape, 0)
        mine = (row >= lo) & (row < hi)

        @pl.when(whole)
        def _():
            y_ref[...] = down()

        @pl.when(jnp.logical_not(whole) & first)
        def _():
            y_ref[...] = jnp.where(mine, down(), U32(0))

        @pl.when(jnp.logical_not(whole) & jnp.logical_not(first))
        def _():
            y_ref[...] = jnp.where(mine, down(), y_ref[...])

    @pl.when((hi <= lo) & (first_ref[v] > 0))
    def _():
        y_ref[...] = jnp.zeros(y_ref.shape, U32)


def _gmm(xs, visits, wg, wu, bg, bu, wd, bd):
    n_rows = xs.shape[0]
    d = wg.shape[1]
    de = wg.shape[2]
    tm = MOE_TM
    nv = visits[0].shape[0]
    ex = lambda shape: pl.BlockSpec((None,) + shape, lambda v, vt, ve, lo, hi, fi: (ve[v], 0, 0))
    rows = pl.BlockSpec((tm, d // 2), lambda v, vt, ve, lo, hi, fi: (vt[v], 0))
    grid_spec = pltpu.PrefetchScalarGridSpec(
        num_scalar_prefetch=5,
        grid=(nv,),
        in_specs=[rows, ex((d, de)), ex((d, de)), ex((1, de)), ex((1, de)), ex((de, d)), ex((1, d))],
        out_specs=rows,
    )
    return pl.pallas_call(
        _gmm_kernel,
        out_shape=jax.ShapeDtypeStruct((n_rows, d // 2), U32),
        grid_spec=grid_spec,
        compiler_params=_cparams(("arbitrary",)),
        name="moe_experts",
    )(*visits, xs, wg, wu, bg, bu, wd, bd)


def _visit_tables(cnt, n_rows):
    tm = MOE_TM
    nt = n_rows // tm
    nv = nt + N_EXPERTS - 1
    end = jnp.cumsum(cnt)
    start = end - cnt
    first_tile = start // tm
    last_tile = jnp.maximum(end - 1, 0) // tm
    nvis = jnp.where(cnt > 0, last_tile - first_tile + 1, 0)
    vend = jnp.cumsum(nvis)
    vstart = vend - nvis
    v = jnp.arange(nv, dtype=I32)
    e = jnp.minimum(jnp.sum((vend[None, :] <= v[:, None]).astype(I32), axis=1), N_EXPERTS - 1)
    n_used = vend[N_EXPERTS - 1]
    used = v < n_used
    tiles_used = (end[N_EXPERTS - 1] + tm - 1) // tm
    tile = jnp.where(used, first_tile[e] + v - vstart[e], jnp.minimum(tiles_used + v - n_used, nt - 1))
    lo = jnp.where(used, jnp.clip(start[e] - tile * tm, 0, tm), 0)
    hi = jnp.where(used, jnp.clip(end[e] - tile * tm, 0, tm), 0)
    prev = jnp.concatenate([jnp.full((1,), -1, I32), tile[:-1].astype(I32)])
    first = (tile != prev).astype(I32)
    return tuple(a.astype(I32) for a in (tile, e, lo, hi, first))


_SPLIT_W = 2 * LANES


def _split_gu_kernel(w_ref, p_ref, g_ref, u_ref):
    w = w_ref[...].astype(BF16)
    for j in range(w.shape[1] // _SPLIT_W):
        r = jnp.dot(w[:, j * _SPLIT_W:(j + 1) * _SPLIT_W], p_ref[...], preferred_element_type=F32)
        g_ref[:, j * LANES:(j + 1) * LANES] = r[:, :LANES].astype(BF16)
        u_ref[:, j * LANES:(j + 1) * LANES] = r[:, LANES:].astype(BF16)


def _split_gu(w_gu, layer):
    _, e, d, n2 = w_gu.shape
    tn = 2 * _SPLIT_W
    src = jnp.arange(_SPLIT_W)
    dst = jnp.where(src % 2 == 0, src // 2, LANES + src // 2)
    perm = (dst[:, None] == jnp.arange(_SPLIT_W)[None, :]).astype(BF16)
    return pl.pallas_call(
        _split_gu_kernel,
        out_shape=[jax.ShapeDtypeStruct((e, d, n2 // 2), BF16)] * 2,
        grid=(e, n2 // tn),
        in_specs=[pl.BlockSpec((None, None, d, tn), lambda ei, j: (layer, ei, 0, j)),
                  pl.BlockSpec((_SPLIT_W, _SPLIT_W), lambda ei, j: (0, 0))],
        out_specs=[pl.BlockSpec((None, d, tn // 2), lambda ei, j: (ei, 0, j))] * 2,
        compiler_params=_cparams(("arbitrary", "arbitrary")),
        name="split_gate_up",
    )(w_gu, perm)


def _combine_kernel(tcur_ref, tnext_ref, sloc_ref, gt_ref, x_ref, g2_ref, y_hbm, o_ref, stage_s, sems, *, n):
    i = pl.program_id(0)
    slot = i % 2

    @pl.when(i == 0)
    def _():
        stage_s[...] = jnp.zeros(stage_s.shape, U32)
        _chunk_copies(tcur_ref, stage_s.at[0], y_hbm, sems.at[0], False)

    @pl.when(i + 1 < n)
    def _():
        _chunk_copies(tnext_ref, stage_s.at[1 - slot], y_hbm, sems.at[1 - slot], False)

    _chunk_waits(tcur_ref, stage_s.at[slot], y_hbm, sems.at[slot])
    rows_lo, rows_hi = _unpack_halves(stage_s[slot])
    lane = lax.broadcasted_iota(I32, (x_ref.shape[0], MOE_STAGE), 1)
    gt = gt_ref[...]
    sloc = sloc_ref[...]
    gmat = jnp.zeros(lane.shape, F32)
    for k in range(TOP_K):
        gmat = gmat + jnp.where(lane == sloc[:, TOP_K + k:TOP_K + k + 1], gt[:, k:k + 1], 0.0)
    g_hi = gmat.astype(BF16)
    g_lo = (gmat - g_hi.astype(F32)).astype(BF16)
    half = rows_lo.shape[1]
    for rows, sl in ((rows_lo, slice(0, half)), (rows_hi, slice(half, 2 * half))):
        f = (jnp.dot(g_hi, rows, preferred_element_type=F32)
             + jnp.dot(g_lo, rows, preferred_element_type=F32))
        o_ref[:, sl] = x_ref[:, sl] + g2_ref[0][:, sl] * f


def _combine(y, tabs, sloc, gates, x, g2, tiles_per_mod):
    t, d = x.shape
    tm = ROUTE_TM
    nt = t // tm
    return pl.pallas_call(
        functools.partial(_combine_kernel, n=nt),
        out_shape=jax.ShapeDtypeStruct((t, d), F32),
        grid=(nt,),
        in_specs=[pl.BlockSpec((1, 1, LANES), lambda i: (i, 0, 0), memory_space=pltpu.SMEM),
                  pl.BlockSpec((1, 1, LANES), lambda i: (jnp.minimum(i + 1, nt - 1), 0, 0),
                               memory_space=pltpu.SMEM),
                  pl.BlockSpec((tm, LANES), lambda i: (i, 0)),
                  pl.BlockSpec((tm, LANES), lambda i: (i, 0)),
                  pl.BlockSpec((tm, d), lambda i: (i, 0)),
                  pl.BlockSpec((1, 1, d), lambda i: (i // tiles_per_mod, 0, 0)),
                  pl.BlockSpec(memory_space=pl.ANY)],
        out_specs=pl.BlockSpec((tm, d), lambda i: (i, 0)),
        scratch_shapes=[pltpu.VMEM((2, MOE_STAGE, d // 2), U32), pltpu.SemaphoreType.DMA((2,))],
        compiler_params=_cparams(("arbitrary",)),
        name="moe_combine",
    )(tabs, tabs, sloc, gates, x, g2, y)


def _moe(streams, layer, g, router_w, router_b, w_gu_all, b_gu, w_down, b_down):
    d = streams[0][0].shape[1]
    tm = ROUTE_TM
    rw_f = jnp.zeros((d, LANES), F32).at[:, :N_EXPERTS].set(router_w.astype(F32))
    rw_hi = rw_f.astype(BF16)
    rw = (rw_hi, (rw_f - rw_hi.astype(F32)).astype(BF16))
    rb = jnp.full((1, LANES), NEG_BIG, F32).at[0, :N_EXPERTS].set(router_b.astype(F32))
    counts = jnp.zeros((1, LANES), I32)
    routed = []
    for x_tok, sh, sc, _, tpm in streams:
        h, route_i, gates, sloc_t, n_tile, seg_base, counts = _route(x_tok, g, sh, sc, tpm, rw, rb, counts)
        routed.append((h, route_i, gates, sloc_t, n_tile, seg_base))
    n_tiles = sum(x_tok.shape[0] for x_tok, *_ in streams) // tm
    n_max = n_tiles * tm * TOP_K + n_tiles * N_EXPERTS * (SUBLANES - 1)
    n_max = (n_max + MOE_TM - 1) // MOE_TM * MOE_TM
    cnt = counts[0, :N_EXPERTS]
    total = jnp.sum(cnt)
    base = jnp.cumsum(cnt) - cnt
    tabs = []
    for h, _, _, _, n_tile, seg_base in routed:
        nt = h.shape[0] // tm
        pad = _round_up8(n_tile[:, 0, :N_EXPERTS])
        off = jnp.cumsum(pad, axis=1) - pad
        dst = base[None, :] + seg_base[:, 0, :N_EXPERTS]
        nch = pad // SUBLANES
        tab = jnp.concatenate([nch, dst, off, jnp.sum(nch, axis=1, keepdims=True),
                               jnp.zeros((nt, LANES - 3 * N_EXPERTS - 1), I32)], axis=1)
        tabs.append(tab.astype(I32).reshape(nt, 1, LANES))
    free = n_max - total
    tail = jnp.stack([free // MOE_TM, (free % MOE_TM) // SUBLANES, total]).astype(I32)
    xs = _dispatch([r[0] for r in routed], jnp.concatenate(tabs, axis=0),
                   jnp.concatenate([r[3] for r in routed], axis=0), tail, n_max)
    wg, wu = _split_gu(w_gu_all, layer)
    bg = b_gu[:, None, 0::2].astype(F32)
    bu = b_gu[:, None, 1::2].astype(F32)
    y = _gmm(xs, _visit_tables(cnt, n_max), wg, wu, bg, bu, w_down.astype(BF16),
             b_down[:, None, :].astype(F32))
    return [_combine(y, tab, r[1], r[2], x_tok, g2, tpm)
            for (x_tok, _, _, g2, tpm), r, tab in zip(streams, routed, tabs)]


def kernel(x, c, ctx, c_ctx, w_mod, b_mod, norm1_g, norm2_g, w_in, w_out, lru_conv_w, lru_conv_b, lru_wa, lru_ba, lru_wx, lru_bx, lru_lam, ssd_conv_w, ssd_conv_b, ssd_a_log, ssd_dt_bias, ssd_d, ssd_norm_g, da_q_norm, da_k_norm, da_lam_q, da_lam_k, da_subln_g, router_w, router_b, exp_w_gu, exp_b_gu, exp_w_down, exp_b_down):
    b, s, d = x.shape
    n_ctx = ctx.shape[1]
    depth = w_mod.shape[0]
    cos, sin = _rope_tables(s)
    x_lat, x_ctx = x, ctx
    mod_rows = jnp.zeros((2 * SUBLANES, d), F32).at[:b].set(c).at[b].set(c_ctx)

    for l in range(depth):
        need_ctx = l < depth - 1
        lam_init = 0.8 - 0.6 * math.exp(-0.3 * l)
        mod = _modulation(mod_rows, w_mod[l], b_mod[l])
        m_lat = mod[:b].reshape(b, 1, 6, d)
        m_ctx = jnp.broadcast_to(mod[b].reshape(1, 1, 6, d), (b, 1, 6, d))
        part = lambda m, j: m[:, :, j, :]

        w_in_l = _prep_w_in(w_in[l])
        p_lat = _proj_in(x_lat, norm1_g[l], part(m_lat, 0), part(m_lat, 1), w_in_l)
        p_ctx = _proj_in(x_ctx, norm1_g[l], part(m_ctx, 0), part(m_ctx, 1), w_in_l)

        zl = jnp.zeros((b, LRU_WIDTH), F32)
        lru_args = (lru_wa[l], lru_ba[l], lru_wx[l], lru_bx[l], lru_lam[l])
        xc_ctx = _dwconv(p_ctx["lx"], lru_conv_w[l], lru_conv_b[l], act=False)
        hf_c, hb_c = _lru_scan(xc_ctx, *lru_args, zl, zl)
        xc_lat = _dwconv(p_lat["lx"], lru_conv_w[l], lru_conv_b[l], act=False)
        hf_l, hb_l = _lru_scan(xc_lat, *lru_args, hf_c[:, -1], hb_c[:, 0])

        zs = jnp.zeros((b, SSD_GROUPS, 2 * SSD_HEAD_DIM, SSD_STATE), F32)
        xbc_ctx = _dwconv(p_ctx["xbc"], ssd_conv_w[l], ssd_conv_b[l], act=True)
        yf_c, yb_c, sf_c, sb_c = _ssd_scan(xbc_ctx, p_ctx["dt"], ssd_a_log[l], ssd_dt_bias[l], zs, zs)
        xbc_lat = _dwconv(p_lat["xbc"], ssd_conv_w[l], ssd_conv_b[l], act=True)
        yf_l, yb_l, _, _ = _ssd_scan(xbc_lat, p_lat["dt"], ssd_a_log[l], ssd_dt_bias[l], sf_c, sb_c)

        lq = da_lam_q[l].astype(F32)
        lk = da_lam_k[l].astype(F32)
        lam = jnp.exp(jnp.sum(lq[0] * lk[0])) - jnp.exp(jnp.sum(lq[1] * lk[1])) + lam_init
        kc = _qk_prep(p_ctx["k"], da_k_norm[l], cos, sin, rope=False, scale=1.0, kc=min(ATT_KC, n_ctx))
        kl = _qk_prep(p_lat["k"], da_k_norm[l], cos, sin, rope=True, scale=1.0, kc=min(ATT_KC, s))
        ql = _qk_prep(p_lat["q"], da_q_norm[l], cos, sin, rope=True, scale=DA_SCALE * LOG2E, kc=0)
        da_l = _diff_attn(ql, [(kc, p_ctx["v"]), (kl, p_lat["v"])], lam, da_subln_g[l], lam_init)

        dvec = jnp.repeat(ssd_d[l].astype(F32), SSD_HEAD_DIM).reshape(1, SSD_WIDTH)
        ng = ssd_norm_g[l].astype(F32).reshape(1, SSD_WIDTH)
        w_out_l = w_out[l].astype(BF16)
        if need_ctx:
            qc = _qk_prep(p_ctx["q"], da_q_norm[l], cos, sin, rope=False, scale=DA_SCALE * LOG2E, kc=0)
            da_c = _diff_attn(qc, [(kc, p_ctx["v"])], lam, da_subln_g[l], lam_init)
            x_ctx = _proj_out(x_ctx, p_ctx["lg"], hf_c, hb_c, yf_c, yb_c, xbc_ctx, p_ctx["z"], da_c,
                              dvec, ng, part(m_ctx, 2), w_out_l)
        x_lat = _proj_out(x_lat, p_lat["lg"], hf_l, hb_l, yf_l, yb_l, xbc_lat, p_lat["z"], da_l,
                          dvec, ng, part(m_lat, 2), w_out_l)

        moe_w = (router_w[l], router_b[l], exp_w_gu, exp_b_gu[l], exp_w_down[l], exp_b_down[l])
        streams = [(x_lat.reshape(-1, d), part(m_lat, 3), part(m_lat, 4), part(m_lat, 5), s // ROUTE_TM)]
        if need_ctx:
            streams.append((x_ctx.reshape(-1, d), part(m_ctx, 3), part(m_ctx, 4), part(m_ctx, 5), n_ctx // ROUTE_TM))
        outs = _moe(streams, l, norm2_g[l], *moe_w)
        x_lat = outs[0].reshape(b, s, d)
        if need_ctx:
            x_ctx = outs[1].reshape(b, n_ctx, d)
    return x_lat
```

```python
import functools
import math

import jax
import jax.numpy as jnp
from jax import lax
from jax.experimental import pallas as pl
from jax.experimental.pallas import tpu as pltpu

F32 = jnp.float32
BF16 = jnp.bfloat16
I32 = jnp.int32

GRID_W = 64
EPS = 1e-6
CONV_K = 4
LRU_HEADS = 4
LRU_HEAD_DIM = 64
LRU_WIDTH = 256
LRU_C = 8.0
SSD_HEADS = 4
SSD_HEAD_DIM = 64
SSD_WIDTH = 256
SSD_GROUPS = 2
SSD_STATE = 128
SSD_CHUNK = 128
SSD_CONV_DIM = 768
DA_HEADS = 4
DA_QK_DIM = 64
DA_V_DIM = 128
DA_QK_WIDTH = 512
DA_WIDTH = 512
DA_SCALE = DA_QK_DIM ** -0.5
ROPE_BASE = 10000.0
N_EXPERTS = 32
TOP_K = 4
SWIGLU_ALPHA = 1.702
SWIGLU_LIMIT = 7.0

LANES = 128
SUBLANES = 8
VMEM_LIMIT = 56 * 1024 * 1024

PROJ_TM = 512
ATT_TQ = 512
ATT_KC = 256
LOG2E = math.log2(math.e)
LRU_TL = 512
SSD_STEP_CHUNKS = 4
ROUTE_TM = 256
MOE_TM = 512
NEG_BIG = -1e30


def _cparams(sem):
    return pltpu.CompilerParams(dimension_semantics=sem, vmem_limit_bytes=VMEM_LIMIT)


def _sigmoid(x):
    return 0.5 * jnp.tanh(0.5 * x) + 0.5


def _silu(x):
    return x * _sigmoid(x)


def _softplus(x):
    return jnp.maximum(x, 0.0) + jnp.log(1.0 + jnp.exp(-jnp.abs(x)))


def _round_up8(n):
    return ((n + (SUBLANES - 1)) >> 3) << 3


def _mod_kernel(c_ref, w_ref, b_ref, o_ref):
    a = _silu(c_ref[...])
    o_ref[...] = jnp.dot(a, w_ref[...], preferred_element_type=F32,
                         precision=lax.Precision.HIGHEST) + b_ref[...]


def _modulation(rows, w, b):
    m, d = rows.shape
    n = w.shape[1]
    tn = 512
    return pl.pallas_call(
        _mod_kernel,
        out_shape=jax.ShapeDtypeStruct((m, n), F32),
        grid=(n // tn,),
        in_specs=[pl.BlockSpec((m, d), lambda j: (0, 0)),
                  pl.BlockSpec((d, tn), lambda j: (0, j)),
                  pl.BlockSpec((1, tn), lambda j: (0, j))],
        out_specs=pl.BlockSpec((m, tn), lambda j: (0, j)),
        compiler_params=_cparams(("arbitrary",)),
        name="modulation",
    )(rows, w, b.reshape(1, n))


def _norm_mod(x, g, sh, sc):
    ms = jnp.mean(x * x, axis=-1, keepdims=True)
    y = x * lax.rsqrt(ms + EPS) * g
    return y * (1.0 + sc) + sh


def _proj_in_kernel(x_ref, g_ref, sh_ref, sc_ref, w_ref, *o_refs, splits):
    h = _norm_mod(x_ref[0], g_ref[...], sh_ref[0], sc_ref[0]).astype(BF16)
    for o_ref, (lo, hi) in zip(o_refs, splits):
        o_ref[0] = jnp.dot(h, w_ref[:, lo:hi], preferred_element_type=F32).astype(o_ref.dtype)


_IN_GROUPS = (("lg", 256, F32), ("lx", 256, F32), ("z", 256, F32), ("xbc", 768, F32),
              ("dt", 128, F32), ("q", 512, F32), ("k", 512, F32), ("v", 512, BF16))


def _prep_w_in(w_in):
    d = w_in.shape[0]
    lo = 2 * LRU_WIDTH + SSD_WIDTH + SSD_CONV_DIM
    hi = lo + 2 * SSD_HEADS
    dt = jnp.concatenate([w_in[:, lo:hi], jnp.zeros((d, LANES - (hi - lo)), w_in.dtype)], axis=1)
    w = jnp.concatenate([w_in[:, :lo], dt, w_in[:, hi:]], axis=1)
    return w.astype(BF16)


def _proj_in(x, g, sh, sc, w):
    b, s, d = x.shape
    tm = min(PROJ_TM, s)
    splits, off = [], 0
    for _, wd, _ in _IN_GROUPS:
        splits.append((off, off + wd))
        off += wd
    out_shape = [jax.ShapeDtypeStruct((b, s, wd), dt) for _, wd, dt in _IN_GROUPS]
    out_specs = [pl.BlockSpec((1, tm, wd), lambda bi, i: (bi, i, 0)) for _, wd, _ in _IN_GROUPS]
    outs = pl.pallas_call(
        functools.partial(_proj_in_kernel, splits=tuple(splits)),
        out_shape=out_shape,
        grid=(b, s // tm),
        in_specs=[pl.BlockSpec((1, tm, d), lambda bi, i: (bi, i, 0)),
                  pl.BlockSpec((1, d), lambda bi, i: (0, 0)),
                  pl.BlockSpec((1, 1, d), lambda bi, i: (bi, 0, 0)),
                  pl.BlockSpec((1, 1, d), lambda bi, i: (bi, 0, 0)),
                  pl.BlockSpec((d, off), lambda bi, i: (0, 0))],
        out_specs=out_specs,
        compiler_params=_cparams(("arbitrary", "arbitrary")),
        name="proj_in",
    )(x, g.reshape(1, d), sh, sc, w)
    return {name: o for (name, _, _), o in zip(_IN_GROUPS, outs)}


def _dwconv_kernel(x_ref, w_ref, b_ref, o_ref, *, act):
    x = x_ref[0]
    s = x.shape[0]
    row = lax.broadcasted_iota(I32, x.shape, 0)
    acc = x * w_ref[2:3, :] + b_ref[...]
    acc = acc + jnp.where(row >= 2, pltpu.roll(x, 2, 0), 0.0) * w_ref[0:1, :]
    acc = acc + jnp.where(row >= 1, pltpu.roll(x, 1, 0), 0.0) * w_ref[1:2, :]
    acc = acc + jnp.where(row < s - 1, pltpu.roll(x, s - 1, 0), 0.0) * w_ref[3:4, :]
    if act:
        acc = _silu(acc)
    o_ref[0] = acc


def _dwconv(x, w, bias, act):
    b, s, c = x.shape
    tc = 256
    return pl.pallas_call(
        functools.partial(_dwconv_kernel, act=act),
        out_shape=jax.ShapeDtypeStruct((b, s, c), F32),
        grid=(b, c // tc),
        in_specs=[pl.BlockSpec((1, s, tc), lambda bi, j: (bi, 0, j)),
                  pl.BlockSpec((CONV_K, tc), lambda bi, j: (0, j)),
                  pl.BlockSpec((1, tc), lambda bi, j: (0, j))],
        out_specs=pl.BlockSpec((1, s, tc), lambda bi, j: (bi, 0, j)),
        compiler_params=_cparams(("arbitrary", "arbitrary")),
        name="dwconv",
    )(x, w, bias.reshape(1, c))


def _lru_coeffs(x, w_ref, b_ref, sp_ref, a_s, u_s):
    g = jnp.dot(x.astype(BF16), w_ref[...], preferred_element_type=F32) + b_ref[...]
    r = _sigmoid(g[:, :LRU_WIDTH])
    i = _sigmoid(g[:, LRU_WIDTH:])
    a = jnp.exp(-LRU_C * r * sp_ref[...])
    a_s[...] = a
    u_s[...] = jnp.sqrt(1.0 - a * a) * (i * x)


def _lru_kernel(xf_ref, xb_ref, wf_ref, wb_ref, bf_ref, bb_ref, spf_ref, spb_ref,
                h0f_ref, h0b_ref, hf_ref, hb_ref, af_s, uf_s, ab_s, ub_s, cf_s, cb_s):
    c = pl.program_id(1)
    tl = xf_ref.shape[1]
    ngroups = tl // SUBLANES
    shape = (SUBLANES, LRU_WIDTH)
    row = lax.broadcasted_iota(I32, shape, 0)

    @pl.when(c == 0)
    def _():
        cf_s[...] = jnp.broadcast_to(h0f_ref[0], shape)
        cb_s[...] = jnp.broadcast_to(h0b_ref[0], shape)

    _lru_coeffs(xf_ref[0], wf_ref, bf_ref, spf_ref, af_s, uf_s)
    _lru_coeffs(xb_ref[0], wb_ref, bb_ref, spb_ref, ab_s, ub_s)

    def group_scan(a, u, carry, reverse):
        for sh in (1, 2, 4):
            keep = (row < SUBLANES - sh) if reverse else (row >= sh)
            amt = SUBLANES - sh if reverse else sh
            a_sh = jnp.where(keep, pltpu.roll(a, amt, 0), 1.0)
            u_sh = jnp.where(keep, pltpu.roll(u, amt, 0), 0.0)
            u = a * u_sh + u
            a = a * a_sh
        return a * carry + u

    def step(gj, carry):
        cf, cb = carry
        slf = pl.ds(pl.multiple_of(gj * SUBLANES, SUBLANES), SUBLANES)
        slb = pl.ds(pl.multiple_of((ngroups - 1 - gj) * SUBLANES, SUBLANES), SUBLANES)
        hf = group_scan(af_s[slf, :], uf_s[slf, :], cf, False)
        hb = group_scan(ab_s[slb, :], ub_s[slb, :], cb, True)
        hf_ref[0, slf, :] = hf
        hb_ref[0, slb, :] = hb
        return (jnp.broadcast_to(hf[SUBLANES - 1:SUBLANES, :], shape),
                jnp.broadcast_to(hb[0:1, :], shape))

    cf, cb = lax.fori_loop(0, ngroups, step, (cf_s[...], cb_s[...]), unroll=2)
    cf_s[...] = cf
    cb_s[...] = cb


def _lru_gate_weights(wa, wx):
    def dense(wh):
        m = jnp.zeros((LRU_WIDTH, LRU_WIDTH), wh.dtype)
        for h in range(LRU_HEADS):
            lo = h * LRU_HEAD_DIM
            m = m.at[lo:lo + LRU_HEAD_DIM, lo:lo + LRU_HEAD_DIM].set(wh[h])
        return m
    return jnp.concatenate([dense(wa), dense(wx)], axis=1).astype(BF16)


def _lru_scan(xc, wa, ba, wx, bx, lam, h0f, h0b):
    b, s, w = xc.shape
    tl = min(LRU_TL, s)
    nc = s // tl
    wf = _lru_gate_weights(wa[0], wx[0])
    wb = _lru_gate_weights(wa[1], wx[1])
    bf = jnp.concatenate([ba[0], bx[0]]).reshape(1, 2 * w)
    bb = jnp.concatenate([ba[1], bx[1]]).reshape(1, 2 * w)
    sp = jax.nn.softplus(-lam.astype(F32))
    const = lambda bi, c: (0, 0)
    hf, hb = pl.pallas_call(
        _lru_kernel,
        out_shape=[jax.ShapeDtypeStruct((b, s, w), F32)] * 2,
        grid=(b, nc),
        in_specs=[pl.BlockSpec((1, tl, w), lambda bi, c: (bi, c, 0)),
                  pl.BlockSpec((1, tl, w), lambda bi, c: (bi, nc - 1 - c, 0)),
                  pl.BlockSpec((w, 2 * w), const), pl.BlockSpec((w, 2 * w), const),
                  pl.BlockSpec((1, 2 * w), const), pl.BlockSpec((1, 2 * w), const),
                  pl.BlockSpec((1, w), const), pl.BlockSpec((1, w), const),
                  pl.BlockSpec((1, 1, w), lambda bi, c: (bi, 0, 0)),
                  pl.BlockSpec((1, 1, w), lambda bi, c: (bi, 0, 0))],
        out_specs=[pl.BlockSpec((1, tl, w), lambda bi, c: (bi, c, 0)),
                   pl.BlockSpec((1, tl, w), lambda bi, c: (bi, nc - 1 - c, 0))],
        scratch_shapes=[pltpu.VMEM((tl, w), F32)] * 4
        + [pltpu.VMEM((SUBLANES, w), F32), pltpu.VMEM((SUBLANES, w), F32)],
        compiler_params=_cparams(("arbitrary", "arbitrary")),
        name="lru_scan",
    )(xc, xc, wf, wb, bf, bb, sp[0:1], sp[1:2], h0f.reshape(b, 1, w), h0b.reshape(b, 1, w))
    return hf, hb


def _ssd_direction(x_ref, dt_ref, row0, avec_ref, dtb_ref, state_s, y_ref, lane0, rev):
    t = SSD_CHUNK
    xbc = x_ref[0, row0:row0 + t, :]
    dtraw = dt_ref[0, row0:row0 + t, :]
    ri = lax.broadcasted_iota(I32, (t, t), 0)
    ci = lax.broadcasted_iota(I32, (t, t), 1)
    causal = (ci >= ri) if rev else (ci <= ri)
    cum_m = jnp.where(causal, 1.0, 0.0).astype(F32)

    dt = _softplus(dtraw + dtb_ref[...])
    a_all = dt * avec_ref[...]
    acs = jnp.dot(cum_m, a_all, preferred_element_type=F32, precision=lax.Precision.HIGHEST)
    acs_t = acs.T
    edge = 0 if rev else t - 1

    xs = xbc[:, :SSD_WIDTH]
    hrow = lax.broadcasted_iota(I32, (t, t), 0)
    for g in range(SSD_GROUPS):
        bm = xbc[:, SSD_WIDTH + g * SSD_STATE:SSD_WIDTH + (g + 1) * SSD_STATE]
        cm = xbc[:, SSD_WIDTH + (SSD_GROUPS + g) * SSD_STATE:SSD_WIDTH + (SSD_GROUPS + g + 1) * SSD_STATE]
        bm_b = bm.astype(BF16)
        cm_b = cm.astype(BF16)
        gmat = lax.dot_general(cm_b, bm_b, (((1,), (1,)), ((), ())), preferred_element_type=F32)
        s_prev = state_s[g]
        off_all = lax.dot_general(cm_b, s_prev.astype(BF16), (((1,), (1,)), ((), ())),
                                  preferred_element_type=F32)
        xd_parts, tot = [], []
        for hl in range(2):
            h = g * 2 + hl
            ln = lane0 + h
            col = jnp.broadcast_to(acs[:, ln:ln + 1], (t, t))
            rowv = acs_t[ln:ln + 1, :]
            lmat = jnp.where(causal, jnp.exp(col - rowv), 0.0)
            xh = xs[:, h * SSD_HEAD_DIM:(h + 1) * SSD_HEAD_DIM] * dt[:, ln:ln + 1]
            ydiag = jnp.dot((gmat * lmat).astype(BF16), xh.astype(BF16), preferred_element_type=F32)
            colh = col[:, :SSD_HEAD_DIM]
            yoff = off_all[:, hl * SSD_HEAD_DIM:(hl + 1) * SSD_HEAD_DIM] * jnp.exp(colh)
            y_ref[0, row0:row0 + t, h * SSD_HEAD_DIM:(h + 1) * SSD_HEAD_DIM] = ydiag + yoff
            total = acs[edge:edge + 1, ln:ln + 1]
            xd_parts.append(xh * jnp.exp(total - colh))
            tot.append(jnp.exp(total))
        xd = jnp.concatenate(xd_parts, axis=1)
        xd_t = xd.T
        st = jnp.dot(xd_t.astype(BF16), bm_b, preferred_element_type=F32)
        fac = jnp.where(hrow < SSD_HEAD_DIM, tot[0], tot[1])
        state_s[g] = fac * s_prev + st


def _ssd_kernel(xf_ref, dtf_ref, xb_ref, dtb_ref, avec_ref, bias_ref, h0f_ref, h0b_ref,
                yf_ref, yb_ref, sf_ref, sb_ref, stf_s, stb_s):
    c = pl.program_id(1)

    @pl.when(c == 0)
    def _():
        stf_s[...] = h0f_ref[0]
        stb_s[...] = h0b_ref[0]

    nsub = xf_ref.shape[1] // SSD_CHUNK
    for j in range(nsub):
        _ssd_direction(xf_ref, dtf_ref, j * SSD_CHUNK, avec_ref, bias_ref, stf_s, yf_ref, 0, False)
        _ssd_direction(xb_ref, dtb_ref, (nsub - 1 - j) * SSD_CHUNK, avec_ref, bias_ref, stb_s, yb_ref,
                       SSD_HEADS, True)
    sf_ref[0] = stf_s[...]
    sb_ref[0] = stb_s[...]


def _ssd_scan(xbc, dt, a_log, dt_bias, h0f, h0b):
    b, s, _ = xbc.shape
    t = min(SSD_STEP_CHUNKS * SSD_CHUNK, s)
    nc = s // t
    avec = jnp.zeros((1, LANES), F32).at[0, :2 * SSD_HEADS].set(-jnp.exp(a_log.astype(F32)).reshape(-1))
    bias = jnp.zeros((1, LANES), F32).at[0, :2 * SSD_HEADS].set(dt_bias.astype(F32).reshape(-1))
    st_shape = (b, SSD_GROUPS, 2 * SSD_HEAD_DIM, SSD_STATE)
    const = lambda bi, c: (0, 0)
    st_spec = pl.BlockSpec((1,) + st_shape[1:], lambda bi, c: (bi, 0, 0, 0))
    yf, yb, sf, sb = pl.pallas_call(
        _ssd_kernel,
        out_shape=[jax.ShapeDtypeStruct((b, s, SSD_WIDTH), F32)] * 2
        + [jax.ShapeDtypeStruct(st_shape, F32)] * 2,
        grid=(b, nc),
        in_specs=[pl.BlockSpec((1, t, SSD_CONV_DIM), lambda bi, c: (bi, c, 0)),
                  pl.BlockSpec((1, t, LANES), lambda bi, c: (bi, c, 0)),
                  pl.BlockSpec((1, t, SSD_CONV_DIM), lambda bi, c: (bi, nc - 1 - c, 0)),
                  pl.BlockSpec((1, t, LANES), lambda bi, c: (bi, nc - 1 - c, 0)),
                  pl.BlockSpec((1, LANES), const), pl.BlockSpec((1, LANES), const),
                  st_spec, st_spec],
        out_specs=[pl.BlockSpec((1, t, SSD_WIDTH), lambda bi, c: (bi, c, 0)),
                   pl.BlockSpec((1, t, SSD_WIDTH), lambda bi, c: (bi, nc - 1 - c, 0)),
                   st_spec, st_spec],
        scratch_shapes=[pltpu.VMEM(st_shape[1:], F32), pltpu.VMEM(st_shape[1:], F32)],
        compiler_params=_cparams(("arbitrary", "arbitrary")),
        name="ssd_scan",
    )(xbc, dt, xbc, dt, avec, bias, h0f, h0b)
    return yf, yb, sf, sb


def _qk_prep_kernel(x_ref, g_ref, cos_ref, sin_ref, seg_ref, o_ref, *, rope, scale, kc):
    x = x_ref[0]
    sq = x * x
    hi = sq.astype(BF16)
    lo = (sq - hi.astype(F32)).astype(BF16)
    ssum = (jnp.dot(hi, seg_ref[...], preferred_element_type=F32)
            + jnp.dot(lo, seg_ref[...], preferred_element_type=F32))
    y = x * lax.rsqrt(ssum * (1.0 / DA_QK_DIM) + EPS) * g_ref[...]
    if rope:
        n = x.shape[1]
        lane = lax.broadcasted_iota(I32, x.shape, 1)
        first = (lane & (DA_QK_DIM - 1)) < DA_QK_DIM // 2
        partner = jnp.where(first, pltpu.roll(y, n - DA_QK_DIM // 2, 1), pltpu.roll(y, DA_QK_DIM // 2, 1))
        cosv = jnp.concatenate([cos_ref[...]] * (n // LANES), axis=1)
        sinv = jnp.concatenate([sin_ref[...]] * (n // LANES), axis=1)
        y = y * cosv + partner * sinv
    y = y * scale
    if kc:
        for h in range(DA_HEADS):
            head = y[:, h * 2 * DA_QK_DIM:(h + 1) * 2 * DA_QK_DIM]
            for cc in range(y.shape[0] // kc):
                t = head[cc * kc:(cc + 1) * kc, :].T.astype(BF16)
                o_ref[0, 2 * h, cc] = t[:DA_QK_DIM]
                o_ref[0, 2 * h + 1, cc] = t[DA_QK_DIM:]
    else:
        yb = y.astype(BF16)
        for j in range(2 * DA_HEADS):
            o_ref[0, j] = yb[:, j * DA_QK_DIM:(j + 1) * DA_QK_DIM]


def _rope_tables(s):
    pos = jnp.arange(s, dtype=I32)
    r = (pos // GRID_W).astype(F32)
    col = (pos % GRID_W).astype(F32)
    per_axis = DA_QK_DIM // 4
    inv = ROPE_BASE ** (-jnp.arange(per_axis, dtype=F32) / per_axis)
    ang = jnp.concatenate([r[:, None] * inv, col[:, None] * inv], axis=-1)
    cos, sin = jnp.cos(ang), jnp.sin(ang)
    cos64 = jnp.concatenate([cos, cos], axis=-1)
    sin64 = jnp.concatenate([-sin, sin], axis=-1)
    return jnp.concatenate([cos64, cos64], axis=-1), jnp.concatenate([sin64, sin64], axis=-1)


def _qk_prep(x, g, cos, sin, rope, scale, kc):
    b, s, n = x.shape
    ts = min(512, s)
    gfull = jnp.tile(g.astype(F32), n // DA_QK_DIM).reshape(1, n)
    idx = jnp.arange(n) // DA_QK_DIM
    seg = (idx[:, None] == idx[None, :]).astype(BF16)
    if kc:
        out_shape = jax.ShapeDtypeStruct((b, 2 * DA_HEADS, s // kc, DA_QK_DIM, kc), BF16)
        out_spec = pl.BlockSpec((1, 2 * DA_HEADS, ts // kc, DA_QK_DIM, kc), lambda bi, i: (bi, 0, i, 0, 0))
    else:
        out_shape = jax.ShapeDtypeStruct((b, 2 * DA_HEADS, s, DA_QK_DIM), BF16)
        out_spec = pl.BlockSpec((1, 2 * DA_HEADS, ts, DA_QK_DIM), lambda bi, i: (bi, 0, i, 0))
    return pl.pallas_call(
        functools.partial(_qk_prep_kernel, rope=rope, scale=scale, kc=kc),
        out_shape=out_shape,
        grid=(b, s // ts),
        in_specs=[pl.BlockSpec((1, ts, n), lambda bi, i: (bi, i, 0)),
                  pl.BlockSpec((1, n), lambda bi, i: (0, 0)),
                  pl.BlockSpec((ts, LANES), lambda bi, i: (i, 0)),
                  pl.BlockSpec((ts, LANES), lambda bi, i: (i, 0)),
                  pl.BlockSpec((n, n), lambda bi, i: (0, 0))],
        out_specs=out_spec,
        compiler_params=_cparams(("arbitrary", "arbitrary")),
        name="qk_prep",
    )(x, gfull, cos, sin, seg)


def _attn_kernel(lam_ref, q_ref, *refs, nseg, out_scale):
    lam = lam_ref[0, 0]
    segs = [(refs[2 * j], refs[2 * j + 1]) for j in range(nseg)]
    g_ref, o_ref = refs[2 * nseg], refs[2 * nseg + 1]
    tq = q_ref.shape[2]
    m = [jnp.full((tq, LANES), NEG_BIG, F32)] * 2
    l = [jnp.zeros((tq, LANES), F32)] * 2
    acc = [jnp.zeros((tq, DA_V_DIM), F32)] * 2

    for k_ref, v_ref in segs:
        nchunk, kc = k_ref.shape[2], k_ref.shape[4]
        for c in range(nchunk):
            v = v_ref[0, c * kc:(c + 1) * kc, :]
            for sub in range(2):
                s = jnp.dot(q_ref[0, sub], k_ref[0, sub, c], preferred_element_type=F32)
                mx = s[:, :LANES]
                for j in range(1, kc // LANES):
                    mx = jnp.maximum(mx, s[:, j * LANES:(j + 1) * LANES])
                m_new = jnp.maximum(m[sub], jnp.max(mx, axis=-1, keepdims=True))
                alpha = jnp.exp2(m[sub] - m_new)
                p = jnp.exp2(s - jnp.concatenate([m_new] * (kc // LANES), axis=1))
                lsum = alpha * l[sub]
                for j in range(kc // LANES):
                    lsum = lsum + p[:, j * LANES:(j + 1) * LANES]
                l[sub] = lsum
                acc[sub] = alpha * acc[sub] + jnp.dot(p.astype(BF16), v, preferred_element_type=F32)
                m[sub] = m_new

    l1 = jnp.sum(l[0], axis=-1, keepdims=True)
    l2 = jnp.sum(l[1], axis=-1, keepdims=True)
    o = acc[0] * (1.0 / l1) - lam * (acc[1] * (1.0 / l2))
    ms = jnp.mean(o * o, axis=-1, keepdims=True)
    o_ref[0] = o * lax.rsqrt(ms + EPS) * g_ref[...] * out_scale


def _diff_attn(qp, segments, lam, g, lam_init):
    b, _, s, d = qp.shape
    tq = min(ATT_TQ, s)
    seg_specs, seg_args = [], []
    for kt, v in segments:
        nchunk, kc = kt.shape[2], kt.shape[4]
        seg_specs += [pl.BlockSpec((1, 2, nchunk, d, kc), lambda bi, h, i: (bi, h, 0, 0, 0)),
                      pl.BlockSpec((1, nchunk * kc, DA_V_DIM), lambda bi, h, i: (bi, 0, h))]
        seg_args += [kt, v]
    return pl.pallas_call(
        functools.partial(_attn_kernel, nseg=len(segments), out_scale=1.0 - lam_init),
        out_shape=jax.ShapeDtypeStruct((b, s, DA_WIDTH), F32),
        grid=(b, DA_HEADS, s // tq),
        in_specs=[pl.BlockSpec(memory_space=pltpu.SMEM),
                  pl.BlockSpec((1, 2, tq, d), lambda bi, h, i: (bi, h, i, 0))] + seg_specs
        + [pl.BlockSpec((1, DA_V_DIM), lambda bi, h, i: (0, 0))],
        out_specs=pl.BlockSpec((1, tq, DA_V_DIM), lambda bi, h, i: (bi, i, h)),
        compiler_params=_cparams(("arbitrary", "arbitrary", "arbitrary")),
        name="diff_attn",
    )(lam.reshape(1, 1).astype(F32), qp, *seg_args, g.reshape(1, DA_V_DIM).astype(F32))


def _gelu_tanh(x):
    return 0.5 * x * (1.0 + jnp.tanh(math.sqrt(2.0 / math.pi) * (x + 0.044715 * x * x * x)))


def _proj_out_kernel(x_ref, lg_ref, hf_ref, hb_ref, yf_ref, yb_ref, xs_ref, z_ref, da_ref,
                     dvec_ref, ng_ref, gate_ref, w_ref, o_ref):
    lru = _gelu_tanh(lg_ref[0]) * (hf_ref[0] + hb_ref[0])
    y = yf_ref[0] + yb_ref[0] + dvec_ref[...] * xs_ref[0]
    y = y * _silu(z_ref[0])
    ssd = y * lax.rsqrt(jnp.mean(y * y, axis=-1, keepdims=True) + EPS) * ng_ref[...]
    o = jnp.dot(lru.astype(BF16), w_ref[0:LRU_WIDTH, :], preferred_element_type=F32)
    o = o + jnp.dot(ssd.astype(BF16), w_ref[LRU_WIDTH:LRU_WIDTH + SSD_WIDTH, :], preferred_element_type=F32)
    o = o + jnp.dot(da_ref[0].astype(BF16), w_ref[LRU_WIDTH + SSD_WIDTH:, :], preferred_element_type=F32)
    o_ref[0] = x_ref[0] + gate_ref[0] * o


def _proj_out(x, lg, hf, hb, yf, yb, xbc, z, da, dvec, ng, gate, w):
    b, s, d = x.shape
    tm = min(PROJ_TM, s)
    tok = lambda wd: pl.BlockSpec((1, tm, wd), lambda bi, i: (bi, i, 0))
    const2 = lambda bi, i: (0, 0)
    return pl.pallas_call(
        _proj_out_kernel,
        out_shape=jax.ShapeDtypeStruct((b, s, d), F32),
        grid=(b, s // tm),
        in_specs=[tok(d), tok(LRU_WIDTH), tok(LRU_WIDTH), tok(LRU_WIDTH), tok(SSD_WIDTH),
                  tok(SSD_WIDTH), tok(SSD_WIDTH), tok(SSD_WIDTH), tok(DA_WIDTH),
                  pl.BlockSpec((1, SSD_WIDTH), const2), pl.BlockSpec((1, SSD_WIDTH), const2),
                  pl.BlockSpec((1, 1, d), lambda bi, i: (bi, 0, 0)),
                  pl.BlockSpec(w.shape, const2)],
        out_specs=tok(d),
        compiler_params=_cparams(("arbitrary", "arbitrary")),
        name="proj_out",
    )(x, lg, hf, hb, yf, yb, xbc, z, da, dvec, ng, gate, w)


def _route_kernel(x_ref, g_ref, sh_ref, sc_ref, rwh_ref, rwl_ref, rb_ref, c0_ref, h_ref, ri_ref, gt_ref,
                  slt_ref, n_ref, base_ref, cnt_ref, carry_s):
    i = pl.program_id(0)
    tm = x_ref.shape[0]

    @pl.when(i == 0)
    def _():
        carry_s[...] = c0_ref[...]

    h = _norm_mod(x_ref[...], g_ref[...], sh_ref[0], sc_ref[0])
    h_hi = h.astype(BF16)
    h_ref[...] = h_hi
    h_lo = (h - h_hi.astype(F32)).astype(BF16)
    logits = (jnp.dot(h_hi, rwh_ref[...], preferred_element_type=F32)
              + jnp.dot(h_hi, rwl_ref[...], preferred_element_type=F32)
              + jnp.dot(h_lo, rwh_ref[...], preferred_element_type=F32)) + rb_ref[...]
    lane = lax.broadcasted_iota(I32, logits.shape, 1)
    lane_f = lane.astype(F32)
    vals, idxs = [], []
    cur = logits
    for _ in range(TOP_K):
        m = jnp.max(cur, axis=-1, keepdims=True)
        idx = jnp.min(jnp.where(cur >= m, lane_f, float(LANES)), axis=-1, keepdims=True).astype(I32)
        vals.append(m)
        idxs.append(idx)
        cur = jnp.where(lane == idx, 2.0 * NEG_BIG, cur)
    es = [jnp.exp(v - vals[0]) for v in vals]
    inv = 1.0 / (es[0] + es[1] + es[2] + es[3])
    onehot = jnp.zeros(logits.shape, F32)
    for idx in idxs:
        onehot = onehot + jnp.where(lane == idx, 1.0, 0.0)
    ri = lax.broadcasted_iota(I32, (tm, tm), 0)
    ci = lax.broadcasted_iota(I32, (tm, tm), 1)
    strict = jnp.where(ci < ri, 1.0, 0.0).astype(BF16)
    before = jnp.dot(strict, onehot.astype(BF16), preferred_element_type=F32)
    n_tile = jnp.sum(onehot, axis=0, keepdims=True).astype(I32)
    pad_tile = _round_up8(n_tile)
    li = lax.broadcasted_iota(I32, (LANES, LANES), 0)
    lj = lax.broadcasted_iota(I32, (LANES, LANES), 1)
    below = jnp.where(li < lj, 1.0, 0.0).astype(BF16)
    pad_rows = jnp.broadcast_to(pad_tile.astype(F32), (SUBLANES, LANES)).astype(BF16)
    off = jnp.dot(pad_rows, below, preferred_element_type=F32)[0:1, :]
    place = before + off
    ri_out = jnp.zeros(logits.shape, I32)
    gt_out = jnp.zeros(logits.shape, F32)
    sl_f = jnp.full(logits.shape, -1.0, F32)
    for k in range(TOP_K):
        sloc = jnp.sum(jnp.where(lane == idxs[k], place, 0.0), axis=-1, keepdims=True)
        ri_out = ri_out + jnp.where(lane == k, idxs[k], 0) + jnp.where(lane == TOP_K + k, sloc.astype(I32), 0)
        sl_f = jnp.where(lane == k, sloc, sl_f)
        gt_out = gt_out + jnp.where(lane == k, es[k] * inv, 0.0)
    ri_ref[...] = ri_out
    gt_ref[...] = gt_out
    slt_ref[0] = sl_f.T[0:SUBLANES, :].astype(I32)
    n_ref[0] = n_tile
    base_ref[0] = carry_s[...]
    carry_s[...] = carry_s[...] + pad_tile
    cnt_ref[...] = carry_s[...]


def _route(x, g, sh, sc, tiles_per_mod, rw, rb, counts0):
    t, d = x.shape
    tm = ROUTE_TM
    tile = lambda wd: pl.BlockSpec((tm, wd), lambda i: (i, 0))
    const = lambda i: (0, 0)
    mod = pl.BlockSpec((1, 1, d), lambda i: (i // tiles_per_mod, 0, 0))
    nt = t // tm
    per_tile = pl.BlockSpec((1, 1, LANES), lambda i: (i, 0, 0))
    return pl.pallas_call(
        _route_kernel,
        out_shape=[jax.ShapeDtypeStruct((t, d), BF16), jax.ShapeDtypeStruct((t, LANES), I32),
                   jax.ShapeDtypeStruct((t, LANES), F32), jax.ShapeDtypeStruct((nt, SUBLANES, tm), I32),
                   jax.ShapeDtypeStruct((nt, 1, LANES), I32),
                   jax.ShapeDtypeStruct((nt, 1, LANES), I32), jax.ShapeDtypeStruct((1, LANES), I32)],
        grid=(nt,),
        in_specs=[tile(d), pl.BlockSpec((1, d), const), mod, mod,
                  pl.BlockSpec((d, LANES), const), pl.BlockSpec((d, LANES), const),
                  pl.BlockSpec((1, LANES), const), pl.BlockSpec((1, LANES), const)],
        out_specs=[tile(d), tile(LANES), tile(LANES), pl.BlockSpec((1, SUBLANES, tm), lambda i: (i, 0, 0)),
                   per_tile, per_tile, pl.BlockSpec((1, LANES), const)],
        scratch_shapes=[pltpu.VMEM((1, LANES), I32)],
        compiler_params=_cparams(("arbitrary",)),
        name="moe_route",
    )(x, g.reshape(1, d), sh, sc, rw[0], rw[1], rb, counts0)


MOE_STAGE = -(-(ROUTE_TM * TOP_K + N_EXPERTS * (SUBLANES - 1)) // 256) * 256
_TAB_N, _TAB_DST, _TAB_OFF, _TAB_TOTAL = 0, N_EXPERTS, 2 * N_EXPERTS, 3 * N_EXPERTS


U32 = jnp.uint32
_HI16 = 0xFFFF0000


def _pack_halves(a):
    n = a.shape[1] // 2
    lo = pltpu.bitcast(a[:, :n].astype(BF16).astype(F32), U32)
    hi = pltpu.bitcast(a[:, n:].astype(BF16).astype(F32), U32)
    return (lo >> 16) | (hi & U32(_HI16))


def _unpack_halves(u):
    lo = pltpu.bitcast(u << 16, F32).astype(BF16)
    hi = pltpu.bitcast(u & U32(_HI16), F32).astype(BF16)
    return lo, hi


_BIG_CHUNKS = 4


def _chunk_copies(tab_ref, stage_ref, rows_hbm, sem, to_hbm):
    def copy(off, dst, rows):
        s = stage_ref.at[pl.ds(pl.multiple_of(off, SUBLANES), rows), :]
        r = rows_hbm.at[pl.ds(pl.multiple_of(dst, SUBLANES), rows), :]
        if to_hbm:
            pltpu.make_async_copy(s, r, sem).start()
        else:
            pltpu.make_async_copy(r, s, sem).start()

    def expert(e, carry):
        n = tab_ref[0, 0, _TAB_N + e]
        dst = tab_ref[0, 0, _TAB_DST + e]
        off = tab_ref[0, 0, _TAB_OFF + e]
        big_rows = _BIG_CHUNKS * SUBLANES
        n_big = n // _BIG_CHUNKS

        def big(j, c):
            copy(off + j * big_rows, dst + j * big_rows, big_rows)
            return c

        def small(j, c):
            copy(off + (n_big * _BIG_CHUNKS + j) * SUBLANES, dst + (n_big * _BIG_CHUNKS + j) * SUBLANES, SUBLANES)
            return c

        lax.fori_loop(0, n_big, big, 0)
        lax.fori_loop(0, n - n_big * _BIG_CHUNKS, small, 0)
        return carry
    lax.fori_loop(0, N_EXPERTS, expert, 0)


def _chunk_waits(tab_ref, stage_ref, rows_hbm, sem):
    rows = tab_ref[0, 0, _TAB_TOTAL] * SUBLANES
    pltpu.make_async_copy(rows_hbm.at[pl.ds(0, rows), :], stage_ref.at[pl.ds(0, rows), :], sem).wait()


def _dispatch_kernel(tab_ref, tprev_ref, tail_ref, sloc_ref, *refs, tiles):
    h_refs = refs[:len(tiles)]
    xs_out, stage_s, zero_s, sems = refs[len(tiles):]
    i = pl.program_id(0)
    n_steps = sum(tiles)
    slot = i % 2
    row = lax.broadcasted_iota(I32, (MOE_STAGE, ROUTE_TM), 0)
    perm = jnp.zeros((MOE_STAGE, ROUTE_TM), F32)
    for k in range(TOP_K):
        perm = perm + jnp.where(row == sloc_ref[0, k:k + 1, :], 1.0, 0.0)
    perm = perm.astype(BF16)
    off = 0
    for h_ref, nt in zip(h_refs, tiles):
        @pl.when((i >= off) & (i < off + nt))
        def _(h_ref=h_ref):
            stage_s[slot] = _pack_halves(jnp.dot(perm, h_ref[...], preferred_element_type=F32))
        off += nt
    _chunk_copies(tab_ref, stage_s.at[slot], xs_out, sems.at[slot], True)

    @pl.when(i > 0)
    def _():
        _chunk_waits(tprev_ref, stage_s.at[1 - slot], xs_out, sems.at[1 - slot])

    @pl.when(i == n_steps - 1)
    def _():
        _chunk_waits(tab_ref, stage_s.at[slot], xs_out, sems.at[slot])
        sem = sems.at[slot]
        zero_s[...] = jnp.zeros(zero_s.shape, U32)
        zr = zero_s.shape[0]
        n_big, n_small, start = tail_ref[0], tail_ref[1], tail_ref[2]

        def big(j, c):
            pltpu.make_async_copy(zero_s, xs_out.at[pl.ds(pl.multiple_of(start + j * zr, SUBLANES), zr), :], sem).start()
            return c

        def small(j, c):
            r0 = pl.multiple_of(start + n_big * zr + j * SUBLANES, SUBLANES)
            pltpu.make_async_copy(zero_s.at[pl.ds(0, SUBLANES), :], xs_out.at[pl.ds(r0, SUBLANES), :], sem).start()
            return c

        def big_wait(j, c):
            pltpu.make_async_copy(zero_s, xs_out.at[pl.ds(0, zr), :], sem).wait()
            return c

        def small_wait(j, c):
            pltpu.make_async_copy(zero_s.at[pl.ds(0, SUBLANES), :], xs_out.at[pl.ds(0, SUBLANES), :], sem).wait()
            return c

        lax.fori_loop(0, n_big, big, 0)
        lax.fori_loop(0, n_small, small, 0)
        lax.fori_loop(0, n_big, big_wait, 0)
        lax.fori_loop(0, n_small, small_wait, 0)


def _dispatch(hs, tabs, slocs, tail, n_max):
    d = hs[0].shape[1]
    tm = ROUTE_TM
    tiles = tuple(h.shape[0] // tm for h in hs)
    specs, off = [], 0
    for nt in tiles:
        specs.append(pl.BlockSpec((tm, d), lambda i, off=off, nt=nt: (jnp.clip(i - off, 0, nt - 1), 0)))
        off += nt
    return pl.pallas_call(
        functools.partial(_dispatch_kernel, tiles=tiles),
        out_shape=jax.ShapeDtypeStruct((n_max, d // 2), U32),
        grid=(sum(tiles),),
        in_specs=[pl.BlockSpec((1, 1, LANES), lambda i: (i, 0, 0), memory_space=pltpu.SMEM),
                  pl.BlockSpec((1, 1, LANES), lambda i: (jnp.maximum(i - 1, 0), 0, 0), memory_space=pltpu.SMEM),
                  pl.BlockSpec(memory_space=pltpu.SMEM),
                  pl.BlockSpec((1, SUBLANES, tm), lambda i: (i, 0, 0))] + specs,
        out_specs=pl.BlockSpec(memory_space=pl.ANY),
        scratch_shapes=[pltpu.VMEM((2, MOE_STAGE, d // 2), U32), pltpu.VMEM((MOE_TM, d // 2), U32),
                        pltpu.SemaphoreType.DMA((2,))],
        compiler_params=_cparams(("arbitrary",)),
        name="moe_dispatch",
    )(tabs, tabs, tail, slocs, *hs)


def _gmm_kernel(vt_ref, ve_ref, lo_ref, hi_ref, first_ref, xs_ref, wg_ref, wu_ref, bg_ref, bu_ref,
                wd_ref, bd_ref, y_ref):
    del vt_ref, ve_ref
    v = pl.program_id(0)
    lo, hi = lo_ref[v], hi_ref[v]

    @pl.when(hi > lo)
    def _():
        x_lo, x_hi = _unpack_halves(xs_ref[...])
        half = x_lo.shape[1]
        gate = (jnp.dot(x_lo, wg_ref[:half, :], preferred_element_type=F32)
                + jnp.dot(x_hi, wg_ref[half:, :], preferred_element_type=F32)) + bg_ref[...]
        up = (jnp.dot(x_lo, wu_ref[:half, :], preferred_element_type=F32)
              + jnp.dot(x_hi, wu_ref[half:, :], preferred_element_type=F32)) + bu_ref[...]
        gate = jnp.minimum(gate, SWIGLU_LIMIT)
        up = jnp.clip(up, -SWIGLU_LIMIT, SWIGLU_LIMIT)
        glu = gate * _sigmoid(SWIGLU_ALPHA * gate)
        act = ((up + 1.0) * glu).astype(BF16)
        down = lambda: _pack_halves(jnp.dot(act, wd_ref[...], preferred_element_type=F32) + bd_ref[...])
        whole = (lo == 0) & (hi == xs_ref.shape[0])
        first = first_ref[v] > 0
        row = lax.broadcasted_iota(I32, y_ref.shape, 0)
        mine = (row >= lo) & (row < hi)

        @pl.when(whole)
        def _():
            y_ref[...] = down()

        @pl.when(jnp.logical_not(whole) & first)
        def _():
            y_ref[...] = jnp.where(mine, down(), U32(0))

        @pl.when(jnp.logical_not(whole) & jnp.logical_not(first))
        def _():
            y_ref[...] = jnp.where(mine, down(), y_ref[...])

    @pl.when((hi <= lo) & (first_ref[v] > 0))
    def _():
        y_ref[...] = jnp.zeros(y_ref.shape, U32)


def _gmm(xs, visits, wg, wu, bg, bu, wd, bd):
    n_rows = xs.shape[0]
    d = wg.shape[1]
    de = wg.shape[2]
    tm = MOE_TM
    nv = visits[0].shape[0]
    ex = lambda shape: pl.BlockSpec((None,) + shape, lambda v, vt, ve, lo, hi, fi: (ve[v], 0, 0))
    rows = pl.BlockSpec((tm, d // 2), lambda v, vt, ve, lo, hi, fi: (vt[v], 0))
    grid_spec = pltpu.PrefetchScalarGridSpec(
        num_scalar_prefetch=5,
        grid=(nv,),
        in_specs=[rows, ex((d, de)), ex((d, de)), ex((1, de)), ex((1, de)), ex((de, d)), ex((1, d))],
        out_specs=rows,
    )
    return pl.pallas_call(
        _gmm_kernel,
        out_shape=jax.ShapeDtypeStruct((n_rows, d // 2), U32),
        grid_spec=grid_spec,
        compiler_params=_cparams(("arbitrary",)),
        name="moe_experts",
    )(*visits, xs, wg, wu, bg, bu, wd, bd)


def _visit_tables(cnt, n_rows):
    tm = MOE_TM
    nt = n_rows // tm
    nv = nt + N_EXPERTS - 1
    end = jnp.cumsum(cnt)
    start = end - cnt
    first_tile = start // tm
    last_tile = jnp.maximum(end - 1, 0) // tm
    nvis = jnp.where(cnt > 0, last_tile - first_tile + 1, 0)
    vend = jnp.cumsum(nvis)
    vstart = vend - nvis
    v = jnp.arange(nv, dtype=I32)
    e = jnp.minimum(jnp.sum((vend[None, :] <= v[:, None]).astype(I32), axis=1), N_EXPERTS - 1)
    n_used = vend[N_EXPERTS - 1]
    used = v < n_used
    tiles_used = (end[N_EXPERTS - 1] + tm - 1) // tm
    tile = jnp.where(used, first_tile[e] + v - vstart[e], jnp.minimum(tiles_used + v - n_used, nt - 1))
    lo = jnp.where(used, jnp.clip(start[e] - tile * tm, 0, tm), 0)
    hi = jnp.where(used, jnp.clip(end[e] - tile * tm, 0, tm), 0)
    prev = jnp.concatenate([jnp.full((1,), -1, I32), tile[:-1].astype(I32)])
    first = (tile != prev).astype(I32)
    return tuple(a.astype(I32) for a in (tile, e, lo, hi, first))


_SPLIT_W = 2 * LANES


def _split_gu_kernel(w_ref, p_ref, g_ref, u_ref):
    w = w_ref[...].astype(BF16)
    for j in range(w.shape[1] // _SPLIT_W):
        r = jnp.dot(w[:, j * _SPLIT_W:(j + 1) * _SPLIT_W], p_ref[...], preferred_element_type=F32)
        g_ref[:, j * LANES:(j + 1) * LANES] = r[:, :LANES].astype(BF16)
        u_ref[:, j * LANES:(j + 1) * LANES] = r[:, LANES:].astype(BF16)


def _split_gu(w_gu, layer):
    _, e, d, n2 = w_gu.shape
    tn = 2 * _SPLIT_W
    src = jnp.arange(_SPLIT_W)
    dst = jnp.where(src % 2 == 0, src // 2, LANES + src // 2)
    perm = (dst[:, None] == jnp.arange(_SPLIT_W)[None, :]).astype(BF16)
    return pl.pallas_call(
        _split_gu_kernel,
        out_shape=[jax.ShapeDtypeStruct((e, d, n2 // 2), BF16)] * 2,
        grid=(e, n2 // tn),
        in_specs=[pl.BlockSpec((None, None, d, tn), lambda ei, j: (layer, ei, 0, j)),
                  pl.BlockSpec((_SPLIT_W, _SPLIT_W), lambda ei, j: (0, 0))],
        out_specs=[pl.BlockSpec((None, d, tn // 2), lambda ei, j: (ei, 0, j))] * 2,
        compiler_params=_cparams(("arbitrary", "arbitrary")),
        name="split_gate_up",
    )(w_gu, perm)


def _combine_kernel(tcur_ref, tnext_ref, sloc_ref, gt_ref, x_ref, g2_ref, y_hbm, o_ref, stage_s, sems, *, n):
    i = pl.program_id(0)
    slot = i % 2

    @pl.when(i == 0)
    def _():
        stage_s[...] = jnp.zeros(stage_s.shape, U32)
        _chunk_copies(tcur_ref, stage_s.at[0], y_hbm, sems.at[0], False)

    @pl.when(i + 1 < n)
    def _():
        _chunk_copies(tnext_ref, stage_s.at[1 - slot], y_hbm, sems.at[1 - slot], False)

    _chunk_waits(tcur_ref, stage_s.at[slot], y_hbm, sems.at[slot])
    rows_lo, rows_hi = _unpack_halves(stage_s[slot])
    lane = lax.broadcasted_iota(I32, (x_ref.shape[0], MOE_STAGE), 1)
    gt = gt_ref[...]
    sloc = sloc_ref[...]
    gmat = jnp.zeros(lane.shape, F32)
    for k in range(TOP_K):
        gmat = gmat + jnp.where(lane == sloc[:, TOP_K + k:TOP_K + k + 1], gt[:, k:k + 1], 0.0)
    g_hi = gmat.astype(BF16)
    g_lo = (gmat - g_hi.astype(F32)).astype(BF16)
    half = rows_lo.shape[1]
    for rows, sl in ((rows_lo, slice(0, half)), (rows_hi, slice(half, 2 * half))):
        f = (jnp.dot(g_hi, rows, preferred_element_type=F32)
             + jnp.dot(g_lo, rows, preferred_element_type=F32))
        o_ref[:, sl] = x_ref[:, sl] + g2_ref[0][:, sl] * f


def _combine(y, tabs, sloc, gates, x, g2, tiles_per_mod):
    t, d = x.shape
    tm = ROUTE_TM
    nt = t // tm
    return pl.pallas_call(
        functools.partial(_combine_kernel, n=nt),
        out_shape=jax.ShapeDtypeStruct((t, d), F32),
        grid=(nt,),
        in_specs=[pl.BlockSpec((1, 1, LANES), lambda i: (i, 0, 0), memory_space=pltpu.SMEM),
                  pl.BlockSpec((1, 1, LANES), lambda i: (jnp.minimum(i + 1, nt - 1), 0, 0),
                               memory_space=pltpu.SMEM),
                  pl.BlockSpec((tm, LANES), lambda i: (i, 0)),
                  pl.BlockSpec((tm, LANES), lambda i: (i, 0)),
                  pl.BlockSpec((tm, d), lambda i: (i, 0)),
                  pl.BlockSpec((1, 1, d), lambda i: (i // tiles_per_mod, 0, 0)),
                  pl.BlockSpec(memory_space=pl.ANY)],
        out_specs=pl.BlockSpec((tm, d), lambda i: (i, 0)),
        scratch_shapes=[pltpu.VMEM((2, MOE_STAGE, d // 2), U32), pltpu.SemaphoreType.DMA((2,))],
        compiler_params=_cparams(("arbitrary",)),
        name="moe_combine",
    )(tabs, tabs, sloc, gates, x, g2, y)


def _moe(streams, layer, g, router_w, router_b, w_gu_all, b_gu, w_down, b_down):
    d = streams[0][0].shape[1]
    tm = ROUTE_TM
    rw_f = jnp.zeros((d, LANES), F32).at[:, :N_EXPERTS].set(router_w.astype(F32))
    rw_hi = rw_f.astype(BF16)
    rw = (rw_hi, (rw_f - rw_hi.astype(F32)).astype(BF16))
    rb = jnp.full((1, LANES), NEG_BIG, F32).at[0, :N_EXPERTS].set(router_b.astype(F32))
    counts = jnp.zeros((1, LANES), I32)
    routed = []
    for x_tok, sh, sc, _, tpm in streams:
        h, route_i, gates, sloc_t, n_tile, seg_base, counts = _route(x_tok, g, sh, sc, tpm, rw, rb, counts)
        routed.append((h, route_i, gates, sloc_t, n_tile, seg_base))
    n_tiles = sum(x_tok.shape[0] for x_tok, *_ in streams) // tm
    n_max = n_tiles * tm * TOP_K + n_tiles * N_EXPERTS * (SUBLANES - 1)
    n_max = (n_max + MOE_TM - 1) // MOE_TM * MOE_TM
    cnt = counts[0, :N_EXPERTS]
    total = jnp.sum(cnt)
    base = jnp.cumsum(cnt) - cnt
    tabs = []
    for h, _, _, _, n_tile, seg_base in routed:
        nt = h.shape[0] // tm
        pad = _round_up8(n_tile[:, 0, :N_EXPERTS])
        off = jnp.cumsum(pad, axis=1) - pad
        dst = base[None, :] + seg_base[:, 0, :N_EXPERTS]
        nch = pad // SUBLANES
        tab = jnp.concatenate([nch, dst, off, jnp.sum(nch, axis=1, keepdims=True),
                               jnp.zeros((nt, LANES - 3 * N_EXPERTS - 1), I32)], axis=1)
        tabs.append(tab.astype(I32).reshape(nt, 1, LANES))
    free = n_max - total
    tail = jnp.stack([free // MOE_TM, (free % MOE_TM) // SUBLANES, total]).astype(I32)
    xs = _dispatch([r[0] for r in routed], jnp.concatenate(tabs, axis=0),
                   jnp.concatenate([r[3] for r in routed], axis=0), tail, n_max)
    wg, wu = _split_gu(w_gu_all, layer)
    bg = b_gu[:, None, 0::2].astype(F32)
    bu = b_gu[:, None, 1::2].astype(F32)
    y = _gmm(xs, _visit_tables(cnt, n_max), wg, wu, bg, bu, w_down.astype(BF16),
             b_down[:, None, :].astype(F32))
    return [_combine(y, tab, r[1], r[2], x_tok, g2, tpm)
            for (x_tok, _, _, g2, tpm), r, tab in zip(streams, routed, tabs)]


def kernel(x, c, ctx, c_ctx, w_mod, b_mod, norm1_g, norm2_g, w_in, w_out, lru_conv_w, lru_conv_b, lru_wa, lru_ba, lru_wx, lru_bx, lru_lam, ssd_conv_w, ssd_conv_b, ssd_a_log, ssd_dt_bias, ssd_d, ssd_norm_g, da_q_norm, da_k_norm, da_lam_q, da_lam_k, da_subln_g, router_w, router_b, exp_w_gu, exp_b_gu, exp_w_down, exp_b_down):
    b, s, d = x.shape
    n_ctx = ctx.shape[1]
    depth = w_mod.shape[0]
    cos, sin = _rope_tables(s)
    x_lat, x_ctx = x, ctx
    mod_rows = jnp.zeros((2 * SUBLANES, d), F32).at[:b].set(c).at[b].set(c_ctx)

    for l in range(depth):
        need_ctx = l < depth - 1
        lam_init = 0.8 - 0.6 * math.exp(-0.3 * l)
        mod = _modulation(mod_rows, w_mod[l], b_mod[l])
        m_lat = mod[:b].reshape(b, 1, 6, d)
        m_ctx = jnp.broadcast_to(mod[b].reshape(1, 1, 6, d), (b, 1, 6, d))
        part = lambda m, j: m[:, :, j, :]

        w_in_l = _prep_w_in(w_in[l])
        p_lat = _proj_in(x_lat, norm1_g[l], part(m_lat, 0), part(m_lat, 1), w_in_l)
        p_ctx = _proj_in(x_ctx, norm1_g[l], part(m_ctx, 0), part(m_ctx, 1), w_in_l)

        zl = jnp.zeros((b, LRU_WIDTH), F32)
        lru_args = (lru_wa[l], lru_ba[l], lru_wx[l], lru_bx[l], lru_lam[l])
        xc_ctx = _dwconv(p_ctx["lx"], lru_conv_w[l], lru_conv_b[l], act=False)
        hf_c, hb_c = _lru_scan(xc_ctx, *lru_args, zl, zl)
        xc_lat = _dwconv(p_lat["lx"], lru_conv_w[l], lru_conv_b[l], act=False)
        hf_l, hb_l = _lru_scan(xc_lat, *lru_args, hf_c[:, -1], hb_c[:, 0])

        zs = jnp.zeros((b, SSD_GROUPS, 2 * SSD_HEAD_DIM, SSD_STATE), F32)
        xbc_ctx = _dwconv(p_ctx["xbc"], ssd_conv_w[l], ssd_conv_b[l], act=True)
        yf_c, yb_c, sf_c, sb_c = _ssd_scan(xbc_ctx, p_ctx["dt"], ssd_a_log[l], ssd_dt_bias[l], zs, zs)
        xbc_lat = _dwconv(p_lat["xbc"], ssd_conv_w[l], ssd_conv_b[l], act=True)
        yf_l, yb_l, _, _ = _ssd_scan(xbc_lat, p_lat["dt"], ssd_a_log[l], ssd_dt_bias[l], sf_c, sb_c)

        lq = da_lam_q[l].astype(F32)
        lk = da_lam_k[l].astype(F32)
        lam = jnp.exp(jnp.sum(lq[0] * lk[0])) - jnp.exp(jnp.sum(lq[1] * lk[1])) + lam_init
        kc = _qk_prep(p_ctx["k"], da_k_norm[l], cos, sin, rope=False, scale=1.0, kc=min(ATT_KC, n_ctx))
        kl = _qk_prep(p_lat["k"], da_k_norm[l], cos, sin, rope=True, scale=1.0, kc=min(ATT_KC, s))
        ql = _qk_prep(p_lat["q"], da_q_norm[l], cos, sin, rope=True, scale=DA_SCALE * LOG2E, kc=0)
        da_l = _diff_attn(ql, [(kc, p_ctx["v"]), (kl, p_lat["v"])], lam, da_subln_g[l], lam_init)

        dvec = jnp.repeat(ssd_d[l].astype(F32), SSD_HEAD_DIM).reshape(1, SSD_WIDTH)
        ng = ssd_norm_g[l].astype(F32).reshape(1, SSD_WIDTH)
        w_out_l = w_out[l].astype(BF16)
        if need_ctx:
            qc = _qk_prep(p_ctx["q"], da_q_norm[l], cos, sin, rope=False, scale=DA_SCALE * LOG2E, kc=0)
            da_c = _diff_attn(qc, [(kc, p_ctx["v"])], lam, da_subln_g[l], lam_init)
            x_ctx = _proj_out(x_ctx, p_ctx["lg"], hf_c, hb_c, yf_c, yb_c, xbc_ctx, p_ctx["z"], da_c,
                              dvec, ng, part(m_ctx, 2), w_out_l)
        x_lat = _proj_out(x_lat, p_lat["lg"], hf_l, hb_l, yf_l, yb_l, xbc_lat, p_lat["z"], da_l,
                          dvec, ng, part(m_lat, 2), w_out_l)

        moe_w = (router_w[l], router_b[l], exp_w_gu, exp_b_gu[l], exp_w_down[l], exp_b_down[l])
        streams = [(x_lat.reshape(-1, d), part(m_lat, 3), part(m_lat, 4), part(m_lat, 5), s // ROUTE_TM)]
        if need_ctx:
            streams.append((x_ctx.reshape(-1, d), part(m_ctx, 3), part(m_ctx, 4), part(m_ctx, 5), n_ctx // ROUTE_TM))
        outs = _moe(streams, l, norm2_g[l], *moe_w)
        x_lat = outs[0].reshape(b, s, d)
        if need_ctx:
            x_ctx = outs[1].reshape(b, n_ctx, d)
    return x_lat
```

```python
import functools
import math

import jax
import jax.numpy as jnp
from jax import lax
from jax.experimental import pallas as pl
from jax.experimental.pallas import tpu as pltpu

F32 = jnp.float32
BF16 = jnp.bfloat16
I32 = jnp.int32

GRID_W = 64
EPS = 1e-6
CONV_K = 4
LRU_HEADS = 4
LRU_HEAD_DIM = 64
LRU_WIDTH = 256
LRU_C = 8.0
SSD_HEADS = 4
SSD_HEAD_DIM = 64
SSD_WIDTH = 256
SSD_GROUPS = 2
SSD_STATE = 128
SSD_CHUNK = 128
SSD_CONV_DIM = 768
DA_HEADS = 4
DA_QK_DIM = 64
DA_V_DIM = 128
DA_QK_WIDTH = 512
DA_WIDTH = 512
DA_SCALE = DA_QK_DIM ** -0.5
ROPE_BASE = 10000.0
N_EXPERTS = 32
TOP_K = 4
SWIGLU_ALPHA = 1.702
SWIGLU_LIMIT = 7.0

LANES = 128
SUBLANES = 8
VMEM_LIMIT = 56 * 1024 * 1024

PROJ_TM = 512
ATT_TQ = 512
ATT_KC = 256
LOG2E = math.log2(math.e)
LRU_TL = 512
SSD_STEP_CHUNKS = 4
ROUTE_TM = 256
MOE_TM = 512
NEG_BIG = -1e30


def _cparams(sem):
    return pltpu.CompilerParams(dimension_semantics=sem, vmem_limit_bytes=VMEM_LIMIT)


def _sigmoid(x):
    return 0.5 * jnp.tanh(0.5 * x) + 0.5


def _silu(x):
    return x * _sigmoid(x)


def _softplus(x):
    return jnp.maximum(x, 0.0) + jnp.log(1.0 + jnp.exp(-jnp.abs(x)))


def _round_up8(n):
    return ((n + (SUBLANES - 1)) >> 3) << 3


def _mod_kernel(c_ref, w_ref, b_ref, o_ref):
    a = _silu(c_ref[...])
    o_ref[...] = jnp.dot(a, w_ref[...], preferred_element_type=F32,
                         precision=lax.Precision.HIGHEST) + b_ref[...]


def _modulation(rows, w, b):
    m, d = rows.shape
    n = w.shape[1]
    tn = 512
    return pl.pallas_call(
        _mod_kernel,
        out_shape=jax.ShapeDtypeStruct((m, n), F32),
        grid=(n // tn,),
        in_specs=[pl.BlockSpec((m, d), lambda j: (0, 0)),
                  pl.BlockSpec((d, tn), lambda j: (0, j)),
                  pl.BlockSpec((1, tn), lambda j: (0, j))],
        out_specs=pl.BlockSpec((m, tn), lambda j: (0, j)),
        compiler_params=_cparams(("arbitrary",)),
        name="modulation",
    )(rows, w, b.reshape(1, n))


def _norm_mod(x, g, sh, sc):
    ms = jnp.mean(x * x, axis=-1, keepdims=True)
    y = x * lax.rsqrt(ms + EPS) * g
    return y * (1.0 + sc) + sh


def _proj_in_kernel(x_ref, g_ref, sh_ref, sc_ref, w_ref, *o_refs, splits):
    h = _norm_mod(x_ref[0], g_ref[...], sh_ref[0], sc_ref[0]).astype(BF16)
    for o_ref, (lo, hi) in zip(o_refs, splits):
        o_ref[0] = jnp.dot(h, w_ref[:, lo:hi], preferred_element_type=F32).astype(o_ref.dtype)


_IN_GROUPS = (("lg", 256, F32), ("lx", 256, F32), ("z", 256, F32), ("xbc", 768, F32),
              ("dt", 128, F32), ("q", 512, F32), ("k", 512, F32), ("v", 512, BF16))


def _prep_w_in(w_in):
    d = w_in.shape[0]
    lo = 2 * LRU_WIDTH + SSD_WIDTH + SSD_CONV_DIM
    hi = lo + 2 * SSD_HEADS
    dt = jnp.concatenate([w_in[:, lo:hi], jnp.zeros((d, LANES - (hi - lo)), w_in.dtype)], axis=1)
    w = jnp.concatenate([w_in[:, :lo], dt, w_in[:, hi:]], axis=1)
    return w.astype(BF16)


def _proj_in(x, g, sh, sc, w):
    b, s, d = x.shape
    tm = min(PROJ_TM, s)
    splits, off = [], 0
    for _, wd, _ in _IN_GROUPS:
        splits.append((off, off + wd))
        off += wd
    out_shape = [jax.ShapeDtypeStruct((b, s, wd), dt) for _, wd, dt in _IN_GROUPS]
    out_specs = [pl.BlockSpec((1, tm, wd), lambda bi, i: (bi, i, 0)) for _, wd, _ in _IN_GROUPS]
    outs = pl.pallas_call(
        functools.partial(_proj_in_kernel, splits=tuple(splits)),
        out_shape=out_shape,
        grid=(b, s // tm),
        in_specs=[pl.BlockSpec((1, tm, d), lambda bi, i: (bi, i, 0)),
                  pl.BlockSpec((1, d), lambda bi, i: (0, 0)),
                  pl.BlockSpec((1, 1, d), lambda bi, i: (bi, 0, 0)),
                  pl.BlockSpec((1, 1, d), lambda bi, i: (bi, 0, 0)),
                  pl.BlockSpec((d, off), lambda bi, i: (0, 0))],
        out_specs=out_specs,
        compiler_params=_cparams(("arbitrary", "arbitrary")),
        name="proj_in",
    )(x, g.reshape(1, d), sh, sc, w)
    return {name: o for (name, _, _), o in zip(_IN_GROUPS, outs)}


def _dwconv_kernel(x_ref, w_ref, b_ref, o_ref, *, act):
    x = x_ref[0]
    s = x.shape[0]
    row = lax.broadcasted_iota(I32, x.shape, 0)
    acc = x * w_ref[2:3, :] + b_ref[...]
    acc = acc + jnp.where(row >= 2, pltpu.roll(x, 2, 0), 0.0) * w_ref[0:1, :]
    acc = acc + jnp.where(row >= 1, pltpu.roll(x, 1, 0), 0.0) * w_ref[1:2, :]
    acc = acc + jnp.where(row < s - 1, pltpu.roll(x, s - 1, 0), 0.0) * w_ref[3:4, :]
    if act:
        acc = _silu(acc)
    o_ref[0] = acc


def _dwconv(x, w, bias, act):
    b, s, c = x.shape
    tc = 256
    return pl.pallas_call(
        functools.partial(_dwconv_kernel, act=act),
        out_shape=jax.ShapeDtypeStruct((b, s, c), F32),
        grid=(b, c // tc),
        in_specs=[pl.BlockSpec((1, s, tc), lambda bi, j: (bi, 0, j)),
                  pl.BlockSpec((CONV_K, tc), lambda bi, j: (0, j)),
                  pl.BlockSpec((1, tc), lambda bi, j: (0, j))],
        out_specs=pl.BlockSpec((1, s, tc), lambda bi, j: (bi, 0, j)),
        compiler_params=_cparams(("arbitrary", "arbitrary")),
        name="dwconv",
    )(x, w, bias.reshape(1, c))


def _lru_coeffs(x, w_ref, b_ref, sp_ref, a_s, u_s):
    g = jnp.dot(x.astype(BF16), w_ref[...], preferred_element_type=F32) + b_ref[...]
    r = _sigmoid(g[:, :LRU_WIDTH])
    i = _sigmoid(g[:, LRU_WIDTH:])
    a = jnp.exp(-LRU_C * r * sp_ref[...])
    a_s[...] = a
    u_s[...] = jnp.sqrt(1.0 - a * a) * (i * x)


def _lru_kernel(xf_ref, xb_ref, wf_ref, wb_ref, bf_ref, bb_ref, spf_ref, spb_ref,
                h0f_ref, h0b_ref, hf_ref, hb_ref, af_s, uf_s, ab_s, ub_s, cf_s, cb_s):
    c = pl.program_id(1)
    tl = xf_ref.shape[1]
    ngroups = tl // SUBLANES
    shape = (SUBLANES, LRU_WIDTH)
    row = lax.broadcasted_iota(I32, shape, 0)

    @pl.when(c == 0)
    def _():
        cf_s[...] = jnp.broadcast_to(h0f_ref[0], shape)
        cb_s[...] = jnp.broadcast_to(h0b_ref[0], shape)

    _lru_coeffs(xf_ref[0], wf_ref, bf_ref, spf_ref, af_s, uf_s)
    _lru_coeffs(xb_ref[0], wb_ref, bb_ref, spb_ref, ab_s, ub_s)

    def group_scan(a, u, carry, reverse):
        for sh in (1, 2, 4):
            keep = (row < SUBLANES - sh) if reverse else (row >= sh)
            amt = SUBLANES - sh if reverse else sh
            a_sh = jnp.where(keep, pltpu.roll(a, amt, 0), 1.0)
            u_sh = jnp.where(keep, pltpu.roll(u, amt, 0), 0.0)
            u = a * u_sh + u
            a = a * a_sh
        return a * carry + u

    def step(gj, carry):
        cf, cb = carry
        slf = pl.ds(pl.multiple_of(gj * SUBLANES, SUBLANES), SUBLANES)
        slb = pl.ds(pl.multiple_of((ngroups - 1 - gj) * SUBLANES, SUBLANES), SUBLANES)
        hf = group_scan(af_s[slf, :], uf_s[slf, :], cf, False)
        hb = group_scan(ab_s[slb, :], ub_s[slb, :], cb, True)
        hf_ref[0, slf, :] = hf
        hb_ref[0, slb, :] = hb
        return (jnp.broadcast_to(hf[SUBLANES - 1:SUBLANES, :], shape),
                jnp.broadcast_to(hb[0:1, :], shape))

    cf, cb = lax.fori_loop(0, ngroups, step, (cf_s[...], cb_s[...]), unroll=2)
    cf_s[...] = cf
    cb_s[...] = cb


def _lru_gate_weights(wa, wx):
    def dense(wh):
        m = jnp.zeros((LRU_WIDTH, LRU_WIDTH), wh.dtype)
        for h in range(LRU_HEADS):
            lo = h * LRU_HEAD_DIM
            m = m.at[lo:lo + LRU_HEAD_DIM, lo:lo + LRU_HEAD_DIM].set(wh[h])
        return m
    return jnp.concatenate([dense(wa), dense(wx)], axis=1).astype(BF16)


def _lru_scan(xc, wa, ba, wx, bx, lam, h0f, h0b):
    b, s, w = xc.shape
    tl = min(LRU_TL, s)
    nc = s // tl
    wf = _lru_gate_weights(wa[0], wx[0])
    wb = _lru_gate_weights(wa[1], wx[1])
    bf = jnp.concatenate([ba[0], bx[0]]).reshape(1, 2 * w)
    bb = jnp.concatenate([ba[1], bx[1]]).reshape(1, 2 * w)
    sp = jax.nn.softplus(-lam.astype(F32))
    const = lambda bi, c: (0, 0)
    hf, hb = pl.pallas_call(
        _lru_kernel,
        out_shape=[jax.ShapeDtypeStruct((b, s, w), F32)] * 2,
        grid=(b, nc),
        in_specs=[pl.BlockSpec((1, tl, w), lambda bi, c: (bi, c, 0)),
                  pl.BlockSpec((1, tl, w), lambda bi, c: (bi, nc - 1 - c, 0)),
                  pl.BlockSpec((w, 2 * w), const), pl.BlockSpec((w, 2 * w), const),
                  pl.BlockSpec((1, 2 * w), const), pl.BlockSpec((1, 2 * w), const),
                  pl.BlockSpec((1, w), const), pl.BlockSpec((1, w), const),
                  pl.BlockSpec((1, 1, w), lambda bi, c: (bi, 0, 0)),
                  pl.BlockSpec((1, 1, w), lambda bi, c: (bi, 0, 0))],
        out_specs=[pl.BlockSpec((1, tl, w), lambda bi, c: (bi, c, 0)),
                   pl.BlockSpec((1, tl, w), lambda bi, c: (bi, nc - 1 - c, 0))],
        scratch_shapes=[pltpu.VMEM((tl, w), F32)] * 4
        + [pltpu.VMEM((SUBLANES, w), F32), pltpu.VMEM((SUBLANES, w), F32)],
        compiler_params=_cparams(("arbitrary", "arbitrary")),
        name="lru_scan",
    )(xc, xc, wf, wb, bf, bb, sp[0:1], sp[1:2], h0f.reshape(b, 1, w), h0b.reshape(b, 1, w))
    return hf, hb


def _ssd_direction(x_ref, dt_ref, row0, avec_ref, dtb_ref, state_s, y_ref, lane0, rev):
    t = SSD_CHUNK
    xbc = x_ref[0, row0:row0 + t, :]
    dtraw = dt_ref[0, row0:row0 + t, :]
    ri = lax.broadcasted_iota(I32, (t, t), 0)
    ci = lax.broadcasted_iota(I32, (t, t), 1)
    causal = (ci >= ri) if rev else (ci <= ri)
    cum_m = jnp.where(causal, 1.0, 0.0).astype(F32)

    dt = _softplus(dtraw + dtb_ref[...])
    a_all = dt * avec_ref[...]
    acs = jnp.dot(cum_m, a_all, preferred_element_type=F32, precision=lax.Precision.HIGHEST)
    acs_t = acs.T
    edge = 0 if rev else t - 1

    xs = xbc[:, :SSD_WIDTH]
    hrow = lax.broadcasted_iota(I32, (t, t), 0)
    for g in range(SSD_GROUPS):
        bm = xbc[:, SSD_WIDTH + g * SSD_STATE:SSD_WIDTH + (g + 1) * SSD_STATE]
        cm = xbc[:, SSD_WIDTH + (SSD_GROUPS + g) * SSD_STATE:SSD_WIDTH + (SSD_GROUPS + g + 1) * SSD_STATE]
        bm_b = bm.astype(BF16)
        cm_b = cm.astype(BF16)
        gmat = lax.dot_general(cm_b, bm_b, (((1,), (1,)), ((), ())), preferred_element_type=F32)
        s_prev = state_s[g]
        off_all = lax.dot_general(cm_b, s_prev.astype(BF16), (((1,), (1,)), ((), ())),
                                  preferred_element_type=F32)
        xd_parts, tot = [], []
        for hl in range(2):
            h = g * 2 + hl
            ln = lane0 + h
            col = jnp.broadcast_to(acs[:, ln:ln + 1], (t, t))
            rowv = acs_t[ln:ln + 1, :]
            lmat = jnp.where(causal, jnp.exp(col - rowv), 0.0)
            xh = xs[:, h * SSD_HEAD_DIM:(h + 1) * SSD_HEAD_DIM] * dt[:, ln:ln + 1]
            ydiag = jnp.dot((gmat * lmat).astype(BF16), xh.astype(BF16), preferred_element_type=F32)
            colh = col[:, :SSD_HEAD_DIM]
            yoff = off_all[:, hl * SSD_HEAD_DIM:(hl + 1) * SSD_HEAD_DIM] * jnp.exp(colh)
            y_ref[0, row0:row0 + t, h * SSD_HEAD_DIM:(h + 1) * SSD_HEAD_DIM] = ydiag + yoff
            total = acs[edge:edge + 1, ln:ln + 1]
            xd_parts.append(xh * jnp.exp(total - colh))
            tot.append(jnp.exp(total))
        xd = jnp.concatenate(xd_parts, axis=1)
        xd_t = xd.T
        st = jnp.dot(xd_t.astype(BF16), bm_b, preferred_element_type=F32)
        fac = jnp.where(hrow < SSD_HEAD_DIM, tot[0], tot[1])
        state_s[g] = fac * s_prev + st


def _ssd_kernel(xf_ref, dtf_ref, xb_ref, dtb_ref, avec_ref, bias_ref, h0f_ref, h0b_ref,
                yf_ref, yb_ref, sf_ref, sb_ref, stf_s, stb_s):
    c = pl.program_id(1)

    @pl.when(c == 0)
    def _():
        stf_s[...] = h0f_ref[0]
        stb_s[...] = h0b_ref[0]

    nsub = xf_ref.shape[1] // SSD_CHUNK
    for j in range(nsub):
        _ssd_direction(xf_ref, dtf_ref, j * SSD_CHUNK, avec_ref, bias_ref, stf_s, yf_ref, 0, False)
        _ssd_direction(xb_ref, dtb_ref, (nsub - 1 - j) * SSD_CHUNK, avec_ref, bias_ref, stb_s, yb_ref,
                       SSD_HEADS, True)
    sf_ref[0] = stf_s[...]
    sb_ref[0] = stb_s[...]


def _ssd_scan(xbc, dt, a_log, dt_bias, h0f, h0b):
    b, s, _ = xbc.shape
    t = min(SSD_STEP_CHUNKS * SSD_CHUNK, s)
    nc = s // t
    avec = jnp.zeros((1, LANES), F32).at[0, :2 * SSD_HEADS].set(-jnp.exp(a_log.astype(F32)).reshape(-1))
    bias = jnp.zeros((1, LANES), F32).at[0, :2 * SSD_HEADS].set(dt_bias.astype(F32).reshape(-1))
    st_shape = (b, SSD_GROUPS, 2 * SSD_HEAD_DIM, SSD_STATE)
    const = lambda bi, c: (0, 0)
    st_spec = pl.BlockSpec((1,) + st_shape[1:], lambda bi, c: (bi, 0, 0, 0))
    yf, yb, sf, sb = pl.pallas_call(
        _ssd_kernel,
        out_shape=[jax.ShapeDtypeStruct((b, s, SSD_WIDTH), F32)] * 2
        + [jax.ShapeDtypeStruct(st_shape, F32)] * 2,
        grid=(b, nc),
        in_specs=[pl.BlockSpec((1, t, SSD_CONV_DIM), lambda bi, c: (bi, c, 0)),
                  pl.BlockSpec((1, t, LANES), lambda bi, c: (bi, c, 0)),
                  pl.BlockSpec((1, t, SSD_CONV_DIM), lambda bi, c: (bi, nc - 1 - c, 0)),
                  pl.BlockSpec((1, t, LANES), lambda bi, c: (bi, nc - 1 - c, 0)),
                  pl.BlockSpec((1, LANES), const), pl.BlockSpec((1, LANES), const),
                  st_spec, st_spec],
        out_specs=[pl.BlockSpec((1, t, SSD_WIDTH), lambda bi, c: (bi, c, 0)),
                   pl.BlockSpec((1, t, SSD_WIDTH), lambda bi, c: (bi, nc - 1 - c, 0)),
                   st_spec, st_spec],
        scratch_shapes=[pltpu.VMEM(st_shape[1:], F32), pltpu.VMEM(st_shape[1:], F32)],
        compiler_params=_cparams(("arbitrary", "arbitrary")),
        name="ssd_scan",
    )(xbc, dt, xbc, dt, avec, bias, h0f, h0b)
    return yf, yb, sf, sb


def _qk_prep_kernel(x_ref, g_ref, cos_ref, sin_ref, seg_ref, o_ref, *, rope, scale, kc):
    x = x_ref[0]
    sq = x * x
    hi = sq.astype(BF16)
    lo = (sq - hi.astype(F32)).astype(BF16)
    ssum = (jnp.dot(hi, seg_ref[...], preferred_element_type=F32)
            + jnp.dot(lo, seg_ref[...], preferred_element_type=F32))
    y = x * lax.rsqrt(ssum * (1.0 / DA_QK_DIM) + EPS) * g_ref[...]
    if rope:
        n = x.shape[1]
        lane = lax.broadcasted_iota(I32, x.shape, 1)
        first = (lane & (DA_QK_DIM - 1)) < DA_QK_DIM // 2
        partner = jnp.where(first, pltpu.roll(y, n - DA_QK_DIM // 2, 1), pltpu.roll(y, DA_QK_DIM // 2, 1))
        cosv = jnp.concatenate([cos_ref[...]] * (n // LANES), axis=1)
        sinv = jnp.concatenate([sin_ref[...]] * (n // LANES), axis=1)
        y = y * cosv + partner * sinv
    y = y * scale
    if kc:
        for h in range(DA_HEADS):
            head = y[:, h * 2 * DA_QK_DIM:(h + 1) * 2 * DA_QK_DIM]
            for cc in range(y.shape[0] // kc):
                t = head[cc * kc:(cc + 1) * kc, :].T.astype(BF16)
                o_ref[0, 2 * h, cc] = t[:DA_QK_DIM]
                o_ref[0, 2 * h + 1, cc] = t[DA_QK_DIM:]
    else:
        yb = y.astype(BF16)
        for j in range(2 * DA_HEADS):
            o_ref[0, j] = yb[:, j * DA_QK_DIM:(j + 1) * DA_QK_DIM]


def _rope_tables(s):
    pos = jnp.arange(s, dtype=I32)
    r = (pos // GRID_W).astype(F32)
    col = (pos % GRID_W).astype(F32)
    per_axis = DA_QK_DIM // 4
    inv = ROPE_BASE ** (-jnp.arange(per_axis, dtype=F32) / per_axis)
    ang = jnp.concatenate([r[:, None] * inv, col[:, None] * inv], axis=-1)
    cos, sin = jnp.cos(ang), jnp.sin(ang)
    cos64 = jnp.concatenate([cos, cos], axis=-1)
    sin64 = jnp.concatenate([-sin, sin], axis=-1)
    return jnp.concatenate([cos64, cos64], axis=-1), jnp.concatenate([sin64, sin64], axis=-1)


def _qk_prep(x, g, cos, sin, rope, scale, kc):
    b, s, n = x.shape
    ts = min(512, s)
    gfull = jnp.tile(g.astype(F32), n // DA_QK_DIM).reshape(1, n)
    idx = jnp.arange(n) // DA_QK_DIM
    seg = (idx[:, None] == idx[None, :]).astype(BF16)
    if kc:
        out_shape = jax.ShapeDtypeStruct((b, 2 * DA_HEADS, s // kc, DA_QK_DIM, kc), BF16)
        out_spec = pl.BlockSpec((1, 2 * DA_HEADS, ts // kc, DA_QK_DIM, kc), lambda bi, i: (bi, 0, i, 0, 0))
    else:
        out_shape = jax.ShapeDtypeStruct((b, 2 * DA_HEADS, s, DA_QK_DIM), BF16)
        out_spec = pl.BlockSpec((1, 2 * DA_HEADS, ts, DA_QK_DIM), lambda bi, i: (bi, 0, i, 0))
    return pl.pallas_call(
        functools.partial(_qk_prep_kernel, rope=rope, scale=scale, kc=kc),
        out_shape=out_shape,
        grid=(b, s // ts),
        in_specs=[pl.BlockSpec((1, ts, n), lambda bi, i: (bi, i, 0)),
                  pl.BlockSpec((1, n), lambda bi, i: (0, 0)),
                  pl.BlockSpec((ts, LANES), lambda bi, i: (i, 0)),
                  pl.BlockSpec((ts, LANES), lambda bi, i: (i, 0)),
                  pl.BlockSpec((n, n), lambda bi, i: (0, 0))],
        out_specs=out_spec,
        compiler_params=_cparams(("arbitrary", "arbitrary")),
        name="qk_prep",
    )(x, gfull, cos, sin, seg)


def _attn_kernel(lam_ref, q_ref, *refs, nseg, out_scale):
    lam = lam_ref[0, 0]
    segs = [(refs[2 * j], refs[2 * j + 1]) for j in range(nseg)]
    g_ref, o_ref = refs[2 * nseg], refs[2 * nseg + 1]
    tq = q_ref.shape[2]
    m = [jnp.full((tq, LANES), NEG_BIG, F32)] * 2
    l = [jnp.zeros((tq, LANES), F32)] * 2
    acc = [jnp.zeros((tq, DA_V_DIM), F32)] * 2

    for k_ref, v_ref in segs:
        nchunk, kc = k_ref.shape[2], k_ref.shape[4]
        for c in range(nchunk):
            v = v_ref[0, c * kc:(c + 1) * kc, :]
            for sub in range(2):
                s = jnp.dot(q_ref[0, sub], k_ref[0, sub, c], preferred_element_type=F32)
                mx = s[:, :LANES]
                for j in range(1, kc // LANES):
                    mx = jnp.maximum(mx, s[:, j * LANES:(j + 1) * LANES])
                m_new = jnp.maximum(m[sub], jnp.max(mx, axis=-1, keepdims=True))
                alpha = jnp.exp2(m[sub] - m_new)
                p = jnp.exp2(s - jnp.concatenate([m_new] * (kc // LANES), axis=1))
                lsum = alpha * l[sub]
                for j in range(kc // LANES):
                    lsum = lsum + p[:, j * LANES:(j + 1) * LANES]
                l[sub] = lsum
                acc[sub] = alpha * acc[sub] + jnp.dot(p.astype(BF16), v, preferred_element_type=F32)
                m[sub] = m_new

    l1 = jnp.sum(l[0], axis=-1, keepdims=True)
    l2 = jnp.sum(l[1], axis=-1, keepdims=True)
    o = acc[0] * (1.0 / l1) - lam * (acc[1] * (1.0 / l2))
    ms = jnp.mean(o * o, axis=-1, keepdims=True)
    o_ref[0] = o * lax.rsqrt(ms + EPS) * g_ref[...] * out_scale


def _diff_attn(qp, segments, lam, g, lam_init):
    b, _, s, d = qp.shape
    tq = min(ATT_TQ, s)
    seg_specs, seg_args = [], []
    for kt, v in segments:
        nchunk, kc = kt.shape[2], kt.shape[4]
        seg_specs += [pl.BlockSpec((1, 2, nchunk, d, kc), lambda bi, h, i: (bi, h, 0, 0, 0)),
                      pl.BlockSpec((1, nchunk * kc, DA_V_DIM), lambda bi, h, i: (bi, 0, h))]
        seg_args += [kt, v]
    return pl.pallas_call(
        functools.partial(_attn_kernel, nseg=len(segments), out_scale=1.0 - lam_init),
        out_shape=jax.ShapeDtypeStruct((b, s, DA_WIDTH), F32),
        grid=(b, DA_HEADS, s // tq),
        in_specs=[pl.BlockSpec(memory_space=pltpu.SMEM),
                  pl.BlockSpec((1, 2, tq, d), lambda bi, h, i: (bi, h, i, 0))] + seg_specs
        + [pl.BlockSpec((1, DA_V_DIM), lambda bi, h, i: (0, 0))],
        out_specs=pl.BlockSpec((1, tq, DA_V_DIM), lambda bi, h, i: (bi, i, h)),
        compiler_params=_cparams(("arbitrary", "arbitrary", "arbitrary")),
        name="diff_attn",
    )(lam.reshape(1, 1).astype(F32), qp, *seg_args, g.reshape(1, DA_V_DIM).astype(F32))


def _gelu_tanh(x):
    return 0.5 * x * (1.0 + jnp.tanh(math.sqrt(2.0 / math.pi) * (x + 0.044715 * x * x * x)))


def _proj_out_kernel(x_ref, lg_ref, hf_ref, hb_ref, yf_ref, yb_ref, xs_ref, z_ref, da_ref,
                     dvec_ref, ng_ref, gate_ref, w_ref, o_ref):
    lru = _gelu_tanh(lg_ref[0]) * (hf_ref[0] + hb_ref[0])
    y = yf_ref[0] + yb_ref[0] + dvec_ref[...] * xs_ref[0]
    y = y * _silu(z_ref[0])
    ssd = y * lax.rsqrt(jnp.mean(y * y, axis=-1, keepdims=True) + EPS) * ng_ref[...]
    o = jnp.dot(lru.astype(BF16), w_ref[0:LRU_WIDTH, :], preferred_element_type=F32)
    o = o + jnp.dot(ssd.astype(BF16), w_ref[LRU_WIDTH:LRU_WIDTH + SSD_WIDTH, :], preferred_element_type=F32)
    o = o + jnp.dot(da_ref[0].astype(BF16), w_ref[LRU_WIDTH + SSD_WIDTH:, :], preferred_element_type=F32)
    o_ref[0] = x_ref[0] + gate_ref[0] * o


def _proj_out(x, lg, hf, hb, yf, yb, xbc, z, da, dvec, ng, gate, w):
    b, s, d = x.shape
    tm = min(PROJ_TM, s)
    tok = lambda wd: pl.BlockSpec((1, tm, wd), lambda bi, i: (bi, i, 0))
    const2 = lambda bi, i: (0, 0)
    return pl.pallas_call(
        _proj_out_kernel,
        out_shape=jax.ShapeDtypeStruct((b, s, d), F32),
        grid=(b, s // tm),
        in_specs=[tok(d), tok(LRU_WIDTH), tok(LRU_WIDTH), tok(LRU_WIDTH), tok(SSD_WIDTH),
                  tok(SSD_WIDTH), tok(SSD_WIDTH), tok(SSD_WIDTH), tok(DA_WIDTH),
                  pl.BlockSpec((1, SSD_WIDTH), const2), pl.BlockSpec((1, SSD_WIDTH), const2),
                  pl.BlockSpec((1, 1, d), lambda bi, i: (bi, 0, 0)),
                  pl.BlockSpec(w.shape, const2)],
        out_specs=tok(d),
        compiler_params=_cparams(("arbitrary", "arbitrary")),
        name="proj_out",
    )(x, lg, hf, hb, yf, yb, xbc, z, da, dvec, ng, gate, w)


def _route_kernel(x_ref, g_ref, sh_ref, sc_ref, rwh_ref, rwl_ref, rb_ref, c0_ref, h_ref, ri_ref, gt_ref,
                  slt_ref, n_ref, base_ref, cnt_ref, carry_s):
    i = pl.program_id(0)
    tm = x_ref.shape[0]

    @pl.when(i == 0)
    def _():
        carry_s[...] = c0_ref[...]

    h = _norm_mod(x_ref[...], g_ref[...], sh_ref[0], sc_ref[0])
    h_hi = h.astype(BF16)
    h_ref[...] = h_hi
    h_lo = (h - h_hi.astype(F32)).astype(BF16)
    logits = (jnp.dot(h_hi, rwh_ref[...], preferred_element_type=F32)
              + jnp.dot(h_hi, rwl_ref[...], preferred_element_type=F32)
              + jnp.dot(h_lo, rwh_ref[...], preferred_element_type=F32)) + rb_ref[...]
    lane = lax.broadcasted_iota(I32, logits.shape, 1)
    lane_f = lane.astype(F32)
    vals, idxs = [], []
    cur = logits
    for _ in range(TOP_K):
        m = jnp.max(cur, axis=-1, keepdims=True)
        idx = jnp.min(jnp.where(cur >= m, lane_f, float(LANES)), axis=-1, keepdims=True).astype(I32)
        vals.append(m)
        idxs.append(idx)
        cur = jnp.where(lane == idx, 2.0 * NEG_BIG, cur)
    es = [jnp.exp(v - vals[0]) for v in vals]
    inv = 1.0 / (es[0] + es[1] + es[2] + es[3])
    onehot = jnp.zeros(logits.shape, F32)
    for idx in idxs:
        onehot = onehot + jnp.where(lane == idx, 1.0, 0.0)
    ri = lax.broadcasted_iota(I32, (tm, tm), 0)
    ci = lax.broadcasted_iota(I32, (tm, tm), 1)
    strict = jnp.where(ci < ri, 1.0, 0.0).astype(BF16)
    before = jnp.dot(strict, onehot.astype(BF16), preferred_element_type=F32)
    n_tile = jnp.sum(onehot, axis=0, keepdims=True).astype(I32)
    pad_tile = _round_up8(n_tile)
    li = lax.broadcasted_iota(I32, (LANES, LANES), 0)
    lj = lax.broadcasted_iota(I32, (LANES, LANES), 1)
    below = jnp.where(li < lj, 1.0, 0.0).astype(BF16)
    pad_rows = jnp.broadcast_to(pad_tile.astype(F32), (SUBLANES, LANES)).astype(BF16)
    off = jnp.dot(pad_rows, below, preferred_element_type=F32)[0:1, :]
    place = before + off
    ri_out = jnp.zeros(logits.shape, I32)
    gt_out = jnp.zeros(logits.shape, F32)
    sl_f = jnp.full(logits.shape, -1.0, F32)
    for k in range(TOP_K):
        sloc = jnp.sum(jnp.where(lane == idxs[k], place, 0.0), axis=-1, keepdims=True)
        ri_out = ri_out + jnp.where(lane == k, idxs[k], 0) + jnp.where(lane == TOP_K + k, sloc.astype(I32), 0)
        sl_f = jnp.where(lane == k, sloc, sl_f)
        gt_out = gt_out + jnp.where(lane == k, es[k] * inv, 0.0)
    ri_ref[...] = ri_out
    gt_ref[...] = gt_out
    slt_ref[0] = sl_f.T[0:SUBLANES, :].astype(I32)
    n_ref[0] = n_tile
    base_ref[0] = carry_s[...]
    carry_s[...] = carry_s[...] + pad_tile
    cnt_ref[...] = carry_s[...]


def _route(x, g, sh, sc, tiles_per_mod, rw, rb, counts0):
    t, d = x.shape
    tm = ROUTE_TM
    tile = lambda wd: pl.BlockSpec((tm, wd), lambda i: (i, 0))
    const = lambda i: (0, 0)
    mod = pl.BlockSpec((1, 1, d), lambda i: (i // tiles_per_mod, 0, 0))
    nt = t // tm
    per_tile = pl.BlockSpec((1, 1, LANES), lambda i: (i, 0, 0))
    return pl.pallas_call(
        _route_kernel,
        out_shape=[jax.ShapeDtypeStruct((t, d), BF16), jax.ShapeDtypeStruct((t, LANES), I32),
                   jax.ShapeDtypeStruct((t, LANES), F32), jax.ShapeDtypeStruct((nt, SUBLANES, tm), I32),
                   jax.ShapeDtypeStruct((nt, 1, LANES), I32),
                   jax.ShapeDtypeStruct((nt, 1, LANES), I32), jax.ShapeDtypeStruct((1, LANES), I32)],
        grid=(nt,),
        in_specs=[tile(d), pl.BlockSpec((1, d), const), mod, mod,
                  pl.BlockSpec((d, LANES), const), pl.BlockSpec((d, LANES), const),
                  pl.BlockSpec((1, LANES), const), pl.BlockSpec((1, LANES), const)],
        out_specs=[tile(d), tile(LANES), tile(LANES), pl.BlockSpec((1, SUBLANES, tm), lambda i: (i, 0, 0)),
                   per_tile, per_tile, pl.BlockSpec((1, LANES), const)],
        scratch_shapes=[pltpu.VMEM((1, LANES), I32)],
        compiler_params=_cparams(("arbitrary",)),
        name="moe_route",
    )(x, g.reshape(1, d), sh, sc, rw[0], rw[1], rb, counts0)


MOE_STAGE = -(-(ROUTE_TM * TOP_K + N_EXPERTS * (SUBLANES - 1)) // 256) * 256
_TAB_N, _TAB_DST, _TAB_OFF, _TAB_TOTAL = 0, N_EXPERTS, 2 * N_EXPERTS, 3 * N_EXPERTS


U32 = jnp.uint32
_HI16 = 0xFFFF0000


def _pack_halves(a, is_bf16=False):
    n = a.shape[1] // 2
    lo, hi = a[:, :n], a[:, n:]
    if not is_bf16:
        lo, hi = lo.astype(BF16).astype(F32), hi.astype(BF16).astype(F32)
    return (pltpu.bitcast(lo, U32) >> 16) | (pltpu.bitcast(hi, U32) & U32(_HI16))


def _unpack_halves(u):
    lo = pltpu.bitcast(u << 16, F32).astype(BF16)
    hi = pltpu.bitcast(u & U32(_HI16), F32).astype(BF16)
    return lo, hi


_BIG_CHUNKS = 4


def _chunk_copies(tab_ref, stage_ref, rows_hbm, sem, to_hbm):
    def copy(off, dst, rows):
        s = stage_ref.at[pl.ds(pl.multiple_of(off, SUBLANES), rows), :]
        r = rows_hbm.at[pl.ds(pl.multiple_of(dst, SUBLANES), rows), :]
        if to_hbm:
            pltpu.make_async_copy(s, r, sem).start()
        else:
            pltpu.make_async_copy(r, s, sem).start()

    def expert(e, carry):
        n = tab_ref[0, 0, _TAB_N + e]
        dst = tab_ref[0, 0, _TAB_DST + e]
        off = tab_ref[0, 0, _TAB_OFF + e]
        big_rows = _BIG_CHUNKS * SUBLANES
        n_big = n // _BIG_CHUNKS

        def big(j, c):
            copy(off + j * big_rows, dst + j * big_rows, big_rows)
            return c

        def small(j, c):
            copy(off + (n_big * _BIG_CHUNKS + j) * SUBLANES, dst + (n_big * _BIG_CHUNKS + j) * SUBLANES, SUBLANES)
            return c

        lax.fori_loop(0, n_big, big, 0)
        lax.fori_loop(0, n - n_big * _BIG_CHUNKS, small, 0)
        return carry
    lax.fori_loop(0, N_EXPERTS, expert, 0)


def _chunk_waits(tab_ref, stage_ref, rows_hbm, sem):
    rows = tab_ref[0, 0, _TAB_TOTAL] * SUBLANES
    pltpu.make_async_copy(rows_hbm.at[pl.ds(0, rows), :], stage_ref.at[pl.ds(0, rows), :], sem).wait()


def _dispatch_kernel(tab_ref, tprev_ref, tail_ref, sloc_ref, *refs, tiles):
    h_refs = refs[:len(tiles)]
    xs_out, stage_s, zero_s, sems = refs[len(tiles):]
    i = pl.program_id(0)
    n_steps = sum(tiles)
    slot = i % 2
    row = lax.broadcasted_iota(I32, (MOE_STAGE, ROUTE_TM), 0)
    perm = jnp.zeros((MOE_STAGE, ROUTE_TM), F32)
    for k in range(TOP_K):
        perm = perm + jnp.where(row == sloc_ref[0, k:k + 1, :], 1.0, 0.0)
    perm = perm.astype(BF16)
    off = 0
    for h_ref, nt in zip(h_refs, tiles):
        @pl.when((i >= off) & (i < off + nt))
        def _(h_ref=h_ref):
            stage_s[slot] = _pack_halves(jnp.dot(perm, h_ref[...], preferred_element_type=F32), is_bf16=True)
        off += nt
    _chunk_copies(tab_ref, stage_s.at[slot], xs_out, sems.at[slot], True)

    @pl.when(i > 0)
    def _():
        _chunk_waits(tprev_ref, stage_s.at[1 - slot], xs_out, sems.at[1 - slot])

    @pl.when(i == n_steps - 1)
    def _():
        _chunk_waits(tab_ref, stage_s.at[slot], xs_out, sems.at[slot])
        sem = sems.at[slot]
        zero_s[...] = jnp.zeros(zero_s.shape, U32)
        zr = zero_s.shape[0]
        n_big, n_small, start = tail_ref[0], tail_ref[1], tail_ref[2]

        def big(j, c):
            pltpu.make_async_copy(zero_s, xs_out.at[pl.ds(pl.multiple_of(start + j * zr, SUBLANES), zr), :], sem).start()
            return c

        def small(j, c):
            r0 = pl.multiple_of(start + n_big * zr + j * SUBLANES, SUBLANES)
            pltpu.make_async_copy(zero_s.at[pl.ds(0, SUBLANES), :], xs_out.at[pl.ds(r0, SUBLANES), :], sem).start()
            return c

        def big_wait(j, c):
            pltpu.make_async_copy(zero_s, xs_out.at[pl.ds(0, zr), :], sem).wait()
            return c

        def small_wait(j, c):
            pltpu.make_async_copy(zero_s.at[pl.ds(0, SUBLANES), :], xs_out.at[pl.ds(0, SUBLANES), :], sem).wait()
            return c

        lax.fori_loop(0, n_big, big, 0)
        lax.fori_loop(0, n_small, small, 0)
        lax.fori_loop(0, n_big, big_wait, 0)
        lax.fori_loop(0, n_small, small_wait, 0)


def _dispatch(hs, tabs, slocs, tail, n_max):
    d = hs[0].shape[1]
    tm = ROUTE_TM
    tiles = tuple(h.shape[0] // tm for h in hs)
    specs, off = [], 0
    for nt in tiles:
        specs.append(pl.BlockSpec((tm, d), lambda i, off=off, nt=nt: (jnp.clip(i - off, 0, nt - 1), 0)))
        off += nt
    return pl.pallas_call(
        functools.partial(_dispatch_kernel, tiles=tiles),
        out_shape=jax.ShapeDtypeStruct((n_max, d // 2), U32),
        grid=(sum(tiles),),
        in_specs=[pl.BlockSpec((1, 1, LANES), lambda i: (i, 0, 0), memory_space=pltpu.SMEM),
                  pl.BlockSpec((1, 1, LANES), lambda i: (jnp.maximum(i - 1, 0), 0, 0), memory_space=pltpu.SMEM),
                  pl.BlockSpec(memory_space=pltpu.SMEM),
                  pl.BlockSpec((1, SUBLANES, tm), lambda i: (i, 0, 0))] + specs,
        out_specs=pl.BlockSpec(memory_space=pl.ANY),
        scratch_shapes=[pltpu.VMEM((2, MOE_STAGE, d // 2), U32), pltpu.VMEM((MOE_TM, d // 2), U32),
                        pltpu.SemaphoreType.DMA((2,))],
        compiler_params=_cparams(("arbitrary",)),
        name="moe_dispatch",
    )(tabs, tabs, tail, slocs, *hs)


_SPLIT_W = 2 * LANES


def _gmm_kernel(vt_ref, ve_ref, lo_ref, hi_ref, first_ref, newe_ref, xs_ref, wgu_ref, perm_ref, bg_ref, bu_ref,
                wdf_ref, bd_ref, y_ref, wg_ref, wu_ref, wd_ref):
    del vt_ref, ve_ref
    v = pl.program_id(0)
    lo, hi = lo_ref[v], hi_ref[v]

    @pl.when((newe_ref[v] > 0) & (hi > lo))
    def _():
        for j in range(wgu_ref.shape[1] // _SPLIT_W):
            r = jnp.dot(wgu_ref[:, j * _SPLIT_W:(j + 1) * _SPLIT_W].astype(BF16), perm_ref[...],
                        preferred_element_type=F32)
            wg_ref[:, j * LANES:(j + 1) * LANES] = r[:, :LANES].astype(BF16)
            wu_ref[:, j * LANES:(j + 1) * LANES] = r[:, LANES:].astype(BF16)
        wd_ref[...] = wdf_ref[...].astype(BF16)

    @pl.when(hi > lo)
    def _():
        x_lo, x_hi = _unpack_halves(xs_ref[...])
        half = x_lo.shape[1]
        gate = (jnp.dot(x_lo, wg_ref[:half, :], preferred_element_type=F32)
                + jnp.dot(x_hi, wg_ref[half:, :], preferred_element_type=F32)) + bg_ref[...]
        up = (jnp.dot(x_lo, wu_ref[:half, :], preferred_element_type=F32)
              + jnp.dot(x_hi, wu_ref[half:, :], preferred_element_type=F32)) + bu_ref[...]
        gate = jnp.minimum(gate, SWIGLU_LIMIT)
        up = jnp.clip(up, -SWIGLU_LIMIT, SWIGLU_LIMIT)
        glu = gate * _sigmoid(SWIGLU_ALPHA * gate)
        act = ((up + 1.0) * glu).astype(BF16)
        down = lambda: _pack_halves(jnp.dot(act, wd_ref[...], preferred_element_type=F32) + bd_ref[...])
        whole = (lo == 0) & (hi == xs_ref.shape[0])
        first = first_ref[v] > 0
        row = lax.broadcasted_iota(I32, y_ref.shape, 0)
        mine = (row >= lo) & (row < hi)

        @pl.when(whole)
        def _():
            y_ref[...] = down()

        @pl.when(jnp.logical_not(whole) & first)
        def _():
            y_ref[...] = jnp.where(mine, down(), U32(0))

        @pl.when(jnp.logical_not(whole) & jnp.logical_not(first))
        def _():
            y_ref[...] = jnp.where(mine, down(), y_ref[...])

    @pl.when((hi <= lo) & (first_ref[v] > 0))
    def _():
        y_ref[...] = jnp.zeros(y_ref.shape, U32)


def _gmm(xs, visits, layer, w_gu, bg, bu, w_down, bd):
    n_rows = xs.shape[0]
    _, _, d, de2 = w_gu.shape
    de = de2 // 2
    tm = MOE_TM
    nv = visits[0].shape[0]
    src = jnp.arange(_SPLIT_W)
    dst = jnp.where(src % 2 == 0, src // 2, LANES + src // 2)
    perm = (dst[:, None] == jnp.arange(_SPLIT_W)[None, :]).astype(BF16)
    ex = lambda shape: pl.BlockSpec((None,) + shape, lambda v, vt, ve, lo, hi, fi, ne: (ve[v], 0, 0))
    lay = lambda shape: pl.BlockSpec((None, None) + shape, lambda v, vt, ve, lo, hi, fi, ne: (layer, ve[v], 0, 0))
    rows = pl.BlockSpec((tm, d // 2), lambda v, vt, ve, lo, hi, fi, ne: (vt[v], 0))
    grid_spec = pltpu.PrefetchScalarGridSpec(
        num_scalar_prefetch=6,
        grid=(nv,),
        in_specs=[rows, lay((d, de2)), pl.BlockSpec((_SPLIT_W, _SPLIT_W), lambda v, *_: (0, 0)),
                  ex((1, de)), ex((1, de)), lay((de, d)), ex((1, d))],
        out_specs=rows,
        scratch_shapes=[pltpu.VMEM((d, de), BF16), pltpu.VMEM((d, de), BF16), pltpu.VMEM((de, d), BF16)],
    )
    return pl.pallas_call(
        _gmm_kernel,
        out_shape=jax.ShapeDtypeStruct((n_rows, d // 2), U32),
        grid_spec=grid_spec,
        compiler_params=_cparams(("arbitrary",)),
        name="moe_experts",
    )(*visits, xs, w_gu, perm, bg, bu, w_down, bd)


def _visit_tables(cnt, n_rows):
    tm = MOE_TM
    nt = n_rows // tm
    nv = nt + N_EXPERTS - 1
    end = jnp.cumsum(cnt)
    start = end - cnt
    first_tile = start // tm
    last_tile = jnp.maximum(end - 1, 0) // tm
    nvis = jnp.where(cnt > 0, last_tile - first_tile + 1, 0)
    vend = jnp.cumsum(nvis)
    vstart = vend - nvis
    v = jnp.arange(nv, dtype=I32)
    e = jnp.minimum(jnp.sum((vend[None, :] <= v[:, None]).astype(I32), axis=1), N_EXPERTS - 1)
    n_used = vend[N_EXPERTS - 1]
    used = v < n_used
    tiles_used = (end[N_EXPERTS - 1] + tm - 1) // tm
    tile = jnp.where(used, first_tile[e] + v - vstart[e], jnp.minimum(tiles_used + v - n_used, nt - 1))
    lo = jnp.where(used, jnp.clip(start[e] - tile * tm, 0, tm), 0)
    hi = jnp.where(used, jnp.clip(end[e] - tile * tm, 0, tm), 0)
    tile = tile.astype(I32)
    e = e.astype(I32)
    first = (tile != jnp.concatenate([jnp.full((1,), -1, I32), tile[:-1]])).astype(I32)
    new_e = (e != jnp.concatenate([jnp.full((1,), -1, I32), e[:-1]])).astype(I32)
    return tile, e, lo.astype(I32), hi.astype(I32), first, new_e


def _combine_kernel(tcur_ref, tnext_ref, sloc_ref, gt_ref, x_ref, g2_ref, y_hbm, o_ref, stage_s, sems, *, n):
    i = pl.program_id(0)
    slot = i % 2

    @pl.when(i == 0)
    def _():
        stage_s[...] = jnp.zeros(stage_s.shape, U32)
        _chunk_copies(tcur_ref, stage_s.at[0], y_hbm, sems.at[0], False)

    @pl.when(i + 1 < n)
    def _():
        _chunk_copies(tnext_ref, stage_s.at[1 - slot], y_hbm, sems.at[1 - slot], False)

    _chunk_waits(tcur_ref, stage_s.at[slot], y_hbm, sems.at[slot])
    rows_lo, rows_hi = _unpack_halves(stage_s[slot])
    lane = lax.broadcasted_iota(I32, (x_ref.shape[0], MOE_STAGE), 1)
    gt = gt_ref[...]
    sloc = sloc_ref[...]
    gmat = jnp.zeros(lane.shape, F32)
    for k in range(TOP_K):
        gmat = gmat + jnp.where(lane == sloc[:, TOP_K + k:TOP_K + k + 1], gt[:, k:k + 1], 0.0)
    g_hi = gmat.astype(BF16)
    g_lo = (gmat - g_hi.astype(F32)).astype(BF16)
    half = rows_lo.shape[1]
    for rows, sl in ((rows_lo, slice(0, half)), (rows_hi, slice(half, 2 * half))):
        f = (jnp.dot(g_hi, rows, preferred_element_type=F32)
             + jnp.dot(g_lo, rows, preferred_element_type=F32))
        o_ref[:, sl] = x_ref[:, sl] + g2_ref[0][:, sl] * f


def _combine(y, tabs, sloc, gates, x, g2, tiles_per_mod):
    t, d = x.shape
    tm = ROUTE_TM
    nt = t // tm
    return pl.pallas_call(
        functools.partial(_combine_kernel, n=nt),
        out_shape=jax.ShapeDtypeStruct((t, d), F32),
        grid=(nt,),
        in_specs=[pl.BlockSpec((1, 1, LANES), lambda i: (i, 0, 0), memory_space=pltpu.SMEM),
                  pl.BlockSpec((1, 1, LANES), lambda i: (jnp.minimum(i + 1, nt - 1), 0, 0),
                               memory_space=pltpu.SMEM),
                  pl.BlockSpec((tm, LANES), lambda i: (i, 0)),
                  pl.BlockSpec((tm, LANES), lambda i: (i, 0)),
                  pl.BlockSpec((tm, d), lambda i: (i, 0)),
                  pl.BlockSpec((1, 1, d), lambda i: (i // tiles_per_mod, 0, 0)),
                  pl.BlockSpec(memory_space=pl.ANY)],
        out_specs=pl.BlockSpec((tm, d), lambda i: (i, 0)),
        scratch_shapes=[pltpu.VMEM((2, MOE_STAGE, d // 2), U32), pltpu.SemaphoreType.DMA((2,))],
        compiler_params=_cparams(("arbitrary",)),
        name="moe_combine",
    )(tabs, tabs, sloc, gates, x, g2, y)


def _moe(streams, layer, g, router_w, router_b, w_gu_all, b_gu, w_down_all, b_down):
    d = streams[0][0].shape[1]
    tm = ROUTE_TM
    rw_f = jnp.zeros((d, LANES), F32).at[:, :N_EXPERTS].set(router_w.astype(F32))
    rw_hi = rw_f.astype(BF16)
    rw = (rw_hi, (rw_f - rw_hi.astype(F32)).astype(BF16))
    rb = jnp.full((1, LANES), NEG_BIG, F32).at[0, :N_EXPERTS].set(router_b.astype(F32))
    counts = jnp.zeros((1, LANES), I32)
    routed = []
    for x_tok, sh, sc, _, tpm in streams:
        h, route_i, gates, sloc_t, n_tile, seg_base, counts = _route(x_tok, g, sh, sc, tpm, rw, rb, counts)
        routed.append((h, route_i, gates, sloc_t, n_tile, seg_base))
    n_tiles = sum(x_tok.shape[0] for x_tok, *_ in streams) // tm
    n_max = n_tiles * tm * TOP_K + n_tiles * N_EXPERTS * (SUBLANES - 1)
    n_max = (n_max + MOE_TM - 1) // MOE_TM * MOE_TM
    cnt = counts[0, :N_EXPERTS]
    total = jnp.sum(cnt)
    base = jnp.cumsum(cnt) - cnt
    tabs = []
    for h, _, _, _, n_tile, seg_base in routed:
        nt = h.shape[0] // tm
        pad = _round_up8(n_tile[:, 0, :N_EXPERTS])
        off = jnp.cumsum(pad, axis=1) - pad
        dst = base[None, :] + seg_base[:, 0, :N_EXPERTS]
        nch = pad // SUBLANES
        tab = jnp.concatenate([nch, dst, off, jnp.sum(nch, axis=1, keepdims=True),
                               jnp.zeros((nt, LANES - 3 * N_EXPERTS - 1), I32)], axis=1)
        tabs.append(tab.astype(I32).reshape(nt, 1, LANES))
    free = n_max - total
    tail = jnp.stack([free // MOE_TM, (free % MOE_TM) // SUBLANES, total]).astype(I32)
    xs = _dispatch([r[0] for r in routed], jnp.concatenate(tabs, axis=0),
                   jnp.concatenate([r[3] for r in routed], axis=0), tail, n_max)
    bg = b_gu[:, None, 0::2].astype(F32)
    bu = b_gu[:, None, 1::2].astype(F32)
    y = _gmm(xs, _visit_tables(cnt, n_max), layer, w_gu_all, bg, bu, w_down_all, b_down[:, None, :].astype(F32))
    return [_combine(y, tab, r[1], r[2], x_tok, g2, tpm)
            for (x_tok, _, _, g2, tpm), r, tab in zip(streams, routed, tabs)]


def kernel(x, c, ctx, c_ctx, w_mod, b_mod, norm1_g, norm2_g, w_in, w_out, lru_conv_w, lru_conv_b, lru_wa, lru_ba, lru_wx, lru_bx, lru_lam, ssd_conv_w, ssd_conv_b, ssd_a_log, ssd_dt_bias, ssd_d, ssd_norm_g, da_q_norm, da_k_norm, da_lam_q, da_lam_k, da_subln_g, router_w, router_b, exp_w_gu, exp_b_gu, exp_w_down, exp_b_down):
    b, s, d = x.shape
    n_ctx = ctx.shape[1]
    depth = w_mod.shape[0]
    cos, sin = _rope_tables(s)
    x_lat, x_ctx = x, ctx
    mod_rows = jnp.zeros((2 * SUBLANES, d), F32).at[:b].set(c).at[b].set(c_ctx)

    for l in range(depth):
        need_ctx = l < depth - 1
        lam_init = 0.8 - 0.6 * math.exp(-0.3 * l)
        mod = _modulation(mod_rows, w_mod[l], b_mod[l])
        m_lat = mod[:b].reshape(b, 1, 6, d)
        m_ctx = jnp.broadcast_to(mod[b].reshape(1, 1, 6, d), (b, 1, 6, d))
        part = lambda m, j: m[:, :, j, :]

        w_in_l = _prep_w_in(w_in[l])
        p_lat = _proj_in(x_lat, norm1_g[l], part(m_lat, 0), part(m_lat, 1), w_in_l)
        p_ctx = _proj_in(x_ctx, norm1_g[l], part(m_ctx, 0), part(m_ctx, 1), w_in_l)

        zl = jnp.zeros((b, LRU_WIDTH), F32)
        lru_args = (lru_wa[l], lru_ba[l], lru_wx[l], lru_bx[l], lru_lam[l])
        xc_ctx = _dwconv(p_ctx["lx"], lru_conv_w[l], lru_conv_b[l], act=False)
        hf_c, hb_c = _lru_scan(xc_ctx, *lru_args, zl, zl)
        xc_lat = _dwconv(p_lat["lx"], lru_conv_w[l], lru_conv_b[l], act=False)
        hf_l, hb_l = _lru_scan(xc_lat, *lru_args, hf_c[:, -1], hb_c[:, 0])

        zs = jnp.zeros((b, SSD_GROUPS, 2 * SSD_HEAD_DIM, SSD_STATE), F32)
        xbc_ctx = _dwconv(p_ctx["xbc"], ssd_conv_w[l], ssd_conv_b[l], act=True)
        yf_c, yb_c, sf_c, sb_c = _ssd_scan(xbc_ctx, p_ctx["dt"], ssd_a_log[l], ssd_dt_bias[l], zs, zs)
        xbc_lat = _dwconv(p_lat["xbc"], ssd_conv_w[l], ssd_conv_b[l], act=True)
        yf_l, yb_l, _, _ = _ssd_scan(xbc_lat, p_lat["dt"], ssd_a_log[l], ssd_dt_bias[l], sf_c, sb_c)

        lq = da_lam_q[l].astype(F32)
        lk = da_lam_k[l].astype(F32)
        lam = jnp.exp(jnp.sum(lq[0] * lk[0])) - jnp.exp(jnp.sum(lq[1] * lk[1])) + lam_init
        kc = _qk_prep(p_ctx["k"], da_k_norm[l], cos, sin, rope=False, scale=1.0, kc=min(ATT_KC, n_ctx))
        kl = _qk_prep(p_lat["k"], da_k_norm[l], cos, sin, rope=True, scale=1.0, kc=min(ATT_KC, s))
        ql = _qk_prep(p_lat["q"], da_q_norm[l], cos, sin, rope=True, scale=DA_SCALE * LOG2E, kc=0)
        da_l = _diff_attn(ql, [(kc, p_ctx["v"]), (kl, p_lat["v"])], lam, da_subln_g[l], lam_init)

        dvec = jnp.repeat(ssd_d[l].astype(F32), SSD_HEAD_DIM).reshape(1, SSD_WIDTH)
        ng = ssd_norm_g[l].astype(F32).reshape(1, SSD_WIDTH)
        w_out_l = w_out[l].astype(BF16)
        if need_ctx:
            qc = _qk_prep(p_ctx["q"], da_q_norm[l], cos, sin, rope=False, scale=DA_SCALE * LOG2E, kc=0)
            da_c = _diff_attn(qc, [(kc, p_ctx["v"])], lam, da_subln_g[l], lam_init)
            x_ctx = _proj_out(x_ctx, p_ctx["lg"], hf_c, hb_c, yf_c, yb_c, xbc_ctx, p_ctx["z"], da_c,
                              dvec, ng, part(m_ctx, 2), w_out_l)
        x_lat = _proj_out(x_lat, p_lat["lg"], hf_l, hb_l, yf_l, yb_l, xbc_lat, p_lat["z"], da_l,
                          dvec, ng, part(m_lat, 2), w_out_l)

        moe_w = (router_w[l], router_b[l], exp_w_gu, exp_b_gu[l], exp_w_down, exp_b_down[l])
        streams = [(x_lat.reshape(-1, d), part(m_lat, 3), part(m_lat, 4), part(m_lat, 5), s // ROUTE_TM)]
        if need_ctx:
            streams.append((x_ctx.reshape(-1, d), part(m_ctx, 3), part(m_ctx, 4), part(m_ctx, 5), n_ctx // ROUTE_TM))
        outs = _moe(streams, l, norm2_g[l], *moe_w)
        x_lat = outs[0].reshape(b, s, d)
        if need_ctx:
            x_ctx = outs[1].reshape(b, n_ctx, d)
    return x_lat
```

```python
import functools
import math

import jax
import jax.numpy as jnp
from jax import lax
from jax.experimental import pallas as pl
from jax.experimental.pallas import tpu as pltpu

F32 = jnp.float32
BF16 = jnp.bfloat16
I32 = jnp.int32

GRID_W = 64
EPS = 1e-6
CONV_K = 4
LRU_HEADS = 4
LRU_HEAD_DIM = 64
LRU_WIDTH = 256
LRU_C = 8.0
SSD_HEADS = 4
SSD_HEAD_DIM = 64
SSD_WIDTH = 256
SSD_GROUPS = 2
SSD_STATE = 128
SSD_CHUNK = 128
SSD_CONV_DIM = 768
DA_HEADS = 4
DA_QK_DIM = 64
DA_V_DIM = 128
DA_QK_WIDTH = 512
DA_WIDTH = 512
DA_SCALE = DA_QK_DIM ** -0.5
ROPE_BASE = 10000.0
N_EXPERTS = 32
TOP_K = 4
SWIGLU_ALPHA = 1.702
SWIGLU_LIMIT = 7.0

LANES = 128
SUBLANES = 8
VMEM_LIMIT = 56 * 1024 * 1024

PROJ_TM = 512
ATT_TQ = 512
ATT_KC = 256
LOG2E = math.log2(math.e)
LRU_TL = 512
SSD_STEP_CHUNKS = 4
ROUTE_TM = 256
MOE_TM = 512
NEG_BIG = -1e30


def _cparams(sem):
    return pltpu.CompilerParams(dimension_semantics=sem, vmem_limit_bytes=VMEM_LIMIT)


def _sigmoid(x):
    return 0.5 * jnp.tanh(0.5 * x) + 0.5


def _silu(x):
    return x * _sigmoid(x)


def _softplus(x):
    return jnp.maximum(x, 0.0) + jnp.log(1.0 + jnp.exp(-jnp.abs(x)))


def _round_up8(n):
    return ((n + (SUBLANES - 1)) >> 3) << 3


def _mod_kernel(c_ref, w_ref, b_ref, o_ref):
    a = _silu(c_ref[...])
    o_ref[...] = jnp.dot(a, w_ref[...], preferred_element_type=F32,
                         precision=lax.Precision.HIGHEST) + b_ref[...]


def _modulation(rows, w, b):
    m, d = rows.shape
    n = w.shape[1]
    tn = 512
    return pl.pallas_call(
        _mod_kernel,
        out_shape=jax.ShapeDtypeStruct((m, n), F32),
        grid=(n // tn,),
        in_specs=[pl.BlockSpec((m, d), lambda j: (0, 0)),
                  pl.BlockSpec((d, tn), lambda j: (0, j)),
                  pl.BlockSpec((1, tn), lambda j: (0, j))],
        out_specs=pl.BlockSpec((m, tn), lambda j: (0, j)),
        compiler_params=_cparams(("arbitrary",)),
        name="modulation",
    )(rows, w, b.reshape(1, n))


def _norm_mod(x, g, sh, sc):
    ms = jnp.mean(x * x, axis=-1, keepdims=True)
    y = x * lax.rsqrt(ms + EPS) * g
    return y * (1.0 + sc) + sh


def _proj_in_kernel(x_ref, g_ref, sh_ref, sc_ref, w_ref, *o_refs, splits):
    h = _norm_mod(x_ref[0], g_ref[...], sh_ref[0], sc_ref[0]).astype(BF16)
    for o_ref, (lo, hi) in zip(o_refs, splits):
        o_ref[0] = jnp.dot(h, w_ref[:, lo:hi], preferred_element_type=F32).astype(o_ref.dtype)


_IN_GROUPS = (("lg", 256, F32), ("lx", 256, F32), ("z", 256, F32), ("xbc", 768, F32),
              ("dt", 128, F32), ("q", 512, F32), ("k", 512, F32), ("v", 512, BF16))


def _prep_w_in(w_in):
    d = w_in.shape[0]
    lo = 2 * LRU_WIDTH + SSD_WIDTH + SSD_CONV_DIM
    hi = lo + 2 * SSD_HEADS
    dt = jnp.concatenate([w_in[:, lo:hi], jnp.zeros((d, LANES - (hi - lo)), w_in.dtype)], axis=1)
    w = jnp.concatenate([w_in[:, :lo], dt, w_in[:, hi:]], axis=1)
    return w.astype(BF16)


def _proj_in(x, g, sh, sc, w):
    b, s, d = x.shape
    tm = min(PROJ_TM, s)
    splits, off = [], 0
    for _, wd, _ in _IN_GROUPS:
        splits.append((off, off + wd))
        off += wd
    out_shape = [jax.ShapeDtypeStruct((b, s, wd), dt) for _, wd, dt in _IN_GROUPS]
    out_specs = [pl.BlockSpec((1, tm, wd), lambda bi, i: (bi, i, 0)) for _, wd, _ in _IN_GROUPS]
    outs = pl.pallas_call(
        functools.partial(_proj_in_kernel, splits=tuple(splits)),
        out_shape=out_shape,
        grid=(b, s // tm),
        in_specs=[pl.BlockSpec((1, tm, d), lambda bi, i: (bi, i, 0)),
                  pl.BlockSpec((1, d), lambda bi, i: (0, 0)),
                  pl.BlockSpec((1, 1, d), lambda bi, i: (bi, 0, 0)),
                  pl.BlockSpec((1, 1, d), lambda bi, i: (bi, 0, 0)),
                  pl.BlockSpec((d, off), lambda bi, i: (0, 0))],
        out_specs=out_specs,
        compiler_params=_cparams(("arbitrary", "arbitrary")),
        name="proj_in",
    )(x, g.reshape(1, d), sh, sc, w)
    return {name: o for (name, _, _), o in zip(_IN_GROUPS, outs)}


def _dwconv_kernel(x_ref, w_ref, b_ref, o_ref, *, act):
    x = x_ref[0]
    s = x.shape[0]
    row = lax.broadcasted_iota(I32, x.shape, 0)
    acc = x * w_ref[2:3, :] + b_ref[...]
    acc = acc + jnp.where(row >= 2, pltpu.roll(x, 2, 0), 0.0) * w_ref[0:1, :]
    acc = acc + jnp.where(row >= 1, pltpu.roll(x, 1, 0), 0.0) * w_ref[1:2, :]
    acc = acc + jnp.where(row < s - 1, pltpu.roll(x, s - 1, 0), 0.0) * w_ref[3:4, :]
    if act:
        acc = _silu(acc)
    o_ref[0] = acc


def _dwconv(x, w, bias, act):
    b, s, c = x.shape
    tc = 256
    return pl.pallas_call(
        functools.partial(_dwconv_kernel, act=act),
        out_shape=jax.ShapeDtypeStruct((b, s, c), F32),
        grid=(b, c // tc),
        in_specs=[pl.BlockSpec((1, s, tc), lambda bi, j: (bi, 0, j)),
                  pl.BlockSpec((CONV_K, tc), lambda bi, j: (0, j)),
                  pl.BlockSpec((1, tc), lambda bi, j: (0, j))],
        out_specs=pl.BlockSpec((1, s, tc), lambda bi, j: (bi, 0, j)),
        compiler_params=_cparams(("arbitrary", "arbitrary")),
        name="dwconv",
    )(x, w, bias.reshape(1, c))


def _lru_coeffs(x, w_ref, b_ref, sp_ref, a_s, u_s):
    g = jnp.dot(x.astype(BF16), w_ref[...], preferred_element_type=F32) + b_ref[...]
    r = _sigmoid(g[:, :LRU_WIDTH])
    i = _sigmoid(g[:, LRU_WIDTH:])
    a = jnp.exp(-LRU_C * r * sp_ref[...])
    a_s[...] = a
    u_s[...] = jnp.sqrt(1.0 - a * a) * (i * x)


def _lru_kernel(xf_ref, xb_ref, wf_ref, wb_ref, bf_ref, bb_ref, spf_ref, spb_ref,
                h0f_ref, h0b_ref, hf_ref, hb_ref, af_s, uf_s, ab_s, ub_s, cf_s, cb_s):
    c = pl.program_id(1)
    tl = xf_ref.shape[1]
    ngroups = tl // SUBLANES
    shape = (SUBLANES, LRU_WIDTH)
    row = lax.broadcasted_iota(I32, shape, 0)

    @pl.when(c == 0)
    def _():
        cf_s[...] = jnp.broadcast_to(h0f_ref[0], shape)
        cb_s[...] = jnp.broadcast_to(h0b_ref[0], shape)

    _lru_coeffs(xf_ref[0], wf_ref, bf_ref, spf_ref, af_s, uf_s)
    _lru_coeffs(xb_ref[0], wb_ref, bb_ref, spb_ref, ab_s, ub_s)

    def group_scan(a, u, carry, reverse):
        for sh in (1, 2, 4):
            keep = (row < SUBLANES - sh) if reverse else (row >= sh)
            amt = SUBLANES - sh if reverse else sh
            a_sh = jnp.where(keep, pltpu.roll(a, amt, 0), 1.0)
            u_sh = jnp.where(keep, pltpu.roll(u, amt, 0), 0.0)
            u = a * u_sh + u
            a = a * a_sh
        return a * carry + u

    def step(gj, carry):
        cf, cb = carry
        slf = pl.ds(pl.multiple_of(gj * SUBLANES, SUBLANES), SUBLANES)
        slb = pl.ds(pl.multiple_of((ngroups - 1 - gj) * SUBLANES, SUBLANES), SUBLANES)
        hf = group_scan(af_s[slf, :], uf_s[slf, :], cf, False)
        hb = group_scan(ab_s[slb, :], ub_s[slb, :], cb, True)
        hf_ref[0, slf, :] = hf
        hb_ref[0, slb, :] = hb
        return (jnp.broadcast_to(hf[SUBLANES - 1:SUBLANES, :], shape),
                jnp.broadcast_to(hb[0:1, :], shape))

    cf, cb = lax.fori_loop(0, ngroups, step, (cf_s[...], cb_s[...]), unroll=2)
    cf_s[...] = cf
    cb_s[...] = cb


def _lru_gate_weights(wa, wx):
    def dense(wh):
        m = jnp.zeros((LRU_WIDTH, LRU_WIDTH), wh.dtype)
        for h in range(LRU_HEADS):
            lo = h * LRU_HEAD_DIM
            m = m.at[lo:lo + LRU_HEAD_DIM, lo:lo + LRU_HEAD_DIM].set(wh[h])
        return m
    return jnp.concatenate([dense(wa), dense(wx)], axis=1).astype(BF16)


def _lru_scan(xc, wa, ba, wx, bx, lam, h0f, h0b):
    b, s, w = xc.shape
    tl = min(LRU_TL, s)
    nc = s // tl
    wf = _lru_gate_weights(wa[0], wx[0])
    wb = _lru_gate_weights(wa[1], wx[1])
    bf = jnp.concatenate([ba[0], bx[0]]).reshape(1, 2 * w)
    bb = jnp.concatenate([ba[1], bx[1]]).reshape(1, 2 * w)
    sp = jax.nn.softplus(-lam.astype(F32))
    const = lambda bi, c: (0, 0)
    hf, hb = pl.pallas_call(
        _lru_kernel,
        out_shape=[jax.ShapeDtypeStruct((b, s, w), F32)] * 2,
        grid=(b, nc),
        in_specs=[pl.BlockSpec((1, tl, w), lambda bi, c: (bi, c, 0)),
                  pl.BlockSpec((1, tl, w), lambda bi, c: (bi, nc - 1 - c, 0)),
                  pl.BlockSpec((w, 2 * w), const), pl.BlockSpec((w, 2 * w), const),
                  pl.BlockSpec((1, 2 * w), const), pl.BlockSpec((1, 2 * w), const),
                  pl.BlockSpec((1, w), const), pl.BlockSpec((1, w), const),
                  pl.BlockSpec((1, 1, w), lambda bi, c: (bi, 0, 0)),
                  pl.BlockSpec((1, 1, w), lambda bi, c: (bi, 0, 0))],
        out_specs=[pl.BlockSpec((1, tl, w), lambda bi, c: (bi, c, 0)),
                   pl.BlockSpec((1, tl, w), lambda bi, c: (bi, nc - 1 - c, 0))],
        scratch_shapes=[pltpu.VMEM((tl, w), F32)] * 4
        + [pltpu.VMEM((SUBLANES, w), F32), pltpu.VMEM((SUBLANES, w), F32)],
        compiler_params=_cparams(("arbitrary", "arbitrary")),
        name="lru_scan",
    )(xc, xc, wf, wb, bf, bb, sp[0:1], sp[1:2], h0f.reshape(b, 1, w), h0b.reshape(b, 1, w))
    return hf, hb


def _ssd_direction(x_ref, dt_ref, row0, avec_ref, dtb_ref, state_s, y_ref, lane0, rev):
    t = SSD_CHUNK
    xbc = x_ref[0, row0:row0 + t, :]
    dtraw = dt_ref[0, row0:row0 + t, :]
    ri = lax.broadcasted_iota(I32, (t, t), 0)
    ci = lax.broadcasted_iota(I32, (t, t), 1)
    causal = (ci >= ri) if rev else (ci <= ri)
    cum_m = jnp.where(causal, 1.0, 0.0).astype(F32)

    dt = _softplus(dtraw + dtb_ref[...])
    a_all = dt * avec_ref[...]
    acs = jnp.dot(cum_m, a_all, preferred_element_type=F32, precision=lax.Precision.HIGHEST)
    acs_t = acs.T
    edge = 0 if rev else t - 1

    xs = xbc[:, :SSD_WIDTH]
    hrow = lax.broadcasted_iota(I32, (t, t), 0)
    for g in range(SSD_GROUPS):
        bm = xbc[:, SSD_WIDTH + g * SSD_STATE:SSD_WIDTH + (g + 1) * SSD_STATE]
        cm = xbc[:, SSD_WIDTH + (SSD_GROUPS + g) * SSD_STATE:SSD_WIDTH + (SSD_GROUPS + g + 1) * SSD_STATE]
        bm_b = bm.astype(BF16)
        cm_b = cm.astype(BF16)
        gmat = lax.dot_general(cm_b, bm_b, (((1,), (1,)), ((), ())), preferred_element_type=F32)
        s_prev = state_s[g]
        off_all = lax.dot_general(cm_b, s_prev.astype(BF16), (((1,), (1,)), ((), ())),
                                  preferred_element_type=F32)
        xd_parts, tot = [], []
        for hl in range(2):
            h = g * 2 + hl
            ln = lane0 + h
            col = jnp.broadcast_to(acs[:, ln:ln + 1], (t, t))
            rowv = acs_t[ln:ln + 1, :]
            lmat = jnp.where(causal, jnp.exp(col - rowv), 0.0)
            xh = xs[:, h * SSD_HEAD_DIM:(h + 1) * SSD_HEAD_DIM] * dt[:, ln:ln + 1]
            ydiag = jnp.dot((gmat * lmat).astype(BF16), xh.astype(BF16), preferred_element_type=F32)
            colh = col[:, :SSD_HEAD_DIM]
            yoff = off_all[:, hl * SSD_HEAD_DIM:(hl + 1) * SSD_HEAD_DIM] * jnp.exp(colh)
            y_ref[0, row0:row0 + t, h * SSD_HEAD_DIM:(h + 1) * SSD_HEAD_DIM] = ydiag + yoff
            total = acs[edge:edge + 1, ln:ln + 1]
            xd_parts.append(xh * jnp.exp(total - colh))
            tot.append(jnp.exp(total))
        xd = jnp.concatenate(xd_parts, axis=1)
        xd_t = xd.T
        st = jnp.dot(xd_t.astype(BF16), bm_b, preferred_element_type=F32)
        fac = jnp.where(hrow < SSD_HEAD_DIM, tot[0], tot[1])
        state_s[g] = fac * s_prev + st


def _ssd_kernel(xf_ref, dtf_ref, xb_ref, dtb_ref, avec_ref, bias_ref, h0f_ref, h0b_ref,
                yf_ref, yb_ref, sf_ref, sb_ref, stf_s, stb_s):
    c = pl.program_id(1)

    @pl.when(c == 0)
    def _():
        stf_s[...] = h0f_ref[0]
        stb_s[...] = h0b_ref[0]

    nsub = xf_ref.shape[1] // SSD_CHUNK
    for j in range(nsub):
        _ssd_direction(xf_ref, dtf_ref, j * SSD_CHUNK, avec_ref, bias_ref, stf_s, yf_ref, 0, False)
        _ssd_direction(xb_ref, dtb_ref, (nsub - 1 - j) * SSD_CHUNK, avec_ref, bias_ref, stb_s, yb_ref,
                       SSD_HEADS, True)
    sf_ref[0] = stf_s[...]
    sb_ref[0] = stb_s[...]


def _ssd_scan(xbc, dt, a_log, dt_bias, h0f, h0b):
    b, s, _ = xbc.shape
    t = min(SSD_STEP_CHUNKS * SSD_CHUNK, s)
    nc = s // t
    avec = jnp.zeros((1, LANES), F32).at[0, :2 * SSD_HEADS].set(-jnp.exp(a_log.astype(F32)).reshape(-1))
    bias = jnp.zeros((1, LANES), F32).at[0, :2 * SSD_HEADS].set(dt_bias.astype(F32).reshape(-1))
    st_shape = (b, SSD_GROUPS, 2 * SSD_HEAD_DIM, SSD_STATE)
    const = lambda bi, c: (0, 0)
    st_spec = pl.BlockSpec((1,) + st_shape[1:], lambda bi, c: (bi, 0, 0, 0))
    yf, yb, sf, sb = pl.pallas_call(
        _ssd_kernel,
        out_shape=[jax.ShapeDtypeStruct((b, s, SSD_WIDTH), F32)] * 2
        + [jax.ShapeDtypeStruct(st_shape, F32)] * 2,
        grid=(b, nc),
        in_specs=[pl.BlockSpec((1, t, SSD_CONV_DIM), lambda bi, c: (bi, c, 0)),
                  pl.BlockSpec((1, t, LANES), lambda bi, c: (bi, c, 0)),
                  pl.BlockSpec((1, t, SSD_CONV_DIM), lambda bi, c: (bi, nc - 1 - c, 0)),
                  pl.BlockSpec((1, t, LANES), lambda bi, c: (bi, nc - 1 - c, 0)),
                  pl.BlockSpec((1, LANES), const), pl.BlockSpec((1, LANES), const),
                  st_spec, st_spec],
        out_specs=[pl.BlockSpec((1, t, SSD_WIDTH), lambda bi, c: (bi, c, 0)),
                   pl.BlockSpec((1, t, SSD_WIDTH), lambda bi, c: (bi, nc - 1 - c, 0)),
                   st_spec, st_spec],
        scratch_shapes=[pltpu.VMEM(st_shape[1:], F32), pltpu.VMEM(st_shape[1:], F32)],
        compiler_params=_cparams(("arbitrary", "arbitrary")),
        name="ssd_scan",
    )(xbc, dt, xbc, dt, avec, bias, h0f, h0b)
    return yf, yb, sf, sb


def _qk_prep_kernel(x_ref, g_ref, cos_ref, sin_ref, seg_ref, o_ref, *, rope, scale, kc):
    x = x_ref[0]
    sq = x * x
    hi = sq.astype(BF16)
    lo = (sq - hi.astype(F32)).astype(BF16)
    ssum = (jnp.dot(hi, seg_ref[...], preferred_element_type=F32)
            + jnp.dot(lo, seg_ref[...], preferred_element_type=F32))
    y = x * lax.rsqrt(ssum * (1.0 / DA_QK_DIM) + EPS) * g_ref[...]
    if rope:
        n = x.shape[1]
        lane = lax.broadcasted_iota(I32, x.shape, 1)
        first = (lane & (DA_QK_DIM - 1)) < DA_QK_DIM // 2
        partner = jnp.where(first, pltpu.roll(y, n - DA_QK_DIM // 2, 1), pltpu.roll(y, DA_QK_DIM // 2, 1))
        cosv = jnp.concatenate([cos_ref[...]] * (n // LANES), axis=1)
        sinv = jnp.concatenate([sin_ref[...]] * (n // LANES), axis=1)
        y = y * cosv + partner * sinv
    y = y * scale
    if kc:
        for h in range(DA_HEADS):
            head = y[:, h * 2 * DA_QK_DIM:(h + 1) * 2 * DA_QK_DIM]
            for cc in range(y.shape[0] // kc):
                t = head[cc * kc:(cc + 1) * kc, :].T.astype(BF16)
                o_ref[0, 2 * h, cc] = t[:DA_QK_DIM]
                o_ref[0, 2 * h + 1, cc] = t[DA_QK_DIM:]
    else:
        yb = y.astype(BF16)
        for j in range(2 * DA_HEADS):
            o_ref[0, j] = yb[:, j * DA_QK_DIM:(j + 1) * DA_QK_DIM]


def _rope_tables(s):
    pos = jnp.arange(s, dtype=I32)
    r = (pos // GRID_W).astype(F32)
    col = (pos % GRID_W).astype(F32)
    per_axis = DA_QK_DIM // 4
    inv = ROPE_BASE ** (-jnp.arange(per_axis, dtype=F32) / per_axis)
    ang = jnp.concatenate([r[:, None] * inv, col[:, None] * inv], axis=-1)
    cos, sin = jnp.cos(ang), jnp.sin(ang)
    cos64 = jnp.concatenate([cos, cos], axis=-1)
    sin64 = jnp.concatenate([-sin, sin], axis=-1)
    return jnp.concatenate([cos64, cos64], axis=-1), jnp.concatenate([sin64, sin64], axis=-1)


def _qk_prep(x, g, cos, sin, rope, scale, kc):
    b, s, n = x.shape
    ts = min(512, s)
    gfull = jnp.tile(g.astype(F32), n // DA_QK_DIM).reshape(1, n)
    idx = jnp.arange(n) // DA_QK_DIM
    seg = (idx[:, None] == idx[None, :]).astype(BF16)
    if kc:
        out_shape = jax.ShapeDtypeStruct((b, 2 * DA_HEADS, s // kc, DA_QK_DIM, kc), BF16)
        out_spec = pl.BlockSpec((1, 2 * DA_HEADS, ts // kc, DA_QK_DIM, kc), lambda bi, i: (bi, 0, i, 0, 0))
    else:
        out_shape = jax.ShapeDtypeStruct((b, 2 * DA_HEADS, s, DA_QK_DIM), BF16)
        out_spec = pl.BlockSpec((1, 2 * DA_HEADS, ts, DA_QK_DIM), lambda bi, i: (bi, 0, i, 0))
    return pl.pallas_call(
        functools.partial(_qk_prep_kernel, rope=rope, scale=scale, kc=kc),
        out_shape=out_shape,
        grid=(b, s // ts),
        in_specs=[pl.BlockSpec((1, ts, n), lambda bi, i: (bi, i, 0)),
                  pl.BlockSpec((1, n), lambda bi, i: (0, 0)),
                  pl.BlockSpec((ts, LANES), lambda bi, i: (i, 0)),
                  pl.BlockSpec((ts, LANES), lambda bi, i: (i, 0)),
                  pl.BlockSpec((n, n), lambda bi, i: (0, 0))],
        out_specs=out_spec,
        compiler_params=_cparams(("arbitrary", "arbitrary")),
        name="qk_prep",
    )(x, gfull, cos, sin, seg)


def _attn_kernel(lam_ref, q_ref, *refs, nseg, out_scale):
    lam = lam_ref[0, 0]
    segs = [(refs[2 * j], refs[2 * j + 1]) for j in range(nseg)]
    g_ref, o_ref = refs[2 * nseg], refs[2 * nseg + 1]
    tq = q_ref.shape[2]
    m = [jnp.full((tq, LANES), NEG_BIG, F32)] * 2
    l = [jnp.zeros((tq, LANES), F32)] * 2
    acc = [jnp.zeros((tq, DA_V_DIM), F32)] * 2

    for k_ref, v_ref in segs:
        nchunk, kc = k_ref.shape[2], k_ref.shape[4]
        for c in range(nchunk):
            v = v_ref[0, c * kc:(c + 1) * kc, :]
            probs, alphas = [], []
            for sub in range(2):
                s = jnp.dot(q_ref[0, sub], k_ref[0, sub, c], preferred_element_type=F32)
                mx = s[:, :LANES]
                for j in range(1, kc // LANES):
                    mx = jnp.maximum(mx, s[:, j * LANES:(j + 1) * LANES])
                m_new = jnp.maximum(m[sub], jnp.max(mx, axis=-1, keepdims=True))
                alpha = jnp.exp2(m[sub] - m_new)
                p = jnp.exp2(s - jnp.concatenate([m_new] * (kc // LANES), axis=1))
                lsum = alpha * l[sub]
                for j in range(kc // LANES):
                    lsum = lsum + p[:, j * LANES:(j + 1) * LANES]
                l[sub] = lsum
                m[sub] = m_new
                probs.append(p.astype(BF16))
                alphas.append(alpha)
            pv = jnp.dot(jnp.concatenate(probs, axis=0), v, preferred_element_type=F32)
            for sub in range(2):
                acc[sub] = alphas[sub] * acc[sub] + pv[sub * tq:(sub + 1) * tq]

    l1 = jnp.sum(l[0], axis=-1, keepdims=True)
    l2 = jnp.sum(l[1], axis=-1, keepdims=True)
    o = acc[0] * (1.0 / l1) - lam * (acc[1] * (1.0 / l2))
    ms = jnp.mean(o * o, axis=-1, keepdims=True)
    o_ref[0] = o * lax.rsqrt(ms + EPS) * g_ref[...] * out_scale


def _diff_attn(qp, segments, lam, g, lam_init):
    b, _, s, d = qp.shape
    tq = min(ATT_TQ, s)
    seg_specs, seg_args = [], []
    for kt, v in segments:
        nchunk, kc = kt.shape[2], kt.shape[4]
        seg_specs += [pl.BlockSpec((1, 2, nchunk, d, kc), lambda bi, h, i: (bi, h, 0, 0, 0)),
                      pl.BlockSpec((1, nchunk * kc, DA_V_DIM), lambda bi, h, i: (bi, 0, h))]
        seg_args += [kt, v]
    return pl.pallas_call(
        functools.partial(_attn_kernel, nseg=len(segments), out_scale=1.0 - lam_init),
        out_shape=jax.ShapeDtypeStruct((b, s, DA_WIDTH), F32),
        grid=(b, DA_HEADS, s // tq),
        in_specs=[pl.BlockSpec(memory_space=pltpu.SMEM),
                  pl.BlockSpec((1, 2, tq, d), lambda bi, h, i: (bi, h, i, 0))] + seg_specs
        + [pl.BlockSpec((1, DA_V_DIM), lambda bi, h, i: (0, 0))],
        out_specs=pl.BlockSpec((1, tq, DA_V_DIM), lambda bi, h, i: (bi, i, h)),
        compiler_params=_cparams(("arbitrary", "arbitrary", "arbitrary")),
        name="diff_attn",
    )(lam.reshape(1, 1).astype(F32), qp, *seg_args, g.reshape(1, DA_V_DIM).astype(F32))


def _gelu_tanh(x):
    return 0.5 * x * (1.0 + jnp.tanh(math.sqrt(2.0 / math.pi) * (x + 0.044715 * x * x * x)))


def _proj_out_kernel(x_ref, lg_ref, hf_ref, hb_ref, yf_ref, yb_ref, xs_ref, z_ref, da_ref,
                     dvec_ref, ng_ref, gate_ref, w_ref, o_ref):
    lru = _gelu_tanh(lg_ref[0]) * (hf_ref[0] + hb_ref[0])
    y = yf_ref[0] + yb_ref[0] + dvec_ref[...] * xs_ref[0]
    y = y * _silu(z_ref[0])
    ssd = y * lax.rsqrt(jnp.mean(y * y, axis=-1, keepdims=True) + EPS) * ng_ref[...]
    o = jnp.dot(lru.astype(BF16), w_ref[0:LRU_WIDTH, :], preferred_element_type=F32)
    o = o + jnp.dot(ssd.astype(BF16), w_ref[LRU_WIDTH:LRU_WIDTH + SSD_WIDTH, :], preferred_element_type=F32)
    o = o + jnp.dot(da_ref[0].astype(BF16), w_ref[LRU_WIDTH + SSD_WIDTH:, :], preferred_element_type=F32)
    o_ref[0] = x_ref[0] + gate_ref[0] * o


def _proj_out(x, lg, hf, hb, yf, yb, xbc, z, da, dvec, ng, gate, w):
    b, s, d = x.shape
    tm = min(PROJ_TM, s)
    tok = lambda wd: pl.BlockSpec((1, tm, wd), lambda bi, i: (bi, i, 0))
    const2 = lambda bi, i: (0, 0)
    return pl.pallas_call(
        _proj_out_kernel,
        out_shape=jax.ShapeDtypeStruct((b, s, d), F32),
        grid=(b, s // tm),
        in_specs=[tok(d), tok(LRU_WIDTH), tok(LRU_WIDTH), tok(LRU_WIDTH), tok(SSD_WIDTH),
                  tok(SSD_WIDTH), tok(SSD_WIDTH), tok(SSD_WIDTH), tok(DA_WIDTH),
                  pl.BlockSpec((1, SSD_WIDTH), const2), pl.BlockSpec((1, SSD_WIDTH), const2),
                  pl.BlockSpec((1, 1, d), lambda bi, i: (bi, 0, 0)),
                  pl.BlockSpec(w.shape, const2)],
        out_specs=tok(d),
        compiler_params=_cparams(("arbitrary", "arbitrary")),
        name="proj_out",
    )(x, lg, hf, hb, yf, yb, xbc, z, da, dvec, ng, gate, w)


def _route_kernel(x_ref, g_ref, sh_ref, sc_ref, rwh_ref, rwl_ref, rb_ref, c0_ref, h_ref, ri_ref, gt_ref,
                  slt_ref, n_ref, base_ref, cnt_ref, carry_s):
    i = pl.program_id(0)
    tm = x_ref.shape[0]

    @pl.when(i == 0)
    def _():
        carry_s[...] = c0_ref[...]

    h = _norm_mod(x_ref[...], g_ref[...], sh_ref[0], sc_ref[0])
    h_hi = h.astype(BF16)
    h_ref[...] = h_hi
    h_lo = (h - h_hi.astype(F32)).astype(BF16)
    logits = (jnp.dot(h_hi, rwh_ref[...], preferred_element_type=F32)
              + jnp.dot(h_hi, rwl_ref[...], preferred_element_type=F32)
              + jnp.dot(h_lo, rwh_ref[...], preferred_element_type=F32)) + rb_ref[...]
    lane = lax.broadcasted_iota(I32, logits.shape, 1)
    lane_f = lane.astype(F32)
    vals, idxs = [], []
    cur = logits
    for _ in range(TOP_K):
        m = jnp.max(cur, axis=-1, keepdims=True)
        idx = jnp.min(jnp.where(cur >= m, lane_f, float(LANES)), axis=-1, keepdims=True).astype(I32)
        vals.append(m)
        idxs.append(idx)
        cur = jnp.where(lane == idx, 2.0 * NEG_BIG, cur)
    es = [jnp.exp(v - vals[0]) for v in vals]
    inv = 1.0 / (es[0] + es[1] + es[2] + es[3])
    onehot = jnp.zeros(logits.shape, F32)
    for idx in idxs:
        onehot = onehot + jnp.where(lane == idx, 1.0, 0.0)
    ri = lax.broadcasted_iota(I32, (tm, tm), 0)
    ci = lax.broadcasted_iota(I32, (tm, tm), 1)
    strict = jnp.where(ci < ri, 1.0, 0.0).astype(BF16)
    before = jnp.dot(strict, onehot.astype(BF16), preferred_element_type=F32)
    n_tile = jnp.sum(onehot, axis=0, keepdims=True).astype(I32)
    pad_tile = _round_up8(n_tile)
    li = lax.broadcasted_iota(I32, (LANES, LANES), 0)
    lj = lax.broadcasted_iota(I32, (LANES, LANES), 1)
    below = jnp.where(li < lj, 1.0, 0.0).astype(BF16)
    pad_rows = jnp.broadcast_to(pad_tile.astype(F32), (SUBLANES, LANES)).astype(BF16)
    off = jnp.dot(pad_rows, below, preferred_element_type=F32)[0:1, :]
    place = before + off
    ri_out = jnp.zeros(logits.shape, I32)
    gt_out = jnp.zeros(logits.shape, F32)
    sl_f = jnp.full(logits.shape, -1.0, F32)
    for k in range(TOP_K):
        sloc = jnp.sum(jnp.where(lane == idxs[k], place, 0.0), axis=-1, keepdims=True)
        ri_out = ri_out + jnp.where(lane == k, idxs[k], 0) + jnp.where(lane == TOP_K + k, sloc.astype(I32), 0)
        sl_f = jnp.where(lane == k, sloc, sl_f)
        gt_out = gt_out + jnp.where(lane == k, es[k] * inv, 0.0)
    ri_ref[...] = ri_out
    gt_ref[...] = gt_out
    slt_ref[0] = sl_f.T[0:SUBLANES, :].astype(I32)
    n_ref[0] = n_tile
    base_ref[0] = carry_s[...]
    carry_s[...] = carry_s[...] + pad_tile
    cnt_ref[...] = carry_s[...]


def _route(x, g, sh, sc, tiles_per_mod, rw, rb, counts0):
    t, d = x.shape
    tm = ROUTE_TM
    tile = lambda wd: pl.BlockSpec((tm, wd), lambda i: (i, 0))
    const = lambda i: (0, 0)
    mod = pl.BlockSpec((1, 1, d), lambda i: (i // tiles_per_mod, 0, 0))
    nt = t // tm
    per_tile = pl.BlockSpec((1, 1, LANES), lambda i: (i, 0, 0))
    return pl.pallas_call(
        _route_kernel,
        out_shape=[jax.ShapeDtypeStruct((t, d), BF16), jax.ShapeDtypeStruct((t, LANES), I32),
                   jax.ShapeDtypeStruct((t, LANES), F32), jax.ShapeDtypeStruct((nt, SUBLANES, tm), I32),
                   jax.ShapeDtypeStruct((nt, 1, LANES), I32),
                   jax.ShapeDtypeStruct((nt, 1, LANES), I32), jax.ShapeDtypeStruct((1, LANES), I32)],
        grid=(nt,),
        in_specs=[tile(d), pl.BlockSpec((1, d), const), mod, mod,
                  pl.BlockSpec((d, LANES), const), pl.BlockSpec((d, LANES), const),
                  pl.BlockSpec((1, LANES), const), pl.BlockSpec((1, LANES), const)],
        out_specs=[tile(d), tile(LANES), tile(LANES), pl.BlockSpec((1, SUBLANES, tm), lambda i: (i, 0, 0)),
                   per_tile, per_tile, pl.BlockSpec((1, LANES), const)],
        scratch_shapes=[pltpu.VMEM((1, LANES), I32)],
        compiler_params=_cparams(("arbitrary",)),
        name="moe_route",
    )(x, g.reshape(1, d), sh, sc, rw[0], rw[1], rb, counts0)


MOE_STAGE = -(-(ROUTE_TM * TOP_K + N_EXPERTS * (SUBLANES - 1)) // 256) * 256
_TAB_N, _TAB_DST, _TAB_OFF, _TAB_TOTAL = 0, N_EXPERTS, 2 * N_EXPERTS, 3 * N_EXPERTS


U32 = jnp.uint32
_HI16 = 0xFFFF0000


def _pack_halves(a, is_bf16=False):
    n = a.shape[1] // 2
    lo, hi = a[:, :n], a[:, n:]
    if not is_bf16:
        lo, hi = lo.astype(BF16).astype(F32), hi.astype(BF16).astype(F32)
    return (pltpu.bitcast(lo, U32) >> 16) | (pltpu.bitcast(hi, U32) & U32(_HI16))


def _unpack_halves(u):
    lo = pltpu.bitcast(u << 16, F32).astype(BF16)
    hi = pltpu.bitcast(u & U32(_HI16), F32).astype(BF16)
    return lo, hi


_BIG_CHUNKS = 4


def _chunk_copies(tab_ref, stage_ref, rows_hbm, sem, to_hbm):
    def copy(off, dst, rows):
        s = stage_ref.at[pl.ds(pl.multiple_of(off, SUBLANES), rows), :]
        r = rows_hbm.at[pl.ds(pl.multiple_of(dst, SUBLANES), rows), :]
        if to_hbm:
            pltpu.make_async_copy(s, r, sem).start()
        else:
            pltpu.make_async_copy(r, s, sem).start()

    def expert(e, carry):
        n = tab_ref[0, 0, _TAB_N + e]
        dst = tab_ref[0, 0, _TAB_DST + e]
        off = tab_ref[0, 0, _TAB_OFF + e]
        big_rows = _BIG_CHUNKS * SUBLANES
        n_big = n // _BIG_CHUNKS

        def big(j, c):
            copy(off + j * big_rows, dst + j * big_rows, big_rows)
            return c

        def small(j, c):
            copy(off + (n_big * _BIG_CHUNKS + j) * SUBLANES, dst + (n_big * _BIG_CHUNKS + j) * SUBLANES, SUBLANES)
            return c

        lax.fori_loop(0, n_big, big, 0)
        lax.fori_loop(0, n - n_big * _BIG_CHUNKS, small, 0)
        return carry
    lax.fori_loop(0, N_EXPERTS, expert, 0)


def _chunk_waits(tab_ref, stage_ref, rows_hbm, sem):
    rows = tab_ref[0, 0, _TAB_TOTAL] * SUBLANES
    pltpu.make_async_copy(rows_hbm.at[pl.ds(0, rows), :], stage_ref.at[pl.ds(0, rows), :], sem).wait()


def _dispatch_kernel(tab_ref, tprev_ref, gap_ref, tail_ref, sloc_ref, *refs, tiles):
    h_refs = refs[:len(tiles)]
    xs_out, stage_s, zero_s, sems = refs[len(tiles):]
    i = pl.program_id(0)
    n_steps = sum(tiles)
    slot = i % 2
    row = lax.broadcasted_iota(I32, (MOE_STAGE, ROUTE_TM), 0)
    perm = jnp.zeros((MOE_STAGE, ROUTE_TM), F32)
    for k in range(TOP_K):
        perm = perm + jnp.where(row == sloc_ref[0, k:k + 1, :], 1.0, 0.0)
    perm = perm.astype(BF16)
    off = 0
    for h_ref, nt in zip(h_refs, tiles):
        @pl.when((i >= off) & (i < off + nt))
        def _(h_ref=h_ref):
            stage_s[slot] = _pack_halves(jnp.dot(perm, h_ref[...], preferred_element_type=F32), is_bf16=True)
        off += nt
    _chunk_copies(tab_ref, stage_s.at[slot], xs_out, sems.at[slot], True)

    @pl.when(i > 0)
    def _():
        _chunk_waits(tprev_ref, stage_s.at[1 - slot], xs_out, sems.at[1 - slot])

    @pl.when(i == n_steps - 1)
    def _():
        _chunk_waits(tab_ref, stage_s.at[slot], xs_out, sems.at[slot])
        sem = sems.at[slot]
        zero_s[...] = jnp.zeros(zero_s.shape, U32)
        zr = zero_s.shape[0]
        n_big, n_small, start = tail_ref[0], tail_ref[1], tail_ref[2]

        def big(j, c):
            pltpu.make_async_copy(zero_s, xs_out.at[pl.ds(pl.multiple_of(start + j * zr, SUBLANES), zr), :], sem).start()
            return c

        def small(j, c):
            r0 = pl.multiple_of(start + n_big * zr + j * SUBLANES, SUBLANES)
            pltpu.make_async_copy(zero_s.at[pl.ds(0, SUBLANES), :], xs_out.at[pl.ds(r0, SUBLANES), :], sem).start()
            return c

        def big_wait(j, c):
            pltpu.make_async_copy(zero_s, xs_out.at[pl.ds(0, zr), :], sem).wait()
            return c

        def small_wait(j, c):
            pltpu.make_async_copy(zero_s.at[pl.ds(0, SUBLANES), :], xs_out.at[pl.ds(0, SUBLANES), :], sem).wait()
            return c

        lax.fori_loop(0, n_big, big, 0)
        lax.fori_loop(0, n_small, small, 0)
        lax.fori_loop(0, n_big, big_wait, 0)
        lax.fori_loop(0, n_small, small_wait, 0)

        _chunk_copies(gap_ref, zero_s, xs_out, sem, True)

        def gap_wait(e, c):
            rows = gap_ref[0, 0, _TAB_N + e] * SUBLANES

            @pl.when(rows > 0)
            def _():
                pltpu.make_async_copy(zero_s.at[pl.ds(0, rows), :], xs_out.at[pl.ds(0, rows), :], sem).wait()
            return c
        lax.fori_loop(0, N_EXPERTS, gap_wait, 0)


def _dispatch(hs, tabs, slocs, gaps, tail, n_max):
    d = hs[0].shape[1]
    tm = ROUTE_TM
    tiles = tuple(h.shape[0] // tm for h in hs)
    specs, off = [], 0
    for nt in tiles:
        specs.append(pl.BlockSpec((tm, d), lambda i, off=off, nt=nt: (jnp.clip(i - off, 0, nt - 1), 0)))
        off += nt
    return pl.pallas_call(
        functools.partial(_dispatch_kernel, tiles=tiles),
        out_shape=jax.ShapeDtypeStruct((n_max, d // 2), U32),
        grid=(sum(tiles),),
        in_specs=[pl.BlockSpec((1, 1, LANES), lambda i: (i, 0, 0), memory_space=pltpu.SMEM),
                  pl.BlockSpec((1, 1, LANES), lambda i: (jnp.maximum(i - 1, 0), 0, 0), memory_space=pltpu.SMEM),
                  pl.BlockSpec((1, 1, LANES), lambda i: (0, 0, 0), memory_space=pltpu.SMEM),
                  pl.BlockSpec(memory_space=pltpu.SMEM),
                  pl.BlockSpec((1, SUBLANES, tm), lambda i: (i, 0, 0))] + specs,
        out_specs=pl.BlockSpec(memory_space=pl.ANY),
        scratch_shapes=[pltpu.VMEM((2, MOE_STAGE, d // 2), U32), pltpu.VMEM((MOE_TM, d // 2), U32),
                        pltpu.SemaphoreType.DMA((2,))],
        compiler_params=_cparams(("arbitrary",)),
        name="moe_dispatch",
    )(tabs, tabs, gaps, tail, slocs, *hs)


_SPLIT_W = 2 * LANES


def _gmm_kernel(te_ref, live_ref, newe_ref, xs_ref, wgu_ref, perm_ref, bg_ref, bu_ref,
                wdf_ref, bd_ref, y_ref, wg_ref, wu_ref, wd_ref):
    del te_ref
    v = pl.program_id(0)
    live = live_ref[v] > 0

    @pl.when((newe_ref[v] > 0) & live)
    def _():
        for j in range(wgu_ref.shape[1] // _SPLIT_W):
            r = jnp.dot(wgu_ref[:, j * _SPLIT_W:(j + 1) * _SPLIT_W].astype(BF16), perm_ref[...],
                        preferred_element_type=F32)
            wg_ref[:, j * LANES:(j + 1) * LANES] = r[:, :LANES].astype(BF16)
            wu_ref[:, j * LANES:(j + 1) * LANES] = r[:, LANES:].astype(BF16)
        wd_ref[...] = wdf_ref[...].astype(BF16)

    @pl.when(live)
    def _():
        x_lo, x_hi = _unpack_halves(xs_ref[...])
        half = x_lo.shape[1]
        gate = (jnp.dot(x_lo, wg_ref[:half, :], preferred_element_type=F32)
                + jnp.dot(x_hi, wg_ref[half:, :], preferred_element_type=F32)) + bg_ref[...]
        up = (jnp.dot(x_lo, wu_ref[:half, :], preferred_element_type=F32)
              + jnp.dot(x_hi, wu_ref[half:, :], preferred_element_type=F32)) + bu_ref[...]
        gate = jnp.minimum(gate, SWIGLU_LIMIT)
        up = jnp.clip(up, -SWIGLU_LIMIT, SWIGLU_LIMIT)
        glu = gate * _sigmoid(SWIGLU_ALPHA * gate)
        act = ((up + 1.0) * glu).astype(BF16)
        y_ref[...] = _pack_halves(jnp.dot(act, wd_ref[...], preferred_element_type=F32) + bd_ref[...])

    @pl.when(jnp.logical_not(live))
    def _():
        y_ref[...] = jnp.zeros(y_ref.shape, U32)


def _gmm(xs, tiles, layer, w_gu, bg, bu, w_down, bd):
    n_rows = xs.shape[0]
    _, _, d, de2 = w_gu.shape
    de = de2 // 2
    tm = MOE_TM
    nv = n_rows // tm
    src = jnp.arange(_SPLIT_W)
    dst = jnp.where(src % 2 == 0, src // 2, LANES + src // 2)
    perm = (dst[:, None] == jnp.arange(_SPLIT_W)[None, :]).astype(BF16)
    ex = lambda shape: pl.BlockSpec((None,) + shape, lambda v, te, lv, ne: (te[v], 0, 0))
    lay = lambda shape: pl.BlockSpec((None, None) + shape, lambda v, te, lv, ne: (layer, te[v], 0, 0))
    rows = pl.BlockSpec((tm, d // 2), lambda v, te, lv, ne: (v, 0))
    grid_spec = pltpu.PrefetchScalarGridSpec(
        num_scalar_prefetch=3,
        grid=(nv,),
        in_specs=[rows, lay((d, de2)), pl.BlockSpec((_SPLIT_W, _SPLIT_W), lambda v, *_: (0, 0)),
                  ex((1, de)), ex((1, de)), lay((de, d)), ex((1, d))],
        out_specs=rows,
        scratch_shapes=[pltpu.VMEM((d, de), BF16), pltpu.VMEM((d, de), BF16), pltpu.VMEM((de, d), BF16)],
    )
    return pl.pallas_call(
        _gmm_kernel,
        out_shape=jax.ShapeDtypeStruct((n_rows, d // 2), U32),
        grid_spec=grid_spec,
        compiler_params=_cparams(("arbitrary",)),
        name="moe_experts",
    )(*tiles, xs, w_gu, perm, bg, bu, w_down, bd)


def _tile_tables(cap, n_rows):
    tm = MOE_TM
    end = jnp.cumsum(cap)
    start = jnp.arange(n_rows // tm, dtype=I32) * tm
    e = jnp.minimum(jnp.sum((end[None, :] <= start[:, None]).astype(I32), axis=1), N_EXPERTS - 1).astype(I32)
    live = (start < end[N_EXPERTS - 1]).astype(I32)
    new_e = (e != jnp.concatenate([jnp.full((1,), -1, I32), e[:-1]])).astype(I32)
    return e, live, new_e


def _combine_kernel(tcur_ref, tnext_ref, sloc_ref, gt_ref, x_ref, g2_ref, y_hbm, o_ref, stage_s, sems, *, n):
    i = pl.program_id(0)
    slot = i % 2

    @pl.when(i == 0)
    def _():
        stage_s[...] = jnp.zeros(stage_s.shape, U32)
        _chunk_copies(tcur_ref, stage_s.at[0], y_hbm, sems.at[0], False)

    @pl.when(i + 1 < n)
    def _():
        _chunk_copies(tnext_ref, stage_s.at[1 - slot], y_hbm, sems.at[1 - slot], False)

    _chunk_waits(tcur_ref, stage_s.at[slot], y_hbm, sems.at[slot])
    rows_lo, rows_hi = _unpack_halves(stage_s[slot])
    lane = lax.broadcasted_iota(I32, (x_ref.shape[0], MOE_STAGE), 1)
    gt = gt_ref[...]
    sloc = sloc_ref[...]
    gmat = jnp.zeros(lane.shape, F32)
    for k in range(TOP_K):
        gmat = gmat + jnp.where(lane == sloc[:, TOP_K + k:TOP_K + k + 1], gt[:, k:k + 1], 0.0)
    g_hi = gmat.astype(BF16)
    g_lo = (gmat - g_hi.astype(F32)).astype(BF16)
    half = rows_lo.shape[1]
    for rows, sl in ((rows_lo, slice(0, half)), (rows_hi, slice(half, 2 * half))):
        f = (jnp.dot(g_hi, rows, preferred_element_type=F32)
             + jnp.dot(g_lo, rows, preferred_element_type=F32))
        o_ref[:, sl] = x_ref[:, sl] + g2_ref[0][:, sl] * f


def _combine(y, tabs, sloc, gates, x, g2, tiles_per_mod):
    t, d = x.shape
    tm = ROUTE_TM
    nt = t // tm
    return pl.pallas_call(
        functools.partial(_combine_kernel, n=nt),
        out_shape=jax.ShapeDtypeStruct((t, d), F32),
        grid=(nt,),
        in_specs=[pl.BlockSpec((1, 1, LANES), lambda i: (i, 0, 0), memory_space=pltpu.SMEM),
                  pl.BlockSpec((1, 1, LANES), lambda i: (jnp.minimum(i + 1, nt - 1), 0, 0),
                               memory_space=pltpu.SMEM),
                  pl.BlockSpec((tm, LANES), lambda i: (i, 0)),
                  pl.BlockSpec((tm, LANES), lambda i: (i, 0)),
                  pl.BlockSpec((tm, d), lambda i: (i, 0)),
                  pl.BlockSpec((1, 1, d), lambda i: (i // tiles_per_mod, 0, 0)),
                  pl.BlockSpec(memory_space=pl.ANY)],
        out_specs=pl.BlockSpec((tm, d), lambda i: (i, 0)),
        scratch_shapes=[pltpu.VMEM((2, MOE_STAGE, d // 2), U32), pltpu.SemaphoreType.DMA((2,))],
        compiler_params=_cparams(("arbitrary",)),
        name="moe_combine",
    )(tabs, tabs, sloc, gates, x, g2, y)


def _moe(streams, layer, g, router_w, router_b, w_gu_all, b_gu, w_down_all, b_down):
    d = streams[0][0].shape[1]
    tm = ROUTE_TM
    rw_f = jnp.zeros((d, LANES), F32).at[:, :N_EXPERTS].set(router_w.astype(F32))
    rw_hi = rw_f.astype(BF16)
    rw = (rw_hi, (rw_f - rw_hi.astype(F32)).astype(BF16))
    rb = jnp.full((1, LANES), NEG_BIG, F32).at[0, :N_EXPERTS].set(router_b.astype(F32))
    counts = jnp.zeros((1, LANES), I32)
    routed = []
    for x_tok, sh, sc, _, tpm in streams:
        h, route_i, gates, sloc_t, n_tile, seg_base, counts = _route(x_tok, g, sh, sc, tpm, rw, rb, counts)
        routed.append((h, route_i, gates, sloc_t, n_tile, seg_base))
    n_tiles = sum(x_tok.shape[0] for x_tok, *_ in streams) // tm
    n_max = n_tiles * tm * TOP_K + n_tiles * N_EXPERTS * (SUBLANES - 1) + N_EXPERTS * (MOE_TM - SUBLANES)
    n_max = (n_max + MOE_TM - 1) // MOE_TM * MOE_TM
    cnt = counts[0, :N_EXPERTS]
    cap = (cnt + MOE_TM - 1) // MOE_TM * MOE_TM
    total = jnp.sum(cap)
    base = jnp.cumsum(cap) - cap
    gap = jnp.concatenate([(cap - cnt) // SUBLANES, base + cnt, jnp.zeros((LANES - 2 * N_EXPERTS,), I32)])
    tabs = []
    for h, _, _, _, n_tile, seg_base in routed:
        nt = h.shape[0] // tm
        pad = _round_up8(n_tile[:, 0, :N_EXPERTS])
        off = jnp.cumsum(pad, axis=1) - pad
        dst = base[None, :] + seg_base[:, 0, :N_EXPERTS]
        nch = pad // SUBLANES
        tab = jnp.concatenate([nch, dst, off, jnp.sum(nch, axis=1, keepdims=True),
                               jnp.zeros((nt, LANES - 3 * N_EXPERTS - 1), I32)], axis=1)
        tabs.append(tab.astype(I32).reshape(nt, 1, LANES))
    free = n_max - total
    tail = jnp.stack([free // MOE_TM, (free % MOE_TM) // SUBLANES, total]).astype(I32)
    xs = _dispatch([r[0] for r in routed], jnp.concatenate(tabs, axis=0),
                   jnp.concatenate([r[3] for r in routed], axis=0), gap.astype(I32).reshape(1, 1, LANES),
                   tail, n_max)
    bg = b_gu[:, None, 0::2].astype(F32)
    bu = b_gu[:, None, 1::2].astype(F32)
    y = _gmm(xs, _tile_tables(cap, n_max), layer, w_gu_all, bg, bu, w_down_all, b_down[:, None, :].astype(F32))
    return [_combine(y, tab, r[1], r[2], x_tok, g2, tpm)
            for (x_tok, _, _, g2, tpm), r, tab in zip(streams, routed, tabs)]


def kernel(x, c, ctx, c_ctx, w_mod, b_mod, norm1_g, norm2_g, w_in, w_out, lru_conv_w, lru_conv_b, lru_wa, lru_ba, lru_wx, lru_bx, lru_lam, ssd_conv_w, ssd_conv_b, ssd_a_log, ssd_dt_bias, ssd_d, ssd_norm_g, da_q_norm, da_k_norm, da_lam_q, da_lam_k, da_subln_g, router_w, router_b, exp_w_gu, exp_b_gu, exp_w_down, exp_b_down):
    b, s, d = x.shape
    n_ctx = ctx.shape[1]
    depth = w_mod.shape[0]
    cos, sin = _rope_tables(s)
    x_lat, x_ctx = x, ctx
    mod_rows = jnp.zeros((2 * SUBLANES, d), F32).at[:b].set(c).at[b].set(c_ctx)

    for l in range(depth):
        need_ctx = l < depth - 1
        lam_init = 0.8 - 0.6 * math.exp(-0.3 * l)
        mod = _modulation(mod_rows, w_mod[l], b_mod[l])
        m_lat = mod[:b].reshape(b, 1, 6, d)
        m_ctx = jnp.broadcast_to(mod[b].reshape(1, 1, 6, d), (b, 1, 6, d))
        part = lambda m, j: m[:, :, j, :]

        w_in_l = _prep_w_in(w_in[l])
        p_lat = _proj_in(x_lat, norm1_g[l], part(m_lat, 0), part(m_lat, 1), w_in_l)
        p_ctx = _proj_in(x_ctx, norm1_g[l], part(m_ctx, 0), part(m_ctx, 1), w_in_l)

        zl = jnp.zeros((b, LRU_WIDTH), F32)
        lru_args = (lru_wa[l], lru_ba[l], lru_wx[l], lru_bx[l], lru_lam[l])
        xc_ctx = _dwconv(p_ctx["lx"], lru_conv_w[l], lru_conv_b[l], act=False)
        hf_c, hb_c = _lru_scan(xc_ctx, *lru_args, zl, zl)
        xc_lat = _dwconv(p_lat["lx"], lru_conv_w[l], lru_conv_b[l], act=False)
        hf_l, hb_l = _lru_scan(xc_lat, *lru_args, hf_c[:, -1], hb_c[:, 0])

        zs = jnp.zeros((b, SSD_GROUPS, 2 * SSD_HEAD_DIM, SSD_STATE), F32)
        xbc_ctx = _dwconv(p_ctx["xbc"], ssd_conv_w[l], ssd_conv_b[l], act=True)
        yf_c, yb_c, sf_c, sb_c = _ssd_scan(xbc_ctx, p_ctx["dt"], ssd_a_log[l], ssd_dt_bias[l], zs, zs)
        xbc_lat = _dwconv(p_lat["xbc"], ssd_conv_w[l], ssd_conv_b[l], act=True)
        yf_l, yb_l, _, _ = _ssd_scan(xbc_lat, p_lat["dt"], ssd_a_log[l], ssd_dt_bias[l], sf_c, sb_c)

        lq = da_lam_q[l].astype(F32)
        lk = da_lam_k[l].astype(F32)
        lam = jnp.exp(jnp.sum(lq[0] * lk[0])) - jnp.exp(jnp.sum(lq[1] * lk[1])) + lam_init
        kc = _qk_prep(p_ctx["k"], da_k_norm[l], cos, sin, rope=False, scale=1.0, kc=min(ATT_KC, n_ctx))
        kl = _qk_prep(p_lat["k"], da_k_norm[l], cos, sin, rope=True, scale=1.0, kc=min(ATT_KC, s))
        ql = _qk_prep(p_lat["q"], da_q_norm[l], cos, sin, rope=True, scale=DA_SCALE * LOG2E, kc=0)
        da_l = _diff_attn(ql, [(kc, p_ctx["v"]), (kl, p_lat["v"])], lam, da_subln_g[l], lam_init)

        dvec = jnp.repeat(ssd_d[l].astype(F32), SSD_HEAD_DIM).reshape(1, SSD_WIDTH)
        ng = ssd_norm_g[l].astype(F32).reshape(1, SSD_WIDTH)
        w_out_l = w_out[l].astype(BF16)
        if need_ctx:
            qc = _qk_prep(p_ctx["q"], da_q_norm[l], cos, sin, rope=False, scale=DA_SCALE * LOG2E, kc=0)
            da_c = _diff_attn(qc, [(kc, p_ctx["v"])], lam, da_subln_g[l], lam_init)
            x_ctx = _proj_out(x_ctx, p_ctx["lg"], hf_c, hb_c, yf_c, yb_c, xbc_ctx, p_ctx["z"], da_c,
                              dvec, ng, part(m_ctx, 2), w_out_l)
        x_lat = _proj_out(x_lat, p_lat["lg"], hf_l, hb_l, yf_l, yb_l, xbc_lat, p_lat["z"], da_l,
                          dvec, ng, part(m_lat, 2), w_out_l)

        moe_w = (router_w[l], router_b[l], exp_w_gu, exp_b_gu[l], exp_w_down, exp_b_down[l])
        streams = [(x_lat.reshape(-1, d), part(m_lat, 3), part(m_lat, 4), part(m_lat, 5), s // ROUTE_TM)]
        if need_ctx:
            streams.append((x_ctx.reshape(-1, d), part(m_ctx, 3), part(m_ctx, 4), part(m_ctx, 5), n_ctx // ROUTE_TM))
        outs = _moe(streams, l, norm2_g[l], *moe_w)
        x_lat = outs[0].reshape(b, s, d)
        if need_ctx:
            x_ctx = outs[1].reshape(b, n_ctx, d)
    return x_lat
```

```python
import functools
import math

import jax
import jax.numpy as jnp
from jax import lax
from jax.experimental import pallas as pl
from jax.experimental.pallas import tpu as pltpu

F32 = jnp.float32
BF16 = jnp.bfloat16
I32 = jnp.int32

GRID_W = 64
EPS = 1e-6
CONV_K = 4
LRU_HEADS = 4
LRU_HEAD_DIM = 64
LRU_WIDTH = 256
LRU_C = 8.0
SSD_HEADS = 4
SSD_HEAD_DIM = 64
SSD_WIDTH = 256
SSD_GROUPS = 2
SSD_STATE = 128
SSD_CHUNK = 128
SSD_CONV_DIM = 768
DA_HEADS = 4
DA_QK_DIM = 64
DA_V_DIM = 128
DA_QK_WIDTH = 512
DA_WIDTH = 512
DA_SCALE = DA_QK_DIM ** -0.5
ROPE_BASE = 10000.0
N_EXPERTS = 32
TOP_K = 4
SWIGLU_ALPHA = 1.702
SWIGLU_LIMIT = 7.0

LANES = 128
SUBLANES = 8
VMEM_LIMIT = 56 * 1024 * 1024

PROJ_TM = 512
ATT_TQ = 512
ATT_KC = 256
LOG2E = math.log2(math.e)
LRU_TL = 512
SSD_STEP_CHUNKS = 4
ROUTE_TM = 256
MOE_TM = 512
NEG_BIG = -1e30


def _cparams(sem):
    return pltpu.CompilerParams(dimension_semantics=sem, vmem_limit_bytes=VMEM_LIMIT)


def _sigmoid(x):
    return 0.5 * jnp.tanh(0.5 * x) + 0.5


def _silu(x):
    return x * _sigmoid(x)


def _softplus(x):
    return jnp.maximum(x, 0.0) + jnp.log(1.0 + jnp.exp(-jnp.abs(x)))


def _round_up8(n):
    return ((n + (SUBLANES - 1)) >> 3) << 3


def _mod_kernel(c_ref, w_ref, b_ref, o_ref):
    a = _silu(c_ref[...])
    o_ref[...] = jnp.dot(a, w_ref[...], preferred_element_type=F32,
                         precision=lax.Precision.HIGHEST) + b_ref[...]


def _modulation(rows, w, b):
    m, d = rows.shape
    n = w.shape[1]
    tn = 512
    return pl.pallas_call(
        _mod_kernel,
        out_shape=jax.ShapeDtypeStruct((m, n), F32),
        grid=(n // tn,),
        in_specs=[pl.BlockSpec((m, d), lambda j: (0, 0)),
                  pl.BlockSpec((d, tn), lambda j: (0, j)),
                  pl.BlockSpec((1, tn), lambda j: (0, j))],
        out_specs=pl.BlockSpec((m, tn), lambda j: (0, j)),
        compiler_params=_cparams(("arbitrary",)),
        name="modulation",
    )(rows, w, b.reshape(1, n))


def _norm_mod(x, g, sh, sc):
    ms = jnp.mean(x * x, axis=-1, keepdims=True)
    y = x * lax.rsqrt(ms + EPS) * g
    return y * (1.0 + sc) + sh


def _proj_in_kernel(x_ref, g_ref, sh_ref, sc_ref, w_ref, gq_ref, gk_ref, cos_ref, sin_ref, seg_ref, *o_refs,
                    splits, rope, kc, q_scale):
    h = _norm_mod(x_ref[0], g_ref[...], sh_ref[0], sc_ref[0]).astype(BF16)
    for (name, _, _), o_ref, (lo, hi) in zip(_IN_GROUPS, o_refs, splits):
        y = jnp.dot(h, w_ref[:, lo:hi], preferred_element_type=F32)
        if name == "q":
            _store_queries(o_ref, _qk_transform(y, gq_ref[...], cos_ref, sin_ref, seg_ref, rope, q_scale))
        elif name == "k":
            _store_keys(o_ref, _qk_transform(y, gk_ref[...], cos_ref, sin_ref, seg_ref, rope, 1.0), kc)
        else:
            o_ref[0] = y.astype(o_ref.dtype)


_IN_GROUPS = (("lg", 256, F32), ("lx", 256, F32), ("z", 256, F32), ("xbc", 768, F32),
              ("dt", 128, F32), ("q", 512, BF16), ("k", 512, BF16), ("v", 512, BF16))


def _prep_w_in(w_in):
    d = w_in.shape[0]
    lo = 2 * LRU_WIDTH + SSD_WIDTH + SSD_CONV_DIM
    hi = lo + 2 * SSD_HEADS
    dt = jnp.concatenate([w_in[:, lo:hi], jnp.zeros((d, LANES - (hi - lo)), w_in.dtype)], axis=1)
    w = jnp.concatenate([w_in[:, :lo], dt, w_in[:, hi:]], axis=1)
    return w.astype(BF16)


def _proj_in(x, g, sh, sc, w, gq, gk, cos, sin, rope, kc):
    b, s, d = x.shape
    tm = min(PROJ_TM, s)
    kc = min(kc, tm)
    splits, off = [], 0
    for _, wd, _ in _IN_GROUPS:
        splits.append((off, off + wd))
        off += wd
    out_shape, out_specs = [], []
    for name, wd, dt in _IN_GROUPS:
        if name == "q":
            out_shape.append(jax.ShapeDtypeStruct((b, 2 * DA_HEADS, s, DA_QK_DIM), dt))
            out_specs.append(pl.BlockSpec((1, 2 * DA_HEADS, tm, DA_QK_DIM), lambda bi, i: (bi, 0, i, 0)))
        elif name == "k":
            out_shape.append(jax.ShapeDtypeStruct((b, 2 * DA_HEADS, s // kc, DA_QK_DIM, kc), dt))
            out_specs.append(pl.BlockSpec((1, 2 * DA_HEADS, tm // kc, DA_QK_DIM, kc), lambda bi, i: (bi, 0, i, 0, 0)))
        else:
            out_shape.append(jax.ShapeDtypeStruct((b, s, wd), dt))
            out_specs.append(pl.BlockSpec((1, tm, wd), lambda bi, i: (bi, i, 0)))
    n = DA_QK_WIDTH
    tile_gain = lambda gg: jnp.tile(gg.astype(F32), n // DA_QK_DIM).reshape(1, n)
    idx = jnp.arange(n) // DA_QK_DIM
    seg = (idx[:, None] == idx[None, :]).astype(BF16)
    const = lambda bi, i: (0, 0)
    outs = pl.pallas_call(
        functools.partial(_proj_in_kernel, splits=tuple(splits), rope=rope, kc=kc, q_scale=DA_SCALE * LOG2E),
        out_shape=out_shape,
        grid=(b, s // tm),
        in_specs=[pl.BlockSpec((1, tm, d), lambda bi, i: (bi, i, 0)),
                  pl.BlockSpec((1, d), const),
                  pl.BlockSpec((1, 1, d), lambda bi, i: (bi, 0, 0)),
                  pl.BlockSpec((1, 1, d), lambda bi, i: (bi, 0, 0)),
                  pl.BlockSpec((d, off), const),
                  pl.BlockSpec((1, n), const), pl.BlockSpec((1, n), const),
                  pl.BlockSpec((tm, LANES), lambda bi, i: (i, 0)),
                  pl.BlockSpec((tm, LANES), lambda bi, i: (i, 0)),
                  pl.BlockSpec((n, n), const)],
        out_specs=out_specs,
        compiler_params=_cparams(("arbitrary", "arbitrary")),
        name="proj_in",
    )(x, g.reshape(1, d), sh, sc, w, tile_gain(gq), tile_gain(gk), cos, sin, seg)
    return {name: o for (name, _, _), o in zip(_IN_GROUPS, outs)}


def _dwconv_kernel(x_ref, w_ref, b_ref, o_ref, *, act):
    x = x_ref[0]
    s = x.shape[0]
    row = lax.broadcasted_iota(I32, x.shape, 0)
    acc = x * w_ref[2:3, :] + b_ref[...]
    acc = acc + jnp.where(row >= 2, pltpu.roll(x, 2, 0), 0.0) * w_ref[0:1, :]
    acc = acc + jnp.where(row >= 1, pltpu.roll(x, 1, 0), 0.0) * w_ref[1:2, :]
    acc = acc + jnp.where(row < s - 1, pltpu.roll(x, s - 1, 0), 0.0) * w_ref[3:4, :]
    if act:
        acc = _silu(acc)
    o_ref[0] = acc


def _dwconv(x, w, bias, act):
    b, s, c = x.shape
    tc = 256
    return pl.pallas_call(
        functools.partial(_dwconv_kernel, act=act),
        out_shape=jax.ShapeDtypeStruct((b, s, c), F32),
        grid=(b, c // tc),
        in_specs=[pl.BlockSpec((1, s, tc), lambda bi, j: (bi, 0, j)),
                  pl.BlockSpec((CONV_K, tc), lambda bi, j: (0, j)),
                  pl.BlockSpec((1, tc), lambda bi, j: (0, j))],
        out_specs=pl.BlockSpec((1, s, tc), lambda bi, j: (bi, 0, j)),
        compiler_params=_cparams(("arbitrary", "arbitrary")),
        name="dwconv",
    )(x, w, bias.reshape(1, c))


def _lru_coeffs(x, w_ref, b_ref, sp_ref, a_s, u_s):
    g = jnp.dot(x.astype(BF16), w_ref[...], preferred_element_type=F32) + b_ref[...]
    r = _sigmoid(g[:, :LRU_WIDTH])
    i = _sigmoid(g[:, LRU_WIDTH:])
    a = jnp.exp(-LRU_C * r * sp_ref[...])
    a_s[...] = a
    u_s[...] = jnp.sqrt(1.0 - a * a) * (i * x)


def _lru_kernel(xf_ref, xb_ref, wf_ref, wb_ref, bf_ref, bb_ref, spf_ref, spb_ref,
                h0f_ref, h0b_ref, hf_ref, hb_ref, af_s, uf_s, ab_s, ub_s, cf_s, cb_s):
    c = pl.program_id(1)
    tl = xf_ref.shape[1]
    ngroups = tl // SUBLANES
    shape = (SUBLANES, LRU_WIDTH)
    row = lax.broadcasted_iota(I32, shape, 0)

    @pl.when(c == 0)
    def _():
        cf_s[...] = jnp.broadcast_to(h0f_ref[0], shape)
        cb_s[...] = jnp.broadcast_to(h0b_ref[0], shape)

    _lru_coeffs(xf_ref[0], wf_ref, bf_ref, spf_ref, af_s, uf_s)
    _lru_coeffs(xb_ref[0], wb_ref, bb_ref, spb_ref, ab_s, ub_s)

    def group_scan(a, u, carry, reverse):
        for sh in (1, 2, 4):
            keep = (row < SUBLANES - sh) if reverse else (row >= sh)
            amt = SUBLANES - sh if reverse else sh
            a_sh = jnp.where(keep, pltpu.roll(a, amt, 0), 1.0)
            u_sh = jnp.where(keep, pltpu.roll(u, amt, 0), 0.0)
            u = a * u_sh + u
            a = a * a_sh
        return a * carry + u

    def step(gj, carry):
        cf, cb = carry
        slf = pl.ds(pl.multiple_of(gj * SUBLANES, SUBLANES), SUBLANES)
        slb = pl.ds(pl.multiple_of((ngroups - 1 - gj) * SUBLANES, SUBLANES), SUBLANES)
        hf = group_scan(af_s[slf, :], uf_s[slf, :], cf, False)
        hb = group_scan(ab_s[slb, :], ub_s[slb, :], cb, True)
        hf_ref[0, slf, :] = hf
        hb_ref[0, slb, :] = hb
        return (jnp.broadcast_to(hf[SUBLANES - 1:SUBLANES, :], shape),
                jnp.broadcast_to(hb[0:1, :], shape))

    cf, cb = lax.fori_loop(0, ngroups, step, (cf_s[...], cb_s[...]), unroll=2)
    cf_s[...] = cf
    cb_s[...] = cb


def _lru_gate_weights(wa, wx):
    def dense(wh):
        m = jnp.zeros((LRU_WIDTH, LRU_WIDTH), wh.dtype)
        for h in range(LRU_HEADS):
            lo = h * LRU_HEAD_DIM
            m = m.at[lo:lo + LRU_HEAD_DIM, lo:lo + LRU_HEAD_DIM].set(wh[h])
        return m
    return jnp.concatenate([dense(wa), dense(wx)], axis=1).astype(BF16)


def _lru_scan(xc, wa, ba, wx, bx, lam, h0f, h0b):
    b, s, w = xc.shape
    tl = min(LRU_TL, s)
    nc = s // tl
    wf = _lru_gate_weights(wa[0], wx[0])
    wb = _lru_gate_weights(wa[1], wx[1])
    bf = jnp.concatenate([ba[0], bx[0]]).reshape(1, 2 * w)
    bb = jnp.concatenate([ba[1], bx[1]]).reshape(1, 2 * w)
    sp = jax.nn.softplus(-lam.astype(F32))
    const = lambda bi, c: (0, 0)
    hf, hb = pl.pallas_call(
        _lru_kernel,
        out_shape=[jax.ShapeDtypeStruct((b, s, w), F32)] * 2,
        grid=(b, nc),
        in_specs=[pl.BlockSpec((1, tl, w), lambda bi, c: (bi, c, 0)),
                  pl.BlockSpec((1, tl, w), lambda bi, c: (bi, nc - 1 - c, 0)),
                  pl.BlockSpec((w, 2 * w), const), pl.BlockSpec((w, 2 * w), const),
                  pl.BlockSpec((1, 2 * w), const), pl.BlockSpec((1, 2 * w), const),
                  pl.BlockSpec((1, w), const), pl.BlockSpec((1, w), const),
                  pl.BlockSpec((1, 1, w), lambda bi, c: (bi, 0, 0)),
                  pl.BlockSpec((1, 1, w), lambda bi, c: (bi, 0, 0))],
        out_specs=[pl.BlockSpec((1, tl, w), lambda bi, c: (bi, c, 0)),
                   pl.BlockSpec((1, tl, w), lambda bi, c: (bi, nc - 1 - c, 0))],
        scratch_shapes=[pltpu.VMEM((tl, w), F32)] * 4
        + [pltpu.VMEM((SUBLANES, w), F32), pltpu.VMEM((SUBLANES, w), F32)],
        compiler_params=_cparams(("arbitrary", "arbitrary")),
        name="lru_scan",
    )(xc, xc, wf, wb, bf, bb, sp[0:1], sp[1:2], h0f.reshape(b, 1, w), h0b.reshape(b, 1, w))
    return hf, hb


def _ssd_direction(x_ref, dt_ref, row0, avec_ref, dtb_ref, state_s, y_ref, lane0, rev):
    t = SSD_CHUNK
    xbc = x_ref[0, row0:row0 + t, :]
    dtraw = dt_ref[0, row0:row0 + t, :]
    ri = lax.broadcasted_iota(I32, (t, t), 0)
    ci = lax.broadcasted_iota(I32, (t, t), 1)
    causal = (ci >= ri) if rev else (ci <= ri)
    cum_m = jnp.where(causal, 1.0, 0.0).astype(F32)

    dt = _softplus(dtraw + dtb_ref[...])
    a_all = dt * avec_ref[...]
    acs = jnp.dot(cum_m, a_all, preferred_element_type=F32, precision=lax.Precision.HIGHEST)
    acs_t = acs.T
    edge = 0 if rev else t - 1

    xs = xbc[:, :SSD_WIDTH]
    hrow = lax.broadcasted_iota(I32, (t, t), 0)
    for g in range(SSD_GROUPS):
        bm = xbc[:, SSD_WIDTH + g * SSD_STATE:SSD_WIDTH + (g + 1) * SSD_STATE]
        cm = xbc[:, SSD_WIDTH + (SSD_GROUPS + g) * SSD_STATE:SSD_WIDTH + (SSD_GROUPS + g + 1) * SSD_STATE]
        bm_b = bm.astype(BF16)
        cm_b = cm.astype(BF16)
        gmat = lax.dot_general(cm_b, bm_b, (((1,), (1,)), ((), ())), preferred_element_type=F32)
        s_prev = state_s[g]
        off_all = lax.dot_general(cm_b, s_prev.astype(BF16), (((1,), (1,)), ((), ())),
                                  preferred_element_type=F32)
        xd_parts, tot = [], []
        for hl in range(2):
            h = g * 2 + hl
            ln = lane0 + h
            col = jnp.broadcast_to(acs[:, ln:ln + 1], (t, t))
            rowv = acs_t[ln:ln + 1, :]
            lmat = jnp.where(causal, jnp.exp(col - rowv), 0.0)
            xh = xs[:, h * SSD_HEAD_DIM:(h + 1) * SSD_HEAD_DIM] * dt[:, ln:ln + 1]
            ydiag = jnp.dot((gmat * lmat).astype(BF16), xh.astype(BF16), preferred_element_type=F32)
            colh = col[:, :SSD_HEAD_DIM]
            yoff = off_all[:, hl * SSD_HEAD_DIM:(hl + 1) * SSD_HEAD_DIM] * jnp.exp(colh)
            y_ref[0, row0:row0 + t, h * SSD_HEAD_DIM:(h + 1) * SSD_HEAD_DIM] = ydiag + yoff
            total = acs[edge:edge + 1, ln:ln + 1]
            xd_parts.append(xh * jnp.exp(total - colh))
            tot.append(jnp.exp(total))
        xd = jnp.concatenate(xd_parts, axis=1)
        xd_t = xd.T
        st = jnp.dot(xd_t.astype(BF16), bm_b, preferred_element_type=F32)
        fac = jnp.where(hrow < SSD_HEAD_DIM, tot[0], tot[1])
        state_s[g] = fac * s_prev + st


def _ssd_kernel(xf_ref, dtf_ref, xb_ref, dtb_ref, avec_ref, bias_ref, h0f_ref, h0b_ref,
                yf_ref, yb_ref, sf_ref, sb_ref, stf_s, stb_s):
    c = pl.program_id(1)

    @pl.when(c == 0)
    def _():
        stf_s[...] = h0f_ref[0]
        stb_s[...] = h0b_ref[0]

    nsub = xf_ref.shape[1] // SSD_CHUNK
    for j in range(nsub):
        _ssd_direction(xf_ref, dtf_ref, j * SSD_CHUNK, avec_ref, bias_ref, stf_s, yf_ref, 0, False)
        _ssd_direction(xb_ref, dtb_ref, (nsub - 1 - j) * SSD_CHUNK, avec_ref, bias_ref, stb_s, yb_ref,
                       SSD_HEADS, True)
    sf_ref[0] = stf_s[...]
    sb_ref[0] = stb_s[...]


def _ssd_scan(xbc, dt, a_log, dt_bias, h0f, h0b):
    b, s, _ = xbc.shape
    t = min(SSD_STEP_CHUNKS * SSD_CHUNK, s)
    nc = s // t
    avec = jnp.zeros((1, LANES), F32).at[0, :2 * SSD_HEADS].set(-jnp.exp(a_log.astype(F32)).reshape(-1))
    bias = jnp.zeros((1, LANES), F32).at[0, :2 * SSD_HEADS].set(dt_bias.astype(F32).reshape(-1))
    st_shape = (b, SSD_GROUPS, 2 * SSD_HEAD_DIM, SSD_STATE)
    const = lambda bi, c: (0, 0)
    st_spec = pl.BlockSpec((1,) + st_shape[1:], lambda bi, c: (bi, 0, 0, 0))
    yf, yb, sf, sb = pl.pallas_call(
        _ssd_kernel,
        out_shape=[jax.ShapeDtypeStruct((b, s, SSD_WIDTH), F32)] * 2
        + [jax.ShapeDtypeStruct(st_shape, F32)] * 2,
        grid=(b, nc),
        in_specs=[pl.BlockSpec((1, t, SSD_CONV_DIM), lambda bi, c: (bi, c, 0)),
                  pl.BlockSpec((1, t, LANES), lambda bi, c: (bi, c, 0)),
                  pl.BlockSpec((1, t, SSD_CONV_DIM), lambda bi, c: (bi, nc - 1 - c, 0)),
                  pl.BlockSpec((1, t, LANES), lambda bi, c: (bi, nc - 1 - c, 0)),
                  pl.BlockSpec((1, LANES), const), pl.BlockSpec((1, LANES), const),
                  st_spec, st_spec],
        out_specs=[pl.BlockSpec((1, t, SSD_WIDTH), lambda bi, c: (bi, c, 0)),
                   pl.BlockSpec((1, t, SSD_WIDTH), lambda bi, c: (bi, nc - 1 - c, 0)),
                   st_spec, st_spec],
        scratch_shapes=[pltpu.VMEM(st_shape[1:], F32), pltpu.VMEM(st_shape[1:], F32)],
        compiler_params=_cparams(("arbitrary", "arbitrary")),
        name="ssd_scan",
    )(xbc, dt, xbc, dt, avec, bias, h0f, h0b)
    return yf, yb, sf, sb


def _qk_transform(x, g, cos_ref, sin_ref, seg_ref, rope, scale):
    sq = x * x
    hi = sq.astype(BF16)
    lo = (sq - hi.astype(F32)).astype(BF16)
    ssum = (jnp.dot(hi, seg_ref[...], preferred_element_type=F32)
            + jnp.dot(lo, seg_ref[...], preferred_element_type=F32))
    y = x * lax.rsqrt(ssum * (1.0 / DA_QK_DIM) + EPS) * g
    if rope:
        n = x.shape[1]
        lane = lax.broadcasted_iota(I32, x.shape, 1)
        first = (lane & (DA_QK_DIM - 1)) < DA_QK_DIM // 2
        partner = jnp.where(first, pltpu.roll(y, n - DA_QK_DIM // 2, 1), pltpu.roll(y, DA_QK_DIM // 2, 1))
        cosv = jnp.concatenate([cos_ref[...]] * (n // LANES), axis=1)
        sinv = jnp.concatenate([sin_ref[...]] * (n // LANES), axis=1)
        y = y * cosv + partner * sinv
    return y * scale


def _store_queries(o_ref, y):
    yb = y.astype(BF16)
    for j in range(2 * DA_HEADS):
        o_ref[0, j] = yb[:, j * DA_QK_DIM:(j + 1) * DA_QK_DIM]


def _store_keys(o_ref, y, kc):
    for h in range(DA_HEADS):
        head = y[:, h * 2 * DA_QK_DIM:(h + 1) * 2 * DA_QK_DIM]
        for cc in range(y.shape[0] // kc):
            t = head[cc * kc:(cc + 1) * kc, :].T.astype(BF16)
            o_ref[0, 2 * h, cc] = t[:DA_QK_DIM]
            o_ref[0, 2 * h + 1, cc] = t[DA_QK_DIM:]


def _rope_tables(s):
    pos = jnp.arange(s, dtype=I32)
    r = (pos // GRID_W).astype(F32)
    col = (pos % GRID_W).astype(F32)
    per_axis = DA_QK_DIM // 4
    inv = ROPE_BASE ** (-jnp.arange(per_axis, dtype=F32) / per_axis)
    ang = jnp.concatenate([r[:, None] * inv, col[:, None] * inv], axis=-1)
    cos, sin = jnp.cos(ang), jnp.sin(ang)
    cos64 = jnp.concatenate([cos, cos], axis=-1)
    sin64 = jnp.concatenate([-sin, sin], axis=-1)
    return jnp.concatenate([cos64, cos64], axis=-1), jnp.concatenate([sin64, sin64], axis=-1)


def _attn_kernel(lam_ref, q_ref, *refs, nseg, out_scale):
    lam = lam_ref[0, 0]
    segs = [(refs[2 * j], refs[2 * j + 1]) for j in range(nseg)]
    g_ref, o_ref = refs[2 * nseg], refs[2 * nseg + 1]
    tq = q_ref.shape[2]
    m = [jnp.full((tq, LANES), NEG_BIG, F32)] * 2
    l = [jnp.zeros((tq, LANES), F32)] * 2
    acc = [jnp.zeros((tq, DA_V_DIM), F32)] * 2

    for k_ref, v_ref in segs:
        nchunk, kc = k_ref.shape[2], k_ref.shape[4]
        for c in range(nchunk):
            v = v_ref[0, c * kc:(c + 1) * kc, :]
            probs, alphas = [], []
            for sub in range(2):
                s = jnp.dot(q_ref[0, sub], k_ref[0, sub, c], preferred_element_type=F32)
                mx = s[:, :LANES]
                for j in range(1, kc // LANES):
                    mx = jnp.maximum(mx, s[:, j * LANES:(j + 1) * LANES])
                m_new = jnp.maximum(m[sub], jnp.max(mx, axis=-1, keepdims=True))
                alpha = jnp.exp2(m[sub] - m_new)
                p = jnp.exp2(s - jnp.concatenate([m_new] * (kc // LANES), axis=1))
                lsum = alpha * l[sub]
                for j in range(kc // LANES):
                    lsum = lsum + p[:, j * LANES:(j + 1) * LANES]
                l[sub] = lsum
                m[sub] = m_new
                probs.append(p.astype(BF16))
                alphas.append(alpha)
            pv = jnp.dot(jnp.concatenate(probs, axis=0), v, preferred_element_type=F32)
            for sub in range(2):
                acc[sub] = alphas[sub] * acc[sub] + pv[sub * tq:(sub + 1) * tq]

    l1 = jnp.sum(l[0], axis=-1, keepdims=True)
    l2 = jnp.sum(l[1], axis=-1, keepdims=True)
    o = acc[0] * (1.0 / l1) - lam * (acc[1] * (1.0 / l2))
    ms = jnp.mean(o * o, axis=-1, keepdims=True)
    o_ref[0] = o * lax.rsqrt(ms + EPS) * g_ref[...] * out_scale


def _diff_attn(qp, segments, lam, g, lam_init):
    b, _, s, d = qp.shape
    tq = min(ATT_TQ, s)
    seg_specs, seg_args = [], []
    for kt, v in segments:
        nchunk, kc = kt.shape[2], kt.shape[4]
        seg_specs += [pl.BlockSpec((1, 2, nchunk, d, kc), lambda bi, h, i: (bi, h, 0, 0, 0)),
                      pl.BlockSpec((1, nchunk * kc, DA_V_DIM), lambda bi, h, i: (bi, 0, h))]
        seg_args += [kt, v]
    return pl.pallas_call(
        functools.partial(_attn_kernel, nseg=len(segments), out_scale=1.0 - lam_init),
        out_shape=jax.ShapeDtypeStruct((b, s, DA_WIDTH), F32),
        grid=(b, DA_HEADS, s // tq),
        in_specs=[pl.BlockSpec(memory_space=pltpu.SMEM),
                  pl.BlockSpec((1, 2, tq, d), lambda bi, h, i: (bi, h, i, 0))] + seg_specs
        + [pl.BlockSpec((1, DA_V_DIM), lambda bi, h, i: (0, 0))],
        out_specs=pl.BlockSpec((1, tq, DA_V_DIM), lambda bi, h, i: (bi, i, h)),
        compiler_params=_cparams(("arbitrary", "arbitrary", "arbitrary")),
        name="diff_attn",
    )(lam.reshape(1, 1).astype(F32), qp, *seg_args, g.reshape(1, DA_V_DIM).astype(F32))


def _gelu_tanh(x):
    return 0.5 * x * (1.0 + jnp.tanh(math.sqrt(2.0 / math.pi) * (x + 0.044715 * x * x * x)))


def _proj_out_kernel(x_ref, lg_ref, hf_ref, hb_ref, yf_ref, yb_ref, xs_ref, z_ref, da_ref,
                     dvec_ref, ng_ref, gate_ref, w_ref, o_ref):
    lru = _gelu_tanh(lg_ref[0]) * (hf_ref[0] + hb_ref[0])
    y = yf_ref[0] + yb_ref[0] + dvec_ref[...] * xs_ref[0]
    y = y * _silu(z_ref[0])
    ssd = y * lax.rsqrt(jnp.mean(y * y, axis=-1, keepdims=True) + EPS) * ng_ref[...]
    o = jnp.dot(lru.astype(BF16), w_ref[0:LRU_WIDTH, :], preferred_element_type=F32)
    o = o + jnp.dot(ssd.astype(BF16), w_ref[LRU_WIDTH:LRU_WIDTH + SSD_WIDTH, :], preferred_element_type=F32)
    o = o + jnp.dot(da_ref[0].astype(BF16), w_ref[LRU_WIDTH + SSD_WIDTH:, :], preferred_element_type=F32)
    o_ref[0] = x_ref[0] + gate_ref[0] * o


def _proj_out(x, lg, hf, hb, yf, yb, xbc, z, da, dvec, ng, gate, w):
    b, s, d = x.shape
    tm = min(PROJ_TM, s)
    tok = lambda wd: pl.BlockSpec((1, tm, wd), lambda bi, i: (bi, i, 0))
    const2 = lambda bi, i: (0, 0)
    return pl.pallas_call(
        _proj_out_kernel,
        out_shape=jax.ShapeDtypeStruct((b, s, d), F32),
        grid=(b, s // tm),
        in_specs=[tok(d), tok(LRU_WIDTH), tok(LRU_WIDTH), tok(LRU_WIDTH), tok(SSD_WIDTH),
                  tok(SSD_WIDTH), tok(SSD_WIDTH), tok(SSD_WIDTH), tok(DA_WIDTH),
                  pl.BlockSpec((1, SSD_WIDTH), const2), pl.BlockSpec((1, SSD_WIDTH), const2),
                  pl.BlockSpec((1, 1, d), lambda bi, i: (bi, 0, 0)),
                  pl.BlockSpec(w.shape, const2)],
        out_specs=tok(d),
        compiler_params=_cparams(("arbitrary", "arbitrary")),
        name="proj_out",
    )(x, lg, hf, hb, yf, yb, xbc, z, da, dvec, ng, gate, w)


def _route_kernel(x_ref, g_ref, sh_ref, sc_ref, rwh_ref, rwl_ref, rb_ref, c0_ref, h_ref, ri_ref, gt_ref,
                  slt_ref, n_ref, base_ref, cnt_ref, carry_s):
    i = pl.program_id(0)
    tm = x_ref.shape[0]

    @pl.when(i == 0)
    def _():
        carry_s[...] = c0_ref[...]

    h = _norm_mod(x_ref[...], g_ref[...], sh_ref[0], sc_ref[0])
    h_hi = h.astype(BF16)
    h_ref[...] = h_hi
    h_lo = (h - h_hi.astype(F32)).astype(BF16)
    logits = (jnp.dot(h_hi, rwh_ref[...], preferred_element_type=F32)
              + jnp.dot(h_hi, rwl_ref[...], preferred_element_type=F32)
              + jnp.dot(h_lo, rwh_ref[...], preferred_element_type=F32)) + rb_ref[...]
    lane = lax.broadcasted_iota(I32, logits.shape, 1)
    lane_f = lane.astype(F32)
    vals, idxs = [], []
    cur = logits
    for _ in range(TOP_K):
        m = jnp.max(cur, axis=-1, keepdims=True)
        idx = jnp.min(jnp.where(cur >= m, lane_f, float(LANES)), axis=-1, keepdims=True).astype(I32)
        vals.append(m)
        idxs.append(idx)
        cur = jnp.where(lane == idx, 2.0 * NEG_BIG, cur)
    es = [jnp.exp(v - vals[0]) for v in vals]
    inv = 1.0 / (es[0] + es[1] + es[2] + es[3])
    onehot = jnp.zeros(logits.shape, F32)
    for idx in idxs:
        onehot = onehot + jnp.where(lane == idx, 1.0, 0.0)
    ri = lax.broadcasted_iota(I32, (tm, tm), 0)
    ci = lax.broadcasted_iota(I32, (tm, tm), 1)
    strict = jnp.where(ci < ri, 1.0, 0.0).astype(BF16)
    before = jnp.dot(strict, onehot.astype(BF16), preferred_element_type=F32)
    n_tile = jnp.sum(onehot, axis=0, keepdims=True).astype(I32)
    pad_tile = _round_up8(n_tile)
    li = lax.broadcasted_iota(I32, (LANES, LANES), 0)
    lj = lax.broadcasted_iota(I32, (LANES, LANES), 1)
    below = jnp.where(li < lj, 1.0, 0.0).astype(BF16)
    pad_rows = jnp.broadcast_to(pad_tile.astype(F32), (SUBLANES, LANES)).astype(BF16)
    off = jnp.dot(pad_rows, below, preferred_element_type=F32)[0:1, :]
    place = before + off
    ri_out = jnp.zeros(logits.shape, I32)
    gt_out = jnp.zeros(logits.shape, F32)
    sl_f = jnp.full(logits.shape, -1.0, F32)
    for k in range(TOP_K):
        sloc = jnp.sum(jnp.where(lane == idxs[k], place, 0.0), axis=-1, keepdims=True)
        ri_out = ri_out + jnp.where(lane == k, idxs[k], 0) + jnp.where(lane == TOP_K + k, sloc.astype(I32), 0)
        sl_f = jnp.where(lane == k, sloc, sl_f)
        gt_out = gt_out + jnp.where(lane == k, es[k] * inv, 0.0)
    ri_ref[...] = ri_out
    gt_ref[...] = gt_out
    slt_ref[0] = sl_f.T[0:SUBLANES, :].astype(I32)
    n_ref[0] = n_tile
    base_ref[0] = carry_s[...]
    carry_s[...] = carry_s[...] + pad_tile
    cnt_ref[...] = carry_s[...]


def _route(x, g, sh, sc, tiles_per_mod, rw, rb, counts0):
    t, d = x.shape
    tm = ROUTE_TM
    tile = lambda wd: pl.BlockSpec((tm, wd), lambda i: (i, 0))
    const = lambda i: (0, 0)
    mod = pl.BlockSpec((1, 1, d), lambda i: (i // tiles_per_mod, 0, 0))
    nt = t // tm
    per_tile = pl.BlockSpec((1, 1, LANES), lambda i: (i, 0, 0))
    return pl.pallas_call(
        _route_kernel,
        out_shape=[jax.ShapeDtypeStruct((t, d), BF16), jax.ShapeDtypeStruct((t, LANES), I32),
                   jax.ShapeDtypeStruct((t, LANES), F32), jax.ShapeDtypeStruct((nt, SUBLANES, tm), I32),
                   jax.ShapeDtypeStruct((nt, 1, LANES), I32),
                   jax.ShapeDtypeStruct((nt, 1, LANES), I32), jax.ShapeDtypeStruct((1, LANES), I32)],
        grid=(nt,),
        in_specs=[tile(d), pl.BlockSpec((1, d), const), mod, mod,
                  pl.BlockSpec((d, LANES), const), pl.BlockSpec((d, LANES), const),
                  pl.BlockSpec((1, LANES), const), pl.BlockSpec((1, LANES), const)],
        out_specs=[tile(d), tile(LANES), tile(LANES), pl.BlockSpec((1, SUBLANES, tm), lambda i: (i, 0, 0)),
                   per_tile, per_tile, pl.BlockSpec((1, LANES), const)],
        scratch_shapes=[pltpu.VMEM((1, LANES), I32)],
        compiler_params=_cparams(("arbitrary",)),
        name="moe_route",
    )(x, g.reshape(1, d), sh, sc, rw[0], rw[1], rb, counts0)


MOE_STAGE = -(-(ROUTE_TM * TOP_K + N_EXPERTS * (SUBLANES - 1)) // 256) * 256
_TAB_N, _TAB_DST, _TAB_OFF, _TAB_TOTAL = 0, N_EXPERTS, 2 * N_EXPERTS, 3 * N_EXPERTS


U32 = jnp.uint32
_HI16 = 0xFFFF0000


def _pack_halves(a, is_bf16=False):
    n = a.shape[1] // 2
    lo, hi = a[:, :n], a[:, n:]
    if not is_bf16:
        lo, hi = lo.astype(BF16).astype(F32), hi.astype(BF16).astype(F32)
    return (pltpu.bitcast(lo, U32) >> 16) | (pltpu.bitcast(hi, U32) & U32(_HI16))


def _unpack_halves(u):
    lo = pltpu.bitcast(u << 16, F32).astype(BF16)
    hi = pltpu.bitcast(u & U32(_HI16), F32).astype(BF16)
    return lo, hi


_BIG_CHUNKS = 4


def _chunk_copies(tab_ref, stage_ref, rows_hbm, sem, to_hbm):
    def copy(off, dst, rows):
        s = stage_ref.at[pl.ds(pl.multiple_of(off, SUBLANES), rows), :]
        r = rows_hbm.at[pl.ds(pl.multiple_of(dst, SUBLANES), rows), :]
        if to_hbm:
            pltpu.make_async_copy(s, r, sem).start()
        else:
            pltpu.make_async_copy(r, s, sem).start()

    def expert(e, carry):
        n = tab_ref[0, 0, _TAB_N + e]
        dst = tab_ref[0, 0, _TAB_DST + e]
        off = tab_ref[0, 0, _TAB_OFF + e]
        big_rows = _BIG_CHUNKS * SUBLANES
        n_big = n // _BIG_CHUNKS

        def big(j, c):
            copy(off + j * big_rows, dst + j * big_rows, big_rows)
            return c

        def small(j, c):
            copy(off + (n_big * _BIG_CHUNKS + j) * SUBLANES, dst + (n_big * _BIG_CHUNKS + j) * SUBLANES, SUBLANES)
            return c

        lax.fori_loop(0, n_big, big, 0)
        lax.fori_loop(0, n - n_big * _BIG_CHUNKS, small, 0)
        return carry
    lax.fori_loop(0, N_EXPERTS, expert, 0)


def _chunk_waits(tab_ref, stage_ref, rows_hbm, sem):
    rows = tab_ref[0, 0, _TAB_TOTAL] * SUBLANES
    pltpu.make_async_copy(rows_hbm.at[pl.ds(0, rows), :], stage_ref.at[pl.ds(0, rows), :], sem).wait()


def _dispatch_kernel(tab_ref, tprev_ref, gap_ref, tail_ref, sloc_ref, *refs, tiles):
    h_refs = refs[:len(tiles)]
    xs_out, stage_s, zero_s, sems = refs[len(tiles):]
    i = pl.program_id(0)
    n_steps = sum(tiles)
    slot = i % 2
    row = lax.broadcasted_iota(I32, (MOE_STAGE, ROUTE_TM), 0)
    perm = jnp.zeros((MOE_STAGE, ROUTE_TM), F32)
    for k in range(TOP_K):
        perm = perm + jnp.where(row == sloc_ref[0, k:k + 1, :], 1.0, 0.0)
    perm = perm.astype(BF16)
    off = 0
    for h_ref, nt in zip(h_refs, tiles):
        @pl.when((i >= off) & (i < off + nt))
        def _(h_ref=h_ref):
            stage_s[slot] = _pack_halves(jnp.dot(perm, h_ref[...], preferred_element_type=F32), is_bf16=True)
        off += nt
    _chunk_copies(tab_ref, stage_s.at[slot], xs_out, sems.at[slot], True)

    @pl.when(i > 0)
    def _():
        _chunk_waits(tprev_ref, stage_s.at[1 - slot], xs_out, sems.at[1 - slot])

    @pl.when(i == n_steps - 1)
    def _():
        _chunk_waits(tab_ref, stage_s.at[slot], xs_out, sems.at[slot])
        sem = sems.at[slot]
        zero_s[...] = jnp.zeros(zero_s.shape, U32)
        zr = zero_s.shape[0]
        n_big, n_small, start = tail_ref[0], tail_ref[1], tail_ref[2]

        def big(j, c):
            pltpu.make_async_copy(zero_s, xs_out.at[pl.ds(pl.multiple_of(start + j * zr, SUBLANES), zr), :], sem).start()
            return c

        def small(j, c):
            r0 = pl.multiple_of(start + n_big * zr + j * SUBLANES, SUBLANES)
            pltpu.make_async_copy(zero_s.at[pl.ds(0, SUBLANES), :], xs_out.at[pl.ds(r0, SUBLANES), :], sem).start()
            return c

        def big_wait(j, c):
            pltpu.make_async_copy(zero_s, xs_out.at[pl.ds(0, zr), :], sem).wait()
            return c

        def small_wait(j, c):
            pltpu.make_async_copy(zero_s.at[pl.ds(0, SUBLANES), :], xs_out.at[pl.ds(0, SUBLANES), :], sem).wait()
            return c

        lax.fori_loop(0, n_big, big, 0)
        lax.fori_loop(0, n_small, small, 0)
        lax.fori_loop(0, n_big, big_wait, 0)
        lax.fori_loop(0, n_small, small_wait, 0)

        _chunk_copies(gap_ref, zero_s, xs_out, sem, True)

        def gap_wait(e, c):
            rows = gap_ref[0, 0, _TAB_N + e] * SUBLANES

            @pl.when(rows > 0)
            def _():
                pltpu.make_async_copy(zero_s.at[pl.ds(0, rows), :], xs_out.at[pl.ds(0, rows), :], sem).wait()
            return c
        lax.fori_loop(0, N_EXPERTS, gap_wait, 0)


def _dispatch(hs, tabs, slocs, gaps, tail, n_max):
    d = hs[0].shape[1]
    tm = ROUTE_TM
    tiles = tuple(h.shape[0] // tm for h in hs)
    specs, off = [], 0
    for nt in tiles:
        specs.append(pl.BlockSpec((tm, d), lambda i, off=off, nt=nt: (jnp.clip(i - off, 0, nt - 1), 0)))
        off += nt
    return pl.pallas_call(
        functools.partial(_dispatch_kernel, tiles=tiles),
        out_shape=jax.ShapeDtypeStruct((n_max, d // 2), U32),
        grid=(sum(tiles),),
        in_specs=[pl.BlockSpec((1, 1, LANES), lambda i: (i, 0, 0), memory_space=pltpu.SMEM),
                  pl.BlockSpec((1, 1, LANES), lambda i: (jnp.maximum(i - 1, 0), 0, 0), memory_space=pltpu.SMEM),
                  pl.BlockSpec((1, 1, LANES), lambda i: (0, 0, 0), memory_space=pltpu.SMEM),
                  pl.BlockSpec(memory_space=pltpu.SMEM),
                  pl.BlockSpec((1, SUBLANES, tm), lambda i: (i, 0, 0))] + specs,
        out_specs=pl.BlockSpec(memory_space=pl.ANY),
        scratch_shapes=[pltpu.VMEM((2, MOE_STAGE, d // 2), U32), pltpu.VMEM((MOE_TM, d // 2), U32),
                        pltpu.SemaphoreType.DMA((2,))],
        compiler_params=_cparams(("arbitrary",)),
        name="moe_dispatch",
    )(tabs, tabs, gaps, tail, slocs, *hs)


_SPLIT_W = 2 * LANES


def _gmm_kernel(te_ref, live_ref, newe_ref, xs_ref, wgu_ref, perm_ref, bg_ref, bu_ref,
                wdf_ref, bd_ref, y_ref, wg_ref, wu_ref, wd_ref):
    del te_ref
    v = pl.program_id(0)
    live = live_ref[v] > 0

    @pl.when((newe_ref[v] > 0) & live)
    def _():
        for j in range(wgu_ref.shape[1] // _SPLIT_W):
            r = jnp.dot(wgu_ref[:, j * _SPLIT_W:(j + 1) * _SPLIT_W].astype(BF16), perm_ref[...],
                        preferred_element_type=F32)
            wg_ref[:, j * LANES:(j + 1) * LANES] = r[:, :LANES].astype(BF16)
            wu_ref[:, j * LANES:(j + 1) * LANES] = r[:, LANES:].astype(BF16)
        wd_ref[...] = wdf_ref[...].astype(BF16)

    @pl.when(live)
    def _():
        x_lo, x_hi = _unpack_halves(xs_ref[...])
        half = x_lo.shape[1]
        gate = (jnp.dot(x_lo, wg_ref[:half, :], preferred_element_type=F32)
                + jnp.dot(x_hi, wg_ref[half:, :], preferred_element_type=F32)) + bg_ref[...]
        up = (jnp.dot(x_lo, wu_ref[:half, :], preferred_element_type=F32)
              + jnp.dot(x_hi, wu_ref[half:, :], preferred_element_type=F32)) + bu_ref[...]
        gate = jnp.minimum(gate, SWIGLU_LIMIT)
        up = jnp.clip(up, -SWIGLU_LIMIT, SWIGLU_LIMIT)
        glu = gate * _sigmoid(SWIGLU_ALPHA * gate)
        act = ((up + 1.0) * glu).astype(BF16)
        y_ref[...] = _pack_halves(jnp.dot(act, wd_ref[...], preferred_element_type=F32) + bd_ref[...])

    @pl.when(jnp.logical_not(live))
    def _():
        y_ref[...] = jnp.zeros(y_ref.shape, U32)


def _gmm(xs, tiles, layer, w_gu, bg, bu, w_down, bd):
    n_rows = xs.shape[0]
    _, _, d, de2 = w_gu.shape
    de = de2 // 2
    tm = MOE_TM
    nv = n_rows // tm
    src = jnp.arange(_SPLIT_W)
    dst = jnp.where(src % 2 == 0, src // 2, LANES + src // 2)
    perm = (dst[:, None] == jnp.arange(_SPLIT_W)[None, :]).astype(BF16)
    ex = lambda shape: pl.BlockSpec((None,) + shape, lambda v, te, lv, ne: (te[v], 0, 0))
    lay = lambda shape: pl.BlockSpec((None, None) + shape, lambda v, te, lv, ne: (layer, te[v], 0, 0))
    rows = pl.BlockSpec((tm, d // 2), lambda v, te, lv, ne: (v, 0))
    grid_spec = pltpu.PrefetchScalarGridSpec(
        num_scalar_prefetch=3,
        grid=(nv,),
        in_specs=[rows, lay((d, de2)), pl.BlockSpec((_SPLIT_W, _SPLIT_W), lambda v, *_: (0, 0)),
                  ex((1, de)), ex((1, de)), lay((de, d)), ex((1, d))],
        out_specs=rows,
        scratch_shapes=[pltpu.VMEM((d, de), BF16), pltpu.VMEM((d, de), BF16), pltpu.VMEM((de, d), BF16)],
    )
    return pl.pallas_call(
        _gmm_kernel,
        out_shape=jax.ShapeDtypeStruct((n_rows, d // 2), U32),
        grid_spec=grid_spec,
        compiler_params=_cparams(("arbitrary",)),
        name="moe_experts",
    )(*tiles, xs, w_gu, perm, bg, bu, w_down, bd)


def _tile_tables(cap, n_rows):
    tm = MOE_TM
    end = jnp.cumsum(cap)
    start = jnp.arange(n_rows // tm, dtype=I32) * tm
    e = jnp.minimum(jnp.sum((end[None, :] <= start[:, None]).astype(I32), axis=1), N_EXPERTS - 1).astype(I32)
    live = (start < end[N_EXPERTS - 1]).astype(I32)
    new_e = (e != jnp.concatenate([jnp.full((1,), -1, I32), e[:-1]])).astype(I32)
    return e, live, new_e


def _combine_kernel(tcur_ref, tnext_ref, sloc_ref, gt_ref, x_ref, g2_ref, y_hbm, o_ref, stage_s, sems, *, n):
    i = pl.program_id(0)
    slot = i % 2

    @pl.when(i == 0)
    def _():
        stage_s[...] = jnp.zeros(stage_s.shape, U32)
        _chunk_copies(tcur_ref, stage_s.at[0], y_hbm, sems.at[0], False)

    @pl.when(i + 1 < n)
    def _():
        _chunk_copies(tnext_ref, stage_s.at[1 - slot], y_hbm, sems.at[1 - slot], False)

    _chunk_waits(tcur_ref, stage_s.at[slot], y_hbm, sems.at[slot])
    rows_lo, rows_hi = _unpack_halves(stage_s[slot])
    lane = lax.broadcasted_iota(I32, (x_ref.shape[0], MOE_STAGE), 1)
    gt = gt_ref[...]
    sloc = sloc_ref[...]
    gmat = jnp.zeros(lane.shape, F32)
    for k in range(TOP_K):
        gmat = gmat + jnp.where(lane == sloc[:, TOP_K + k:TOP_K + k + 1], gt[:, k:k + 1], 0.0)
    g_hi = gmat.astype(BF16)
    g_lo = (gmat - g_hi.astype(F32)).astype(BF16)
    half = rows_lo.shape[1]
    for rows, sl in ((rows_lo, slice(0, half)), (rows_hi, slice(half, 2 * half))):
        f = (jnp.dot(g_hi, rows, preferred_element_type=F32)
             + jnp.dot(g_lo, rows, preferred_element_type=F32))
        o_ref[:, sl] = x_ref[:, sl] + g2_ref[0][:, sl] * f


def _combine(y, tabs, sloc, gates, x, g2, tiles_per_mod):
    t, d = x.shape
    tm = ROUTE_TM
    nt = t // tm
    return pl.pallas_call(
        functools.partial(_combine_kernel, n=nt),
        out_shape=jax.ShapeDtypeStruct((t, d), F32),
        grid=(nt,),
        in_specs=[pl.BlockSpec((1, 1, LANES), lambda i: (i, 0, 0), memory_space=pltpu.SMEM),
                  pl.BlockSpec((1, 1, LANES), lambda i: (jnp.minimum(i + 1, nt - 1), 0, 0),
                               memory_space=pltpu.SMEM),
                  pl.BlockSpec((tm, LANES), lambda i: (i, 0)),
                  pl.BlockSpec((tm, LANES), lambda i: (i, 0)),
                  pl.BlockSpec((tm, d), lambda i: (i, 0)),
                  pl.BlockSpec((1, 1, d), lambda i: (i // tiles_per_mod, 0, 0)),
                  pl.BlockSpec(memory_space=pl.ANY)],
        out_specs=pl.BlockSpec((tm, d), lambda i: (i, 0)),
        scratch_shapes=[pltpu.VMEM((2, MOE_STAGE, d // 2), U32), pltpu.SemaphoreType.DMA((2,))],
        compiler_params=_cparams(("arbitrary",)),
        name="moe_combine",
    )(tabs, tabs, sloc, gates, x, g2, y)


def _moe(streams, layer, g, router_w, router_b, w_gu_all, b_gu, w_down_all, b_down):
    d = streams[0][0].shape[1]
    tm = ROUTE_TM
    rw_f = jnp.zeros((d, LANES), F32).at[:, :N_EXPERTS].set(router_w.astype(F32))
    rw_hi = rw_f.astype(BF16)
    rw = (rw_hi, (rw_f - rw_hi.astype(F32)).astype(BF16))
    rb = jnp.full((1, LANES), NEG_BIG, F32).at[0, :N_EXPERTS].set(router_b.astype(F32))
    counts = jnp.zeros((1, LANES), I32)
    routed = []
    for x_tok, sh, sc, _, tpm in streams:
        h, route_i, gates, sloc_t, n_tile, seg_base, counts = _route(x_tok, g, sh, sc, tpm, rw, rb, counts)
        routed.append((h, route_i, gates, sloc_t, n_tile, seg_base))
    n_tiles = sum(x_tok.shape[0] for x_tok, *_ in streams) // tm
    n_max = n_tiles * tm * TOP_K + n_tiles * N_EXPERTS * (SUBLANES - 1) + N_EXPERTS * (MOE_TM - SUBLANES)
    n_max = (n_max + MOE_TM - 1) // MOE_TM * MOE_TM
    cnt = counts[0, :N_EXPERTS]
    cap = (cnt + MOE_TM - 1) // MOE_TM * MOE_TM
    total = jnp.sum(cap)
    base = jnp.cumsum(cap) - cap
    gap = jnp.concatenate([(cap - cnt) // SUBLANES, base + cnt, jnp.zeros((LANES - 2 * N_EXPERTS,), I32)])
    tabs = []
    for h, _, _, _, n_tile, seg_base in routed:
        nt = h.shape[0] // tm
        pad = _round_up8(n_tile[:, 0, :N_EXPERTS])
        off = jnp.cumsum(pad, axis=1) - pad
        dst = base[None, :] + seg_base[:, 0, :N_EXPERTS]
        nch = pad // SUBLANES
        tab = jnp.concatenate([nch, dst, off, jnp.sum(nch, axis=1, keepdims=True),
                               jnp.zeros((nt, LANES - 3 * N_EXPERTS - 1), I32)], axis=1)
        tabs.append(tab.astype(I32).reshape(nt, 1, LANES))
    free = n_max - total
    tail = jnp.stack([free // MOE_TM, (free % MOE_TM) // SUBLANES, total]).astype(I32)
    xs = _dispatch([r[0] for r in routed], jnp.concatenate(tabs, axis=0),
                   jnp.concatenate([r[3] for r in routed], axis=0), gap.astype(I32).reshape(1, 1, LANES),
                   tail, n_max)
    bg = b_gu[:, None, 0::2].astype(F32)
    bu = b_gu[:, None, 1::2].astype(F32)
    y = _gmm(xs, _tile_tables(cap, n_max), layer, w_gu_all, bg, bu, w_down_all, b_down[:, None, :].astype(F32))
    return [_combine(y, tab, r[1], r[2], x_tok, g2, tpm)
            for (x_tok, _, _, g2, tpm), r, tab in zip(streams, routed, tabs)]


def kernel(x, c, ctx, c_ctx, w_mod, b_mod, norm1_g, norm2_g, w_in, w_out, lru_conv_w, lru_conv_b, lru_wa, lru_ba, lru_wx, lru_bx, lru_lam, ssd_conv_w, ssd_conv_b, ssd_a_log, ssd_dt_bias, ssd_d, ssd_norm_g, da_q_norm, da_k_norm, da_lam_q, da_lam_k, da_subln_g, router_w, router_b, exp_w_gu, exp_b_gu, exp_w_down, exp_b_down):
    b, s, d = x.shape
    n_ctx = ctx.shape[1]
    depth = w_mod.shape[0]
    cos, sin = _rope_tables(s)
    x_lat, x_ctx = x, ctx
    mod_rows = jnp.zeros((2 * SUBLANES, d), F32).at[:b].set(c).at[b].set(c_ctx)

    for l in range(depth):
        need_ctx = l < depth - 1
        lam_init = 0.8 - 0.6 * math.exp(-0.3 * l)
        mod = _modulation(mod_rows, w_mod[l], b_mod[l])
        m_lat = mod[:b].reshape(b, 1, 6, d)
        m_ctx = jnp.broadcast_to(mod[b].reshape(1, 1, 6, d), (b, 1, 6, d))
        part = lambda m, j: m[:, :, j, :]

        w_in_l = _prep_w_in(w_in[l])
        qk_gains = (da_q_norm[l], da_k_norm[l], cos, sin)
        p_lat = _proj_in(x_lat, norm1_g[l], part(m_lat, 0), part(m_lat, 1), w_in_l, *qk_gains, rope=True, kc=ATT_KC)
        p_ctx = _proj_in(x_ctx, norm1_g[l], part(m_ctx, 0), part(m_ctx, 1), w_in_l, *qk_gains, rope=False, kc=ATT_KC)

        zl = jnp.zeros((b, LRU_WIDTH), F32)
        lru_args = (lru_wa[l], lru_ba[l], lru_wx[l], lru_bx[l], lru_lam[l])
        xc_ctx = _dwconv(p_ctx["lx"], lru_conv_w[l], lru_conv_b[l], act=False)
        hf_c, hb_c = _lru_scan(xc_ctx, *lru_args, zl, zl)
        xc_lat = _dwconv(p_lat["lx"], lru_conv_w[l], lru_conv_b[l], act=False)
        hf_l, hb_l = _lru_scan(xc_lat, *lru_args, hf_c[:, -1], hb_c[:, 0])

        zs = jnp.zeros((b, SSD_GROUPS, 2 * SSD_HEAD_DIM, SSD_STATE), F32)
        xbc_ctx = _dwconv(p_ctx["xbc"], ssd_conv_w[l], ssd_conv_b[l], act=True)
        yf_c, yb_c, sf_c, sb_c = _ssd_scan(xbc_ctx, p_ctx["dt"], ssd_a_log[l], ssd_dt_bias[l], zs, zs)
        xbc_lat = _dwconv(p_lat["xbc"], ssd_conv_w[l], ssd_conv_b[l], act=True)
        yf_l, yb_l, _, _ = _ssd_scan(xbc_lat, p_lat["dt"], ssd_a_log[l], ssd_dt_bias[l], sf_c, sb_c)

        lq = da_lam_q[l].astype(F32)
        lk = da_lam_k[l].astype(F32)
        lam = jnp.exp(jnp.sum(lq[0] * lk[0])) - jnp.exp(jnp.sum(lq[1] * lk[1])) + lam_init
        da_l = _diff_attn(p_lat["q"], [(p_ctx["k"], p_ctx["v"]), (p_lat["k"], p_lat["v"])], lam,
                          da_subln_g[l], lam_init)

        dvec = jnp.repeat(ssd_d[l].astype(F32), SSD_HEAD_DIM).reshape(1, SSD_WIDTH)
        ng = ssd_norm_g[l].astype(F32).reshape(1, SSD_WIDTH)
        w_out_l = w_out[l].astype(BF16)
        if need_ctx:
            da_c = _diff_attn(p_ctx["q"], [(p_ctx["k"], p_ctx["v"])], lam, da_subln_g[l], lam_init)
            x_ctx = _proj_out(x_ctx, p_ctx["lg"], hf_c, hb_c, yf_c, yb_c, xbc_ctx, p_ctx["z"], da_c,
                              dvec, ng, part(m_ctx, 2), w_out_l)
        x_lat = _proj_out(x_lat, p_lat["lg"], hf_l, hb_l, yf_l, yb_l, xbc_lat, p_lat["z"], da_l,
                          dvec, ng, part(m_lat, 2), w_out_l)

        moe_w = (router_w[l], router_b[l], exp_w_gu, exp_b_gu[l], exp_w_down, exp_b_down[l])
        streams = [(x_lat.reshape(-1, d), part(m_lat, 3), part(m_lat, 4), part(m_lat, 5), s // ROUTE_TM)]
        if need_ctx:
            streams.append((x_ctx.reshape(-1, d), part(m_ctx, 3), part(m_ctx, 4), part(m_ctx, 5), n_ctx // ROUTE_TM))
        outs = _moe(streams, l, norm2_g[l], *moe_w)
        x_lat = outs[0].reshape(b, s, d)
        if need_ctx:
            x_ctx = outs[1].reshape(b, n_ctx, d)
    return x_lat
```

```python
import functools
import math

import jax
import jax.numpy as jnp
from jax import lax
from jax.experimental import pallas as pl
from jax.experimental.pallas import tpu as pltpu

F32 = jnp.float32
BF16 = jnp.bfloat16
I32 = jnp.int32

GRID_W = 64
EPS = 1e-6
CONV_K = 4
LRU_HEADS = 4
LRU_HEAD_DIM = 64
LRU_WIDTH = 256
LRU_C = 8.0
SSD_HEADS = 4
SSD_HEAD_DIM = 64
SSD_WIDTH = 256
SSD_GROUPS = 2
SSD_STATE = 128
SSD_CHUNK = 128
SSD_CONV_DIM = 768
DA_HEADS = 4
DA_QK_DIM = 64
DA_V_DIM = 128
DA_QK_WIDTH = 512
DA_WIDTH = 512
DA_SCALE = DA_QK_DIM ** -0.5
ROPE_BASE = 10000.0
N_EXPERTS = 32
TOP_K = 4
SWIGLU_ALPHA = 1.702
SWIGLU_LIMIT = 7.0

LANES = 128
SUBLANES = 8
VMEM_LIMIT = 56 * 1024 * 1024

PROJ_TM = 512
ATT_TQ = 512
ATT_KC = 256
LOG2E = math.log2(math.e)
LRU_TL = 512
SSD_STEP_CHUNKS = 8
ROUTE_TM = 256
MOE_TM = 512
NEG_BIG = -1e30


def _cparams(sem):
    return pltpu.CompilerParams(dimension_semantics=sem, vmem_limit_bytes=VMEM_LIMIT)


def _sigmoid(x):
    return 0.5 * jnp.tanh(0.5 * x) + 0.5


def _silu(x):
    return x * _sigmoid(x)


def _softplus(x):
    return jnp.maximum(x, 0.0) + jnp.log(1.0 + jnp.exp(-jnp.abs(x)))


def _round_up8(n):
    return ((n + (SUBLANES - 1)) >> 3) << 3


def _mod_kernel(c_ref, w_ref, b_ref, o_ref):
    a = _silu(c_ref[...])
    o_ref[...] = jnp.dot(a, w_ref[...], preferred_element_type=F32,
                         precision=lax.Precision.HIGHEST) + b_ref[...]


def _modulation(rows, w, b):
    m, d = rows.shape
    n = w.shape[1]
    tn = 512
    return pl.pallas_call(
        _mod_kernel,
        out_shape=jax.ShapeDtypeStruct((m, n), F32),
        grid=(n // tn,),
        in_specs=[pl.BlockSpec((m, d), lambda j: (0, 0)),
                  pl.BlockSpec((d, tn), lambda j: (0, j)),
                  pl.BlockSpec((1, tn), lambda j: (0, j))],
        out_specs=pl.BlockSpec((m, tn), lambda j: (0, j)),
        compiler_params=_cparams(("arbitrary",)),
        name="modulation",
    )(rows, w, b.reshape(1, n))


def _norm_mod(x, g, sh, sc):
    ms = jnp.mean(x * x, axis=-1, keepdims=True)
    y = x * lax.rsqrt(ms + EPS) * g
    return y * (1.0 + sc) + sh


def _proj_in_kernel(x_ref, g_ref, sh_ref, sc_ref, w_ref, gq_ref, gk_ref, cos_ref, sin_ref, seg_ref, *o_refs,
                    splits, rope, kc, q_scale):
    h = _norm_mod(x_ref[0], g_ref[...], sh_ref[0], sc_ref[0]).astype(BF16)
    for (name, _, _), o_ref, (lo, hi) in zip(_IN_GROUPS, o_refs, splits):
        y = jnp.dot(h, w_ref[:, lo:hi], preferred_element_type=F32)
        if name == "q":
            _store_queries(o_ref, _qk_transform(y, gq_ref[...], cos_ref, sin_ref, seg_ref, rope, q_scale))
        elif name == "k":
            _store_keys(o_ref, _qk_transform(y, gk_ref[...], cos_ref, sin_ref, seg_ref, rope, 1.0), kc)
        else:
            o_ref[0] = y.astype(o_ref.dtype)


_IN_GROUPS = (("lg", 256, F32), ("lx", 256, F32), ("z", 256, F32), ("xbc", 768, F32),
              ("dt", 128, F32), ("q", 512, BF16), ("k", 512, BF16), ("v", 512, BF16))


def _prep_w_in(w_in):
    d = w_in.shape[0]
    lo = 2 * LRU_WIDTH + SSD_WIDTH + SSD_CONV_DIM
    hi = lo + 2 * SSD_HEADS
    dt = jnp.concatenate([w_in[:, lo:hi], jnp.zeros((d, LANES - (hi - lo)), w_in.dtype)], axis=1)
    w = jnp.concatenate([w_in[:, :lo], dt, w_in[:, hi:]], axis=1)
    return w.astype(BF16)


def _proj_in(x, g, sh, sc, w, gq, gk, cos, sin, rope, kc):
    b, s, d = x.shape
    tm = min(PROJ_TM, s)
    kc = min(kc, tm)
    splits, off = [], 0
    for _, wd, _ in _IN_GROUPS:
        splits.append((off, off + wd))
        off += wd
    out_shape, out_specs = [], []
    for name, wd, dt in _IN_GROUPS:
        if name == "q":
            out_shape.append(jax.ShapeDtypeStruct((b, 2 * DA_HEADS, s, DA_QK_DIM), dt))
            out_specs.append(pl.BlockSpec((1, 2 * DA_HEADS, tm, DA_QK_DIM), lambda bi, i: (bi, 0, i, 0)))
        elif name == "k":
            out_shape.append(jax.ShapeDtypeStruct((b, 2 * DA_HEADS, s // kc, DA_QK_DIM, kc), dt))
            out_specs.append(pl.BlockSpec((1, 2 * DA_HEADS, tm // kc, DA_QK_DIM, kc), lambda bi, i: (bi, 0, i, 0, 0)))
        else:
            out_shape.append(jax.ShapeDtypeStruct((b, s, wd), dt))
            out_specs.append(pl.BlockSpec((1, tm, wd), lambda bi, i: (bi, i, 0)))
    n = DA_QK_WIDTH
    tile_gain = lambda gg: jnp.tile(gg.astype(F32), n // DA_QK_DIM).reshape(1, n)
    idx = jnp.arange(n) // DA_QK_DIM
    seg = (idx[:, None] == idx[None, :]).astype(BF16)
    const = lambda bi, i: (0, 0)
    outs = pl.pallas_call(
        functools.partial(_proj_in_kernel, splits=tuple(splits), rope=rope, kc=kc, q_scale=DA_SCALE * LOG2E),
        out_shape=out_shape,
        grid=(b, s // tm),
        in_specs=[pl.BlockSpec((1, tm, d), lambda bi, i: (bi, i, 0)),
                  pl.BlockSpec((1, d), const),
                  pl.BlockSpec((1, 1, d), lambda bi, i: (bi, 0, 0)),
                  pl.BlockSpec((1, 1, d), lambda bi, i: (bi, 0, 0)),
                  pl.BlockSpec((d, off), const),
                  pl.BlockSpec((1, n), const), pl.BlockSpec((1, n), const),
                  pl.BlockSpec((tm, LANES), lambda bi, i: (i, 0)),
                  pl.BlockSpec((tm, LANES), lambda bi, i: (i, 0)),
                  pl.BlockSpec((n, n), const)],
        out_specs=out_specs,
        compiler_params=_cparams(("arbitrary", "arbitrary")),
        name="proj_in",
    )(x, g.reshape(1, d), sh, sc, w, tile_gain(gq), tile_gain(gk), cos, sin, seg)
    return {name: o for (name, _, _), o in zip(_IN_GROUPS, outs)}


def _dwconv_kernel(x_ref, w_ref, b_ref, o_ref, *, act):
    x = x_ref[0]
    s = x.shape[0]
    row = lax.broadcasted_iota(I32, x.shape, 0)
    acc = x * w_ref[2:3, :] + b_ref[...]
    acc = acc + jnp.where(row >= 2, pltpu.roll(x, 2, 0), 0.0) * w_ref[0:1, :]
    acc = acc + jnp.where(row >= 1, pltpu.roll(x, 1, 0), 0.0) * w_ref[1:2, :]
    acc = acc + jnp.where(row < s - 1, pltpu.roll(x, s - 1, 0), 0.0) * w_ref[3:4, :]
    if act:
        acc = _silu(acc)
    o_ref[0] = acc


def _dwconv(x, w, bias, act):
    b, s, c = x.shape
    tc = 256
    return pl.pallas_call(
        functools.partial(_dwconv_kernel, act=act),
        out_shape=jax.ShapeDtypeStruct((b, s, c), F32),
        grid=(b, c // tc),
        in_specs=[pl.BlockSpec((1, s, tc), lambda bi, j: (bi, 0, j)),
                  pl.BlockSpec((CONV_K, tc), lambda bi, j: (0, j)),
                  pl.BlockSpec((1, tc), lambda bi, j: (0, j))],
        out_specs=pl.BlockSpec((1, s, tc), lambda bi, j: (bi, 0, j)),
        compiler_params=_cparams(("arbitrary", "arbitrary")),
        name="dwconv",
    )(x, w, bias.reshape(1, c))


def _lru_coeffs(x, w_ref, b_ref, sp_ref, a_s, u_s):
    g = jnp.dot(x.astype(BF16), w_ref[...], preferred_element_type=F32) + b_ref[...]
    r = _sigmoid(g[:, :LRU_WIDTH])
    i = _sigmoid(g[:, LRU_WIDTH:])
    a = jnp.exp(-LRU_C * r * sp_ref[...])
    a_s[...] = a
    u_s[...] = jnp.sqrt(1.0 - a * a) * (i * x)


def _lru_kernel(xf_ref, xb_ref, wf_ref, wb_ref, bf_ref, bb_ref, spf_ref, spb_ref,
                h0f_ref, h0b_ref, hf_ref, hb_ref, af_s, uf_s, ab_s, ub_s, cf_s, cb_s):
    c = pl.program_id(1)
    tl = xf_ref.shape[1]
    ngroups = tl // SUBLANES
    shape = (SUBLANES, LRU_WIDTH)
    row = lax.broadcasted_iota(I32, shape, 0)

    @pl.when(c == 0)
    def _():
        cf_s[...] = jnp.broadcast_to(h0f_ref[0], shape)
        cb_s[...] = jnp.broadcast_to(h0b_ref[0], shape)

    _lru_coeffs(xf_ref[0], wf_ref, bf_ref, spf_ref, af_s, uf_s)
    _lru_coeffs(xb_ref[0], wb_ref, bb_ref, spb_ref, ab_s, ub_s)

    def group_scan(a, u, carry, reverse):
        for sh in (1, 2, 4):
            keep = (row < SUBLANES - sh) if reverse else (row >= sh)
            amt = SUBLANES - sh if reverse else sh
            a_sh = jnp.where(keep, pltpu.roll(a, amt, 0), 1.0)
            u_sh = jnp.where(keep, pltpu.roll(u, amt, 0), 0.0)
            u = a * u_sh + u
            a = a * a_sh
        return a * carry + u

    def step(gj, carry):
        cf, cb = carry
        slf = pl.ds(pl.multiple_of(gj * SUBLANES, SUBLANES), SUBLANES)
        slb = pl.ds(pl.multiple_of((ngroups - 1 - gj) * SUBLANES, SUBLANES), SUBLANES)
        hf = group_scan(af_s[slf, :], uf_s[slf, :], cf, False)
        hb = group_scan(ab_s[slb, :], ub_s[slb, :], cb, True)
        hf_ref[0, slf, :] = hf
        hb_ref[0, slb, :] = hb
        return (jnp.broadcast_to(hf[SUBLANES - 1:SUBLANES, :], shape),
                jnp.broadcast_to(hb[0:1, :], shape))

    cf, cb = lax.fori_loop(0, ngroups, step, (cf_s[...], cb_s[...]), unroll=2)
    cf_s[...] = cf
    cb_s[...] = cb


def _lru_gate_weights(wa, wx):
    def dense(wh):
        m = jnp.zeros((LRU_WIDTH, LRU_WIDTH), wh.dtype)
        for h in range(LRU_HEADS):
            lo = h * LRU_HEAD_DIM
            m = m.at[lo:lo + LRU_HEAD_DIM, lo:lo + LRU_HEAD_DIM].set(wh[h])
        return m
    return jnp.concatenate([dense(wa), dense(wx)], axis=1).astype(BF16)


def _lru_scan(xc, wa, ba, wx, bx, lam, h0f, h0b):
    b, s, w = xc.shape
    tl = min(LRU_TL, s)
    nc = s // tl
    wf = _lru_gate_weights(wa[0], wx[0])
    wb = _lru_gate_weights(wa[1], wx[1])
    bf = jnp.concatenate([ba[0], bx[0]]).reshape(1, 2 * w)
    bb = jnp.concatenate([ba[1], bx[1]]).reshape(1, 2 * w)
    sp = jax.nn.softplus(-lam.astype(F32))
    const = lambda bi, c: (0, 0)
    hf, hb = pl.pallas_call(
        _lru_kernel,
        out_shape=[jax.ShapeDtypeStruct((b, s, w), F32)] * 2,
        grid=(b, nc),
        in_specs=[pl.BlockSpec((1, tl, w), lambda bi, c: (bi, c, 0)),
                  pl.BlockSpec((1, tl, w), lambda bi, c: (bi, nc - 1 - c, 0)),
                  pl.BlockSpec((w, 2 * w), const), pl.BlockSpec((w, 2 * w), const),
                  pl.BlockSpec((1, 2 * w), const), pl.BlockSpec((1, 2 * w), const),
                  pl.BlockSpec((1, w), const), pl.BlockSpec((1, w), const),
                  pl.BlockSpec((1, 1, w), lambda bi, c: (bi, 0, 0)),
                  pl.BlockSpec((1, 1, w), lambda bi, c: (bi, 0, 0))],
        out_specs=[pl.BlockSpec((1, tl, w), lambda bi, c: (bi, c, 0)),
                   pl.BlockSpec((1, tl, w), lambda bi, c: (bi, nc - 1 - c, 0))],
        scratch_shapes=[pltpu.VMEM((tl, w), F32)] * 4
        + [pltpu.VMEM((SUBLANES, w), F32), pltpu.VMEM((SUBLANES, w), F32)],
        compiler_params=_cparams(("arbitrary", "arbitrary")),
        name="lru_scan",
    )(xc, xc, wf, wb, bf, bb, sp[0:1], sp[1:2], h0f.reshape(b, 1, w), h0b.reshape(b, 1, w))
    return hf, hb


def _ssd_direction(x_ref, dt_ref, row0, avec_ref, dtb_ref, state_s, y_ref, lane0, rev):
    t = SSD_CHUNK
    xbc = x_ref[0, row0:row0 + t, :]
    dtraw = dt_ref[0, row0:row0 + t, :]
    ri = lax.broadcasted_iota(I32, (t, t), 0)
    ci = lax.broadcasted_iota(I32, (t, t), 1)
    causal = (ci >= ri) if rev else (ci <= ri)
    cum_m = jnp.where(causal, 1.0, 0.0).astype(F32)

    dt = _softplus(dtraw + dtb_ref[...])
    a_all = dt * avec_ref[...]
    acs = jnp.dot(cum_m, a_all, preferred_element_type=F32, precision=lax.Precision.HIGHEST)
    acs_t = acs.T
    edge = 0 if rev else t - 1

    xs = xbc[:, :SSD_WIDTH]
    hrow = lax.broadcasted_iota(I32, (t, t), 0)
    for g in range(SSD_GROUPS):
        bm = xbc[:, SSD_WIDTH + g * SSD_STATE:SSD_WIDTH + (g + 1) * SSD_STATE]
        cm = xbc[:, SSD_WIDTH + (SSD_GROUPS + g) * SSD_STATE:SSD_WIDTH + (SSD_GROUPS + g + 1) * SSD_STATE]
        bm_b = bm.astype(BF16)
        cm_b = cm.astype(BF16)
        gmat = lax.dot_general(cm_b, bm_b, (((1,), (1,)), ((), ())), preferred_element_type=F32)
        s_prev = state_s[g]
        off_all = lax.dot_general(cm_b, s_prev.astype(BF16), (((1,), (1,)), ((), ())),
                                  preferred_element_type=F32)
        xd_parts, tot = [], []
        for hl in range(2):
            h = g * 2 + hl
            ln = lane0 + h
            col = jnp.broadcast_to(acs[:, ln:ln + 1], (t, t))
            rowv = acs_t[ln:ln + 1, :]
            lmat = jnp.where(causal, jnp.exp(col - rowv), 0.0)
            xh = xs[:, h * SSD_HEAD_DIM:(h + 1) * SSD_HEAD_DIM] * dt[:, ln:ln + 1]
            ydiag = jnp.dot((gmat * lmat).astype(BF16), xh.astype(BF16), preferred_element_type=F32)
            colh = col[:, :SSD_HEAD_DIM]
            yoff = off_all[:, hl * SSD_HEAD_DIM:(hl + 1) * SSD_HEAD_DIM] * jnp.exp(colh)
            y_ref[0, row0:row0 + t, h * SSD_HEAD_DIM:(h + 1) * SSD_HEAD_DIM] = ydiag + yoff
            total = acs[edge:edge + 1, ln:ln + 1]
            xd_parts.append(xh * jnp.exp(total - colh))
            tot.append(jnp.exp(total))
        xd = jnp.concatenate(xd_parts, axis=1)
        xd_t = xd.T
        st = jnp.dot(xd_t.astype(BF16), bm_b, preferred_element_type=F32)
        fac = jnp.where(hrow < SSD_HEAD_DIM, tot[0], tot[1])
        state_s[g] = fac * s_prev + st


def _ssd_kernel(xf_ref, dtf_ref, xb_ref, dtb_ref, avec_ref, bias_ref, h0f_ref, h0b_ref,
                yf_ref, yb_ref, sf_ref, sb_ref, stf_s, stb_s):
    c = pl.program_id(1)

    @pl.when(c == 0)
    def _():
        stf_s[...] = h0f_ref[0]
        stb_s[...] = h0b_ref[0]

    nsub = xf_ref.shape[1] // SSD_CHUNK
    for j in range(nsub):
        _ssd_direction(xf_ref, dtf_ref, j * SSD_CHUNK, avec_ref, bias_ref, stf_s, yf_ref, 0, False)
        _ssd_direction(xb_ref, dtb_ref, (nsub - 1 - j) * SSD_CHUNK, avec_ref, bias_ref, stb_s, yb_ref,
                       SSD_HEADS, True)
    sf_ref[0] = stf_s[...]
    sb_ref[0] = stb_s[...]


def _ssd_scan(xbc, dt, a_log, dt_bias, h0f, h0b):
    b, s, _ = xbc.shape
    t = min(SSD_STEP_CHUNKS * SSD_CHUNK, s)
    nc = s // t
    avec = jnp.zeros((1, LANES), F32).at[0, :2 * SSD_HEADS].set(-jnp.exp(a_log.astype(F32)).reshape(-1))
    bias = jnp.zeros((1, LANES), F32).at[0, :2 * SSD_HEADS].set(dt_bias.astype(F32).reshape(-1))
    st_shape = (b, SSD_GROUPS, 2 * SSD_HEAD_DIM, SSD_STATE)
    const = lambda bi, c: (0, 0)
    st_spec = pl.BlockSpec((1,) + st_shape[1:], lambda bi, c: (bi, 0, 0, 0))
    yf, yb, sf, sb = pl.pallas_call(
        _ssd_kernel,
        out_shape=[jax.ShapeDtypeStruct((b, s, SSD_WIDTH), F32)] * 2
        + [jax.ShapeDtypeStruct(st_shape, F32)] * 2,
        grid=(b, nc),
        in_specs=[pl.BlockSpec((1, t, SSD_CONV_DIM), lambda bi, c: (bi, c, 0)),
                  pl.BlockSpec((1, t, LANES), lambda bi, c: (bi, c, 0)),
                  pl.BlockSpec((1, t, SSD_CONV_DIM), lambda bi, c: (bi, nc - 1 - c, 0)),
                  pl.BlockSpec((1, t, LANES), lambda bi, c: (bi, nc - 1 - c, 0)),
                  pl.BlockSpec((1, LANES), const), pl.BlockSpec((1, LANES), const),
                  st_spec, st_spec],
        out_specs=[pl.BlockSpec((1, t, SSD_WIDTH), lambda bi, c: (bi, c, 0)),
                   pl.BlockSpec((1, t, SSD_WIDTH), lambda bi, c: (bi, nc - 1 - c, 0)),
                   st_spec, st_spec],
        scratch_shapes=[pltpu.VMEM(st_shape[1:], F32), pltpu.VMEM(st_shape[1:], F32)],
        compiler_params=_cparams(("arbitrary", "arbitrary")),
        name="ssd_scan",
    )(xbc, dt, xbc, dt, avec, bias, h0f, h0b)
    return yf, yb, sf, sb


def _qk_transform(x, g, cos_ref, sin_ref, seg_ref, rope, scale):
    sq = x * x
    hi = sq.astype(BF16)
    lo = (sq - hi.astype(F32)).astype(BF16)
    ssum = (jnp.dot(hi, seg_ref[...], preferred_element_type=F32)
            + jnp.dot(lo, seg_ref[...], preferred_element_type=F32))
    y = x * lax.rsqrt(ssum * (1.0 / DA_QK_DIM) + EPS) * g
    if rope:
        n = x.shape[1]
        lane = lax.broadcasted_iota(I32, x.shape, 1)
        first = (lane & (DA_QK_DIM - 1)) < DA_QK_DIM // 2
        partner = jnp.where(first, pltpu.roll(y, n - DA_QK_DIM // 2, 1), pltpu.roll(y, DA_QK_DIM // 2, 1))
        cosv = jnp.concatenate([cos_ref[...]] * (n // LANES), axis=1)
        sinv = jnp.concatenate([sin_ref[...]] * (n // LANES), axis=1)
        y = y * cosv + partner * sinv
    return y * scale


def _store_queries(o_ref, y):
    yb = y.astype(BF16)
    for j in range(2 * DA_HEADS):
        o_ref[0, j] = yb[:, j * DA_QK_DIM:(j + 1) * DA_QK_DIM]


def _store_keys(o_ref, y, kc):
    for h in range(DA_HEADS):
        head = y[:, h * 2 * DA_QK_DIM:(h + 1) * 2 * DA_QK_DIM]
        for cc in range(y.shape[0] // kc):
            t = head[cc * kc:(cc + 1) * kc, :].T.astype(BF16)
            o_ref[0, 2 * h, cc] = t[:DA_QK_DIM]
            o_ref[0, 2 * h + 1, cc] = t[DA_QK_DIM:]


def _rope_tables(s):
    pos = jnp.arange(s, dtype=I32)
    r = (pos // GRID_W).astype(F32)
    col = (pos % GRID_W).astype(F32)
    per_axis = DA_QK_DIM // 4
    inv = ROPE_BASE ** (-jnp.arange(per_axis, dtype=F32) / per_axis)
    ang = jnp.concatenate([r[:, None] * inv, col[:, None] * inv], axis=-1)
    cos, sin = jnp.cos(ang), jnp.sin(ang)
    cos64 = jnp.concatenate([cos, cos], axis=-1)
    sin64 = jnp.concatenate([-sin, sin], axis=-1)
    return jnp.concatenate([cos64, cos64], axis=-1), jnp.concatenate([sin64, sin64], axis=-1)


def _attn_kernel(lam_ref, q_ref, *refs, nseg, out_scale):
    lam = lam_ref[0, 0]
    segs = [(refs[2 * j], refs[2 * j + 1]) for j in range(nseg)]
    g_ref, o_ref = refs[2 * nseg], refs[2 * nseg + 1]
    tq = q_ref.shape[2]
    m = [jnp.full((tq, LANES), NEG_BIG, F32)] * 2
    l = [jnp.zeros((tq, LANES), F32)] * 2
    acc = [jnp.zeros((tq, DA_V_DIM), F32)] * 2

    for k_ref, v_ref in segs:
        nchunk, kc = k_ref.shape[2], k_ref.shape[4]
        for c in range(nchunk):
            v = v_ref[0, c * kc:(c + 1) * kc, :]
            probs, alphas = [], []
            for sub in range(2):
                s = jnp.dot(q_ref[0, sub], k_ref[0, sub, c], preferred_element_type=F32)
                mx = s[:, :LANES]
                for j in range(1, kc // LANES):
                    mx = jnp.maximum(mx, s[:, j * LANES:(j + 1) * LANES])
                m_new = jnp.maximum(m[sub], jnp.max(mx, axis=-1, keepdims=True))
                alpha = jnp.exp2(m[sub] - m_new)
                p = jnp.exp2(s - jnp.concatenate([m_new] * (kc // LANES), axis=1))
                lsum = alpha * l[sub]
                for j in range(kc // LANES):
                    lsum = lsum + p[:, j * LANES:(j + 1) * LANES]
                l[sub] = lsum
                m[sub] = m_new
                probs.append(p.astype(BF16))
                alphas.append(alpha)
            pv = jnp.dot(jnp.concatenate(probs, axis=0), v, preferred_element_type=F32)
            for sub in range(2):
                acc[sub] = alphas[sub] * acc[sub] + pv[sub * tq:(sub + 1) * tq]

    l1 = jnp.sum(l[0], axis=-1, keepdims=True)
    l2 = jnp.sum(l[1], axis=-1, keepdims=True)
    o = acc[0] * (1.0 / l1) - lam * (acc[1] * (1.0 / l2))
    ms = jnp.mean(o * o, axis=-1, keepdims=True)
    o_ref[0] = o * lax.rsqrt(ms + EPS) * g_ref[...] * out_scale


def _diff_attn(qp, segments, lam, g, lam_init):
    b, _, s, d = qp.shape
    tq = min(ATT_TQ, s)
    seg_specs, seg_args = [], []
    for kt, v in segments:
        nchunk, kc = kt.shape[2], kt.shape[4]
        seg_specs += [pl.BlockSpec((1, 2, nchunk, d, kc), lambda bi, h, i: (bi, h, 0, 0, 0)),
                      pl.BlockSpec((1, nchunk * kc, DA_V_DIM), lambda bi, h, i: (bi, 0, h))]
        seg_args += [kt, v]
    return pl.pallas_call(
        functools.partial(_attn_kernel, nseg=len(segments), out_scale=1.0 - lam_init),
        out_shape=jax.ShapeDtypeStruct((b, s, DA_WIDTH), F32),
        grid=(b, DA_HEADS, s // tq),
        in_specs=[pl.BlockSpec(memory_space=pltpu.SMEM),
                  pl.BlockSpec((1, 2, tq, d), lambda bi, h, i: (bi, h, i, 0))] + seg_specs
        + [pl.BlockSpec((1, DA_V_DIM), lambda bi, h, i: (0, 0))],
        out_specs=pl.BlockSpec((1, tq, DA_V_DIM), lambda bi, h, i: (bi, i, h)),
        compiler_params=_cparams(("arbitrary", "arbitrary", "arbitrary")),
        name="diff_attn",
    )(lam.reshape(1, 1).astype(F32), qp, *seg_args, g.reshape(1, DA_V_DIM).astype(F32))


def _gelu_tanh(x):
    return 0.5 * x * (1.0 + jnp.tanh(math.sqrt(2.0 / math.pi) * (x + 0.044715 * x * x * x)))


def _proj_out_kernel(x_ref, lg_ref, hf_ref, hb_ref, yf_ref, yb_ref, xs_ref, z_ref, da_ref,
                     dvec_ref, ng_ref, gate_ref, w_ref, o_ref):
    lru = _gelu_tanh(lg_ref[0]) * (hf_ref[0] + hb_ref[0])
    y = yf_ref[0] + yb_ref[0] + dvec_ref[...] * xs_ref[0]
    y = y * _silu(z_ref[0])
    ssd = y * lax.rsqrt(jnp.mean(y * y, axis=-1, keepdims=True) + EPS) * ng_ref[...]
    o = jnp.dot(lru.astype(BF16), w_ref[0:LRU_WIDTH, :], preferred_element_type=F32)
    o = o + jnp.dot(ssd.astype(BF16), w_ref[LRU_WIDTH:LRU_WIDTH + SSD_WIDTH, :], preferred_element_type=F32)
    o = o + jnp.dot(da_ref[0].astype(BF16), w_ref[LRU_WIDTH + SSD_WIDTH:, :], preferred_element_type=F32)
    o_ref[0] = x_ref[0] + gate_ref[0] * o


def _proj_out(x, lg, hf, hb, yf, yb, xbc, z, da, dvec, ng, gate, w):
    b, s, d = x.shape
    tm = min(PROJ_TM, s)
    tok = lambda wd: pl.BlockSpec((1, tm, wd), lambda bi, i: (bi, i, 0))
    const2 = lambda bi, i: (0, 0)
    return pl.pallas_call(
        _proj_out_kernel,
        out_shape=jax.ShapeDtypeStruct((b, s, d), F32),
        grid=(b, s // tm),
        in_specs=[tok(d), tok(LRU_WIDTH), tok(LRU_WIDTH), tok(LRU_WIDTH), tok(SSD_WIDTH),
                  tok(SSD_WIDTH), tok(SSD_WIDTH), tok(SSD_WIDTH), tok(DA_WIDTH),
                  pl.BlockSpec((1, SSD_WIDTH), const2), pl.BlockSpec((1, SSD_WIDTH), const2),
                  pl.BlockSpec((1, 1, d), lambda bi, i: (bi, 0, 0)),
                  pl.BlockSpec(w.shape, const2)],
        out_specs=tok(d),
        compiler_params=_cparams(("arbitrary", "arbitrary")),
        name="proj_out",
    )(x, lg, hf, hb, yf, yb, xbc, z, da, dvec, ng, gate, w)


def _route_kernel(x_ref, g_ref, sh_ref, sc_ref, rwh_ref, rwl_ref, rb_ref, c0_ref, h_ref, ri_ref, gt_ref,
                  slt_ref, n_ref, base_ref, cnt_ref, carry_s):
    i = pl.program_id(0)
    tm = x_ref.shape[0]

    @pl.when(i == 0)
    def _():
        carry_s[...] = c0_ref[...]

    h = _norm_mod(x_ref[...], g_ref[...], sh_ref[0], sc_ref[0])
    h_hi = h.astype(BF16)
    h_ref[...] = h_hi
    h_lo = (h - h_hi.astype(F32)).astype(BF16)
    logits = (jnp.dot(h_hi, rwh_ref[...], preferred_element_type=F32)
              + jnp.dot(h_hi, rwl_ref[...], preferred_element_type=F32)
              + jnp.dot(h_lo, rwh_ref[...], preferred_element_type=F32)) + rb_ref[...]
    lane = lax.broadcasted_iota(I32, logits.shape, 1)
    lane_f = lane.astype(F32)
    vals, idxs = [], []
    cur = logits
    for _ in range(TOP_K):
        m = jnp.max(cur, axis=-1, keepdims=True)
        idx = jnp.min(jnp.where(cur >= m, lane_f, float(LANES)), axis=-1, keepdims=True).astype(I32)
        vals.append(m)
        idxs.append(idx)
        cur = jnp.where(lane == idx, 2.0 * NEG_BIG, cur)
    es = [jnp.exp(v - vals[0]) for v in vals]
    inv = 1.0 / (es[0] + es[1] + es[2] + es[3])
    onehot = jnp.zeros(logits.shape, F32)
    for idx in idxs:
        onehot = onehot + jnp.where(lane == idx, 1.0, 0.0)
    ri = lax.broadcasted_iota(I32, (tm, tm), 0)
    ci = lax.broadcasted_iota(I32, (tm, tm), 1)
    strict = jnp.where(ci < ri, 1.0, 0.0).astype(BF16)
    before = jnp.dot(strict, onehot.astype(BF16), preferred_element_type=F32)
    n_tile = jnp.sum(onehot, axis=0, keepdims=True).astype(I32)
    pad_tile = _round_up8(n_tile)
    li = lax.broadcasted_iota(I32, (LANES, LANES), 0)
    lj = lax.broadcasted_iota(I32, (LANES, LANES), 1)
    below = jnp.where(li < lj, 1.0, 0.0).astype(BF16)
    pad_rows = jnp.broadcast_to(pad_tile.astype(F32), (SUBLANES, LANES)).astype(BF16)
    off = jnp.dot(pad_rows, below, preferred_element_type=F32)[0:1, :]
    place = before + off
    ri_out = jnp.zeros(logits.shape, I32)
    gt_out = jnp.zeros(logits.shape, F32)
    sl_f = jnp.full(logits.shape, -1.0, F32)
    for k in range(TOP_K):
        sloc = jnp.sum(jnp.where(lane == idxs[k], place, 0.0), axis=-1, keepdims=True)
        ri_out = ri_out + jnp.where(lane == k, idxs[k], 0) + jnp.where(lane == TOP_K + k, sloc.astype(I32), 0)
        sl_f = jnp.where(lane == k, sloc, sl_f)
        gt_out = gt_out + jnp.where(lane == k, es[k] * inv, 0.0)
    ri_ref[...] = ri_out
    gt_ref[...] = gt_out
    slt_ref[0] = sl_f.T[0:SUBLANES, :].astype(I32)
    n_ref[0] = n_tile
    base_ref[0] = carry_s[...]
    carry_s[...] = carry_s[...] + pad_tile
    cnt_ref[...] = carry_s[...]


def _route(x, g, sh, sc, tiles_per_mod, rw, rb, counts0):
    t, d = x.shape
    tm = ROUTE_TM
    tile = lambda wd: pl.BlockSpec((tm, wd), lambda i: (i, 0))
    const = lambda i: (0, 0)
    mod = pl.BlockSpec((1, 1, d), lambda i: (i // tiles_per_mod, 0, 0))
    nt = t // tm
    per_tile = pl.BlockSpec((1, 1, LANES), lambda i: (i, 0, 0))
    return pl.pallas_call(
        _route_kernel,
        out_shape=[jax.ShapeDtypeStruct((t, d), BF16), jax.ShapeDtypeStruct((t, LANES), I32),
                   jax.ShapeDtypeStruct((t, LANES), F32), jax.ShapeDtypeStruct((nt, SUBLANES, tm), I32),
                   jax.ShapeDtypeStruct((nt, 1, LANES), I32),
                   jax.ShapeDtypeStruct((nt, 1, LANES), I32), jax.ShapeDtypeStruct((1, LANES), I32)],
        grid=(nt,),
        in_specs=[tile(d), pl.BlockSpec((1, d), const), mod, mod,
                  pl.BlockSpec((d, LANES), const), pl.BlockSpec((d, LANES), const),
                  pl.BlockSpec((1, LANES), const), pl.BlockSpec((1, LANES), const)],
        out_specs=[tile(d), tile(LANES), tile(LANES), pl.BlockSpec((1, SUBLANES, tm), lambda i: (i, 0, 0)),
                   per_tile, per_tile, pl.BlockSpec((1, LANES), const)],
        scratch_shapes=[pltpu.VMEM((1, LANES), I32)],
        compiler_params=_cparams(("arbitrary",)),
        name="moe_route",
    )(x, g.reshape(1, d), sh, sc, rw[0], rw[1], rb, counts0)


MOE_STAGE = -(-(ROUTE_TM * TOP_K + N_EXPERTS * (SUBLANES - 1)) // 256) * 256
_TAB_N, _TAB_DST, _TAB_OFF, _TAB_TOTAL = 0, N_EXPERTS, 2 * N_EXPERTS, 3 * N_EXPERTS


U32 = jnp.uint32
_HI16 = 0xFFFF0000


def _pack_halves(a, is_bf16=False):
    n = a.shape[1] // 2
    lo, hi = a[:, :n], a[:, n:]
    if not is_bf16:
        lo, hi = lo.astype(BF16).astype(F32), hi.astype(BF16).astype(F32)
    return (pltpu.bitcast(lo, U32) >> 16) | (pltpu.bitcast(hi, U32) & U32(_HI16))


def _unpack_halves(u):
    lo = pltpu.bitcast(u << 16, F32).astype(BF16)
    hi = pltpu.bitcast(u & U32(_HI16), F32).astype(BF16)
    return lo, hi


_BIG_SHIFT = 2
_BIG_CHUNKS = 1 << _BIG_SHIFT


def _chunk_copies(tab_ref, stage_ref, rows_hbm, sem, to_hbm):
    def copy(off, dst, rows):
        s = stage_ref.at[pl.ds(pl.multiple_of(off, SUBLANES), rows), :]
        r = rows_hbm.at[pl.ds(pl.multiple_of(dst, SUBLANES), rows), :]
        if to_hbm:
            pltpu.make_async_copy(s, r, sem).start()
        else:
            pltpu.make_async_copy(r, s, sem).start()

    def expert(e, carry):
        n = tab_ref[0, 0, _TAB_N + e]
        dst = tab_ref[0, 0, _TAB_DST + e]
        off = tab_ref[0, 0, _TAB_OFF + e]
        big_rows = _BIG_CHUNKS * SUBLANES
        n_big = lax.shift_right_logical(n, _BIG_SHIFT)

        def big(j, c):
            copy(off + j * big_rows, dst + j * big_rows, big_rows)
            return c

        def small(j, c):
            copy(off + (n_big * _BIG_CHUNKS + j) * SUBLANES, dst + (n_big * _BIG_CHUNKS + j) * SUBLANES, SUBLANES)
            return c

        lax.fori_loop(0, n_big, big, 0)
        lax.fori_loop(0, n & (_BIG_CHUNKS - 1), small, 0)
        return carry
    lax.fori_loop(0, N_EXPERTS, expert, 0)


def _chunk_waits(tab_ref, stage_ref, rows_hbm, sem):
    rows = tab_ref[0, 0, _TAB_TOTAL] * SUBLANES
    pltpu.make_async_copy(rows_hbm.at[pl.ds(0, rows), :], stage_ref.at[pl.ds(0, rows), :], sem).wait()


def _dispatch_kernel(tab_ref, tprev_ref, gap_ref, tail_ref, sloc_ref, *refs, tiles):
    h_refs = refs[:len(tiles)]
    xs_out, stage_s, zero_s, sems = refs[len(tiles):]
    i = pl.program_id(0)
    n_steps = sum(tiles)
    slot = i % 2
    row = lax.broadcasted_iota(I32, (MOE_STAGE, ROUTE_TM), 0)
    perm = jnp.zeros((MOE_STAGE, ROUTE_TM), F32)
    for k in range(TOP_K):
        perm = perm + jnp.where(row == sloc_ref[0, k:k + 1, :], 1.0, 0.0)
    perm = perm.astype(BF16)
    off = 0
    for h_ref, nt in zip(h_refs, tiles):
        @pl.when((i >= off) & (i < off + nt))
        def _(h_ref=h_ref):
            stage_s[slot] = _pack_halves(jnp.dot(perm, h_ref[...], preferred_element_type=F32), is_bf16=True)
        off += nt
    _chunk_copies(tab_ref, stage_s.at[slot], xs_out, sems.at[slot], True)

    @pl.when(i > 0)
    def _():
        _chunk_waits(tprev_ref, stage_s.at[1 - slot], xs_out, sems.at[1 - slot])

    @pl.when(i == n_steps - 1)
    def _():
        _chunk_waits(tab_ref, stage_s.at[slot], xs_out, sems.at[slot])
        sem = sems.at[slot]
        zero_s[...] = jnp.zeros(zero_s.shape, U32)
        zr = zero_s.shape[0]
        n_big, n_small, start = tail_ref[0], tail_ref[1], tail_ref[2]

        def big(j, c):
            pltpu.make_async_copy(zero_s, xs_out.at[pl.ds(pl.multiple_of(start + j * zr, SUBLANES), zr), :], sem).start()
            return c

        def small(j, c):
            r0 = pl.multiple_of(start + n_big * zr + j * SUBLANES, SUBLANES)
            pltpu.make_async_copy(zero_s.at[pl.ds(0, SUBLANES), :], xs_out.at[pl.ds(r0, SUBLANES), :], sem).start()
            return c

        def big_wait(j, c):
            pltpu.make_async_copy(zero_s, xs_out.at[pl.ds(0, zr), :], sem).wait()
            return c

        def small_wait(j, c):
            pltpu.make_async_copy(zero_s.at[pl.ds(0, SUBLANES), :], xs_out.at[pl.ds(0, SUBLANES), :], sem).wait()
            return c

        lax.fori_loop(0, n_big, big, 0)
        lax.fori_loop(0, n_small, small, 0)
        lax.fori_loop(0, n_big, big_wait, 0)
        lax.fori_loop(0, n_small, small_wait, 0)

        _chunk_copies(gap_ref, zero_s, xs_out, sem, True)

        def gap_wait(e, c):
            rows = gap_ref[0, 0, _TAB_N + e] * SUBLANES

            @pl.when(rows > 0)
            def _():
                pltpu.make_async_copy(zero_s.at[pl.ds(0, rows), :], xs_out.at[pl.ds(0, rows), :], sem).wait()
            return c
        lax.fori_loop(0, N_EXPERTS, gap_wait, 0)


def _dispatch(hs, tabs, slocs, gaps, tail, n_max):
    d = hs[0].shape[1]
    tm = ROUTE_TM
    tiles = tuple(h.shape[0] // tm for h in hs)
    specs, off = [], 0
    for nt in tiles:
        specs.append(pl.BlockSpec((tm, d), lambda i, off=off, nt=nt: (jnp.clip(i - off, 0, nt - 1), 0)))
        off += nt
    return pl.pallas_call(
        functools.partial(_dispatch_kernel, tiles=tiles),
        out_shape=jax.ShapeDtypeStruct((n_max, d // 2), U32),
        grid=(sum(tiles),),
        in_specs=[pl.BlockSpec((1, 1, LANES), lambda i: (i, 0, 0), memory_space=pltpu.SMEM),
                  pl.BlockSpec((1, 1, LANES), lambda i: (jnp.maximum(i - 1, 0), 0, 0), memory_space=pltpu.SMEM),
                  pl.BlockSpec((1, 1, LANES), lambda i: (0, 0, 0), memory_space=pltpu.SMEM),
                  pl.BlockSpec(memory_space=pltpu.SMEM),
                  pl.BlockSpec((1, SUBLANES, tm), lambda i: (i, 0, 0))] + specs,
        out_specs=pl.BlockSpec(memory_space=pl.ANY),
        scratch_shapes=[pltpu.VMEM((2, MOE_STAGE, d // 2), U32), pltpu.VMEM((MOE_TM, d // 2), U32),
                        pltpu.SemaphoreType.DMA((2,))],
        compiler_params=_cparams(("arbitrary",)),
        name="moe_dispatch",
    )(tabs, tabs, gaps, tail, slocs, *hs)


_SPLIT_W = 2 * LANES


def _gmm_kernel(te_ref, live_ref, newe_ref, xs_ref, wgu_ref, perm_ref, bg_ref, bu_ref,
                wdf_ref, bd_ref, y_ref, wg_ref, wu_ref, wd_ref):
    del te_ref
    v = pl.program_id(0)
    live = live_ref[v] > 0

    @pl.when((newe_ref[v] > 0) & live)
    def _():
        for j in range(wgu_ref.shape[1] // _SPLIT_W):
            r = jnp.dot(wgu_ref[:, j * _SPLIT_W:(j + 1) * _SPLIT_W].astype(BF16), perm_ref[...],
                        preferred_element_type=F32)
            wg_ref[:, j * LANES:(j + 1) * LANES] = r[:, :LANES].astype(BF16)
            wu_ref[:, j * LANES:(j + 1) * LANES] = r[:, LANES:].astype(BF16)
        wd_ref[...] = wdf_ref[...].astype(BF16)

    @pl.when(live)
    def _():
        x_lo, x_hi = _unpack_halves(xs_ref[...])
        half = x_lo.shape[1]
        gate = (jnp.dot(x_lo, wg_ref[:half, :], preferred_element_type=F32)
                + jnp.dot(x_hi, wg_ref[half:, :], preferred_element_type=F32)) + bg_ref[...]
        up = (jnp.dot(x_lo, wu_ref[:half, :], preferred_element_type=F32)
              + jnp.dot(x_hi, wu_ref[half:, :], preferred_element_type=F32)) + bu_ref[...]
        gate = jnp.minimum(gate, SWIGLU_LIMIT)
        up = jnp.clip(up, -SWIGLU_LIMIT, SWIGLU_LIMIT)
        glu = gate * _sigmoid(SWIGLU_ALPHA * gate)
        act = ((up + 1.0) * glu).astype(BF16)
        y_ref[...] = _pack_halves(jnp.dot(act, wd_ref[...], preferred_element_type=F32) + bd_ref[...])

    @pl.when(jnp.logical_not(live))
    def _():
        y_ref[...] = jnp.zeros(y_ref.shape, U32)


def _gmm(xs, tiles, layer, w_gu, bg, bu, w_down, bd):
    n_rows = xs.shape[0]
    _, _, d, de2 = w_gu.shape
    de = de2 // 2
    tm = MOE_TM
    nv = n_rows // tm
    src = jnp.arange(_SPLIT_W)
    dst = jnp.where(src % 2 == 0, src // 2, LANES + src // 2)
    perm = (dst[:, None] == jnp.arange(_SPLIT_W)[None, :]).astype(BF16)
    ex = lambda shape: pl.BlockSpec((None,) + shape, lambda v, te, lv, ne: (te[v], 0, 0))
    lay = lambda shape: pl.BlockSpec((None, None) + shape, lambda v, te, lv, ne: (layer, te[v], 0, 0))
    rows = pl.BlockSpec((tm, d // 2), lambda v, te, lv, ne: (v, 0))
    grid_spec = pltpu.PrefetchScalarGridSpec(
        num_scalar_prefetch=3,
        grid=(nv,),
        in_specs=[rows, lay((d, de2)), pl.BlockSpec((_SPLIT_W, _SPLIT_W), lambda v, *_: (0, 0)),
                  ex((1, de)), ex((1, de)), lay((de, d)), ex((1, d))],
        out_specs=rows,
        scratch_shapes=[pltpu.VMEM((d, de), BF16), pltpu.VMEM((d, de), BF16), pltpu.VMEM((de, d), BF16)],
    )
    return pl.pallas_call(
        _gmm_kernel,
        out_shape=jax.ShapeDtypeStruct((n_rows, d // 2), U32),
        grid_spec=grid_spec,
        compiler_params=_cparams(("arbitrary",)),
        name="moe_experts",
    )(*tiles, xs, w_gu, perm, bg, bu, w_down, bd)


def _tile_tables(cap, n_rows):
    tm = MOE_TM
    end = jnp.cumsum(cap)
    start = jnp.arange(n_rows // tm, dtype=I32) * tm
    e = jnp.minimum(jnp.sum((end[None, :] <= start[:, None]).astype(I32), axis=1), N_EXPERTS - 1).astype(I32)
    live = (start < end[N_EXPERTS - 1]).astype(I32)
    new_e = (e != jnp.concatenate([jnp.full((1,), -1, I32), e[:-1]])).astype(I32)
    return e, live, new_e


def _combine_kernel(tcur_ref, tnext_ref, sloc_ref, gt_ref, x_ref, g2_ref, y_hbm, o_ref, stage_s, sems, *, n):
    i = pl.program_id(0)
    slot = i % 2

    @pl.when(i == 0)
    def _():
        stage_s[...] = jnp.zeros(stage_s.shape, U32)
        _chunk_copies(tcur_ref, stage_s.at[0], y_hbm, sems.at[0], False)

    @pl.when(i + 1 < n)
    def _():
        _chunk_copies(tnext_ref, stage_s.at[1 - slot], y_hbm, sems.at[1 - slot], False)

    _chunk_waits(tcur_ref, stage_s.at[slot], y_hbm, sems.at[slot])
    rows_lo, rows_hi = _unpack_halves(stage_s[slot])
    lane = lax.broadcasted_iota(I32, (x_ref.shape[0], MOE_STAGE), 1)
    gt = gt_ref[...]
    sloc = sloc_ref[...]
    gmat = jnp.zeros(lane.shape, F32)
    for k in range(TOP_K):
        gmat = gmat + jnp.where(lane == sloc[:, TOP_K + k:TOP_K + k + 1], gt[:, k:k + 1], 0.0)
    g_hi = gmat.astype(BF16)
    g_lo = (gmat - g_hi.astype(F32)).astype(BF16)
    half = rows_lo.shape[1]
    for rows, sl in ((rows_lo, slice(0, half)), (rows_hi, slice(half, 2 * half))):
        f = (jnp.dot(g_hi, rows, preferred_element_type=F32)
             + jnp.dot(g_lo, rows, preferred_element_type=F32))
        o_ref[:, sl] = x_ref[:, sl] + g2_ref[0][:, sl] * f


def _combine(y, tabs, sloc, gates, x, g2, tiles_per_mod):
    t, d = x.shape
    tm = ROUTE_TM
    nt = t // tm
    return pl.pallas_call(
        functools.partial(_combine_kernel, n=nt),
        out_shape=jax.ShapeDtypeStruct((t, d), F32),
        grid=(nt,),
        in_specs=[pl.BlockSpec((1, 1, LANES), lambda i: (i, 0, 0), memory_space=pltpu.SMEM),
                  pl.BlockSpec((1, 1, LANES), lambda i: (jnp.minimum(i + 1, nt - 1), 0, 0),
                               memory_space=pltpu.SMEM),
                  pl.BlockSpec((tm, LANES), lambda i: (i, 0)),
                  pl.BlockSpec((tm, LANES), lambda i: (i, 0)),
                  pl.BlockSpec((tm, d), lambda i: (i, 0)),
                  pl.BlockSpec((1, 1, d), lambda i: (i // tiles_per_mod, 0, 0)),
                  pl.BlockSpec(memory_space=pl.ANY)],
        out_specs=pl.BlockSpec((tm, d), lambda i: (i, 0)),
        scratch_shapes=[pltpu.VMEM((2, MOE_STAGE, d // 2), U32), pltpu.SemaphoreType.DMA((2,))],
        compiler_params=_cparams(("arbitrary",)),
        name="moe_combine",
    )(tabs, tabs, sloc, gates, x, g2, y)


def _moe(streams, layer, g, router_w, router_b, w_gu_all, b_gu, w_down_all, b_down):
    d = streams[0][0].shape[1]
    tm = ROUTE_TM
    rw_f = jnp.zeros((d, LANES), F32).at[:, :N_EXPERTS].set(router_w.astype(F32))
    rw_hi = rw_f.astype(BF16)
    rw = (rw_hi, (rw_f - rw_hi.astype(F32)).astype(BF16))
    rb = jnp.full((1, LANES), NEG_BIG, F32).at[0, :N_EXPERTS].set(router_b.astype(F32))
    counts = jnp.zeros((1, LANES), I32)
    routed = []
    for x_tok, sh, sc, _, tpm in streams:
        h, route_i, gates, sloc_t, n_tile, seg_base, counts = _route(x_tok, g, sh, sc, tpm, rw, rb, counts)
        routed.append((h, route_i, gates, sloc_t, n_tile, seg_base))
    n_tiles = sum(x_tok.shape[0] for x_tok, *_ in streams) // tm
    n_max = n_tiles * tm * TOP_K + n_tiles * N_EXPERTS * (SUBLANES - 1) + N_EXPERTS * (MOE_TM - SUBLANES)
    n_max = (n_max + MOE_TM - 1) // MOE_TM * MOE_TM
    cnt = counts[0, :N_EXPERTS]
    cap = (cnt + MOE_TM - 1) // MOE_TM * MOE_TM
    total = jnp.sum(cap)
    base = jnp.cumsum(cap) - cap
    gap = jnp.concatenate([(cap - cnt) // SUBLANES, base + cnt, jnp.zeros((LANES - 2 * N_EXPERTS,), I32)])
    tabs = []
    for h, _, _, _, n_tile, seg_base in routed:
        nt = h.shape[0] // tm
        pad = _round_up8(n_tile[:, 0, :N_EXPERTS])
        off = jnp.cumsum(pad, axis=1) - pad
        dst = base[None, :] + seg_base[:, 0, :N_EXPERTS]
        nch = pad // SUBLANES
        tab = jnp.concatenate([nch, dst, off, jnp.sum(nch, axis=1, keepdims=True),
                               jnp.zeros((nt, LANES - 3 * N_EXPERTS - 1), I32)], axis=1)
        tabs.append(tab.astype(I32).reshape(nt, 1, LANES))
    free = n_max - total
    tail = jnp.stack([free // MOE_TM, (free % MOE_TM) // SUBLANES, total]).astype(I32)
    xs = _dispatch([r[0] for r in routed], jnp.concatenate(tabs, axis=0),
                   jnp.concatenate([r[3] for r in routed], axis=0), gap.astype(I32).reshape(1, 1, LANES),
                   tail, n_max)
    bg = b_gu[:, None, 0::2].astype(F32)
    bu = b_gu[:, None, 1::2].astype(F32)
    y = _gmm(xs, _tile_tables(cap, n_max), layer, w_gu_all, bg, bu, w_down_all, b_down[:, None, :].astype(F32))
    return [_combine(y, tab, r[1], r[2], x_tok, g2, tpm)
            for (x_tok, _, _, g2, tpm), r, tab in zip(streams, routed, tabs)]


def kernel(x, c, ctx, c_ctx, w_mod, b_mod, norm1_g, norm2_g, w_in, w_out, lru_conv_w, lru_conv_b, lru_wa, lru_ba, lru_wx, lru_bx, lru_lam, ssd_conv_w, ssd_conv_b, ssd_a_log, ssd_dt_bias, ssd_d, ssd_norm_g, da_q_norm, da_k_norm, da_lam_q, da_lam_k, da_subln_g, router_w, router_b, exp_w_gu, exp_b_gu, exp_w_down, exp_b_down):
    b, s, d = x.shape
    n_ctx = ctx.shape[1]
    depth = w_mod.shape[0]
    cos, sin = _rope_tables(s)
    x_lat, x_ctx = x, ctx
    mod_rows = jnp.zeros((2 * SUBLANES, d), F32).at[:b].set(c).at[b].set(c_ctx)

    for l in range(depth):
        need_ctx = l < depth - 1
        lam_init = 0.8 - 0.6 * math.exp(-0.3 * l)
        mod = _modulation(mod_rows, w_mod[l], b_mod[l])
        m_lat = mod[:b].reshape(b, 1, 6, d)
        m_ctx = jnp.broadcast_to(mod[b].reshape(1, 1, 6, d), (b, 1, 6, d))
        part = lambda m, j: m[:, :, j, :]

        w_in_l = _prep_w_in(w_in[l])
        qk_gains = (da_q_norm[l], da_k_norm[l], cos, sin)
        p_lat = _proj_in(x_lat, norm1_g[l], part(m_lat, 0), part(m_lat, 1), w_in_l, *qk_gains, rope=True, kc=ATT_KC)
        p_ctx = _proj_in(x_ctx, norm1_g[l], part(m_ctx, 0), part(m_ctx, 1), w_in_l, *qk_gains, rope=False, kc=ATT_KC)

        zl = jnp.zeros((b, LRU_WIDTH), F32)
        lru_args = (lru_wa[l], lru_ba[l], lru_wx[l], lru_bx[l], lru_lam[l])
        xc_ctx = _dwconv(p_ctx["lx"], lru_conv_w[l], lru_conv_b[l], act=False)
        hf_c, hb_c = _lru_scan(xc_ctx, *lru_args, zl, zl)
        xc_lat = _dwconv(p_lat["lx"], lru_conv_w[l], lru_conv_b[l], act=False)
        hf_l, hb_l = _lru_scan(xc_lat, *lru_args, hf_c[:, -1], hb_c[:, 0])

        zs = jnp.zeros((b, SSD_GROUPS, 2 * SSD_HEAD_DIM, SSD_STATE), F32)
        xbc_ctx = _dwconv(p_ctx["xbc"], ssd_conv_w[l], ssd_conv_b[l], act=True)
        yf_c, yb_c, sf_c, sb_c = _ssd_scan(xbc_ctx, p_ctx["dt"], ssd_a_log[l], ssd_dt_bias[l], zs, zs)
        xbc_lat = _dwconv(p_lat["xbc"], ssd_conv_w[l], ssd_conv_b[l], act=True)
        yf_l, yb_l, _, _ = _ssd_scan(xbc_lat, p_lat["dt"], ssd_a_log[l], ssd_dt_bias[l], sf_c, sb_c)

        lq = da_lam_q[l].astype(F32)
        lk = da_lam_k[l].astype(F32)
        lam = jnp.exp(jnp.sum(lq[0] * lk[0])) - jnp.exp(jnp.sum(lq[1] * lk[1])) + lam_init
        da_l = _diff_attn(p_lat["q"], [(p_ctx["k"], p_ctx["v"]), (p_lat["k"], p_lat["v"])], lam,
                          da_subln_g[l], lam_init)

        dvec = jnp.repeat(ssd_d[l].astype(F32), SSD_HEAD_DIM).reshape(1, SSD_WIDTH)
        ng = ssd_norm_g[l].astype(F32).reshape(1, SSD_WIDTH)
        w_out_l = w_out[l].astype(BF16)
        if need_ctx:
            da_c = _diff_attn(p_ctx["q"], [(p_ctx["k"], p_ctx["v"])], lam, da_subln_g[l], lam_init)
            x_ctx = _proj_out(x_ctx, p_ctx["lg"], hf_c, hb_c, yf_c, yb_c, xbc_ctx, p_ctx["z"], da_c,
                              dvec, ng, part(m_ctx, 2), w_out_l)
        x_lat = _proj_out(x_lat, p_lat["lg"], hf_l, hb_l, yf_l, yb_l, xbc_lat, p_lat["z"], da_l,
                          dvec, ng, part(m_lat, 2), w_out_l)

        moe_w = (router_w[l], router_b[l], exp_w_gu, exp_b_gu[l], exp_w_down, exp_b_down[l])
        streams = [(x_lat.reshape(-1, d), part(m_lat, 3), part(m_lat, 4), part(m_lat, 5), s // ROUTE_TM)]
        if need_ctx:
            streams.append((x_ctx.reshape(-1, d), part(m_ctx, 3), part(m_ctx, 4), part(m_ctx, 5), n_ctx // ROUTE_TM))
        outs = _moe(streams, l, norm2_g[l], *moe_w)
        x_lat = outs[0].reshape(b, s, d)
        if need_ctx:
            x_ctx = outs[1].reshape(b, n_ctx, d)
    return x_lat
```

```python
import functools
import math

import jax
import jax.numpy as jnp
from jax import lax
from jax.experimental import pallas as pl
from jax.experimental.pallas import tpu as pltpu

F32 = jnp.float32
BF16 = jnp.bfloat16
I32 = jnp.int32

GRID_W = 64
EPS = 1e-6
CONV_K = 4
LRU_HEADS = 4
LRU_HEAD_DIM = 64
LRU_WIDTH = 256
LRU_C = 8.0
SSD_HEADS = 4
SSD_HEAD_DIM = 64
SSD_WIDTH = 256
SSD_GROUPS = 2
SSD_STATE = 128
SSD_CHUNK = 128
SSD_CONV_DIM = 768
DA_HEADS = 4
DA_QK_DIM = 64
DA_V_DIM = 128
DA_QK_WIDTH = 512
DA_WIDTH = 512
DA_SCALE = DA_QK_DIM ** -0.5
ROPE_BASE = 10000.0
N_EXPERTS = 32
TOP_K = 4
SWIGLU_ALPHA = 1.702
SWIGLU_LIMIT = 7.0

LANES = 128
SUBLANES = 8
VMEM_LIMIT = 56 * 1024 * 1024

PROJ_TM = 1024
ATT_TQ = 512
ATT_KC = 256
LOG2E = math.log2(math.e)
LRU_TL = 512
SSD_STEP_CHUNKS = 8
ROUTE_TM = 256
MOE_TM = 512
NEG_BIG = -1e30


def _cparams(sem):
    return pltpu.CompilerParams(dimension_semantics=sem, vmem_limit_bytes=VMEM_LIMIT)


def _sigmoid(x):
    return 0.5 * jnp.tanh(0.5 * x) + 0.5


def _silu(x):
    return x * _sigmoid(x)


def _softplus(x):
    return jnp.maximum(x, 0.0) + jnp.log(1.0 + jnp.exp(-jnp.abs(x)))


def _round_up8(n):
    return ((n + (SUBLANES - 1)) >> 3) << 3


def _mod_kernel(c_ref, w_ref, b_ref, o_ref):
    a = _silu(c_ref[...])
    o_ref[...] = jnp.dot(a, w_ref[...], preferred_element_type=F32,
                         precision=lax.Precision.HIGHEST) + b_ref[...]


def _modulation(rows, w, b):
    m, d = rows.shape
    n = w.shape[1]
    tn = 512
    return pl.pallas_call(
        _mod_kernel,
        out_shape=jax.ShapeDtypeStruct((m, n), F32),
        grid=(n // tn,),
        in_specs=[pl.BlockSpec((m, d), lambda j: (0, 0)),
                  pl.BlockSpec((d, tn), lambda j: (0, j)),
                  pl.BlockSpec((1, tn), lambda j: (0, j))],
        out_specs=pl.BlockSpec((m, tn), lambda j: (0, j)),
        compiler_params=_cparams(("arbitrary",)),
        name="modulation",
    )(rows, w, b.reshape(1, n))


def _norm_mod(x, g, sh, sc):
    ms = jnp.mean(x * x, axis=-1, keepdims=True)
    y = x * lax.rsqrt(ms + EPS) * g
    return y * (1.0 + sc) + sh


def _proj_in_kernel(x_ref, g_ref, sh_ref, sc_ref, w_ref, gq_ref, gk_ref, cos_ref, sin_ref, seg_ref, *o_refs,
                    splits, rope, kc, q_scale):
    h = _norm_mod(x_ref[0], g_ref[...], sh_ref[0], sc_ref[0]).astype(BF16)
    for (name, _, _), o_ref, (lo, hi) in zip(_IN_GROUPS, o_refs, splits):
        y = jnp.dot(h, w_ref[:, lo:hi], preferred_element_type=F32)
        if name == "q":
            _store_queries(o_ref, _qk_transform(y, gq_ref[...], cos_ref, sin_ref, seg_ref, rope, q_scale))
        elif name == "k":
            _store_keys(o_ref, _qk_transform(y, gk_ref[...], cos_ref, sin_ref, seg_ref, rope, 1.0), kc)
        else:
            o_ref[0] = y.astype(o_ref.dtype)


_IN_GROUPS = (("lg", 256, F32), ("lx", 256, F32), ("z", 256, F32), ("xbc", 768, F32),
              ("dt", 128, F32), ("q", 512, BF16), ("k", 512, BF16), ("v", 512, BF16))


def _prep_w_in(w_in):
    d = w_in.shape[0]
    lo = 2 * LRU_WIDTH + SSD_WIDTH + SSD_CONV_DIM
    hi = lo + 2 * SSD_HEADS
    dt = jnp.concatenate([w_in[:, lo:hi], jnp.zeros((d, LANES - (hi - lo)), w_in.dtype)], axis=1)
    w = jnp.concatenate([w_in[:, :lo], dt, w_in[:, hi:]], axis=1)
    return w.astype(BF16)


def _proj_in(x, g, sh, sc, w, gq, gk, cos, sin, rope, kc):
    b, s, d = x.shape
    tm = min(PROJ_TM, s)
    kc = min(kc, tm)
    splits, off = [], 0
    for _, wd, _ in _IN_GROUPS:
        splits.append((off, off + wd))
        off += wd
    out_shape, out_specs = [], []
    for name, wd, dt in _IN_GROUPS:
        if name == "q":
            out_shape.append(jax.ShapeDtypeStruct((b, 2 * DA_HEADS, s, DA_QK_DIM), dt))
            out_specs.append(pl.BlockSpec((1, 2 * DA_HEADS, tm, DA_QK_DIM), lambda bi, i: (bi, 0, i, 0)))
        elif name == "k":
            out_shape.append(jax.ShapeDtypeStruct((b, 2 * DA_HEADS, s // kc, DA_QK_DIM, kc), dt))
            out_specs.append(pl.BlockSpec((1, 2 * DA_HEADS, tm // kc, DA_QK_DIM, kc), lambda bi, i: (bi, 0, i, 0, 0)))
        else:
            out_shape.append(jax.ShapeDtypeStruct((b, s, wd), dt))
            out_specs.append(pl.BlockSpec((1, tm, wd), lambda bi, i: (bi, i, 0)))
    n = DA_QK_WIDTH
    tile_gain = lambda gg: jnp.tile(gg.astype(F32), n // DA_QK_DIM).reshape(1, n)
    idx = jnp.arange(n) // DA_QK_DIM
    seg = (idx[:, None] == idx[None, :]).astype(BF16)
    const = lambda bi, i: (0, 0)
    outs = pl.pallas_call(
        functools.partial(_proj_in_kernel, splits=tuple(splits), rope=rope, kc=kc, q_scale=DA_SCALE * LOG2E),
        out_shape=out_shape,
        grid=(b, s // tm),
        in_specs=[pl.BlockSpec((1, tm, d), lambda bi, i: (bi, i, 0)),
                  pl.BlockSpec((1, d), const),
                  pl.BlockSpec((1, 1, d), lambda bi, i: (bi, 0, 0)),
                  pl.BlockSpec((1, 1, d), lambda bi, i: (bi, 0, 0)),
                  pl.BlockSpec((d, off), const),
                  pl.BlockSpec((1, n), const), pl.BlockSpec((1, n), const),
                  pl.BlockSpec((tm, LANES), lambda bi, i: (i, 0)),
                  pl.BlockSpec((tm, LANES), lambda bi, i: (i, 0)),
                  pl.BlockSpec((n, n), const)],
        out_specs=out_specs,
        compiler_params=_cparams(("arbitrary", "arbitrary")),
        name="proj_in",
    )(x, g.reshape(1, d), sh, sc, w, tile_gain(gq), tile_gain(gk), cos, sin, seg)
    return {name: o for (name, _, _), o in zip(_IN_GROUPS, outs)}


def _dwconv_kernel(x_ref, w_ref, b_ref, o_ref, *, act):
    x = x_ref[0]
    s = x.shape[0]
    row = lax.broadcasted_iota(I32, x.shape, 0)
    acc = x * w_ref[2:3, :] + b_ref[...]
    acc = acc + jnp.where(row >= 2, pltpu.roll(x, 2, 0), 0.0) * w_ref[0:1, :]
    acc = acc + jnp.where(row >= 1, pltpu.roll(x, 1, 0), 0.0) * w_ref[1:2, :]
    acc = acc + jnp.where(row < s - 1, pltpu.roll(x, s - 1, 0), 0.0) * w_ref[3:4, :]
    if act:
        acc = _silu(acc)
    o_ref[0] = acc


def _dwconv(x, w, bias, act):
    b, s, c = x.shape
    tc = 256
    return pl.pallas_call(
        functools.partial(_dwconv_kernel, act=act),
        out_shape=jax.ShapeDtypeStruct((b, s, c), F32),
        grid=(b, c // tc),
        in_specs=[pl.BlockSpec((1, s, tc), lambda bi, j: (bi, 0, j)),
                  pl.BlockSpec((CONV_K, tc), lambda bi, j: (0, j)),
                  pl.BlockSpec((1, tc), lambda bi, j: (0, j))],
        out_specs=pl.BlockSpec((1, s, tc), lambda bi, j: (bi, 0, j)),
        compiler_params=_cparams(("arbitrary", "arbitrary")),
        name="dwconv",
    )(x, w, bias.reshape(1, c))


def _lru_coeffs(x, w_ref, b_ref, sp_ref, a_s, u_s):
    g = jnp.dot(x.astype(BF16), w_ref[...], preferred_element_type=F32) + b_ref[...]
    r = _sigmoid(g[:, :LRU_WIDTH])
    i = _sigmoid(g[:, LRU_WIDTH:])
    a = jnp.exp(-LRU_C * r * sp_ref[...])
    a_s[...] = a
    u_s[...] = jnp.sqrt(1.0 - a * a) * (i * x)


def _lru_kernel(xf_ref, xb_ref, wf_ref, wb_ref, bf_ref, bb_ref, spf_ref, spb_ref,
                h0f_ref, h0b_ref, hf_ref, hb_ref, af_s, uf_s, ab_s, ub_s, cf_s, cb_s):
    c = pl.program_id(1)
    tl = xf_ref.shape[1]
    ngroups = tl // SUBLANES
    shape = (SUBLANES, LRU_WIDTH)
    row = lax.broadcasted_iota(I32, shape, 0)

    @pl.when(c == 0)
    def _():
        cf_s[...] = jnp.broadcast_to(h0f_ref[0], shape)
        cb_s[...] = jnp.broadcast_to(h0b_ref[0], shape)

    _lru_coeffs(xf_ref[0], wf_ref, bf_ref, spf_ref, af_s, uf_s)
    _lru_coeffs(xb_ref[0], wb_ref, bb_ref, spb_ref, ab_s, ub_s)

    def group_scan(a, u, carry, reverse):
        for sh in (1, 2, 4):
            keep = (row < SUBLANES - sh) if reverse else (row >= sh)
            amt = SUBLANES - sh if reverse else sh
            a_sh = jnp.where(keep, pltpu.roll(a, amt, 0), 1.0)
            u_sh = jnp.where(keep, pltpu.roll(u, amt, 0), 0.0)
            u = a * u_sh + u
            a = a * a_sh
        return a * carry + u

    def step(gj, carry):
        cf, cb = carry
        slf = pl.ds(pl.multiple_of(gj * SUBLANES, SUBLANES), SUBLANES)
        slb = pl.ds(pl.multiple_of((ngroups - 1 - gj) * SUBLANES, SUBLANES), SUBLANES)
        hf = group_scan(af_s[slf, :], uf_s[slf, :], cf, False)
        hb = group_scan(ab_s[slb, :], ub_s[slb, :], cb, True)
        hf_ref[0, slf, :] = hf
        hb_ref[0, slb, :] = hb
        return (jnp.broadcast_to(hf[SUBLANES - 1:SUBLANES, :], shape),
                jnp.broadcast_to(hb[0:1, :], shape))

    cf, cb = lax.fori_loop(0, ngroups, step, (cf_s[...], cb_s[...]), unroll=2)
    cf_s[...] = cf
    cb_s[...] = cb


def _lru_gate_weights(wa, wx):
    def dense(wh):
        m = jnp.zeros((LRU_WIDTH, LRU_WIDTH), wh.dtype)
        for h in range(LRU_HEADS):
            lo = h * LRU_HEAD_DIM
            m = m.at[lo:lo + LRU_HEAD_DIM, lo:lo + LRU_HEAD_DIM].set(wh[h])
        return m
    return jnp.concatenate([dense(wa), dense(wx)], axis=1).astype(BF16)


def _lru_scan(xc, wa, ba, wx, bx, lam, h0f, h0b):
    b, s, w = xc.shape
    tl = min(LRU_TL, s)
    nc = s // tl
    wf = _lru_gate_weights(wa[0], wx[0])
    wb = _lru_gate_weights(wa[1], wx[1])
    bf = jnp.concatenate([ba[0], bx[0]]).reshape(1, 2 * w)
    bb = jnp.concatenate([ba[1], bx[1]]).reshape(1, 2 * w)
    sp = jax.nn.softplus(-lam.astype(F32))
    const = lambda bi, c: (0, 0)
    hf, hb = pl.pallas_call(
        _lru_kernel,
        out_shape=[jax.ShapeDtypeStruct((b, s, w), F32)] * 2,
        grid=(b, nc),
        in_specs=[pl.BlockSpec((1, tl, w), lambda bi, c: (bi, c, 0)),
                  pl.BlockSpec((1, tl, w), lambda bi, c: (bi, nc - 1 - c, 0)),
                  pl.BlockSpec((w, 2 * w), const), pl.BlockSpec((w, 2 * w), const),
                  pl.BlockSpec((1, 2 * w), const), pl.BlockSpec((1, 2 * w), const),
                  pl.BlockSpec((1, w), const), pl.BlockSpec((1, w), const),
                  pl.BlockSpec((1, 1, w), lambda bi, c: (bi, 0, 0)),
                  pl.BlockSpec((1, 1, w), lambda bi, c: (bi, 0, 0))],
        out_specs=[pl.BlockSpec((1, tl, w), lambda bi, c: (bi, c, 0)),
                   pl.BlockSpec((1, tl, w), lambda bi, c: (bi, nc - 1 - c, 0))],
        scratch_shapes=[pltpu.VMEM((tl, w), F32)] * 4
        + [pltpu.VMEM((SUBLANES, w), F32), pltpu.VMEM((SUBLANES, w), F32)],
        compiler_params=_cparams(("arbitrary", "arbitrary")),
        name="lru_scan",
    )(xc, xc, wf, wb, bf, bb, sp[0:1], sp[1:2], h0f.reshape(b, 1, w), h0b.reshape(b, 1, w))
    return hf, hb


def _ssd_direction(x_ref, dt_ref, row0, avec_ref, dtb_ref, state_s, y_ref, lane0, rev):
    t = SSD_CHUNK
    xbc = x_ref[0, row0:row0 + t, :]
    dtraw = dt_ref[0, row0:row0 + t, :]
    ri = lax.broadcasted_iota(I32, (t, t), 0)
    ci = lax.broadcasted_iota(I32, (t, t), 1)
    causal = (ci >= ri) if rev else (ci <= ri)
    cum_m = jnp.where(causal, 1.0, 0.0).astype(F32)

    dt = _softplus(dtraw + dtb_ref[...])
    a_all = dt * avec_ref[...]
    acs = jnp.dot(cum_m, a_all, preferred_element_type=F32, precision=lax.Precision.HIGHEST)
    acs_t = acs.T
    edge = 0 if rev else t - 1

    xs = xbc[:, :SSD_WIDTH]
    hrow = lax.broadcasted_iota(I32, (t, t), 0)
    for g in range(SSD_GROUPS):
        bm = xbc[:, SSD_WIDTH + g * SSD_STATE:SSD_WIDTH + (g + 1) * SSD_STATE]
        cm = xbc[:, SSD_WIDTH + (SSD_GROUPS + g) * SSD_STATE:SSD_WIDTH + (SSD_GROUPS + g + 1) * SSD_STATE]
        bm_b = bm.astype(BF16)
        cm_b = cm.astype(BF16)
        gmat = lax.dot_general(cm_b, bm_b, (((1,), (1,)), ((), ())), preferred_element_type=F32)
        s_prev = state_s[g]
        off_all = lax.dot_general(cm_b, s_prev.astype(BF16), (((1,), (1,)), ((), ())),
                                  preferred_element_type=F32)
        xd_parts, tot = [], []
        for hl in range(2):
            h = g * 2 + hl
            ln = lane0 + h
            col = jnp.broadcast_to(acs[:, ln:ln + 1], (t, t))
            rowv = acs_t[ln:ln + 1, :]
            lmat = jnp.where(causal, jnp.exp(col - rowv), 0.0)
            xh = xs[:, h * SSD_HEAD_DIM:(h + 1) * SSD_HEAD_DIM] * dt[:, ln:ln + 1]
            ydiag = jnp.dot((gmat * lmat).astype(BF16), xh.astype(BF16), preferred_element_type=F32)
            colh = col[:, :SSD_HEAD_DIM]
            yoff = off_all[:, hl * SSD_HEAD_DIM:(hl + 1) * SSD_HEAD_DIM] * jnp.exp(colh)
            y_ref[0, row0:row0 + t, h * SSD_HEAD_DIM:(h + 1) * SSD_HEAD_DIM] = ydiag + yoff
            total = acs[edge:edge + 1, ln:ln + 1]
            xd_parts.append(xh * jnp.exp(total - colh))
            tot.append(jnp.exp(total))
        xd = jnp.concatenate(xd_parts, axis=1)
        xd_t = xd.T
        st = jnp.dot(xd_t.astype(BF16), bm_b, preferred_element_type=F32)
        fac = jnp.where(hrow < SSD_HEAD_DIM, tot[0], tot[1])
        state_s[g] = fac * s_prev + st


def _ssd_kernel(xf_ref, dtf_ref, xb_ref, dtb_ref, avec_ref, bias_ref, h0f_ref, h0b_ref,
                yf_ref, yb_ref, sf_ref, sb_ref, stf_s, stb_s):
    c = pl.program_id(1)

    @pl.when(c == 0)
    def _():
        stf_s[...] = h0f_ref[0]
        stb_s[...] = h0b_ref[0]

    nsub = xf_ref.shape[1] // SSD_CHUNK
    for j in range(nsub):
        _ssd_direction(xf_ref, dtf_ref, j * SSD_CHUNK, avec_ref, bias_ref, stf_s, yf_ref, 0, False)
        _ssd_direction(xb_ref, dtb_ref, (nsub - 1 - j) * SSD_CHUNK, avec_ref, bias_ref, stb_s, yb_ref,
                       SSD_HEADS, True)
    sf_ref[0] = stf_s[...]
    sb_ref[0] = stb_s[...]


def _ssd_scan(xbc, dt, a_log, dt_bias, h0f, h0b):
    b, s, _ = xbc.shape
    t = min(SSD_STEP_CHUNKS * SSD_CHUNK, s)
    nc = s // t
    avec = jnp.zeros((1, LANES), F32).at[0, :2 * SSD_HEADS].set(-jnp.exp(a_log.astype(F32)).reshape(-1))
    bias = jnp.zeros((1, LANES), F32).at[0, :2 * SSD_HEADS].set(dt_bias.astype(F32).reshape(-1))
    st_shape = (b, SSD_GROUPS, 2 * SSD_HEAD_DIM, SSD_STATE)
    const = lambda bi, c: (0, 0)
    st_spec = pl.BlockSpec((1,) + st_shape[1:], lambda bi, c: (bi, 0, 0, 0))
    yf, yb, sf, sb = pl.pallas_call(
        _ssd_kernel,
        out_shape=[jax.ShapeDtypeStruct((b, s, SSD_WIDTH), F32)] * 2
        + [jax.ShapeDtypeStruct(st_shape, F32)] * 2,
        grid=(b, nc),
        in_specs=[pl.BlockSpec((1, t, SSD_CONV_DIM), lambda bi, c: (bi, c, 0)),
                  pl.BlockSpec((1, t, LANES), lambda bi, c: (bi, c, 0)),
                  pl.BlockSpec((1, t, SSD_CONV_DIM), lambda bi, c: (bi, nc - 1 - c, 0)),
                  pl.BlockSpec((1, t, LANES), lambda bi, c: (bi, nc - 1 - c, 0)),
                  pl.BlockSpec((1, LANES), const), pl.BlockSpec((1, LANES), const),
                  st_spec, st_spec],
        out_specs=[pl.BlockSpec((1, t, SSD_WIDTH), lambda bi, c: (bi, c, 0)),
                   pl.BlockSpec((1, t, SSD_WIDTH), lambda bi, c: (bi, nc - 1 - c, 0)),
                   st_spec, st_spec],
        scratch_shapes=[pltpu.VMEM(st_shape[1:], F32), pltpu.VMEM(st_shape[1:], F32)],
        compiler_params=_cparams(("arbitrary", "arbitrary")),
        name="ssd_scan",
    )(xbc, dt, xbc, dt, avec, bias, h0f, h0b)
    return yf, yb, sf, sb


def _qk_transform(x, g, cos_ref, sin_ref, seg_ref, rope, scale):
    sq = x * x
    hi = sq.astype(BF16)
    lo = (sq - hi.astype(F32)).astype(BF16)
    ssum = (jnp.dot(hi, seg_ref[...], preferred_element_type=F32)
            + jnp.dot(lo, seg_ref[...], preferred_element_type=F32))
    y = x * lax.rsqrt(ssum * (1.0 / DA_QK_DIM) + EPS) * g
    if rope:
        n = x.shape[1]
        lane = lax.broadcasted_iota(I32, x.shape, 1)
        first = (lane & (DA_QK_DIM - 1)) < DA_QK_DIM // 2
        partner = jnp.where(first, pltpu.roll(y, n - DA_QK_DIM // 2, 1), pltpu.roll(y, DA_QK_DIM // 2, 1))
        cosv = jnp.concatenate([cos_ref[...]] * (n // LANES), axis=1)
        sinv = jnp.concatenate([sin_ref[...]] * (n // LANES), axis=1)
        y = y * cosv + partner * sinv
    return y * scale


def _store_queries(o_ref, y):
    yb = y.astype(BF16)
    for j in range(2 * DA_HEADS):
        o_ref[0, j] = yb[:, j * DA_QK_DIM:(j + 1) * DA_QK_DIM]


def _store_keys(o_ref, y, kc):
    for h in range(DA_HEADS):
        head = y[:, h * 2 * DA_QK_DIM:(h + 1) * 2 * DA_QK_DIM]
        for cc in range(y.shape[0] // kc):
            t = head[cc * kc:(cc + 1) * kc, :].T.astype(BF16)
            o_ref[0, 2 * h, cc] = t[:DA_QK_DIM]
            o_ref[0, 2 * h + 1, cc] = t[DA_QK_DIM:]


def _rope_tables(s):
    pos = jnp.arange(s, dtype=I32)
    r = (pos // GRID_W).astype(F32)
    col = (pos % GRID_W).astype(F32)
    per_axis = DA_QK_DIM // 4
    inv = ROPE_BASE ** (-jnp.arange(per_axis, dtype=F32) / per_axis)
    ang = jnp.concatenate([r[:, None] * inv, col[:, None] * inv], axis=-1)
    cos, sin = jnp.cos(ang), jnp.sin(ang)
    cos64 = jnp.concatenate([cos, cos], axis=-1)
    sin64 = jnp.concatenate([-sin, sin], axis=-1)
    return jnp.concatenate([cos64, cos64], axis=-1), jnp.concatenate([sin64, sin64], axis=-1)


def _attn_kernel(lam_ref, q_ref, *refs, nseg, out_scale):
    lam = lam_ref[0, 0]
    segs = [(refs[2 * j], refs[2 * j + 1]) for j in range(nseg)]
    g_ref, o_ref = refs[2 * nseg], refs[2 * nseg + 1]
    tq = q_ref.shape[2]
    m = [jnp.full((tq, LANES), NEG_BIG, F32)] * 2
    l = [jnp.zeros((tq, LANES), F32)] * 2
    acc = [jnp.zeros((tq, DA_V_DIM), F32)] * 2

    for k_ref, v_ref in segs:
        nchunk, kc = k_ref.shape[2], k_ref.shape[4]
        for c in range(nchunk):
            v = v_ref[0, c * kc:(c + 1) * kc, :]
            probs, alphas = [], []
            for sub in range(2):
                s = jnp.dot(q_ref[0, sub], k_ref[0, sub, c], preferred_element_type=F32)
                mx = s[:, :LANES]
                for j in range(1, kc // LANES):
                    mx = jnp.maximum(mx, s[:, j * LANES:(j + 1) * LANES])
                m_new = jnp.maximum(m[sub], jnp.max(mx, axis=-1, keepdims=True))
                alpha = jnp.exp2(m[sub] - m_new)
                p = jnp.exp2(s - jnp.concatenate([m_new] * (kc // LANES), axis=1))
                lsum = alpha * l[sub]
                for j in range(kc // LANES):
                    lsum = lsum + p[:, j * LANES:(j + 1) * LANES]
                l[sub] = lsum
                m[sub] = m_new
                probs.append(p.astype(BF16))
                alphas.append(alpha)
            pv = jnp.dot(jnp.concatenate(probs, axis=0), v, preferred_element_type=F32)
            for sub in range(2):
                acc[sub] = alphas[sub] * acc[sub] + pv[sub * tq:(sub + 1) * tq]

    l1 = jnp.sum(l[0], axis=-1, keepdims=True)
    l2 = jnp.sum(l[1], axis=-1, keepdims=True)
    o = acc[0] * (1.0 / l1) - lam * (acc[1] * (1.0 / l2))
    ms = jnp.mean(o * o, axis=-1, keepdims=True)
    o_ref[0] = o * lax.rsqrt(ms + EPS) * g_ref[...] * out_scale


def _diff_attn(qp, segments, lam, g, lam_init):
    b, _, s, d = qp.shape
    tq = min(ATT_TQ, s)
    seg_specs, seg_args = [], []
    for kt, v in segments:
        nchunk, kc = kt.shape[2], kt.shape[4]
        seg_specs += [pl.BlockSpec((1, 2, nchunk, d, kc), lambda bi, h, i: (bi, h, 0, 0, 0)),
                      pl.BlockSpec((1, nchunk * kc, DA_V_DIM), lambda bi, h, i: (bi, 0, h))]
        seg_args += [kt, v]
    return pl.pallas_call(
        functools.partial(_attn_kernel, nseg=len(segments), out_scale=1.0 - lam_init),
        out_shape=jax.ShapeDtypeStruct((b, s, DA_WIDTH), F32),
        grid=(b, DA_HEADS, s // tq),
        in_specs=[pl.BlockSpec(memory_space=pltpu.SMEM),
                  pl.BlockSpec((1, 2, tq, d), lambda bi, h, i: (bi, h, i, 0))] + seg_specs
        + [pl.BlockSpec((1, DA_V_DIM), lambda bi, h, i: (0, 0))],
        out_specs=pl.BlockSpec((1, tq, DA_V_DIM), lambda bi, h, i: (bi, i, h)),
        compiler_params=_cparams(("arbitrary", "arbitrary", "arbitrary")),
        name="diff_attn",
    )(lam.reshape(1, 1).astype(F32), qp, *seg_args, g.reshape(1, DA_V_DIM).astype(F32))


def _gelu_tanh(x):
    return 0.5 * x * (1.0 + jnp.tanh(math.sqrt(2.0 / math.pi) * (x + 0.044715 * x * x * x)))


def _proj_out_kernel(x_ref, lg_ref, hf_ref, hb_ref, yf_ref, yb_ref, xs_ref, z_ref, da_ref,
                     dvec_ref, ng_ref, gate_ref, w_ref, o_ref):
    lru = _gelu_tanh(lg_ref[0]) * (hf_ref[0] + hb_ref[0])
    y = yf_ref[0] + yb_ref[0] + dvec_ref[...] * xs_ref[0]
    y = y * _silu(z_ref[0])
    ssd = y * lax.rsqrt(jnp.mean(y * y, axis=-1, keepdims=True) + EPS) * ng_ref[...]
    o = jnp.dot(lru.astype(BF16), w_ref[0:LRU_WIDTH, :], preferred_element_type=F32)
    o = o + jnp.dot(ssd.astype(BF16), w_ref[LRU_WIDTH:LRU_WIDTH + SSD_WIDTH, :], preferred_element_type=F32)
    o = o + jnp.dot(da_ref[0].astype(BF16), w_ref[LRU_WIDTH + SSD_WIDTH:, :], preferred_element_type=F32)
    o_ref[0] = x_ref[0] + gate_ref[0] * o


def _proj_out(x, lg, hf, hb, yf, yb, xbc, z, da, dvec, ng, gate, w):
    b, s, d = x.shape
    tm = min(PROJ_TM, s)
    tok = lambda wd: pl.BlockSpec((1, tm, wd), lambda bi, i: (bi, i, 0))
    const2 = lambda bi, i: (0, 0)
    return pl.pallas_call(
        _proj_out_kernel,
        out_shape=jax.ShapeDtypeStruct((b, s, d), F32),
        grid=(b, s // tm),
        in_specs=[tok(d), tok(LRU_WIDTH), tok(LRU_WIDTH), tok(LRU_WIDTH), tok(SSD_WIDTH),
                  tok(SSD_WIDTH), tok(SSD_WIDTH), tok(SSD_WIDTH), tok(DA_WIDTH),
                  pl.BlockSpec((1, SSD_WIDTH), const2), pl.BlockSpec((1, SSD_WIDTH), const2),
                  pl.BlockSpec((1, 1, d), lambda bi, i: (bi, 0, 0)),
                  pl.BlockSpec(w.shape, const2)],
        out_specs=tok(d),
        compiler_params=_cparams(("arbitrary", "arbitrary")),
        name="proj_out",
    )(x, lg, hf, hb, yf, yb, xbc, z, da, dvec, ng, gate, w)


def _route_kernel(x_ref, g_ref, sh_ref, sc_ref, rwh_ref, rwl_ref, rb_ref, c0_ref, h_ref, ri_ref, gt_ref,
                  slt_ref, n_ref, base_ref, cnt_ref, carry_s):
    i = pl.program_id(0)
    tm = x_ref.shape[0]

    @pl.when(i == 0)
    def _():
        carry_s[...] = c0_ref[...]

    h = _norm_mod(x_ref[...], g_ref[...], sh_ref[0], sc_ref[0])
    h_hi = h.astype(BF16)
    h_ref[...] = h_hi
    h_lo = (h - h_hi.astype(F32)).astype(BF16)
    logits = (jnp.dot(h_hi, rwh_ref[...], preferred_element_type=F32)
              + jnp.dot(h_hi, rwl_ref[...], preferred_element_type=F32)
              + jnp.dot(h_lo, rwh_ref[...], preferred_element_type=F32)) + rb_ref[...]
    lane = lax.broadcasted_iota(I32, logits.shape, 1)
    lane_f = lane.astype(F32)
    vals, idxs = [], []
    cur = logits
    for _ in range(TOP_K):
        m = jnp.max(cur, axis=-1, keepdims=True)
        idx = jnp.min(jnp.where(cur >= m, lane_f, float(LANES)), axis=-1, keepdims=True).astype(I32)
        vals.append(m)
        idxs.append(idx)
        cur = jnp.where(lane == idx, 2.0 * NEG_BIG, cur)
    es = [jnp.exp(v - vals[0]) for v in vals]
    inv = 1.0 / (es[0] + es[1] + es[2] + es[3])
    onehot = jnp.zeros(logits.shape, F32)
    for idx in idxs:
        onehot = onehot + jnp.where(lane == idx, 1.0, 0.0)
    ri = lax.broadcasted_iota(I32, (tm, tm), 0)
    ci = lax.broadcasted_iota(I32, (tm, tm), 1)
    strict = jnp.where(ci < ri, 1.0, 0.0).astype(BF16)
    before = jnp.dot(strict, onehot.astype(BF16), preferred_element_type=F32)
    n_tile = jnp.sum(onehot, axis=0, keepdims=True).astype(I32)
    pad_tile = _round_up8(n_tile)
    li = lax.broadcasted_iota(I32, (LANES, LANES), 0)
    lj = lax.broadcasted_iota(I32, (LANES, LANES), 1)
    below = jnp.where(li < lj, 1.0, 0.0).astype(BF16)
    pad_rows = jnp.broadcast_to(pad_tile.astype(F32), (SUBLANES, LANES)).astype(BF16)
    off = jnp.dot(pad_rows, below, preferred_element_type=F32)[0:1, :]
    place = before + off
    ri_out = jnp.zeros(logits.shape, I32)
    gt_out = jnp.zeros(logits.shape, F32)
    sl_f = jnp.full(logits.shape, -1.0, F32)
    for k in range(TOP_K):
        sloc = jnp.sum(jnp.where(lane == idxs[k], place, 0.0), axis=-1, keepdims=True)
        ri_out = ri_out + jnp.where(lane == k, idxs[k], 0) + jnp.where(lane == TOP_K + k, sloc.astype(I32), 0)
        sl_f = jnp.where(lane == k, sloc, sl_f)
        gt_out = gt_out + jnp.where(lane == k, es[k] * inv, 0.0)
    ri_ref[...] = ri_out
    gt_ref[...] = gt_out
    slt_ref[0] = sl_f.T[0:SUBLANES, :].astype(I32)
    n_ref[0] = n_tile
    base_ref[0] = carry_s[...]
    carry_s[...] = carry_s[...] + pad_tile
    cnt_ref[...] = carry_s[...]


def _route(x, g, sh, sc, tiles_per_mod, rw, rb, counts0):
    t, d = x.shape
    tm = ROUTE_TM
    tile = lambda wd: pl.BlockSpec((tm, wd), lambda i: (i, 0))
    const = lambda i: (0, 0)
    mod = pl.BlockSpec((1, 1, d), lambda i: (i // tiles_per_mod, 0, 0))
    nt = t // tm
    per_tile = pl.BlockSpec((1, 1, LANES), lambda i: (i, 0, 0))
    return pl.pallas_call(
        _route_kernel,
        out_shape=[jax.ShapeDtypeStruct((t, d), BF16), jax.ShapeDtypeStruct((t, LANES), I32),
                   jax.ShapeDtypeStruct((t, LANES), F32), jax.ShapeDtypeStruct((nt, SUBLANES, tm), I32),
                   jax.ShapeDtypeStruct((nt, 1, LANES), I32),
                   jax.ShapeDtypeStruct((nt, 1, LANES), I32), jax.ShapeDtypeStruct((1, LANES), I32)],
        grid=(nt,),
        in_specs=[tile(d), pl.BlockSpec((1, d), const), mod, mod,
                  pl.BlockSpec((d, LANES), const), pl.BlockSpec((d, LANES), const),
                  pl.BlockSpec((1, LANES), const), pl.BlockSpec((1, LANES), const)],
        out_specs=[tile(d), tile(LANES), tile(LANES), pl.BlockSpec((1, SUBLANES, tm), lambda i: (i, 0, 0)),
                   per_tile, per_tile, pl.BlockSpec((1, LANES), const)],
        scratch_shapes=[pltpu.VMEM((1, LANES), I32)],
        compiler_params=_cparams(("arbitrary",)),
        name="moe_route",
    )(x, g.reshape(1, d), sh, sc, rw[0], rw[1], rb, counts0)


MOE_STAGE = -(-(ROUTE_TM * TOP_K + N_EXPERTS * (SUBLANES - 1)) // 256) * 256
_TAB_N, _TAB_DST, _TAB_OFF, _TAB_TOTAL = 0, N_EXPERTS, 2 * N_EXPERTS, 3 * N_EXPERTS


U32 = jnp.uint32
_HI16 = 0xFFFF0000


def _pack_halves(a, is_bf16=False):
    n = a.shape[1] // 2
    lo, hi = a[:, :n], a[:, n:]
    if not is_bf16:
        lo, hi = lo.astype(BF16).astype(F32), hi.astype(BF16).astype(F32)
    return (pltpu.bitcast(lo, U32) >> 16) | (pltpu.bitcast(hi, U32) & U32(_HI16))


def _unpack_halves(u):
    lo = pltpu.bitcast(u << 16, F32).astype(BF16)
    hi = pltpu.bitcast(u & U32(_HI16), F32).astype(BF16)
    return lo, hi


_BIG_SHIFT = 2
_BIG_CHUNKS = 1 << _BIG_SHIFT


def _chunk_copies(tab_ref, stage_ref, rows_hbm, sem, to_hbm):
    def copy(off, dst, rows):
        s = stage_ref.at[pl.ds(pl.multiple_of(off, SUBLANES), rows), :]
        r = rows_hbm.at[pl.ds(pl.multiple_of(dst, SUBLANES), rows), :]
        if to_hbm:
            pltpu.make_async_copy(s, r, sem).start()
        else:
            pltpu.make_async_copy(r, s, sem).start()

    def expert(e, carry):
        n = tab_ref[0, 0, _TAB_N + e]
        dst = tab_ref[0, 0, _TAB_DST + e]
        off = tab_ref[0, 0, _TAB_OFF + e]
        big_rows = _BIG_CHUNKS * SUBLANES
        n_big = lax.shift_right_logical(n, _BIG_SHIFT)

        def big(j, c):
            copy(off + j * big_rows, dst + j * big_rows, big_rows)
            return c

        def small(j, c):
            copy(off + (n_big * _BIG_CHUNKS + j) * SUBLANES, dst + (n_big * _BIG_CHUNKS + j) * SUBLANES, SUBLANES)
            return c

        lax.fori_loop(0, n_big, big, 0)
        lax.fori_loop(0, n & (_BIG_CHUNKS - 1), small, 0)
        return carry
    lax.fori_loop(0, N_EXPERTS, expert, 0)


def _chunk_waits(tab_ref, stage_ref, rows_hbm, sem):
    rows = tab_ref[0, 0, _TAB_TOTAL] * SUBLANES
    pltpu.make_async_copy(rows_hbm.at[pl.ds(0, rows), :], stage_ref.at[pl.ds(0, rows), :], sem).wait()


def _dispatch_kernel(tab_ref, tprev_ref, gap_ref, tail_ref, sloc_ref, *refs, tiles):
    h_refs = refs[:len(tiles)]
    xs_out, stage_s, zero_s, sems = refs[len(tiles):]
    i = pl.program_id(0)
    n_steps = sum(tiles)
    slot = i % 2
    row = lax.broadcasted_iota(I32, (MOE_STAGE, ROUTE_TM), 0)
    perm = jnp.zeros((MOE_STAGE, ROUTE_TM), F32)
    for k in range(TOP_K):
        perm = perm + jnp.where(row == sloc_ref[0, k:k + 1, :], 1.0, 0.0)
    perm = perm.astype(BF16)
    off = 0
    for h_ref, nt in zip(h_refs, tiles):
        @pl.when((i >= off) & (i < off + nt))
        def _(h_ref=h_ref):
            stage_s[slot] = _pack_halves(jnp.dot(perm, h_ref[...], preferred_element_type=F32), is_bf16=True)
        off += nt
    _chunk_copies(tab_ref, stage_s.at[slot], xs_out, sems.at[slot], True)

    @pl.when(i > 0)
    def _():
        _chunk_waits(tprev_ref, stage_s.at[1 - slot], xs_out, sems.at[1 - slot])

    @pl.when(i == n_steps - 1)
    def _():
        _chunk_waits(tab_ref, stage_s.at[slot], xs_out, sems.at[slot])
        sem = sems.at[slot]
        zero_s[...] = jnp.zeros(zero_s.shape, U32)
        zr = zero_s.shape[0]
        n_big, n_small, start = tail_ref[0], tail_ref[1], tail_ref[2]

        def big(j, c):
            pltpu.make_async_copy(zero_s, xs_out.at[pl.ds(pl.multiple_of(start + j * zr, SUBLANES), zr), :], sem).start()
            return c

        def small(j, c):
            r0 = pl.multiple_of(start + n_big * zr + j * SUBLANES, SUBLANES)
            pltpu.make_async_copy(zero_s.at[pl.ds(0, SUBLANES), :], xs_out.at[pl.ds(r0, SUBLANES), :], sem).start()
            return c

        def big_wait(j, c):
            pltpu.make_async_copy(zero_s, xs_out.at[pl.ds(0, zr), :], sem).wait()
            return c

        def small_wait(j, c):
            pltpu.make_async_copy(zero_s.at[pl.ds(0, SUBLANES), :], xs_out.at[pl.ds(0, SUBLANES), :], sem).wait()
            return c

        lax.fori_loop(0, n_big, big, 0)
        lax.fori_loop(0, n_small, small, 0)
        lax.fori_loop(0, n_big, big_wait, 0)
        lax.fori_loop(0, n_small, small_wait, 0)

        _chunk_copies(gap_ref, zero_s, xs_out, sem, True)

        def gap_wait(e, c):
            rows = gap_ref[0, 0, _TAB_N + e] * SUBLANES

            @pl.when(rows > 0)
            def _():
                pltpu.make_async_copy(zero_s.at[pl.ds(0, rows), :], xs_out.at[pl.ds(0, rows), :], sem).wait()
            return c
        lax.fori_loop(0, N_EXPERTS, gap_wait, 0)


def _dispatch(hs, tabs, slocs, gaps, tail, n_max):
    d = hs[0].shape[1]
    tm = ROUTE_TM
    tiles = tuple(h.shape[0] // tm for h in hs)
    specs, off = [], 0
    for nt in tiles:
        specs.append(pl.BlockSpec((tm, d), lambda i, off=off, nt=nt: (jnp.clip(i - off, 0, nt - 1), 0)))
        off += nt
    return pl.pallas_call(
        functools.partial(_dispatch_kernel, tiles=tiles),
        out_shape=jax.ShapeDtypeStruct((n_max, d // 2), U32),
        grid=(sum(tiles),),
        in_specs=[pl.BlockSpec((1, 1, LANES), lambda i: (i, 0, 0), memory_space=pltpu.SMEM),
                  pl.BlockSpec((1, 1, LANES), lambda i: (jnp.maximum(i - 1, 0), 0, 0), memory_space=pltpu.SMEM),
                  pl.BlockSpec((1, 1, LANES), lambda i: (0, 0, 0), memory_space=pltpu.SMEM),
                  pl.BlockSpec(memory_space=pltpu.SMEM),
                  pl.BlockSpec((1, SUBLANES, tm), lambda i: (i, 0, 0))] + specs,
        out_specs=pl.BlockSpec(memory_space=pl.ANY),
        scratch_shapes=[pltpu.VMEM((2, MOE_STAGE, d // 2), U32), pltpu.VMEM((MOE_TM, d // 2), U32),
                        pltpu.SemaphoreType.DMA((2,))],
        compiler_params=_cparams(("arbitrary",)),
        name="moe_dispatch",
    )(tabs, tabs, gaps, tail, slocs, *hs)


_SPLIT_W = 2 * LANES


def _gmm_kernel(te_ref, live_ref, newe_ref, xs_ref, wgu_ref, perm_ref, bg_ref, bu_ref,
                wdf_ref, bd_ref, y_ref, wg_ref, wu_ref, wd_ref):
    del te_ref
    v = pl.program_id(0)
    live = live_ref[v] > 0

    @pl.when((newe_ref[v] > 0) & live)
    def _():
        for j in range(wgu_ref.shape[1] // _SPLIT_W):
            r = jnp.dot(wgu_ref[:, j * _SPLIT_W:(j + 1) * _SPLIT_W].astype(BF16), perm_ref[...],
                        preferred_element_type=F32)
            wg_ref[:, j * LANES:(j + 1) * LANES] = r[:, :LANES].astype(BF16)
            wu_ref[:, j * LANES:(j + 1) * LANES] = r[:, LANES:].astype(BF16)
        wd_ref[...] = wdf_ref[...].astype(BF16)

    @pl.when(live)
    def _():
        x_lo, x_hi = _unpack_halves(xs_ref[...])
        half = x_lo.shape[1]
        gate = (jnp.dot(x_lo, wg_ref[:half, :], preferred_element_type=F32)
                + jnp.dot(x_hi, wg_ref[half:, :], preferred_element_type=F32)) + bg_ref[...]
        up = (jnp.dot(x_lo, wu_ref[:half, :], preferred_element_type=F32)
              + jnp.dot(x_hi, wu_ref[half:, :], preferred_element_type=F32)) + bu_ref[...]
        gate = jnp.minimum(gate, SWIGLU_LIMIT)
        up = jnp.clip(up, -SWIGLU_LIMIT, SWIGLU_LIMIT)
        glu = gate * _sigmoid(SWIGLU_ALPHA * gate)
        act = ((up + 1.0) * glu).astype(BF16)
        y_ref[...] = _pack_halves(jnp.dot(act, wd_ref[...], preferred_element_type=F32) + bd_ref[...])

    @pl.when(jnp.logical_not(live))
    def _():
        y_ref[...] = jnp.zeros(y_ref.shape, U32)


def _gmm(xs, tiles, layer, w_gu, bg, bu, w_down, bd):
    n_rows = xs.shape[0]
    _, _, d, de2 = w_gu.shape
    de = de2 // 2
    tm = MOE_TM
    nv = n_rows // tm
    src = jnp.arange(_SPLIT_W)
    dst = jnp.where(src % 2 == 0, src // 2, LANES + src // 2)
    perm = (dst[:, None] == jnp.arange(_SPLIT_W)[None, :]).astype(BF16)
    ex = lambda shape: pl.BlockSpec((None,) + shape, lambda v, te, lv, ne: (te[v], 0, 0))
    lay = lambda shape: pl.BlockSpec((None, None) + shape, lambda v, te, lv, ne: (layer, te[v], 0, 0))
    rows = pl.BlockSpec((tm, d // 2), lambda v, te, lv, ne: (v, 0))
    grid_spec = pltpu.PrefetchScalarGridSpec(
        num_scalar_prefetch=3,
        grid=(nv,),
        in_specs=[rows, lay((d, de2)), pl.BlockSpec((_SPLIT_W, _SPLIT_W), lambda v, *_: (0, 0)),
                  ex((1, de)), ex((1, de)), lay((de, d)), ex((1, d))],
        out_specs=rows,
        scratch_shapes=[pltpu.VMEM((d, de), BF16), pltpu.VMEM((d, de), BF16), pltpu.VMEM((de, d), BF16)],
    )
    return pl.pallas_call(
        _gmm_kernel,
        out_shape=jax.ShapeDtypeStruct((n_rows, d // 2), U32),
        grid_spec=grid_spec,
        compiler_params=_cparams(("arbitrary",)),
        name="moe_experts",
    )(*tiles, xs, w_gu, perm, bg, bu, w_down, bd)


def _tile_tables(cap, n_rows):
    tm = MOE_TM
    end = jnp.cumsum(cap)
    start = jnp.arange(n_rows // tm, dtype=I32) * tm
    e = jnp.minimum(jnp.sum((end[None, :] <= start[:, None]).astype(I32), axis=1), N_EXPERTS - 1).astype(I32)
    live = (start < end[N_EXPERTS - 1]).astype(I32)
    new_e = (e != jnp.concatenate([jnp.full((1,), -1, I32), e[:-1]])).astype(I32)
    return e, live, new_e


def _combine_kernel(tcur_ref, tnext_ref, sloc_ref, gt_ref, x_ref, g2_ref, y_hbm, o_ref, stage_s, sems, *, n):
    i = pl.program_id(0)
    slot = i % 2

    @pl.when(i == 0)
    def _():
        stage_s[...] = jnp.zeros(stage_s.shape, U32)
        _chunk_copies(tcur_ref, stage_s.at[0], y_hbm, sems.at[0], False)

    @pl.when(i + 1 < n)
    def _():
        _chunk_copies(tnext_ref, stage_s.at[1 - slot], y_hbm, sems.at[1 - slot], False)

    _chunk_waits(tcur_ref, stage_s.at[slot], y_hbm, sems.at[slot])
    rows_lo, rows_hi = _unpack_halves(stage_s[slot])
    lane = lax.broadcasted_iota(I32, (x_ref.shape[0], MOE_STAGE), 1)
    gt = gt_ref[...]
    sloc = sloc_ref[...]
    gmat = jnp.zeros(lane.shape, F32)
    for k in range(TOP_K):
        gmat = gmat + jnp.where(lane == sloc[:, TOP_K + k:TOP_K + k + 1], gt[:, k:k + 1], 0.0)
    g_hi = gmat.astype(BF16)
    g_lo = (gmat - g_hi.astype(F32)).astype(BF16)
    half = rows_lo.shape[1]
    for rows, sl in ((rows_lo, slice(0, half)), (rows_hi, slice(half, 2 * half))):
        f = (jnp.dot(g_hi, rows, preferred_element_type=F32)
             + jnp.dot(g_lo, rows, preferred_element_type=F32))
        o_ref[:, sl] = x_ref[:, sl] + g2_ref[0][:, sl] * f


def _combine(y, tabs, sloc, gates, x, g2, tiles_per_mod):
    t, d = x.shape
    tm = ROUTE_TM
    nt = t // tm
    return pl.pallas_call(
        functools.partial(_combine_kernel, n=nt),
        out_shape=jax.ShapeDtypeStruct((t, d), F32),
        grid=(nt,),
        in_specs=[pl.BlockSpec((1, 1, LANES), lambda i: (i, 0, 0), memory_space=pltpu.SMEM),
                  pl.BlockSpec((1, 1, LANES), lambda i: (jnp.minimum(i + 1, nt - 1), 0, 0),
                               memory_space=pltpu.SMEM),
                  pl.BlockSpec((tm, LANES), lambda i: (i, 0)),
                  pl.BlockSpec((tm, LANES), lambda i: (i, 0)),
                  pl.BlockSpec((tm, d), lambda i: (i, 0)),
                  pl.BlockSpec((1, 1, d), lambda i: (i // tiles_per_mod, 0, 0)),
                  pl.BlockSpec(memory_space=pl.ANY)],
        out_specs=pl.BlockSpec((tm, d), lambda i: (i, 0)),
        scratch_shapes=[pltpu.VMEM((2, MOE_STAGE, d // 2), U32), pltpu.SemaphoreType.DMA((2,))],
        compiler_params=_cparams(("arbitrary",)),
        name="moe_combine",
    )(tabs, tabs, sloc, gates, x, g2, y)


def _moe(streams, layer, g, router_w, router_b, w_gu_all, b_gu, w_down_all, b_down):
    d = streams[0][0].shape[1]
    tm = ROUTE_TM
    rw_f = jnp.zeros((d, LANES), F32).at[:, :N_EXPERTS].set(router_w.astype(F32))
    rw_hi = rw_f.astype(BF16)
    rw = (rw_hi, (rw_f - rw_hi.astype(F32)).astype(BF16))
    rb = jnp.full((1, LANES), NEG_BIG, F32).at[0, :N_EXPERTS].set(router_b.astype(F32))
    counts = jnp.zeros((1, LANES), I32)
    routed = []
    for x_tok, sh, sc, _, tpm in streams:
        h, route_i, gates, sloc_t, n_tile, seg_base, counts = _route(x_tok, g, sh, sc, tpm, rw, rb, counts)
        routed.append((h, route_i, gates, sloc_t, n_tile, seg_base))
    n_tiles = sum(x_tok.shape[0] for x_tok, *_ in streams) // tm
    n_max = n_tiles * tm * TOP_K + n_tiles * N_EXPERTS * (SUBLANES - 1) + N_EXPERTS * (MOE_TM - SUBLANES)
    n_max = (n_max + MOE_TM - 1) // MOE_TM * MOE_TM
    cnt = counts[0, :N_EXPERTS]
    cap = (cnt + MOE_TM - 1) // MOE_TM * MOE_TM
    total = jnp.sum(cap)
    base = jnp.cumsum(cap) - cap
    gap = jnp.concatenate([(cap - cnt) // SUBLANES, base + cnt, jnp.zeros((LANES - 2 * N_EXPERTS,), I32)])
    tabs = []
    for h, _, _, _, n_tile, seg_base in routed:
        nt = h.shape[0] // tm
        pad = _round_up8(n_tile[:, 0, :N_EXPERTS])
        off = jnp.cumsum(pad, axis=1) - pad
        dst = base[None, :] + seg_base[:, 0, :N_EXPERTS]
        nch = pad // SUBLANES
        tab = jnp.concatenate([nch, dst, off, jnp.sum(nch, axis=1, keepdims=True),
                               jnp.zeros((nt, LANES - 3 * N_EXPERTS - 1), I32)], axis=1)
        tabs.append(tab.astype(I32).reshape(nt, 1, LANES))
    free = n_max - total
    tail = jnp.stack([free // MOE_TM, (free % MOE_TM) // SUBLANES, total]).astype(I32)
    xs = _dispatch([r[0] for r in routed], jnp.concatenate(tabs, axis=0),
                   jnp.concatenate([r[3] for r in routed], axis=0), gap.astype(I32).reshape(1, 1, LANES),
                   tail, n_max)
    bg = b_gu[:, None, 0::2].astype(F32)
    bu = b_gu[:, None, 1::2].astype(F32)
    y = _gmm(xs, _tile_tables(cap, n_max), layer, w_gu_all, bg, bu, w_down_all, b_down[:, None, :].astype(F32))
    return [_combine(y, tab, r[1], r[2], x_tok, g2, tpm)
            for (x_tok, _, _, g2, tpm), r, tab in zip(streams, routed, tabs)]


def kernel(x, c, ctx, c_ctx, w_mod, b_mod, norm1_g, norm2_g, w_in, w_out, lru_conv_w, lru_conv_b, lru_wa, lru_ba, lru_wx, lru_bx, lru_lam, ssd_conv_w, ssd_conv_b, ssd_a_log, ssd_dt_bias, ssd_d, ssd_norm_g, da_q_norm, da_k_norm, da_lam_q, da_lam_k, da_subln_g, router_w, router_b, exp_w_gu, exp_b_gu, exp_w_down, exp_b_down):
    b, s, d = x.shape
    n_ctx = ctx.shape[1]
    depth = w_mod.shape[0]
    cos, sin = _rope_tables(s)
    x_lat, x_ctx = x, ctx
    mod_rows = jnp.zeros((2 * SUBLANES, d), F32).at[:b].set(c).at[b].set(c_ctx)

    for l in range(depth):
        need_ctx = l < depth - 1
        lam_init = 0.8 - 0.6 * math.exp(-0.3 * l)
        mod = _modulation(mod_rows, w_mod[l], b_mod[l])
        m_lat = mod[:b].reshape(b, 1, 6, d)
        m_ctx = jnp.broadcast_to(mod[b].reshape(1, 1, 6, d), (b, 1, 6, d))
        part = lambda m, j: m[:, :, j, :]

        w_in_l = _prep_w_in(w_in[l])
        qk_gains = (da_q_norm[l], da_k_norm[l], cos, sin)
        p_lat = _proj_in(x_lat, norm1_g[l], part(m_lat, 0), part(m_lat, 1), w_in_l, *qk_gains, rope=True, kc=ATT_KC)
        p_ctx = _proj_in(x_ctx, norm1_g[l], part(m_ctx, 0), part(m_ctx, 1), w_in_l, *qk_gains, rope=False, kc=ATT_KC)

        zl = jnp.zeros((b, LRU_WIDTH), F32)
        lru_args = (lru_wa[l], lru_ba[l], lru_wx[l], lru_bx[l], lru_lam[l])
        xc_ctx = _dwconv(p_ctx["lx"], lru_conv_w[l], lru_conv_b[l], act=False)
        hf_c, hb_c = _lru_scan(xc_ctx, *lru_args, zl, zl)
        xc_lat = _dwconv(p_lat["lx"], lru_conv_w[l], lru_conv_b[l], act=False)
        hf_l, hb_l = _lru_scan(xc_lat, *lru_args, hf_c[:, -1], hb_c[:, 0])

        zs = jnp.zeros((b, SSD_GROUPS, 2 * SSD_HEAD_DIM, SSD_STATE), F32)
        xbc_ctx = _dwconv(p_ctx["xbc"], ssd_conv_w[l], ssd_conv_b[l], act=True)
        yf_c, yb_c, sf_c, sb_c = _ssd_scan(xbc_ctx, p_ctx["dt"], ssd_a_log[l], ssd_dt_bias[l], zs, zs)
        xbc_lat = _dwconv(p_lat["xbc"], ssd_conv_w[l], ssd_conv_b[l], act=True)
        yf_l, yb_l, _, _ = _ssd_scan(xbc_lat, p_lat["dt"], ssd_a_log[l], ssd_dt_bias[l], sf_c, sb_c)

        lq = da_lam_q[l].astype(F32)
        lk = da_lam_k[l].astype(F32)
        lam = jnp.exp(jnp.sum(lq[0] * lk[0])) - jnp.exp(jnp.sum(lq[1] * lk[1])) + lam_init
        da_l = _diff_attn(p_lat["q"], [(p_ctx["k"], p_ctx["v"]), (p_lat["k"], p_lat["v"])], lam,
                          da_subln_g[l], lam_init)

        dvec = jnp.repeat(ssd_d[l].astype(F32), SSD_HEAD_DIM).reshape(1, SSD_WIDTH)
        ng = ssd_norm_g[l].astype(F32).reshape(1, SSD_WIDTH)
        w_out_l = w_out[l].astype(BF16)
        if need_ctx:
            da_c = _diff_attn(p_ctx["q"], [(p_ctx["k"], p_ctx["v"])], lam, da_subln_g[l], lam_init)
            x_ctx = _proj_out(x_ctx, p_ctx["lg"], hf_c, hb_c, yf_c, yb_c, xbc_ctx, p_ctx["z"], da_c,
                              dvec, ng, part(m_ctx, 2), w_out_l)
        x_lat = _proj_out(x_lat, p_lat["lg"], hf_l, hb_l, yf_l, yb_l, xbc_lat, p_lat["z"], da_l,
                          dvec, ng, part(m_lat, 2), w_out_l)

        moe_w = (router_w[l], router_b[l], exp_w_gu, exp_b_gu[l], exp_w_down, exp_b_down[l])
        streams = [(x_lat.reshape(-1, d), part(m_lat, 3), part(m_lat, 4), part(m_lat, 5), s // ROUTE_TM)]
        if need_ctx:
            streams.append((x_ctx.reshape(-1, d), part(m_ctx, 3), part(m_ctx, 4), part(m_ctx, 5), n_ctx // ROUTE_TM))
        outs = _moe(streams, l, norm2_g[l], *moe_w)
        x_lat = outs[0].reshape(b, s, d)
        if need_ctx:
            x_ctx = outs[1].reshape(b, n_ctx, d)
    return x_lat
```

```python
import functools
import math

import jax
import jax.numpy as jnp
from jax import lax
from jax.experimental import pallas as pl
from jax.experimental.pallas import tpu as pltpu

F32 = jnp.float32
BF16 = jnp.bfloat16
I32 = jnp.int32

GRID_W = 64
EPS = 1e-6
CONV_K = 4
LRU_HEADS = 4
LRU_HEAD_DIM = 64
LRU_WIDTH = 256
LRU_C = 8.0
SSD_HEADS = 4
SSD_HEAD_DIM = 64
SSD_WIDTH = 256
SSD_GROUPS = 2
SSD_STATE = 128
SSD_CHUNK = 128
SSD_CONV_DIM = 768
DA_HEADS = 4
DA_QK_DIM = 64
DA_V_DIM = 128
DA_QK_WIDTH = 512
DA_WIDTH = 512
DA_SCALE = DA_QK_DIM ** -0.5
ROPE_BASE = 10000.0
N_EXPERTS = 32
TOP_K = 4
SWIGLU_ALPHA = 1.702
SWIGLU_LIMIT = 7.0

LANES = 128
SUBLANES = 8
VMEM_LIMIT = 56 * 1024 * 1024

PROJ_TM = 512
ATT_TQ = 512
ATT_KC = 256
LOG2E = math.log2(math.e)
LRU_TL = 512
SSD_STEP_CHUNKS = 8
ROUTE_TM = 256
MOE_TM = 512
NEG_BIG = -1e30


def _cparams(sem):
    return pltpu.CompilerParams(dimension_semantics=sem, vmem_limit_bytes=VMEM_LIMIT)


def _sigmoid(x):
    return 0.5 * jnp.tanh(0.5 * x) + 0.5


def _silu(x):
    return x * _sigmoid(x)


def _softplus(x):
    return jnp.maximum(x, 0.0) + jnp.log(1.0 + jnp.exp(-jnp.abs(x)))


def _round_up8(n):
    return ((n + (SUBLANES - 1)) >> 3) << 3


def _mod_kernel(c_ref, w_ref, b_ref, o_ref):
    a = _silu(c_ref[...])
    o_ref[...] = jnp.dot(a, w_ref[...], preferred_element_type=F32,
                         precision=lax.Precision.HIGHEST) + b_ref[...]


def _modulation(rows, w, b):
    m, d = rows.shape
    n = w.shape[1]
    tn = 512
    return pl.pallas_call(
        _mod_kernel,
        out_shape=jax.ShapeDtypeStruct((m, n), F32),
        grid=(n // tn,),
        in_specs=[pl.BlockSpec((m, d), lambda j: (0, 0)),
                  pl.BlockSpec((d, tn), lambda j: (0, j)),
                  pl.BlockSpec((1, tn), lambda j: (0, j))],
        out_specs=pl.BlockSpec((m, tn), lambda j: (0, j)),
        compiler_params=_cparams(("arbitrary",)),
        name="modulation",
    )(rows, w, b.reshape(1, n))


def _norm_mod(x, g, sh, sc):
    ms = jnp.mean(x * x, axis=-1, keepdims=True)
    y = x * lax.rsqrt(ms + EPS) * g
    return y * (1.0 + sc) + sh


def _proj_in_kernel(x_ref, g_ref, sh_ref, sc_ref, w_ref, gq_ref, gk_ref, cos_ref, sin_ref, seg_ref, *o_refs,
                    splits, rope, kc, q_scale):
    h = _norm_mod(x_ref[0], g_ref[...], sh_ref[0], sc_ref[0]).astype(BF16)
    for (name, _, _), o_ref, (lo, hi) in zip(_IN_GROUPS, o_refs, splits):
        y = jnp.dot(h, w_ref[:, lo:hi], preferred_element_type=F32)
        if name == "q":
            _store_queries(o_ref, _qk_transform(y, gq_ref[...], cos_ref, sin_ref, seg_ref, rope, q_scale))
        elif name == "k":
            _store_keys(o_ref, _qk_transform(y, gk_ref[...], cos_ref, sin_ref, seg_ref, rope, 1.0), kc)
        else:
            o_ref[0] = y.astype(o_ref.dtype)


_IN_GROUPS = (("lg", 256, F32), ("lx", 256, F32), ("z", 256, F32), ("xbc", 768, F32),
              ("dt", 128, F32), ("q", 512, BF16), ("k", 512, BF16), ("v", 512, BF16))


def _prep_w_in(w_in):
    d = w_in.shape[0]
    lo = 2 * LRU_WIDTH + SSD_WIDTH + SSD_CONV_DIM
    hi = lo + 2 * SSD_HEADS
    dt = jnp.concatenate([w_in[:, lo:hi], jnp.zeros((d, LANES - (hi - lo)), w_in.dtype)], axis=1)
    w = jnp.concatenate([w_in[:, :lo], dt, w_in[:, hi:]], axis=1)
    return w.astype(BF16)


def _proj_in(x, g, sh, sc, w, gq, gk, cos, sin, rope, kc):
    b, s, d = x.shape
    tm = min(PROJ_TM, s)
    kc = min(kc, tm)
    splits, off = [], 0
    for _, wd, _ in _IN_GROUPS:
        splits.append((off, off + wd))
        off += wd
    out_shape, out_specs = [], []
    for name, wd, dt in _IN_GROUPS:
        if name == "q":
            out_shape.append(jax.ShapeDtypeStruct((b, 2 * DA_HEADS, s, DA_QK_DIM), dt))
            out_specs.append(pl.BlockSpec((1, 2 * DA_HEADS, tm, DA_QK_DIM), lambda bi, i: (bi, 0, i, 0)))
        elif name == "k":
            out_shape.append(jax.ShapeDtypeStruct((b, 2 * DA_HEADS, s // kc, DA_QK_DIM, kc), dt))
            out_specs.append(pl.BlockSpec((1, 2 * DA_HEADS, tm // kc, DA_QK_DIM, kc), lambda bi, i: (bi, 0, i, 0, 0)))
        else:
            out_shape.append(jax.ShapeDtypeStruct((b, s, wd), dt))
            out_specs.append(pl.BlockSpec((1, tm, wd), lambda bi, i: (bi, i, 0)))
    n = DA_QK_WIDTH
    tile_gain = lambda gg: jnp.tile(gg.astype(F32), n // DA_QK_DIM).reshape(1, n)
    idx = jnp.arange(n) // DA_QK_DIM
    seg = (idx[:, None] == idx[None, :]).astype(BF16)
    const = lambda bi, i: (0, 0)
    outs = pl.pallas_call(
        functools.partial(_proj_in_kernel, splits=tuple(splits), rope=rope, kc=kc, q_scale=DA_SCALE * LOG2E),
        out_shape=out_shape,
        grid=(b, s // tm),
        in_specs=[pl.BlockSpec((1, tm, d), lambda bi, i: (bi, i, 0)),
                  pl.BlockSpec((1, d), const),
                  pl.BlockSpec((1, 1, d), lambda bi, i: (bi, 0, 0)),
                  pl.BlockSpec((1, 1, d), lambda bi, i: (bi, 0, 0)),
                  pl.BlockSpec((d, off), const),
                  pl.BlockSpec((1, n), const), pl.BlockSpec((1, n), const),
                  pl.BlockSpec((tm, LANES), lambda bi, i: (i, 0)),
                  pl.BlockSpec((tm, LANES), lambda bi, i: (i, 0)),
                  pl.BlockSpec((n, n), const)],
        out_specs=out_specs,
        compiler_params=_cparams(("arbitrary", "arbitrary")),
        name="proj_in",
    )(x, g.reshape(1, d), sh, sc, w, tile_gain(gq), tile_gain(gk), cos, sin, seg)
    return {name: o for (name, _, _), o in zip(_IN_GROUPS, outs)}


def _dwconv_kernel(x_ref, w_ref, b_ref, o_ref, *, act):
    x = x_ref[0]
    s = x.shape[0]
    row = lax.broadcasted_iota(I32, x.shape, 0)
    acc = x * w_ref[2:3, :] + b_ref[...]
    acc = acc + jnp.where(row >= 2, pltpu.roll(x, 2, 0), 0.0) * w_ref[0:1, :]
    acc = acc + jnp.where(row >= 1, pltpu.roll(x, 1, 0), 0.0) * w_ref[1:2, :]
    acc = acc + jnp.where(row < s - 1, pltpu.roll(x, s - 1, 0), 0.0) * w_ref[3:4, :]
    if act:
        acc = _silu(acc)
    o_ref[0] = acc


def _dwconv(x, w, bias, act):
    b, s, c = x.shape
    tc = 256
    return pl.pallas_call(
        functools.partial(_dwconv_kernel, act=act),
        out_shape=jax.ShapeDtypeStruct((b, s, c), F32),
        grid=(b, c // tc),
        in_specs=[pl.BlockSpec((1, s, tc), lambda bi, j: (bi, 0, j)),
                  pl.BlockSpec((CONV_K, tc), lambda bi, j: (0, j)),
                  pl.BlockSpec((1, tc), lambda bi, j: (0, j))],
        out_specs=pl.BlockSpec((1, s, tc), lambda bi, j: (bi, 0, j)),
        compiler_params=_cparams(("arbitrary", "arbitrary")),
        name="dwconv",
    )(x, w, bias.reshape(1, c))


def _lru_coeffs(x, w_ref, b_ref, sp_ref, a_s, u_s):
    g = jnp.dot(x.astype(BF16), w_ref[...], preferred_element_type=F32) + b_ref[...]
    r = _sigmoid(g[:, :LRU_WIDTH])
    i = _sigmoid(g[:, LRU_WIDTH:])
    a = jnp.exp(-LRU_C * r * sp_ref[...])
    a_s[...] = a
    u_s[...] = jnp.sqrt(1.0 - a * a) * (i * x)


def _lru_kernel(xf_ref, xb_ref, wf_ref, wb_ref, bf_ref, bb_ref, spf_ref, spb_ref,
                h0f_ref, h0b_ref, hf_ref, hb_ref, af_s, uf_s, ab_s, ub_s, cf_s, cb_s):
    c = pl.program_id(1)
    tl = xf_ref.shape[1]
    ngroups = tl // SUBLANES
    shape = (SUBLANES, LRU_WIDTH)
    row = lax.broadcasted_iota(I32, shape, 0)

    @pl.when(c == 0)
    def _():
        cf_s[...] = jnp.broadcast_to(h0f_ref[0], shape)
        cb_s[...] = jnp.broadcast_to(h0b_ref[0], shape)

    _lru_coeffs(xf_ref[0], wf_ref, bf_ref, spf_ref, af_s, uf_s)
    _lru_coeffs(xb_ref[0], wb_ref, bb_ref, spb_ref, ab_s, ub_s)

    def group_scan(a, u, carry, reverse):
        for sh in (1, 2, 4):
            keep = (row < SUBLANES - sh) if reverse else (row >= sh)
            amt = SUBLANES - sh if reverse else sh
            a_sh = jnp.where(keep, pltpu.roll(a, amt, 0), 1.0)
            u_sh = jnp.where(keep, pltpu.roll(u, amt, 0), 0.0)
            u = a * u_sh + u
            a = a * a_sh
        return a * carry + u

    def step(gj, carry):
        cf, cb = carry
        slf = pl.ds(pl.multiple_of(gj * SUBLANES, SUBLANES), SUBLANES)
        slb = pl.ds(pl.multiple_of((ngroups - 1 - gj) * SUBLANES, SUBLANES), SUBLANES)
        hf = group_scan(af_s[slf, :], uf_s[slf, :], cf, False)
        hb = group_scan(ab_s[slb, :], ub_s[slb, :], cb, True)
        hf_ref[0, slf, :] = hf
        hb_ref[0, slb, :] = hb
        return (jnp.broadcast_to(hf[SUBLANES - 1:SUBLANES, :], shape),
                jnp.broadcast_to(hb[0:1, :], shape))

    cf, cb = lax.fori_loop(0, ngroups, step, (cf_s[...], cb_s[...]), unroll=2)
    cf_s[...] = cf
    cb_s[...] = cb


def _lru_gate_weights(wa, wx):
    def dense(wh):
        m = jnp.zeros((LRU_WIDTH, LRU_WIDTH), wh.dtype)
        for h in range(LRU_HEADS):
            lo = h * LRU_HEAD_DIM
            m = m.at[lo:lo + LRU_HEAD_DIM, lo:lo + LRU_HEAD_DIM].set(wh[h])
        return m
    return jnp.concatenate([dense(wa), dense(wx)], axis=1).astype(BF16)


def _lru_scan(xc, wa, ba, wx, bx, lam, h0f, h0b):
    b, s, w = xc.shape
    tl = min(LRU_TL, s)
    nc = s // tl
    wf = _lru_gate_weights(wa[0], wx[0])
    wb = _lru_gate_weights(wa[1], wx[1])
    bf = jnp.concatenate([ba[0], bx[0]]).reshape(1, 2 * w)
    bb = jnp.concatenate([ba[1], bx[1]]).reshape(1, 2 * w)
    sp = jax.nn.softplus(-lam.astype(F32))
    const = lambda bi, c: (0, 0)
    hf, hb = pl.pallas_call(
        _lru_kernel,
        out_shape=[jax.ShapeDtypeStruct((b, s, w), F32)] * 2,
        grid=(b, nc),
        in_specs=[pl.BlockSpec((1, tl, w), lambda bi, c: (bi, c, 0)),
                  pl.BlockSpec((1, tl, w), lambda bi, c: (bi, nc - 1 - c, 0)),
                  pl.BlockSpec((w, 2 * w), const), pl.BlockSpec((w, 2 * w), const),
                  pl.BlockSpec((1, 2 * w), const), pl.BlockSpec((1, 2 * w), const),
                  pl.BlockSpec((1, w), const), pl.BlockSpec((1, w), const),
                  pl.BlockSpec((1, 1, w), lambda bi, c: (bi, 0, 0)),
                  pl.BlockSpec((1, 1, w), lambda bi, c: (bi, 0, 0))],
        out_specs=[pl.BlockSpec((1, tl, w), lambda bi, c: (bi, c, 0)),
                   pl.BlockSpec((1, tl, w), lambda bi, c: (bi, nc - 1 - c, 0))],
        scratch_shapes=[pltpu.VMEM((tl, w), F32)] * 4
        + [pltpu.VMEM((SUBLANES, w), F32), pltpu.VMEM((SUBLANES, w), F32)],
        compiler_params=_cparams(("arbitrary", "arbitrary")),
        name="lru_scan",
    )(xc, xc, wf, wb, bf, bb, sp[0:1], sp[1:2], h0f.reshape(b, 1, w), h0b.reshape(b, 1, w))
    return hf, hb


def _ssd_direction(x_ref, dt_ref, row0, avec_ref, dtb_ref, state_s, y_ref, lane0, rev):
    t = SSD_CHUNK
    xbc = x_ref[0, row0:row0 + t, :]
    dtraw = dt_ref[0, row0:row0 + t, :]
    ri = lax.broadcasted_iota(I32, (t, t), 0)
    ci = lax.broadcasted_iota(I32, (t, t), 1)
    causal = (ci >= ri) if rev else (ci <= ri)
    cum_m = jnp.where(causal, 1.0, 0.0).astype(F32)

    dt = _softplus(dtraw + dtb_ref[...])
    a_all = dt * avec_ref[...]
    acs = jnp.dot(cum_m, a_all, preferred_element_type=F32, precision=lax.Precision.HIGHEST)
    acs_t = acs.T
    edge = 0 if rev else t - 1

    xs = xbc[:, :SSD_WIDTH]
    hrow = lax.broadcasted_iota(I32, (t, t), 0)
    for g in range(SSD_GROUPS):
        bm = xbc[:, SSD_WIDTH + g * SSD_STATE:SSD_WIDTH + (g + 1) * SSD_STATE]
        cm = xbc[:, SSD_WIDTH + (SSD_GROUPS + g) * SSD_STATE:SSD_WIDTH + (SSD_GROUPS + g + 1) * SSD_STATE]
        bm_b = bm.astype(BF16)
        cm_b = cm.astype(BF16)
        gmat = lax.dot_general(cm_b, bm_b, (((1,), (1,)), ((), ())), preferred_element_type=F32)
        s_prev = state_s[g]
        off_all = lax.dot_general(cm_b, s_prev.astype(BF16), (((1,), (1,)), ((), ())),
                                  preferred_element_type=F32)
        xd_parts, tot = [], []
        for hl in range(2):
            h = g * 2 + hl
            ln = lane0 + h
            col = jnp.broadcast_to(acs[:, ln:ln + 1], (t, t))
            rowv = acs_t[ln:ln + 1, :]
            lmat = jnp.where(causal, jnp.exp(col - rowv), 0.0)
            xh = xs[:, h * SSD_HEAD_DIM:(h + 1) * SSD_HEAD_DIM] * dt[:, ln:ln + 1]
            ydiag = jnp.dot((gmat * lmat).astype(BF16), xh.astype(BF16), preferred_element_type=F32)
            colh = col[:, :SSD_HEAD_DIM]
            yoff = off_all[:, hl * SSD_HEAD_DIM:(hl + 1) * SSD_HEAD_DIM] * jnp.exp(colh)
            y_ref[0, row0:row0 + t, h * SSD_HEAD_DIM:(h + 1) * SSD_HEAD_DIM] = ydiag + yoff
            total = acs[edge:edge + 1, ln:ln + 1]
            xd_parts.append(xh * jnp.exp(total - colh))
            tot.append(jnp.exp(total))
        xd = jnp.concatenate(xd_parts, axis=1)
        xd_t = xd.T
        st = jnp.dot(xd_t.astype(BF16), bm_b, preferred_element_type=F32)
        fac = jnp.where(hrow < SSD_HEAD_DIM, tot[0], tot[1])
        state_s[g] = fac * s_prev + st


def _ssd_kernel(xf_ref, dtf_ref, xb_ref, dtb_ref, avec_ref, bias_ref, h0f_ref, h0b_ref,
                yf_ref, yb_ref, sf_ref, sb_ref, stf_s, stb_s):
    c = pl.program_id(1)

    @pl.when(c == 0)
    def _():
        stf_s[...] = h0f_ref[0]
        stb_s[...] = h0b_ref[0]

    nsub = xf_ref.shape[1] // SSD_CHUNK
    for j in range(nsub):
        _ssd_direction(xf_ref, dtf_ref, j * SSD_CHUNK, avec_ref, bias_ref, stf_s, yf_ref, 0, False)
        _ssd_direction(xb_ref, dtb_ref, (nsub - 1 - j) * SSD_CHUNK, avec_ref, bias_ref, stb_s, yb_ref,
                       SSD_HEADS, True)
    sf_ref[0] = stf_s[...]
    sb_ref[0] = stb_s[...]


def _ssd_scan(xbc, dt, a_log, dt_bias, h0f, h0b):
    b, s, _ = xbc.shape
    t = min(SSD_STEP_CHUNKS * SSD_CHUNK, s)
    nc = s // t
    avec = jnp.zeros((1, LANES), F32).at[0, :2 * SSD_HEADS].set(-jnp.exp(a_log.astype(F32)).reshape(-1))
    bias = jnp.zeros((1, LANES), F32).at[0, :2 * SSD_HEADS].set(dt_bias.astype(F32).reshape(-1))
    st_shape = (b, SSD_GROUPS, 2 * SSD_HEAD_DIM, SSD_STATE)
    const = lambda bi, c: (0, 0)
    st_spec = pl.BlockSpec((1,) + st_shape[1:], lambda bi, c: (bi, 0, 0, 0))
    yf, yb, sf, sb = pl.pallas_call(
        _ssd_kernel,
        out_shape=[jax.ShapeDtypeStruct((b, s, SSD_WIDTH), F32)] * 2
        + [jax.ShapeDtypeStruct(st_shape, F32)] * 2,
        grid=(b, nc),
        in_specs=[pl.BlockSpec((1, t, SSD_CONV_DIM), lambda bi, c: (bi, c, 0)),
                  pl.BlockSpec((1, t, LANES), lambda bi, c: (bi, c, 0)),
                  pl.BlockSpec((1, t, SSD_CONV_DIM), lambda bi, c: (bi, nc - 1 - c, 0)),
                  pl.BlockSpec((1, t, LANES), lambda bi, c: (bi, nc - 1 - c, 0)),
                  pl.BlockSpec((1, LANES), const), pl.BlockSpec((1, LANES), const),
                  st_spec, st_spec],
        out_specs=[pl.BlockSpec((1, t, SSD_WIDTH), lambda bi, c: (bi, c, 0)),
                   pl.BlockSpec((1, t, SSD_WIDTH), lambda bi, c: (bi, nc - 1 - c, 0)),
                   st_spec, st_spec],
        scratch_shapes=[pltpu.VMEM(st_shape[1:], F32), pltpu.VMEM(st_shape[1:], F32)],
        compiler_params=_cparams(("arbitrary", "arbitrary")),
        name="ssd_scan",
    )(xbc, dt, xbc, dt, avec, bias, h0f, h0b)
    return yf, yb, sf, sb


def _qk_transform(x, g, cos_ref, sin_ref, seg_ref, rope, scale):
    sq = x * x
    hi = sq.astype(BF16)
    lo = (sq - hi.astype(F32)).astype(BF16)
    ssum = (jnp.dot(hi, seg_ref[...], preferred_element_type=F32)
            + jnp.dot(lo, seg_ref[...], preferred_element_type=F32))
    y = x * lax.rsqrt(ssum * (1.0 / DA_QK_DIM) + EPS) * g
    if rope:
        n = x.shape[1]
        lane = lax.broadcasted_iota(I32, x.shape, 1)
        first = (lane & (DA_QK_DIM - 1)) < DA_QK_DIM // 2
        partner = jnp.where(first, pltpu.roll(y, n - DA_QK_DIM // 2, 1), pltpu.roll(y, DA_QK_DIM // 2, 1))
        cosv = jnp.concatenate([cos_ref[...]] * (n // LANES), axis=1)
        sinv = jnp.concatenate([sin_ref[...]] * (n // LANES), axis=1)
        y = y * cosv + partner * sinv
    return y * scale


def _store_queries(o_ref, y):
    yb = y.astype(BF16)
    for j in range(2 * DA_HEADS):
        o_ref[0, j] = yb[:, j * DA_QK_DIM:(j + 1) * DA_QK_DIM]


def _store_keys(o_ref, y, kc):
    for h in range(DA_HEADS):
        head = y[:, h * 2 * DA_QK_DIM:(h + 1) * 2 * DA_QK_DIM]
        for cc in range(y.shape[0] // kc):
            t = head[cc * kc:(cc + 1) * kc, :].T.astype(BF16)
            o_ref[0, 2 * h, cc] = t[:DA_QK_DIM]
            o_ref[0, 2 * h + 1, cc] = t[DA_QK_DIM:]


def _rope_tables(s):
    pos = jnp.arange(s, dtype=I32)
    r = (pos // GRID_W).astype(F32)
    col = (pos % GRID_W).astype(F32)
    per_axis = DA_QK_DIM // 4
    inv = ROPE_BASE ** (-jnp.arange(per_axis, dtype=F32) / per_axis)
    ang = jnp.concatenate([r[:, None] * inv, col[:, None] * inv], axis=-1)
    cos, sin = jnp.cos(ang), jnp.sin(ang)
    cos64 = jnp.concatenate([cos, cos], axis=-1)
    sin64 = jnp.concatenate([-sin, sin], axis=-1)
    return jnp.concatenate([cos64, cos64], axis=-1), jnp.concatenate([sin64, sin64], axis=-1)


def _attn_kernel(lam_ref, q_ref, *refs, nseg, out_scale):
    lam = lam_ref[0, 0]
    segs = [(refs[2 * j], refs[2 * j + 1]) for j in range(nseg)]
    g_ref, o_ref = refs[2 * nseg], refs[2 * nseg + 1]
    tq = q_ref.shape[2]
    m = [jnp.full((tq, LANES), NEG_BIG, F32)] * 2
    l = [jnp.zeros((tq, LANES), F32)] * 2
    acc = [jnp.zeros((tq, DA_V_DIM), F32)] * 2

    for k_ref, v_ref in segs:
        nchunk, kc = k_ref.shape[2], k_ref.shape[4]
        for c in range(nchunk):
            v = v_ref[0, c * kc:(c + 1) * kc, :]
            probs, alphas = [], []
            for sub in range(2):
                s = jnp.dot(q_ref[0, sub], k_ref[0, sub, c], preferred_element_type=F32)
                mx = s[:, :LANES]
                for j in range(1, kc // LANES):
                    mx = jnp.maximum(mx, s[:, j * LANES:(j + 1) * LANES])
                m_new = jnp.maximum(m[sub], jnp.max(mx, axis=-1, keepdims=True))
                alpha = jnp.exp2(m[sub] - m_new)
                p = jnp.exp2(s - jnp.concatenate([m_new] * (kc // LANES), axis=1))
                lsum = alpha * l[sub]
                for j in range(kc // LANES):
                    lsum = lsum + p[:, j * LANES:(j + 1) * LANES]
                l[sub] = lsum
                m[sub] = m_new
                probs.append(p.astype(BF16))
                alphas.append(alpha)
            pv = jnp.dot(jnp.concatenate(probs, axis=0), v, preferred_element_type=F32)
            for sub in range(2):
                acc[sub] = alphas[sub] * acc[sub] + pv[sub * tq:(sub + 1) * tq]

    l1 = jnp.sum(l[0], axis=-1, keepdims=True)
    l2 = jnp.sum(l[1], axis=-1, keepdims=True)
    o = acc[0] * (1.0 / l1) - lam * (acc[1] * (1.0 / l2))
    ms = jnp.mean(o * o, axis=-1, keepdims=True)
    o_ref[0] = o * lax.rsqrt(ms + EPS) * g_ref[...] * out_scale


def _diff_attn(qp, segments, lam, g, lam_init):
    b, _, s, d = qp.shape
    tq = min(ATT_TQ, s)
    seg_specs, seg_args = [], []
    for kt, v in segments:
        nchunk, kc = kt.shape[2], kt.shape[4]
        seg_specs += [pl.BlockSpec((1, 2, nchunk, d, kc), lambda bi, h, i: (bi, h, 0, 0, 0)),
                      pl.BlockSpec((1, nchunk * kc, DA_V_DIM), lambda bi, h, i: (bi, 0, h))]
        seg_args += [kt, v]
    return pl.pallas_call(
        functools.partial(_attn_kernel, nseg=len(segments), out_scale=1.0 - lam_init),
        out_shape=jax.ShapeDtypeStruct((b, s, DA_WIDTH), F32),
        grid=(b, DA_HEADS, s // tq),
        in_specs=[pl.BlockSpec(memory_space=pltpu.SMEM),
                  pl.BlockSpec((1, 2, tq, d), lambda bi, h, i: (bi, h, i, 0))] + seg_specs
        + [pl.BlockSpec((1, DA_V_DIM), lambda bi, h, i: (0, 0))],
        out_specs=pl.BlockSpec((1, tq, DA_V_DIM), lambda bi, h, i: (bi, i, h)),
        compiler_params=_cparams(("arbitrary", "arbitrary", "arbitrary")),
        name="diff_attn",
    )(lam.reshape(1, 1).astype(F32), qp, *seg_args, g.reshape(1, DA_V_DIM).astype(F32))


def _gelu_tanh(x):
    return 0.5 * x * (1.0 + jnp.tanh(math.sqrt(2.0 / math.pi) * (x + 0.044715 * x * x * x)))


def _proj_out_kernel(x_ref, lg_ref, hf_ref, hb_ref, yf_ref, yb_ref, xs_ref, z_ref, da_ref,
                     dvec_ref, ng_ref, gate_ref, w_ref, o_ref):
    lru = _gelu_tanh(lg_ref[0]) * (hf_ref[0] + hb_ref[0])
    y = yf_ref[0] + yb_ref[0] + dvec_ref[...] * xs_ref[0]
    y = y * _silu(z_ref[0])
    ssd = y * lax.rsqrt(jnp.mean(y * y, axis=-1, keepdims=True) + EPS) * ng_ref[...]
    o = jnp.dot(lru.astype(BF16), w_ref[0:LRU_WIDTH, :], preferred_element_type=F32)
    o = o + jnp.dot(ssd.astype(BF16), w_ref[LRU_WIDTH:LRU_WIDTH + SSD_WIDTH, :], preferred_element_type=F32)
    o = o + jnp.dot(da_ref[0].astype(BF16), w_ref[LRU_WIDTH + SSD_WIDTH:, :], preferred_element_type=F32)
    o_ref[0] = x_ref[0] + gate_ref[0] * o


def _proj_out(x, lg, hf, hb, yf, yb, xbc, z, da, dvec, ng, gate, w):
    b, s, d = x.shape
    tm = min(PROJ_TM, s)
    tok = lambda wd: pl.BlockSpec((1, tm, wd), lambda bi, i: (bi, i, 0))
    const2 = lambda bi, i: (0, 0)
    return pl.pallas_call(
        _proj_out_kernel,
        out_shape=jax.ShapeDtypeStruct((b, s, d), F32),
        grid=(b, s // tm),
        in_specs=[tok(d), tok(LRU_WIDTH), tok(LRU_WIDTH), tok(LRU_WIDTH), tok(SSD_WIDTH),
                  tok(SSD_WIDTH), tok(SSD_WIDTH), tok(SSD_WIDTH), tok(DA_WIDTH),
                  pl.BlockSpec((1, SSD_WIDTH), const2), pl.BlockSpec((1, SSD_WIDTH), const2),
                  pl.BlockSpec((1, 1, d), lambda bi, i: (bi, 0, 0)),
                  pl.BlockSpec(w.shape, const2)],
        out_specs=tok(d),
        compiler_params=_cparams(("arbitrary", "arbitrary")),
        name="proj_out",
    )(x, lg, hf, hb, yf, yb, xbc, z, da, dvec, ng, gate, w)


def _route_kernel(x_ref, g_ref, sh_ref, sc_ref, rwh_ref, rwl_ref, rb_ref, c0_ref, h_ref, ri_ref, gt_ref,
                  slt_ref, n_ref, base_ref, cnt_ref, carry_s):
    i = pl.program_id(0)
    tm = x_ref.shape[0]

    @pl.when(i == 0)
    def _():
        carry_s[...] = c0_ref[...]

    h = _norm_mod(x_ref[...], g_ref[...], sh_ref[0], sc_ref[0])
    h_hi = h.astype(BF16)
    h_ref[...] = h_hi
    h_lo = (h - h_hi.astype(F32)).astype(BF16)
    logits = (jnp.dot(h_hi, rwh_ref[...], preferred_element_type=F32)
              + jnp.dot(h_hi, rwl_ref[...], preferred_element_type=F32)
              + jnp.dot(h_lo, rwh_ref[...], preferred_element_type=F32)) + rb_ref[...]
    lane = lax.broadcasted_iota(I32, logits.shape, 1)
    lane_f = lane.astype(F32)
    vals, idxs = [], []
    cur = logits
    for _ in range(TOP_K):
        m = jnp.max(cur, axis=-1, keepdims=True)
        idx = jnp.min(jnp.where(cur >= m, lane_f, float(LANES)), axis=-1, keepdims=True).astype(I32)
        vals.append(m)
        idxs.append(idx)
        cur = jnp.where(lane == idx, 2.0 * NEG_BIG, cur)
    es = [jnp.exp(v - vals[0]) for v in vals]
    inv = 1.0 / (es[0] + es[1] + es[2] + es[3])
    onehot = jnp.zeros(logits.shape, F32)
    for idx in idxs:
        onehot = onehot + jnp.where(lane == idx, 1.0, 0.0)
    ri = lax.broadcasted_iota(I32, (tm, tm), 0)
    ci = lax.broadcasted_iota(I32, (tm, tm), 1)
    strict = jnp.where(ci < ri, 1.0, 0.0).astype(BF16)
    before = jnp.dot(strict, onehot.astype(BF16), preferred_element_type=F32)
    n_tile = jnp.sum(onehot, axis=0, keepdims=True).astype(I32)
    pad_tile = _round_up8(n_tile)
    li = lax.broadcasted_iota(I32, (LANES, LANES), 0)
    lj = lax.broadcasted_iota(I32, (LANES, LANES), 1)
    below = jnp.where(li < lj, 1.0, 0.0).astype(BF16)
    pad_rows = jnp.broadcast_to(pad_tile.astype(F32), (SUBLANES, LANES)).astype(BF16)
    off = jnp.dot(pad_rows, below, preferred_element_type=F32)[0:1, :]
    place = before + off
    ri_out = jnp.zeros(logits.shape, I32)
    gt_out = jnp.zeros(logits.shape, F32)
    sl_f = jnp.full(logits.shape, -1.0, F32)
    for k in range(TOP_K):
        sloc = jnp.sum(jnp.where(lane == idxs[k], place, 0.0), axis=-1, keepdims=True)
        ri_out = ri_out + jnp.where(lane == k, idxs[k], 0) + jnp.where(lane == TOP_K + k, sloc.astype(I32), 0)
        sl_f = jnp.where(lane == k, sloc, sl_f)
        gt_out = gt_out + jnp.where(lane == k, es[k] * inv, 0.0)
    ri_ref[...] = ri_out
    gt_ref[...] = gt_out
    slt_ref[0] = sl_f.T[0:SUBLANES, :].astype(I32)
    n_ref[0] = n_tile
    base_ref[0] = carry_s[...]
    carry_s[...] = carry_s[...] + pad_tile
    cnt_ref[...] = carry_s[...]


def _route(x, g, sh, sc, tiles_per_mod, rw, rb, counts0):
    t, d = x.shape
    tm = ROUTE_TM
    tile = lambda wd: pl.BlockSpec((tm, wd), lambda i: (i, 0))
    const = lambda i: (0, 0)
    mod = pl.BlockSpec((1, 1, d), lambda i: (i // tiles_per_mod, 0, 0))
    nt = t // tm
    per_tile = pl.BlockSpec((1, 1, LANES), lambda i: (i, 0, 0))
    return pl.pallas_call(
        _route_kernel,
        out_shape=[jax.ShapeDtypeStruct((t, d), BF16), jax.ShapeDtypeStruct((t, LANES), I32),
                   jax.ShapeDtypeStruct((t, LANES), F32), jax.ShapeDtypeStruct((nt, SUBLANES, tm), I32),
                   jax.ShapeDtypeStruct((nt, 1, LANES), I32),
                   jax.ShapeDtypeStruct((nt, 1, LANES), I32), jax.ShapeDtypeStruct((1, LANES), I32)],
        grid=(nt,),
        in_specs=[tile(d), pl.BlockSpec((1, d), const), mod, mod,
                  pl.BlockSpec((d, LANES), const), pl.BlockSpec((d, LANES), const),
                  pl.BlockSpec((1, LANES), const), pl.BlockSpec((1, LANES), const)],
        out_specs=[tile(d), tile(LANES), tile(LANES), pl.BlockSpec((1, SUBLANES, tm), lambda i: (i, 0, 0)),
                   per_tile, per_tile, pl.BlockSpec((1, LANES), const)],
        scratch_shapes=[pltpu.VMEM((1, LANES), I32)],
        compiler_params=_cparams(("arbitrary",)),
        name="moe_route",
    )(x, g.reshape(1, d), sh, sc, rw[0], rw[1], rb, counts0)


MOE_STAGE = -(-(ROUTE_TM * TOP_K + N_EXPERTS * (SUBLANES - 1)) // 256) * 256
_TAB_N, _TAB_DST, _TAB_OFF, _TAB_TOTAL = 0, N_EXPERTS, 2 * N_EXPERTS, 3 * N_EXPERTS


U32 = jnp.uint32
_HI16 = 0xFFFF0000


def _pack_halves(a, is_bf16=False):
    n = a.shape[1] // 2
    lo, hi = a[:, :n], a[:, n:]
    if not is_bf16:
        lo, hi = lo.astype(BF16).astype(F32), hi.astype(BF16).astype(F32)
    return (pltpu.bitcast(lo, U32) >> 16) | (pltpu.bitcast(hi, U32) & U32(_HI16))


def _unpack_halves(u):
    lo = pltpu.bitcast(u << 16, F32).astype(BF16)
    hi = pltpu.bitcast(u & U32(_HI16), F32).astype(BF16)
    return lo, hi


_BIG_SHIFT = 2
_BIG_CHUNKS = 1 << _BIG_SHIFT


def _chunk_copies(tab_ref, stage_ref, rows_hbm, sem, to_hbm):
    def copy(off, dst, rows):
        prio = 0 if rows > SUBLANES else 1
        s = stage_ref.at[pl.ds(pl.multiple_of(off, SUBLANES), rows), :]
        r = rows_hbm.at[pl.ds(pl.multiple_of(dst, SUBLANES), rows), :]
        if to_hbm:
            pltpu.make_async_copy(s, r, sem).start(priority=prio)
        else:
            pltpu.make_async_copy(r, s, sem).start(priority=prio)

    def expert(e, carry):
        n = tab_ref[0, 0, _TAB_N + e]
        dst = tab_ref[0, 0, _TAB_DST + e]
        off = tab_ref[0, 0, _TAB_OFF + e]
        big_rows = _BIG_CHUNKS * SUBLANES
        n_big = lax.shift_right_logical(n, _BIG_SHIFT)

        def big(j, c):
            copy(off + j * big_rows, dst + j * big_rows, big_rows)
            return c

        def small(j, c):
            copy(off + (n_big * _BIG_CHUNKS + j) * SUBLANES, dst + (n_big * _BIG_CHUNKS + j) * SUBLANES, SUBLANES)
            return c

        lax.fori_loop(0, n_big, big, 0)
        lax.fori_loop(0, n & (_BIG_CHUNKS - 1), small, 0)
        return carry
    lax.fori_loop(0, N_EXPERTS, expert, 0)


def _chunk_waits(tab_ref, stage_ref, rows_hbm, sem):
    rows = tab_ref[0, 0, _TAB_TOTAL] * SUBLANES
    pltpu.make_async_copy(rows_hbm.at[pl.ds(0, rows), :], stage_ref.at[pl.ds(0, rows), :], sem).wait()


def _dispatch_kernel(tab_ref, tprev_ref, gap_ref, tail_ref, sloc_ref, *refs, tiles):
    h_refs = refs[:len(tiles)]
    xs_out, stage_s, zero_s, sems = refs[len(tiles):]
    i = pl.program_id(0)
    n_steps = sum(tiles)
    slot = i % 2
    row = lax.broadcasted_iota(I32, (MOE_STAGE, ROUTE_TM), 0)
    perm = jnp.zeros((MOE_STAGE, ROUTE_TM), F32)
    for k in range(TOP_K):
        perm = perm + jnp.where(row == sloc_ref[0, k:k + 1, :], 1.0, 0.0)
    perm = perm.astype(BF16)
    off = 0
    for h_ref, nt in zip(h_refs, tiles):
        @pl.when((i >= off) & (i < off + nt))
        def _(h_ref=h_ref):
            stage_s[slot] = _pack_halves(jnp.dot(perm, h_ref[...], preferred_element_type=F32), is_bf16=True)
        off += nt
    _chunk_copies(tab_ref, stage_s.at[slot], xs_out, sems.at[slot], True)

    @pl.when(i > 0)
    def _():
        _chunk_waits(tprev_ref, stage_s.at[1 - slot], xs_out, sems.at[1 - slot])

    @pl.when(i == n_steps - 1)
    def _():
        _chunk_waits(tab_ref, stage_s.at[slot], xs_out, sems.at[slot])
        sem = sems.at[slot]
        zero_s[...] = jnp.zeros(zero_s.shape, U32)
        zr = zero_s.shape[0]
        n_big, n_small, start = tail_ref[0], tail_ref[1], tail_ref[2]

        def big(j, c):
            pltpu.make_async_copy(zero_s, xs_out.at[pl.ds(pl.multiple_of(start + j * zr, SUBLANES), zr), :], sem).start()
            return c

        def small(j, c):
            r0 = pl.multiple_of(start + n_big * zr + j * SUBLANES, SUBLANES)
            pltpu.make_async_copy(zero_s.at[pl.ds(0, SUBLANES), :], xs_out.at[pl.ds(r0, SUBLANES), :], sem).start()
            return c

        def big_wait(j, c):
            pltpu.make_async_copy(zero_s, xs_out.at[pl.ds(0, zr), :], sem).wait()
            return c

        def small_wait(j, c):
            pltpu.make_async_copy(zero_s.at[pl.ds(0, SUBLANES), :], xs_out.at[pl.ds(0, SUBLANES), :], sem).wait()
            return c

        lax.fori_loop(0, n_big, big, 0)
        lax.fori_loop(0, n_small, small, 0)
        lax.fori_loop(0, n_big, big_wait, 0)
        lax.fori_loop(0, n_small, small_wait, 0)

        _chunk_copies(gap_ref, zero_s, xs_out, sem, True)

        def gap_wait(e, c):
            rows = gap_ref[0, 0, _TAB_N + e] * SUBLANES

            @pl.when(rows > 0)
            def _():
                pltpu.make_async_copy(zero_s.at[pl.ds(0, rows), :], xs_out.at[pl.ds(0, rows), :], sem).wait()
            return c
        lax.fori_loop(0, N_EXPERTS, gap_wait, 0)


def _dispatch(hs, tabs, slocs, gaps, tail, n_max):
    d = hs[0].shape[1]
    tm = ROUTE_TM
    tiles = tuple(h.shape[0] // tm for h in hs)
    specs, off = [], 0
    for nt in tiles:
        specs.append(pl.BlockSpec((tm, d), lambda i, off=off, nt=nt: (jnp.clip(i - off, 0, nt - 1), 0)))
        off += nt
    return pl.pallas_call(
        functools.partial(_dispatch_kernel, tiles=tiles),
        out_shape=jax.ShapeDtypeStruct((n_max, d // 2), U32),
        grid=(sum(tiles),),
        in_specs=[pl.BlockSpec((1, 1, LANES), lambda i: (i, 0, 0), memory_space=pltpu.SMEM),
                  pl.BlockSpec((1, 1, LANES), lambda i: (jnp.maximum(i - 1, 0), 0, 0), memory_space=pltpu.SMEM),
                  pl.BlockSpec((1, 1, LANES), lambda i: (0, 0, 0), memory_space=pltpu.SMEM),
                  pl.BlockSpec(memory_space=pltpu.SMEM),
                  pl.BlockSpec((1, SUBLANES, tm), lambda i: (i, 0, 0))] + specs,
        out_specs=pl.BlockSpec(memory_space=pl.ANY),
        scratch_shapes=[pltpu.VMEM((2, MOE_STAGE, d // 2), U32), pltpu.VMEM((MOE_TM, d // 2), U32),
                        pltpu.SemaphoreType.DMA((2,))],
        compiler_params=_cparams(("arbitrary",)),
        name="moe_dispatch",
    )(tabs, tabs, gaps, tail, slocs, *hs)


_SPLIT_W = 2 * LANES


def _gmm_kernel(te_ref, live_ref, newe_ref, xs_ref, wgu_ref, perm_ref, bg_ref, bu_ref,
                wdf_ref, bd_ref, y_ref, wg_ref, wu_ref, wd_ref):
    del te_ref
    v = pl.program_id(0)
    live = live_ref[v] > 0

    @pl.when((newe_ref[v] > 0) & live)
    def _():
        for j in range(wgu_ref.shape[1] // _SPLIT_W):
            r = jnp.dot(wgu_ref[:, j * _SPLIT_W:(j + 1) * _SPLIT_W].astype(BF16), perm_ref[...],
                        preferred_element_type=F32)
            wg_ref[:, j * LANES:(j + 1) * LANES] = r[:, :LANES].astype(BF16)
            wu_ref[:, j * LANES:(j + 1) * LANES] = r[:, LANES:].astype(BF16)
        wd_ref[...] = wdf_ref[...].astype(BF16)

    @pl.when(live)
    def _():
        x_lo, x_hi = _unpack_halves(xs_ref[...])
        half = x_lo.shape[1]
        gate = (jnp.dot(x_lo, wg_ref[:half, :], preferred_element_type=F32)
                + jnp.dot(x_hi, wg_ref[half:, :], preferred_element_type=F32)) + bg_ref[...]
        up = (jnp.dot(x_lo, wu_ref[:half, :], preferred_element_type=F32)
              + jnp.dot(x_hi, wu_ref[half:, :], preferred_element_type=F32)) + bu_ref[...]
        gate = jnp.minimum(gate, SWIGLU_LIMIT)
        up = jnp.clip(up, -SWIGLU_LIMIT, SWIGLU_LIMIT)
        glu = gate * _sigmoid(SWIGLU_ALPHA * gate)
        act = ((up + 1.0) * glu).astype(BF16)
        y_ref[...] = _pack_halves(jnp.dot(act, wd_ref[...], preferred_element_type=F32) + bd_ref[...])

    @pl.when(jnp.logical_not(live))
    def _():
        y_ref[...] = jnp.zeros(y_ref.shape, U32)


def _gmm(xs, tiles, layer, w_gu, bg, bu, w_down, bd):
    n_rows = xs.shape[0]
    _, _, d, de2 = w_gu.shape
    de = de2 // 2
    tm = MOE_TM
    nv = n_rows // tm
    src = jnp.arange(_SPLIT_W)
    dst = jnp.where(src % 2 == 0, src // 2, LANES + src // 2)
    perm = (dst[:, None] == jnp.arange(_SPLIT_W)[None, :]).astype(BF16)
    ex = lambda shape: pl.BlockSpec((None,) + shape, lambda v, te, lv, ne: (te[v], 0, 0))
    lay = lambda shape: pl.BlockSpec((None, None) + shape, lambda v, te, lv, ne: (layer, te[v], 0, 0))
    rows = pl.BlockSpec((tm, d // 2), lambda v, te, lv, ne: (v, 0))
    grid_spec = pltpu.PrefetchScalarGridSpec(
        num_scalar_prefetch=3,
        grid=(nv,),
        in_specs=[rows, lay((d, de2)), pl.BlockSpec((_SPLIT_W, _SPLIT_W), lambda v, *_: (0, 0)),
                  ex((1, de)), ex((1, de)), lay((de, d)), ex((1, d))],
        out_specs=rows,
        scratch_shapes=[pltpu.VMEM((d, de), BF16), pltpu.VMEM((d, de), BF16), pltpu.VMEM((de, d), BF16)],
    )
    return pl.pallas_call(
        _gmm_kernel,
        out_shape=jax.ShapeDtypeStruct((n_rows, d // 2), U32),
        grid_spec=grid_spec,
        compiler_params=_cparams(("arbitrary",)),
        name="moe_experts",
    )(*tiles, xs, w_gu, perm, bg, bu, w_down, bd)


def _tile_tables(cap, n_rows):
    tm = MOE_TM
    end = jnp.cumsum(cap)
    start = jnp.arange(n_rows // tm, dtype=I32) * tm
    e = jnp.minimum(jnp.sum((end[None, :] <= start[:, None]).astype(I32), axis=1), N_EXPERTS - 1).astype(I32)
    live = (start < end[N_EXPERTS - 1]).astype(I32)
    new_e = (e != jnp.concatenate([jnp.full((1,), -1, I32), e[:-1]])).astype(I32)
    return e, live, new_e


def _combine_kernel(tcur_ref, tnext_ref, sloc_ref, gt_ref, x_ref, g2_ref, y_hbm, o_ref, stage_s, sems, *, n):
    i = pl.program_id(0)
    slot = i % 2

    @pl.when(i == 0)
    def _():
        stage_s[...] = jnp.zeros(stage_s.shape, U32)
        _chunk_copies(tcur_ref, stage_s.at[0], y_hbm, sems.at[0], False)

    @pl.when(i + 1 < n)
    def _():
        _chunk_copies(tnext_ref, stage_s.at[1 - slot], y_hbm, sems.at[1 - slot], False)

    _chunk_waits(tcur_ref, stage_s.at[slot], y_hbm, sems.at[slot])
    rows_lo, rows_hi = _unpack_halves(stage_s[slot])
    lane = lax.broadcasted_iota(I32, (x_ref.shape[0], MOE_STAGE), 1)
    gt = gt_ref[...]
    sloc = sloc_ref[...]
    gmat = jnp.zeros(lane.shape, F32)
    for k in range(TOP_K):
        gmat = gmat + jnp.where(lane == sloc[:, TOP_K + k:TOP_K + k + 1], gt[:, k:k + 1], 0.0)
    g_hi = gmat.astype(BF16)
    g_lo = (gmat - g_hi.astype(F32)).astype(BF16)
    half = rows_lo.shape[1]
    for rows, sl in ((rows_lo, slice(0, half)), (rows_hi, slice(half, 2 * half))):
        f = (jnp.dot(g_hi, rows, preferred_element_type=F32)
             + jnp.dot(g_lo, rows, preferred_element_type=F32))
        o_ref[:, sl] = x_ref[:, sl] + g2_ref[0][:, sl] * f


def _combine(y, tabs, sloc, gates, x, g2, tiles_per_mod):
    t, d = x.shape
    tm = ROUTE_TM
    nt = t // tm
    return pl.pallas_call(
        functools.partial(_combine_kernel, n=nt),
        out_shape=jax.ShapeDtypeStruct((t, d), F32),
        grid=(nt,),
        in_specs=[pl.BlockSpec((1, 1, LANES), lambda i: (i, 0, 0), memory_space=pltpu.SMEM),
                  pl.BlockSpec((1, 1, LANES), lambda i: (jnp.minimum(i + 1, nt - 1), 0, 0),
                               memory_space=pltpu.SMEM),
                  pl.BlockSpec((tm, LANES), lambda i: (i, 0)),
                  pl.BlockSpec((tm, LANES), lambda i: (i, 0)),
                  pl.BlockSpec((tm, d), lambda i: (i, 0)),
                  pl.BlockSpec((1, 1, d), lambda i: (i // tiles_per_mod, 0, 0)),
                  pl.BlockSpec(memory_space=pl.ANY)],
        out_specs=pl.BlockSpec((tm, d), lambda i: (i, 0)),
        scratch_shapes=[pltpu.VMEM((2, MOE_STAGE, d // 2), U32), pltpu.SemaphoreType.DMA((2,))],
        compiler_params=_cparams(("arbitrary",)),
        name="moe_combine",
    )(tabs, tabs, sloc, gates, x, g2, y)


def _moe(streams, layer, g, router_w, router_b, w_gu_all, b_gu, w_down_all, b_down):
    d = streams[0][0].shape[1]
    tm = ROUTE_TM
    rw_f = jnp.zeros((d, LANES), F32).at[:, :N_EXPERTS].set(router_w.astype(F32))
    rw_hi = rw_f.astype(BF16)
    rw = (rw_hi, (rw_f - rw_hi.astype(F32)).astype(BF16))
    rb = jnp.full((1, LANES), NEG_BIG, F32).at[0, :N_EXPERTS].set(router_b.astype(F32))
    counts = jnp.zeros((1, LANES), I32)
    routed = []
    for x_tok, sh, sc, _, tpm in streams:
        h, route_i, gates, sloc_t, n_tile, seg_base, counts = _route(x_tok, g, sh, sc, tpm, rw, rb, counts)
        routed.append((h, route_i, gates, sloc_t, n_tile, seg_base))
    n_tiles = sum(x_tok.shape[0] for x_tok, *_ in streams) // tm
    n_max = n_tiles * tm * TOP_K + n_tiles * N_EXPERTS * (SUBLANES - 1) + N_EXPERTS * (MOE_TM - SUBLANES)
    n_max = (n_max + MOE_TM - 1) // MOE_TM * MOE_TM
    cnt = counts[0, :N_EXPERTS]
    cap = (cnt + MOE_TM - 1) // MOE_TM * MOE_TM
    total = jnp.sum(cap)
    base = jnp.cumsum(cap) - cap
    gap = jnp.concatenate([(cap - cnt) // SUBLANES, base + cnt, jnp.zeros((LANES - 2 * N_EXPERTS,), I32)])
    tabs = []
    for h, _, _, _, n_tile, seg_base in routed:
        nt = h.shape[0] // tm
        pad = _round_up8(n_tile[:, 0, :N_EXPERTS])
        off = jnp.cumsum(pad, axis=1) - pad
        dst = base[None, :] + seg_base[:, 0, :N_EXPERTS]
        nch = pad // SUBLANES
        tab = jnp.concatenate([nch, dst, off, jnp.sum(nch, axis=1, keepdims=True),
                               jnp.zeros((nt, LANES - 3 * N_EXPERTS - 1), I32)], axis=1)
        tabs.append(tab.astype(I32).reshape(nt, 1, LANES))
    free = n_max - total
    tail = jnp.stack([free // MOE_TM, (free % MOE_TM) // SUBLANES, total]).astype(I32)
    xs = _dispatch([r[0] for r in routed], jnp.concatenate(tabs, axis=0),
                   jnp.concatenate([r[3] for r in routed], axis=0), gap.astype(I32).reshape(1, 1, LANES),
                   tail, n_max)
    bg = b_gu[:, None, 0::2].astype(F32)
    bu = b_gu[:, None, 1::2].astype(F32)
    y = _gmm(xs, _tile_tables(cap, n_max), layer, w_gu_all, bg, bu, w_down_all, b_down[:, None, :].astype(F32))
    return [_combine(y, tab, r[1], r[2], x_tok, g2, tpm)
            for (x_tok, _, _, g2, tpm), r, tab in zip(streams, routed, tabs)]


def kernel(x, c, ctx, c_ctx, w_mod, b_mod, norm1_g, norm2_g, w_in, w_out, lru_conv_w, lru_conv_b, lru_wa, lru_ba, lru_wx, lru_bx, lru_lam, ssd_conv_w, ssd_conv_b, ssd_a_log, ssd_dt_bias, ssd_d, ssd_norm_g, da_q_norm, da_k_norm, da_lam_q, da_lam_k, da_subln_g, router_w, router_b, exp_w_gu, exp_b_gu, exp_w_down, exp_b_down):
    b, s, d = x.shape
    n_ctx = ctx.shape[1]
    depth = w_mod.shape[0]
    cos, sin = _rope_tables(s)
    x_lat, x_ctx = x, ctx
    mod_rows = jnp.zeros((2 * SUBLANES, d), F32).at[:b].set(c).at[b].set(c_ctx)

    for l in range(depth):
        need_ctx = l < depth - 1
        lam_init = 0.8 - 0.6 * math.exp(-0.3 * l)
        mod = _modulation(mod_rows, w_mod[l], b_mod[l])
        m_lat = mod[:b].reshape(b, 1, 6, d)
        m_ctx = jnp.broadcast_to(mod[b].reshape(1, 1, 6, d), (b, 1, 6, d))
        part = lambda m, j: m[:, :, j, :]

        w_in_l = _prep_w_in(w_in[l])
        qk_gains = (da_q_norm[l], da_k_norm[l], cos, sin)
        p_lat = _proj_in(x_lat, norm1_g[l], part(m_lat, 0), part(m_lat, 1), w_in_l, *qk_gains, rope=True, kc=ATT_KC)
        p_ctx = _proj_in(x_ctx, norm1_g[l], part(m_ctx, 0), part(m_ctx, 1), w_in_l, *qk_gains, rope=False, kc=ATT_KC)

        zl = jnp.zeros((b, LRU_WIDTH), F32)
        lru_args = (lru_wa[l], lru_ba[l], lru_wx[l], lru_bx[l], lru_lam[l])
        xc_ctx = _dwconv(p_ctx["lx"], lru_conv_w[l], lru_conv_b[l], act=False)
        hf_c, hb_c = _lru_scan(xc_ctx, *lru_args, zl, zl)
        xc_lat = _dwconv(p_lat["lx"], lru_conv_w[l], lru_conv_b[l], act=False)
        hf_l, hb_l = _lru_scan(xc_lat, *lru_args, hf_c[:, -1], hb_c[:, 0])

        zs = jnp.zeros((b, SSD_GROUPS, 2 * SSD_HEAD_DIM, SSD_STATE), F32)
        xbc_ctx = _dwconv(p_ctx["xbc"], ssd_conv_w[l], ssd_conv_b[l], act=True)
        yf_c, yb_c, sf_c, sb_c = _ssd_scan(xbc_ctx, p_ctx["dt"], ssd_a_log[l], ssd_dt_bias[l], zs, zs)
        xbc_lat = _dwconv(p_lat["xbc"], ssd_conv_w[l], ssd_conv_b[l], act=True)
        yf_l, yb_l, _, _ = _ssd_scan(xbc_lat, p_lat["dt"], ssd_a_log[l], ssd_dt_bias[l], sf_c, sb_c)

        lq = da_lam_q[l].astype(F32)
        lk = da_lam_k[l].astype(F32)
        lam = jnp.exp(jnp.sum(lq[0] * lk[0])) - jnp.exp(jnp.sum(lq[1] * lk[1])) + lam_init
        da_l = _diff_attn(p_lat["q"], [(p_ctx["k"], p_ctx["v"]), (p_lat["k"], p_lat["v"])], lam,
                          da_subln_g[l], lam_init)

        dvec = jnp.repeat(ssd_d[l].astype(F32), SSD_HEAD_DIM).reshape(1, SSD_WIDTH)
        ng = ssd_norm_g[l].astype(F32).reshape(1, SSD_WIDTH)
        w_out_l = w_out[l].astype(BF16)
        if need_ctx:
            da_c = _diff_attn(p_ctx["q"], [(p_ctx["k"], p_ctx["v"])], lam, da_subln_g[l], lam_init)
            x_ctx = _proj_out(x_ctx, p_ctx["lg"], hf_c, hb_c, yf_c, yb_c, xbc_ctx, p_ctx["z"], da_c,
                              dvec, ng, part(m_ctx, 2), w_out_l)
        x_lat = _proj_out(x_lat, p_lat["lg"], hf_l, hb_l, yf_l, yb_l, xbc_lat, p_lat["z"], da_l,
                          dvec, ng, part(m_lat, 2), w_out_l)

        moe_w = (router_w[l], router_b[l], exp_w_gu, exp_b_gu[l], exp_w_down, exp_b_down[l])
        streams = [(x_lat.reshape(-1, d), part(m_lat, 3), part(m_lat, 4), part(m_lat, 5), s // ROUTE_TM)]
        if need_ctx:
            streams.append((x_ctx.reshape(-1, d), part(m_ctx, 3), part(m_ctx, 4), part(m_ctx, 5), n_ctx // ROUTE_TM))
        outs = _moe(streams, l, norm2_g[l], *moe_w)
        x_lat = outs[0].reshape(b, s, d)
        if need_ctx:
            x_ctx = outs[1].reshape(b, n_ctx, d)
    return x_lat
```
